```python
import math
import jax, jax.numpy as jnp
from jax import lax
import numpy as np

D_MODEL = 1024
BATCH = 4
SEQ = 4096
DEPTH = 1

EPS = 1e-6
ROPE_BASE = 10000.0
N_MLA_HEADS = 8
QK_NOPE = 64
QK_ROPE = 32
V_HEAD = 64
Q_LORA = 256
KV_LORA = 128
Q_BLOCK = 128
N_RET_HEADS = 4
RET_DK = 128
RET_DV = 128
RET_CHUNK = 128
MLA_WIDTH = N_MLA_HEADS * V_HEAD
RET_WIDTH = N_RET_HEADS * RET_DV
D_MIX = MLA_WIDTH + RET_WIDTH
IN_SPLITS = (Q_LORA, KV_LORA, QK_ROPE,
             N_RET_HEADS * RET_DK, N_RET_HEADS * RET_DK,
             N_RET_HEADS * RET_DV, N_RET_HEADS * RET_DV)
IN_COLS = sum(IN_SPLITS)
MEM_LEN = 256
N_XATTN_HEADS = 4
XATTN_HEAD = D_MODEL // N_XATTN_HEADS
D_FF = 2816
CONV_W = 3

kernel_name = "hymba_mla_retnet_convffn_layer"


def rmsnorm(x, g):
    x32 = x.astype(jnp.float32)
    inv = lax.rsqrt(jnp.mean(x32 * x32, axis=-1, keepdims=True) + EPS)
    return (x32 * inv * g.astype(jnp.float32)).astype(x.dtype)


def rope(x, positions):
    d = x.shape[-1]
    inv_freq = 1.0 / (ROPE_BASE ** (jnp.arange(0, d, 2, dtype=jnp.float32) / d))
    ang = positions.astype(jnp.float32)[..., None] * inv_freq
    ang = ang.reshape(ang.shape[:2] + (1,) * (x.ndim - 3) + (d // 2,))
    cos, sin = jnp.cos(ang), jnp.sin(ang)
    x32 = x.astype(jnp.float32)
    x1, x2 = x32[..., : d // 2], x32[..., d // 2:]
    out = jnp.concatenate([x1 * cos - x2 * sin, x2 * cos + x1 * sin], axis=-1)
    return out.astype(x.dtype)


def mla_group(c_q, c_kv, k_rope_raw, positions, g_q_lat, w_uq, g_kv_lat, w_ukv):
    B, S, _ = c_q.shape
    q = (rmsnorm(c_q, g_q_lat) @ w_uq).reshape(B, S, N_MLA_HEADS, QK_NOPE + QK_ROPE)
    q_nope, q_rope = q[..., :QK_NOPE], rope(q[..., QK_NOPE:], positions)
    kv = (rmsnorm(c_kv, g_kv_lat) @ w_ukv).reshape(B, S, N_MLA_HEADS, QK_NOPE + V_HEAD)
    k_nope, v = kv[..., :QK_NOPE], kv[..., QK_NOPE:]
    k_rope = rope(k_rope_raw, positions)
    scale = 1.0 / math.sqrt(QK_NOPE + QK_ROPE)

    nb = S // Q_BLOCK
    qn_blocks = q_nope.reshape(B, nb, Q_BLOCK, N_MLA_HEADS, QK_NOPE).transpose(1, 0, 2, 3, 4)
    qr_blocks = q_rope.reshape(B, nb, Q_BLOCK, N_MLA_HEADS, QK_ROPE).transpose(1, 0, 2, 3, 4)
    starts = jnp.arange(nb, dtype=jnp.int32) * Q_BLOCK
    key_idx = jnp.arange(S, dtype=jnp.int32)

    def one_block(args):
        qn_b, qr_b, start = args
        s = (jnp.einsum('bqhd,bkhd->bhqk', qn_b, k_nope)
             + jnp.einsum('bqhd,bkd->bhqk', qr_b, k_rope)).astype(jnp.float32) * scale
        q_idx = start + jnp.arange(Q_BLOCK, dtype=jnp.int32)
        mask = key_idx[None, :] <= q_idx[:, None]
        s = jnp.where(mask[None, None], s, jnp.finfo(jnp.float32).min)
        p = jax.nn.softmax(s, axis=-1).astype(v.dtype)
        return jnp.einsum('bhqk,bkhd->bqhd', p, v)

    out = lax.map(one_block, (qn_blocks, qr_blocks, starts))
    return out.transpose(1, 0, 2, 3, 4).reshape(B, S, MLA_WIDTH)


def retention_group(q, k, v, g, positions):
    B, S, _ = q.shape
    H, L = N_RET_HEADS, RET_CHUNK
    C = S // L
    q = rope(q.reshape(B, S, H, RET_DK), positions)
    k = rope(k.reshape(B, S, H, RET_DK), positions) * (RET_DK ** -0.5)
    v = v.reshape(B, S, H, RET_DV)
    dt = q.dtype

    log_gamma = jnp.log(1.0 - 2.0 ** (-5.0 - jnp.arange(H, dtype=jnp.float32)))
    j = jnp.arange(L, dtype=jnp.float32)
    diff = j[:, None] - j[None, :]
    intra = jnp.where(diff[None] >= 0,
                      jnp.exp(jnp.maximum(diff, 0.0)[None] * log_gamma[:, None, None]),
                      0.0).astype(dt)
    k_to_end = jnp.exp((L - 1 - j)[:, None] * log_gamma[None, :]).astype(dt)
    q_from_start = jnp.exp((j + 1)[:, None] * log_gamma[None, :]).astype(dt)
    chunk_decay = jnp.exp(L * log_gamma).astype(dt)

    qc = q.reshape(B, C, L, H, RET_DK)
    kc = k.reshape(B, C, L, H, RET_DK)
    vc = v.reshape(B, C, L, H, RET_DV)

    scores = jnp.einsum('bclhd,bcmhd->bchlm', qc, kc) * intra[None, None]
    inner = jnp.einsum('bchlm,bcmhe->bclhe', scores, vc)

    chunk_kv = jnp.einsum('bclhd,bclhe->cbhde', kc * k_to_end[None, None, :, :, None], vc)

    def step(state, kv_c):
        return chunk_decay[None, :, None, None] * state + kv_c, state

    init = jnp.zeros((B, H, RET_DK, RET_DV), dtype=chunk_kv.dtype)
    _, prev_states = lax.scan(step, init, chunk_kv)
    cross = jnp.einsum('bclhd,cbhde->bclhe',
                       qc * q_from_start[None, None, :, :, None], prev_states)
    out = (inner + cross).reshape(B, S, H, RET_DV)

    o32 = out.astype(jnp.float32)
    mu = jnp.mean(o32, axis=-1, keepdims=True)
    var = jnp.mean(jnp.square(o32 - mu), axis=-1, keepdims=True)
    o = ((o32 - mu) * lax.rsqrt(var + EPS)).astype(dt).reshape(B, S, RET_WIDTH)
    return o * jax.nn.silu(g)


def memory_cross_attention(h, mem_n, w_xq, w_xkv, w_xo):
    B, S, _ = h.shape
    M = mem_n.shape[1]
    q = (h @ w_xq).reshape(B, S, N_XATTN_HEADS, XATTN_HEAD)
    kv = (mem_n @ w_xkv).reshape(B, M, 2, N_XATTN_HEADS, XATTN_HEAD)
    k, v = kv[:, :, 0], kv[:, :, 1]
    s = jnp.einsum('bqhd,bkhd->bhqk', q, k).astype(jnp.float32) / math.sqrt(XATTN_HEAD)
    p = jax.nn.softmax(s, axis=-1).astype(v.dtype)
    o = jnp.einsum('bhqk,bkhd->bqhd', p, v).reshape(B, S, N_XATTN_HEADS * XATTN_HEAD)
    return o @ w_xo


def conv_gated_mlp(h, w_ffn_in, conv_w, conv_b, w_ffn_out):
    gu = h @ w_ffn_in
    gate, up = gu[..., :D_FF], gu[..., D_FF:]
    gp = jnp.pad(gate, ((0, 0), (CONV_W - 1, 0), (0, 0)))
    S = gate.shape[1]
    conv = conv_b + sum(gp[:, i:i + S] * conv_w[i] for i in range(CONV_W))
    return (jax.nn.silu(conv) * up) @ w_ffn_out


def setup_inputs(seed: int = 0) -> dict:
    key = jax.random.key(seed)
    ks = jax.random.split(key, 24)

    def nrm(k, shape, fan_in):
        return jax.random.normal(k, shape, jnp.float32) * (fan_in ** -0.5)

    def gain(k, shape):
        return 1.0 + 0.01 * jax.random.normal(k, shape, jnp.float32)

    x = jax.random.normal(ks[0], (BATCH, SEQ, D_MODEL), jnp.float32)
    mem = jax.random.normal(ks[1], (BATCH, MEM_LEN, D_MODEL), jnp.float32)
    offset = jax.random.randint(ks[2], (BATCH, 1), 0, 1024, dtype=jnp.int32)
    positions = offset + jnp.arange(SEQ, dtype=jnp.int32)[None, :]
    return {
        "x": x,
        "mem": mem,
        "positions": positions,
        "g_mix": gain(ks[3], (DEPTH, D_MODEL)),
        "w_in": nrm(ks[4], (DEPTH, D_MODEL, IN_COLS), D_MODEL),
        "g_q_lat": gain(ks[5], (DEPTH, Q_LORA)),
        "w_uq": nrm(ks[6], (DEPTH, Q_LORA, N_MLA_HEADS * (QK_NOPE + QK_ROPE)), Q_LORA),
        "g_kv_lat": gain(ks[7], (DEPTH, KV_LORA)),
        "w_ukv": nrm(ks[8], (DEPTH, KV_LORA, N_MLA_HEADS * (QK_NOPE + V_HEAD)), KV_LORA),
        "w_out": nrm(ks[9], (DEPTH, D_MIX, D_MODEL), D_MIX),
        "g_xattn": gain(ks[10], (DEPTH, D_MODEL)),
        "g_mem": gain(ks[11], (DEPTH, D_MODEL)),
        "w_xq": nrm(ks[12], (DEPTH, D_MODEL, N_XATTN_HEADS * XATTN_HEAD), D_MODEL),
        "w_xkv": nrm(ks[13], (DEPTH, D_MODEL, 2 * N_XATTN_HEADS * XATTN_HEAD), D_MODEL),
        "w_xo": nrm(ks[14], (DEPTH, N_XATTN_HEADS * XATTN_HEAD, D_MODEL), N_XATTN_HEADS * XATTN_HEAD),
        "g_ffn": gain(ks[15], (DEPTH, D_MODEL)),
        "w_ffn_in": nrm(ks[16], (DEPTH, D_MODEL, 2 * D_FF), D_MODEL),
        "conv_w": nrm(ks[17], (DEPTH, CONV_W, D_FF), CONV_W),
        "conv_b": 0.01 * jax.random.normal(ks[18], (DEPTH, D_FF), jnp.float32),
        "w_ffn_out": nrm(ks[19], (DEPTH, D_FF, D_MODEL), D_FF),
        "g_final": gain(ks[20], (D_MODEL,)),
    }


def reference(x, mem, positions, g_mix, w_in, g_q_lat, w_uq, g_kv_lat, w_ukv, w_out,
              g_xattn, g_mem, w_xq, w_xkv, w_xo, g_ffn, w_ffn_in, conv_w, conv_b,
              w_ffn_out, g_final):
    offs = np.cumsum((0,) + IN_SPLITS)
    for l in range(DEPTH):
        h = rmsnorm(x, g_mix[l])
        proj = h @ w_in[l]
        c_q, c_kv, k_rope_raw, rq, rk, rv, rg = [proj[..., offs[i]:offs[i + 1]]
                                                 for i in range(len(IN_SPLITS))]
        y_mla = mla_group(c_q, c_kv, k_rope_raw, positions,
                          g_q_lat[l], w_uq[l], g_kv_lat[l], w_ukv[l])
        y_ret = retention_group(rq, rk, rv, rg, positions)
        x = x + jnp.concatenate([y_mla, y_ret], axis=-1) @ w_out[l]
        x = x + memory_cross_attention(rmsnorm(x, g_xattn[l]), rmsnorm(mem, g_mem[l]),
                                       w_xq[l], w_xkv[l], w_xo[l])
        x = x + conv_gated_mlp(rmsnorm(x, g_ffn[l]), w_ffn_in[l], conv_w[l], conv_b[l],
                               w_ffn_out[l])
    return rmsnorm(x, g_final)
```

```python
import functools
import math

import jax
import jax.numpy as jnp
from jax import lax
from jax.experimental import pallas as pl
from jax.experimental.pallas import tpu as pltpu

D_MODEL = 1024
EPS = 1e-6
ROPE_BASE = 10000.0
N_MLA_HEADS = 8
QK_NOPE = 64
QK_ROPE = 32
V_HEAD = 64
Q_LORA = 256
KV_LORA = 128
N_RET_HEADS = 4
RET_DK = 128
RET_DV = 128
RET_CHUNK = 128
MLA_WIDTH = N_MLA_HEADS * V_HEAD
RET_WIDTH = N_RET_HEADS * RET_DV
N_XATTN_HEADS = 4
XATTN_HEAD = D_MODEL // N_XATTN_HEADS
D_FF = 2816
CONV_W = 3

LANES = 128
SUBLANES = 8
VMEM_LIMIT = 56 * 1024 * 1024

HEAD_PAD = LANES
ROPE_HALF = QK_ROPE // 2
MLA_QK_WIDTH = N_MLA_HEADS * HEAD_PAD
IN_LAT = Q_LORA + KV_LORA + 2 * HEAD_PAD
IN_PERM = IN_LAT + 4 * RET_WIDTH

TM_TABLE = 1024
TM_IN = 512
TQ_MLA = 256
TS_RET = 512
TM_MIX = 512
TM_FFN = 256

_BF16 = jnp.bfloat16
_F32 = jnp.float32


def _dot(a, b):
    return jnp.dot(a, b, preferred_element_type=_F32)


def _dot_nt(a, b):
    return lax.dot_general(a, b, (((1,), (1,)), ((), ())), preferred_element_type=_F32)


def _dot_tn(a, b):
    return lax.dot_general(a, b, (((0,), (0,)), ((), ())), preferred_element_type=_F32)


def _rms(x, g):
    inv = lax.rsqrt(jnp.mean(x * x, axis=-1, keepdims=True) + EPS)
    return x * inv * g


def _const_spec(shape):
    nd = len(shape)
    return pl.BlockSpec(shape, lambda *_: (0,) * nd, pipeline_mode=pl.Buffered(1))


def _params(sem):
    return pltpu.CompilerParams(dimension_semantics=sem, vmem_limit_bytes=VMEM_LIMIT)


def _rope_table_kernel(pos_ref, inv_ref, sgn_ref, cos_ref, sin_ref):
    ang = pos_ref[...] * inv_ref[...]
    cos_ref[...] = jnp.cos(ang)
    sin_ref[...] = jnp.sin(ang) * sgn_ref[...]


def _rope_tables(positions):
    t = positions.size
    pos = positions.astype(_F32).reshape(t, 1)
    f_ret = 1.0 / (ROPE_BASE ** (jnp.arange(0, RET_DK, 2, dtype=_F32) / RET_DK))
    f_mla = 1.0 / (ROPE_BASE ** (jnp.arange(0, QK_ROPE, 2, dtype=_F32) / QK_ROPE))
    z = lambda n: jnp.zeros((n,), _F32)
    o = lambda n: jnp.ones((n,), _F32)
    inv = jnp.concatenate([f_ret, f_ret, z(QK_NOPE), f_mla, f_mla, z(HEAD_PAD - QK_NOPE - QK_ROPE)])
    sgn = jnp.concatenate([-o(RET_DK // 2), o(RET_DK // 2), z(QK_NOPE), -o(ROPE_HALF), o(ROPE_HALF),
                           z(HEAD_PAD - QK_NOPE - QK_ROPE)])
    w = 2 * LANES
    return pl.pallas_call(
        _rope_table_kernel,
        out_shape=(jax.ShapeDtypeStruct((t, w), _F32), jax.ShapeDtypeStruct((t, w), _F32)),
        grid=(t // TM_TABLE,),
        in_specs=[pl.BlockSpec((TM_TABLE, 1), lambda i: (i, 0)),
                  pl.BlockSpec((1, w), lambda i: (0, 0)),
                  pl.BlockSpec((1, w), lambda i: (0, 0))],
        out_specs=(pl.BlockSpec((TM_TABLE, w), lambda i: (i, 0)),
                   pl.BlockSpec((TM_TABLE, w), lambda i: (i, 0))),
        compiler_params=_params(("arbitrary",)),
        name="rope_tables",
    )(pos, inv.reshape(1, w), sgn.reshape(1, w))


def _in_proj_kernel(x_ref, gmix_ref, win_ref, gq_ref, wuq_ref, gkv_ref, wukv_ref, cos_ref, sin_ref,
                    q_ref, k_ref, v_ref, rq_ref, rk_ref, rv_ref, rg_ref):
    h = _rms(x_ref[0], gmix_ref[...]).astype(_BF16)
    cos_r, sin_r = cos_ref[:, :LANES], sin_ref[:, :LANES]
    cos_m, sin_m = cos_ref[:, LANES:], sin_ref[:, LANES:]

    lat = _dot(h, win_ref[:, :IN_LAT])
    c_q = lat[:, :Q_LORA]
    c_kv = lat[:, Q_LORA:Q_LORA + KV_LORA]
    kr = lat[:, Q_LORA + KV_LORA:Q_LORA + KV_LORA + HEAD_PAD]
    kr_sw = lat[:, Q_LORA + KV_LORA + HEAD_PAD:IN_LAT]
    k_rope = kr * cos_m + kr_sw * sin_m

    cqn = _rms(c_q, gq_ref[...]).astype(_BF16)
    q = _dot(cqn, wuq_ref[:, :MLA_QK_WIDTH])
    q_sw = _dot(cqn, wuq_ref[:, MLA_QK_WIDTH:])
    scale = 1.0 / math.sqrt(QK_NOPE + QK_ROPE)
    for hd in range(N_MLA_HEADS):
        sl = slice(hd * HEAD_PAD, (hd + 1) * HEAD_PAD)
        q_ref[0, :, sl] = ((q[:, sl] * cos_m + q_sw[:, sl] * sin_m) * scale).astype(_BF16)

    ckvn = _rms(c_kv, gkv_ref[...]).astype(_BF16)
    k_nope = _dot(ckvn, wukv_ref[:, :MLA_QK_WIDTH])
    for hd in range(N_MLA_HEADS):
        sl = slice(hd * HEAD_PAD, (hd + 1) * HEAD_PAD)
        k_ref[0, :, sl] = (k_nope[:, sl] + k_rope).astype(_BF16)
    v_ref[0] = _dot(ckvn, wukv_ref[:, MLA_QK_WIDTH:]).astype(_BF16)

    def ret_rope(col0, out_ref, mult):
        r = _dot(h, win_ref[:, col0:col0 + RET_WIDTH])
        for hd in range(N_RET_HEADS):
            sl = slice(hd * RET_DK, (hd + 1) * RET_DK)
            rh = r[:, sl]
            roped = rh * cos_r + pltpu.roll(rh, RET_DK // 2, axis=1) * sin_r
            if mult is not None:
                roped = roped * mult
            out_ref[0, :, sl] = roped.astype(_BF16)

    ret_rope(IN_LAT, rq_ref, None)
    ret_rope(IN_LAT + RET_WIDTH, rk_ref, RET_DK ** -0.5)
    rv_ref[0] = _dot(h, win_ref[:, IN_LAT + 2 * RET_WIDTH:IN_LAT + 3 * RET_WIDTH]).astype(_BF16)
    rg_ref[0] = _dot(h, win_ref[:, IN_LAT + 3 * RET_WIDTH:IN_PERM]).astype(_BF16)


def _permute_in_weights(w_in, w_uq, w_ukv):
    d = w_in.shape[0]
    o_kr = Q_LORA + KV_LORA
    x1 = w_in[:, o_kr:o_kr + ROPE_HALF]
    x2 = w_in[:, o_kr + ROPE_HALF:o_kr + QK_ROPE]
    zl = jnp.zeros((d, QK_NOPE), w_in.dtype)
    zr = jnp.zeros((d, HEAD_PAD - QK_NOPE - QK_ROPE), w_in.dtype)
    win_p = jnp.concatenate([w_in[:, :o_kr], zl, x1, x2, zr, zl, x2, x1, zr, w_in[:, o_kr + QK_ROPE:]],
                            axis=1).astype(_BF16)

    wq = w_uq.reshape(Q_LORA, N_MLA_HEADS, QK_NOPE + QK_ROPE)
    nope, r1, r2 = wq[..., :QK_NOPE], wq[..., QK_NOPE:QK_NOPE + ROPE_HALF], wq[..., QK_NOPE + ROPE_HALF:]
    z32 = jnp.zeros((Q_LORA, N_MLA_HEADS, HEAD_PAD - QK_NOPE - QK_ROPE), w_uq.dtype)
    z64 = jnp.zeros((Q_LORA, N_MLA_HEADS, QK_NOPE), w_uq.dtype)
    main = jnp.concatenate([nope, r1, r2, z32], axis=-1).reshape(Q_LORA, MLA_QK_WIDTH)
    swap = jnp.concatenate([z64, r2, r1, z32], axis=-1).reshape(Q_LORA, MLA_QK_WIDTH)
    wuq_p = jnp.concatenate([main, swap], axis=1).astype(_BF16)

    wkv = w_ukv.reshape(KV_LORA, N_MLA_HEADS, QK_NOPE + V_HEAD)
    zk = jnp.zeros((KV_LORA, N_MLA_HEADS, HEAD_PAD - QK_NOPE), w_ukv.dtype)
    wk = jnp.concatenate([wkv[..., :QK_NOPE], zk], axis=-1).reshape(KV_LORA, MLA_QK_WIDTH)
    wv = wkv[..., QK_NOPE:].reshape(KV_LORA, MLA_WIDTH)
    wukv_p = jnp.concatenate([wk, wv], axis=1).astype(_BF16)
    return win_p, wuq_p, wukv_p


def _in_proj(x, g_mix, win_p, g_q_lat, wuq_p, g_kv_lat, wukv_p, cos_t, sin_t):
    b, s, d = x.shape
    nt = s // TM_IN
    tok = lambda w: pl.BlockSpec((1, TM_IN, w), lambda bi, i: (bi, i, 0))
    tab = pl.BlockSpec((TM_IN, 2 * LANES), lambda bi, i: (bi * nt + i, 0))
    bf = lambda w: jax.ShapeDtypeStruct((b, s, w), _BF16)
    return pl.pallas_call(
        _in_proj_kernel,
        out_shape=(bf(MLA_QK_WIDTH), bf(MLA_QK_WIDTH), bf(MLA_WIDTH),
                   bf(RET_WIDTH), bf(RET_WIDTH), bf(RET_WIDTH), bf(RET_WIDTH)),
        grid=(b, nt),
        in_specs=[tok(d), _const_spec((1, d)), _const_spec(win_p.shape), _const_spec((1, Q_LORA)),
                  _const_spec(wuq_p.shape), _const_spec((1, KV_LORA)), _const_spec(wukv_p.shape), tab, tab],
        out_specs=(tok(MLA_QK_WIDTH), tok(MLA_QK_WIDTH), tok(MLA_WIDTH),
                   tok(RET_WIDTH), tok(RET_WIDTH), tok(RET_WIDTH), tok(RET_WIDTH)),
        compiler_params=_params(("arbitrary", "arbitrary")),
        name="in_proj",
    )(x, g_mix.reshape(1, d), win_p, g_q_lat.reshape(1, Q_LORA), wuq_p, g_kv_lat.reshape(1, KV_LORA),
      wukv_p, cos_t, sin_t)


_MASK_VALUE = -0.7 * float(jnp.finfo(jnp.float32).max)


def _mla_attn_kernel(q_ref, k_ref, v_ref, o_ref, m_ref, l_ref, acc_ref):
    tq = TQ_MLA
    qi = pl.program_id(2)
    m_ref[...] = jnp.full(m_ref.shape, _MASK_VALUE, _F32)
    l_ref[...] = jnp.zeros(l_ref.shape, _F32)
    acc_ref[...] = jnp.zeros(acc_ref.shape, _F32)
    lane = lax.broadcasted_iota(jnp.int32, (tq, LANES), 1)
    first_head = lane < V_HEAD

    def tile(kk, diagonal):
        ks = pl.multiple_of(kk * tq, tq)
        v = v_ref[0, pl.ds(ks, tq), :]
        v_halves = (jnp.where(first_head, v, jnp.zeros_like(v)), jnp.where(first_head, jnp.zeros_like(v), v))
        pv = None
        alphas = []
        for hd in range(2):
            sl = slice(hd * HEAD_PAD, (hd + 1) * HEAD_PAD)
            s = _dot_nt(q_ref[0, :, sl], k_ref[0, pl.ds(ks, tq), sl])
            if diagonal:
                row = lax.broadcasted_iota(jnp.int32, s.shape, 0)
                col = lax.broadcasted_iota(jnp.int32, s.shape, 1)
                s = jnp.where(col <= row, s, _MASK_VALUE)
            m_prev = m_ref[hd]
            m_next = jnp.maximum(m_prev, jnp.max(s, axis=-1, keepdims=True))
            alpha = jnp.exp(m_prev - m_next)
            p = jnp.exp(s - jnp.concatenate([m_next] * (tq // LANES), axis=1))
            l_ref[hd] = alpha * l_ref[hd] + jnp.sum(p, axis=-1, keepdims=True)
            m_ref[hd] = m_next
            alphas.append(alpha)
            contrib = _dot(p.astype(_BF16), v_halves[hd])
            pv = contrib if pv is None else pv + contrib
        acc_ref[...] = acc_ref[...] * jnp.where(first_head, alphas[0], alphas[1]) + pv

    def body(kk, carry):
        tile(kk, False)
        return carry

    lax.fori_loop(0, qi, body, 0)
    tile(qi, True)
    l_sel = jnp.where(first_head, l_ref[0], l_ref[1])
    o_ref[0] = (acc_ref[...] / l_sel).astype(o_ref.dtype)


def _mla_attn(q, k, v):
    b, s, _ = q.shape
    pairs = N_MLA_HEADS // 2
    return pl.pallas_call(
        _mla_attn_kernel,
        out_shape=jax.ShapeDtypeStruct((b, s, MLA_WIDTH), _BF16),
        grid=(b, pairs, s // TQ_MLA),
        in_specs=[pl.BlockSpec((1, TQ_MLA, 2 * HEAD_PAD), lambda bi, p, i: (bi, i, p)),
                  pl.BlockSpec((1, s, 2 * HEAD_PAD), lambda bi, p, i: (bi, 0, p)),
                  pl.BlockSpec((1, s, 2 * V_HEAD), lambda bi, p, i: (bi, 0, p))],
        out_specs=pl.BlockSpec((1, TQ_MLA, 2 * V_HEAD), lambda bi, p, i: (bi, i, p)),
        scratch_shapes=[pltpu.VMEM((2, TQ_MLA, LANES), _F32), pltpu.VMEM((2, TQ_MLA, LANES), _F32),
                        pltpu.VMEM((TQ_MLA, LANES), _F32)],
        compiler_params=_params(("arbitrary", "arbitrary", "arbitrary")),
        name="mla_attn",
    )(q, k, v)


def _retention_kernel(q_ref, k_ref, v_ref, g_ref, intra_ref, kend_ref, qstart_ref, decay_ref,
                      o_ref, state_ref):
    @pl.when(pl.program_id(1) == 0)
    def _():
        state_ref[...] = jnp.zeros(state_ref.shape, _F32)

    L = RET_CHUNK
    for c in range(TS_RET // L):
        rows = slice(c * L, (c + 1) * L)
        for hd in range(N_RET_HEADS):
            cols = slice(hd * RET_DK, (hd + 1) * RET_DK)
            q = q_ref[0, rows, cols]
            k = k_ref[0, rows, cols]
            v = v_ref[0, rows, cols]
            state = state_ref[hd]
            scores = _dot_nt(q, k) * intra_ref[hd]
            inner = _dot(scores.astype(_BF16), v)
            cross = _dot(q, state.astype(_BF16)) * qstart_ref[hd]
            out = inner + cross
            v_dec = (v.astype(_F32) * kend_ref[hd]).astype(_BF16)
            state_ref[hd] = decay_ref[hd] * state + _dot_tn(k, v_dec)
            mu = jnp.mean(out, axis=-1, keepdims=True)
            cen = out - mu
            var = jnp.mean(cen * cen, axis=-1, keepdims=True)
            normed = cen * lax.rsqrt(var + EPS)
            g = g_ref[0, rows, cols].astype(_F32)
            o_ref[0, rows, cols] = (normed * (g * jax.nn.sigmoid(g))).astype(o_ref.dtype)


def _retention_tables():
    h, L = N_RET_HEADS, RET_CHUNK
    log_gamma = jnp.log(1.0 - 2.0 ** (-5.0 - jnp.arange(h, dtype=_F32)))
    j = jnp.arange(L, dtype=_F32)
    diff = j[:, None] - j[None, :]
    intra = jnp.where(diff[None] >= 0,
                      jnp.exp(jnp.maximum(diff, 0.0)[None] * log_gamma[:, None, None]), 0.0)
    rowb = lambda t: jnp.broadcast_to(t.T[:, :, None], (h, L, LANES))
    k_to_end = jnp.exp((L - 1 - j)[:, None] * log_gamma[None, :])
    q_from_start = jnp.exp((j + 1)[:, None] * log_gamma[None, :])
    chunk_decay = jnp.broadcast_to(jnp.exp(L * log_gamma)[:, None, None], (h, RET_DK, RET_DV))
    return intra, rowb(k_to_end), rowb(q_from_start), chunk_decay


def _retention(rq, rk, rv, rg):
    b, s, w = rq.shape
    tok = pl.BlockSpec((1, TS_RET, w), lambda bi, i: (bi, i, 0))
    tables = _retention_tables()
    return pl.pallas_call(
        _retention_kernel,
        out_shape=jax.ShapeDtypeStruct((b, s, w), _BF16),
        grid=(b, s // TS_RET),
        in_specs=[tok, tok, tok, tok] + [_const_spec(t.shape) for t in tables],
        out_specs=tok,
        scratch_shapes=[pltpu.VMEM((N_RET_HEADS, RET_DK, RET_DV), _F32)],
        compiler_params=_params(("arbitrary", "arbitrary")),
        name="retention",
    )(rq, rk, rv, rg, *tables)


def _mem_kv_kernel(mem_ref, g_ref, w_ref, kv_ref):
    mem_n = _rms(mem_ref[0], g_ref[...]).astype(_BF16)
    kv_ref[0] = _dot(mem_n, w_ref[...]).astype(_BF16)


def _mem_kv(mem, g_mem, w_xkv):
    b, m, d = mem.shape
    n = w_xkv.shape[1]
    return pl.pallas_call(
        _mem_kv_kernel,
        out_shape=jax.ShapeDtypeStruct((b, m, n), _BF16),
        grid=(b,),
        in_specs=[pl.BlockSpec((1, m, d), lambda bi: (bi, 0, 0)), _const_spec((1, d)), _const_spec((d, n))],
        out_specs=pl.BlockSpec((1, m, n), lambda bi: (bi, 0, 0)),
        compiler_params=_params(("arbitrary",)),
        name="mem_kv",
    )(mem, g_mem.reshape(1, d), w_xkv)


def _mix_xattn_kernel(x_ref, ymla_ref, yret_ref, wout_ref, gx_ref, wxq_ref, kv_ref, wxo_ref, o_ref):
    x1 = x_ref[0] + _dot(ymla_ref[0], wout_ref[:MLA_WIDTH, :]) + _dot(yret_ref[0], wout_ref[MLA_WIDTH:, :])
    h = _rms(x1, gx_ref[...]).astype(_BF16)
    q = (_dot(h, wxq_ref[...]) * (1.0 / math.sqrt(XATTN_HEAD))).astype(_BF16)
    heads = []
    for hd in range(N_XATTN_HEADS):
        cols = slice(hd * XATTN_HEAD, (hd + 1) * XATTN_HEAD)
        vcols = slice(D_MODEL + hd * XATTN_HEAD, D_MODEL + (hd + 1) * XATTN_HEAD)
        s = _dot_nt(q[:, cols], kv_ref[0, :, cols])
        p = jnp.exp(s - jnp.max(s, axis=-1, keepdims=True))
        o = _dot(p.astype(_BF16), kv_ref[0, :, vcols]) / jnp.sum(p, axis=-1, keepdims=True)
        heads.append(o.astype(_BF16))
    o_ref[0] = x1 + _dot(jnp.concatenate(heads, axis=1), wxo_ref[...])


def _mix_xattn(x, y_mla, y_ret, w_out, g_xattn, w_xq, kv_mem, w_xo):
    b, s, d = x.shape
    m, n = kv_mem.shape[1:]
    tok = lambda w: pl.BlockSpec((1, TM_MIX, w), lambda bi, i: (bi, i, 0))
    return pl.pallas_call(
        _mix_xattn_kernel,
        out_shape=jax.ShapeDtypeStruct((b, s, d), _F32),
        grid=(b, s // TM_MIX),
        in_specs=[tok(d), tok(MLA_WIDTH), tok(RET_WIDTH), _const_spec(w_out.shape), _const_spec((1, d)),
                  _const_spec(w_xq.shape), pl.BlockSpec((1, m, n), lambda bi, i: (bi, 0, 0)),
                  _const_spec(w_xo.shape)],
        out_specs=tok(d),
        compiler_params=_params(("arbitrary", "arbitrary")),
        name="mix_xattn",
    )(x, y_mla, y_ret, w_out, g_xattn.reshape(1, d), w_xq, kv_mem, w_xo)


def _conv_ffn_kernel(x_ref, g_ref, win_ref, cw_ref, cb_ref, wout_ref, gfin_ref, o_ref, gate_ref):
    tm = TM_FFN
    halo = SUBLANES

    @pl.when(pl.program_id(1) == 0)
    def _():
        gate_ref[:halo, :] = jnp.zeros((halo, D_FF), _F32)

    x = x_ref[0]
    h = _rms(x, g_ref[...]).astype(_BF16)
    gate = _dot(h, win_ref[:, :D_FF])
    up = _dot(h, win_ref[:, D_FF:])
    gate_ref[halo:, :] = gate
    conv = cb_ref[...] + gate * cw_ref[CONV_W - 1:CONV_W, :]
    for tap in range(CONV_W - 1):
        back = CONV_W - 1 - tap
        conv = conv + gate_ref[halo - back:halo - back + tm, :] * cw_ref[tap:tap + 1, :]
    act = (conv * jax.nn.sigmoid(conv) * up).astype(_BF16)
    x3 = x + _dot(act, wout_ref[...])
    o_ref[0] = _rms(x3, gfin_ref[...])
    gate_ref[:halo, :] = gate_ref[tm:, :]


def _conv_ffn(x, g_ffn, w_ffn_in, conv_w, conv_b, w_ffn_out, g_final):
    b, s, d = x.shape
    tok = pl.BlockSpec((1, TM_FFN, d), lambda bi, i: (bi, i, 0))
    return pl.pallas_call(
        _conv_ffn_kernel,
        out_shape=jax.ShapeDtypeStruct((b, s, d), _F32),
        grid=(b, s // TM_FFN),
        in_specs=[tok, _const_spec((1, d)), _const_spec(w_ffn_in.shape), _const_spec(conv_w.shape),
                  _const_spec((1, D_FF)), _const_spec(w_ffn_out.shape), _const_spec((1, d))],
        out_specs=tok,
        scratch_shapes=[pltpu.VMEM((TM_FFN + SUBLANES, D_FF), _F32)],
        compiler_params=_params(("arbitrary", "arbitrary")),
        name="conv_ffn",
    )(x, g_ffn.reshape(1, d), w_ffn_in, conv_w, conv_b.reshape(1, D_FF), w_ffn_out, g_final.reshape(1, d))


def kernel(x, mem, positions, g_mix, w_in, g_q_lat, w_uq, g_kv_lat, w_ukv, w_out, g_xattn, g_mem, w_xq,
           w_xkv, w_xo, g_ffn, w_ffn_in, conv_w, conv_b, w_ffn_out, g_final):
    assert w_in.shape[0] == 1, "one layer supported"
    l = 0
    cos_t, sin_t = _rope_tables(positions)
    win_p, wuq_p, wukv_p = _permute_in_weights(w_in[l], w_uq[l], w_ukv[l])
    q, k, v, rq, rk, rv, rg = _in_proj(x, g_mix[l], win_p, g_q_lat[l], wuq_p, g_kv_lat[l], wukv_p,
                                       cos_t, sin_t)
    y_mla = _mla_attn(q, k, v)
    y_ret = _retention(rq, rk, rv, rg)
    kv_mem = _mem_kv(mem, g_mem[l], w_xkv[l].astype(_BF16))
    x = _mix_xattn(x, y_mla, y_ret, w_out[l].astype(_BF16), g_xattn[l], w_xq[l].astype(_BF16), kv_mem,
                   w_xo[l].astype(_BF16))
    return _conv_ffn(x, g_ffn[l], w_ffn_in[l].astype(_BF16), conv_w[l], conv_b[l], w_ffn_out[l].astype(_BF16),
                     g_final)
```

```python
import functools
import math

import jax
import jax.numpy as jnp
from jax import lax
from jax.experimental import pallas as pl
from jax.experimental.pallas import tpu as pltpu

D_MODEL = 1024
EPS = 1e-6
ROPE_BASE = 10000.0
N_MLA_HEADS = 8
QK_NOPE = 64
QK_ROPE = 32
V_HEAD = 64
Q_LORA = 256
KV_LORA = 128
N_RET_HEADS = 4
RET_DK = 128
RET_DV = 128
RET_CHUNK = 128
MLA_WIDTH = N_MLA_HEADS * V_HEAD
RET_WIDTH = N_RET_HEADS * RET_DV
N_XATTN_HEADS = 4
XATTN_HEAD = D_MODEL // N_XATTN_HEADS
D_FF = 2816
CONV_W = 3

LANES = 128
SUBLANES = 8
VMEM_LIMIT = 56 * 1024 * 1024

HEAD_PAD = LANES
ROPE_HALF = QK_ROPE // 2
MLA_QK_WIDTH = N_MLA_HEADS * HEAD_PAD
IN_LAT = Q_LORA + KV_LORA + 2 * HEAD_PAD
IN_PERM = IN_LAT + 4 * RET_WIDTH

TM_TABLE = 1024
TM_IN = 512
TQ_MLA = 256
TS_RET = 512
TM_MIX = 512
TM_FFN = 256

_BF16 = jnp.bfloat16
_F32 = jnp.float32


def _dot(a, b):
    return jnp.dot(a, b, preferred_element_type=_F32)


def _dot_nt(a, b):
    return lax.dot_general(a, b, (((1,), (1,)), ((), ())), preferred_element_type=_F32)


def _dot_tn(a, b):
    return lax.dot_general(a, b, (((0,), (0,)), ((), ())), preferred_element_type=_F32)


def _rms(x, g):
    inv = lax.rsqrt(jnp.mean(x * x, axis=-1, keepdims=True) + EPS)
    return x * inv * g


def _const_spec(shape):
    nd = len(shape)
    return pl.BlockSpec(shape, lambda *_: (0,) * nd, pipeline_mode=pl.Buffered(1))


def _params(sem):
    return pltpu.CompilerParams(dimension_semantics=sem, vmem_limit_bytes=VMEM_LIMIT)


def _rope_table_kernel(pos_ref, inv_ref, sgn_ref, cos_ref, sin_ref):
    ang = pos_ref[...] * inv_ref[...]
    cos_ref[...] = jnp.cos(ang)
    sin_ref[...] = jnp.sin(ang) * sgn_ref[...]


def _rope_tables(positions):
    t = positions.size
    pos = positions.astype(_F32).reshape(t, 1)
    f_ret = 1.0 / (ROPE_BASE ** (jnp.arange(0, RET_DK, 2, dtype=_F32) / RET_DK))
    f_mla = 1.0 / (ROPE_BASE ** (jnp.arange(0, QK_ROPE, 2, dtype=_F32) / QK_ROPE))
    z = lambda n: jnp.zeros((n,), _F32)
    o = lambda n: jnp.ones((n,), _F32)
    inv = jnp.concatenate([f_ret, f_ret, z(QK_NOPE), f_mla, f_mla, z(HEAD_PAD - QK_NOPE - QK_ROPE)])
    sgn = jnp.concatenate([-o(RET_DK // 2), o(RET_DK // 2), z(QK_NOPE), -o(ROPE_HALF), o(ROPE_HALF),
                           z(HEAD_PAD - QK_NOPE - QK_ROPE)])
    w = 2 * LANES
    return pl.pallas_call(
        _rope_table_kernel,
        out_shape=(jax.ShapeDtypeStruct((t, w), _F32), jax.ShapeDtypeStruct((t, w), _F32)),
        grid=(t // TM_TABLE,),
        in_specs=[pl.BlockSpec((TM_TABLE, 1), lambda i: (i, 0)),
                  pl.BlockSpec((1, w), lambda i: (0, 0)),
                  pl.BlockSpec((1, w), lambda i: (0, 0))],
        out_specs=(pl.BlockSpec((TM_TABLE, w), lambda i: (i, 0)),
                   pl.BlockSpec((TM_TABLE, w), lambda i: (i, 0))),
        compiler_params=_params(("arbitrary",)),
        name="rope_tables",
    )(pos, inv.reshape(1, w), sgn.reshape(1, w))


def _in_proj_kernel(x_ref, gmix_ref, win_ref, gq_ref, wuq_ref, gkv_ref, wuk_ref, wuvt_ref, cos_ref, sin_ref,
                    q_ref, k_ref, vt_ref, rq_ref, rk_ref, rv_ref, rg_ref):
    h = _rms(x_ref[0], gmix_ref[...]).astype(_BF16)
    cos_r, sin_r = cos_ref[:, :LANES], sin_ref[:, :LANES]
    cos_m, sin_m = cos_ref[:, LANES:], sin_ref[:, LANES:]

    lat = _dot(h, win_ref[:, :IN_LAT])
    c_q = lat[:, :Q_LORA]
    c_kv = lat[:, Q_LORA:Q_LORA + KV_LORA]
    kr = lat[:, Q_LORA + KV_LORA:Q_LORA + KV_LORA + HEAD_PAD]
    kr_sw = lat[:, Q_LORA + KV_LORA + HEAD_PAD:IN_LAT]
    k_rope = kr * cos_m + kr_sw * sin_m

    cqn = _rms(c_q, gq_ref[...]).astype(_BF16)
    q = _dot(cqn, wuq_ref[:, :MLA_QK_WIDTH])
    q_sw = _dot(cqn, wuq_ref[:, MLA_QK_WIDTH:])
    scale = math.log2(math.e) / math.sqrt(QK_NOPE + QK_ROPE)
    for hd in range(N_MLA_HEADS):
        sl = slice(hd * HEAD_PAD, (hd + 1) * HEAD_PAD)
        q_ref[0, :, sl] = ((q[:, sl] * cos_m + q_sw[:, sl] * sin_m) * scale).astype(_BF16)

    ckvn = _rms(c_kv, gkv_ref[...]).astype(_BF16)
    k_nope = _dot(ckvn, wuk_ref[...])
    for hd in range(N_MLA_HEADS):
        sl = slice(hd * HEAD_PAD, (hd + 1) * HEAD_PAD)
        k_ref[0, :, sl] = (k_nope[:, sl] + k_rope).astype(_BF16)
    v_t = _dot_nt(wuvt_ref[...], ckvn).astype(_BF16)
    for j in range(TM_IN // TQ_MLA):
        vt_ref[0, j] = v_t[:, j * TQ_MLA:(j + 1) * TQ_MLA]

    def ret_rope(col0, out_ref, mult):
        r = _dot(h, win_ref[:, col0:col0 + RET_WIDTH])
        for hd in range(N_RET_HEADS):
            sl = slice(hd * RET_DK, (hd + 1) * RET_DK)
            rh = r[:, sl]
            roped = rh * cos_r + pltpu.roll(rh, RET_DK // 2, axis=1) * sin_r
            if mult is not None:
                roped = roped * mult
            out_ref[0, :, sl] = roped.astype(_BF16)

    ret_rope(IN_LAT, rq_ref, None)
    ret_rope(IN_LAT + RET_WIDTH, rk_ref, RET_DK ** -0.5)
    rv_ref[0] = _dot(h, win_ref[:, IN_LAT + 2 * RET_WIDTH:IN_LAT + 3 * RET_WIDTH]).astype(_BF16)
    rg_ref[0] = _dot(h, win_ref[:, IN_LAT + 3 * RET_WIDTH:IN_PERM]).astype(_BF16)


def _permute_in_weights(w_in, w_uq, w_ukv):
    d = w_in.shape[0]
    o_kr = Q_LORA + KV_LORA
    x1 = w_in[:, o_kr:o_kr + ROPE_HALF]
    x2 = w_in[:, o_kr + ROPE_HALF:o_kr + QK_ROPE]
    zl = jnp.zeros((d, QK_NOPE), w_in.dtype)
    zr = jnp.zeros((d, HEAD_PAD - QK_NOPE - QK_ROPE), w_in.dtype)
    win_p = jnp.concatenate([w_in[:, :o_kr], zl, x1, x2, zr, zl, x2, x1, zr, w_in[:, o_kr + QK_ROPE:]],
                            axis=1).astype(_BF16)

    wq = w_uq.reshape(Q_LORA, N_MLA_HEADS, QK_NOPE + QK_ROPE)
    nope, r1, r2 = wq[..., :QK_NOPE], wq[..., QK_NOPE:QK_NOPE + ROPE_HALF], wq[..., QK_NOPE + ROPE_HALF:]
    z32 = jnp.zeros((Q_LORA, N_MLA_HEADS, HEAD_PAD - QK_NOPE - QK_ROPE), w_uq.dtype)
    z64 = jnp.zeros((Q_LORA, N_MLA_HEADS, QK_NOPE), w_uq.dtype)
    main = jnp.concatenate([nope, r1, r2, z32], axis=-1).reshape(Q_LORA, MLA_QK_WIDTH)
    swap = jnp.concatenate([z64, r2, r1, z32], axis=-1).reshape(Q_LORA, MLA_QK_WIDTH)
    wuq_p = jnp.concatenate([main, swap], axis=1).astype(_BF16)

    wkv = w_ukv.reshape(KV_LORA, N_MLA_HEADS, QK_NOPE + V_HEAD)
    zk = jnp.zeros((KV_LORA, N_MLA_HEADS, HEAD_PAD - QK_NOPE), w_ukv.dtype)
    wk = jnp.concatenate([wkv[..., :QK_NOPE], zk], axis=-1).reshape(KV_LORA, MLA_QK_WIDTH)
    wv_t = wkv[..., QK_NOPE:].reshape(KV_LORA, MLA_WIDTH).T
    return win_p, wuq_p, wk.astype(_BF16), wv_t.astype(_BF16)


def _in_proj(x, g_mix, win_p, g_q_lat, wuq_p, g_kv_lat, wuk_p, wuvt_p, cos_t, sin_t):
    b, s, d = x.shape
    nt = s // TM_IN
    per = TM_IN // TQ_MLA
    tok = lambda w: pl.BlockSpec((1, TM_IN, w), lambda bi, i: (bi, i, 0))
    tab = pl.BlockSpec((TM_IN, 2 * LANES), lambda bi, i: (bi * nt + i, 0))
    bf = lambda w: jax.ShapeDtypeStruct((b, s, w), _BF16)
    return pl.pallas_call(
        _in_proj_kernel,
        out_shape=(bf(MLA_QK_WIDTH), bf(MLA_QK_WIDTH),
                   jax.ShapeDtypeStruct((b, s // TQ_MLA, MLA_WIDTH, TQ_MLA), _BF16),
                   bf(RET_WIDTH), bf(RET_WIDTH), bf(RET_WIDTH), bf(RET_WIDTH)),
        grid=(b, nt),
        in_specs=[tok(d), _const_spec((1, d)), _const_spec(win_p.shape), _const_spec((1, Q_LORA)),
                  _const_spec(wuq_p.shape), _const_spec((1, KV_LORA)), _const_spec(wuk_p.shape),
                  _const_spec(wuvt_p.shape), tab, tab],
        out_specs=(tok(MLA_QK_WIDTH), tok(MLA_QK_WIDTH),
                   pl.BlockSpec((1, per, MLA_WIDTH, TQ_MLA), lambda bi, i: (bi, i, 0, 0)),
                   tok(RET_WIDTH), tok(RET_WIDTH), tok(RET_WIDTH), tok(RET_WIDTH)),
        compiler_params=_params(("arbitrary", "arbitrary")),
        name="in_proj",
    )(x, g_mix.reshape(1, d), win_p, g_q_lat.reshape(1, Q_LORA), wuq_p, g_kv_lat.reshape(1, KV_LORA),
      wuk_p, wuvt_p, cos_t, sin_t)


_MASK_VALUE = -0.7 * float(jnp.finfo(jnp.float32).max)
_MLA_AHEAD = 2


def _mla_attn_kernel(q_ref, k_ref, vt_ref, o_ref, s_ref, m_ref, l_ref, acc_ref):
    tq = TQ_MLA
    qi = pl.program_id(1)
    m_ref[...] = jnp.full(m_ref.shape, _MASK_VALUE, _F32)
    l_ref[...] = jnp.zeros(l_ref.shape, _F32)
    acc_ref[...] = jnp.zeros(acc_ref.shape, _F32)

    def scores(kk, hd):
        sl = slice(hd * HEAD_PAD, (hd + 1) * HEAD_PAD)
        ks = pl.multiple_of(kk * tq, tq)
        s_ref[hd] = _dot_nt(k_ref[0, pl.ds(ks, tq), sl], q_ref[0, :, sl])

    def update(kk, hd, keep):
        row = slice(hd, hd + 1)

        def st():
            s = s_ref[hd]
            return s if keep is None else jnp.where(keep, s, _MASK_VALUE)

        m_prev = m_ref[row, :]
        m_next = jnp.maximum(m_prev, jnp.max(st(), axis=0, keepdims=True))
        alpha = jnp.exp2(m_prev - m_next)
        p = jnp.exp2(st() - m_next)
        l_ref[row, :] = alpha * l_ref[row, :] + jnp.sum(p, axis=0, keepdims=True)
        m_ref[row, :] = m_next
        vt = vt_ref[0, kk, hd * V_HEAD:(hd + 1) * V_HEAD, :]
        acc_ref[hd] = acc_ref[hd] * alpha + _dot(vt, p.astype(_BF16))

    def tile(kk, diagonal):
        keep = None
        if diagonal:
            key = lax.broadcasted_iota(jnp.int32, (tq, tq), 0)
            qry = lax.broadcasted_iota(jnp.int32, (tq, tq), 1)
            keep = key <= qry
        for hd in range(N_MLA_HEADS):
            nxt = hd + _MLA_AHEAD
            if nxt < N_MLA_HEADS:
                scores(kk, nxt)
            elif not diagonal:
                scores(kk + 1, nxt - N_MLA_HEADS)
            update(kk, hd, keep)

    def body(kk, carry):
        tile(kk, False)
        return carry

    for hd in range(_MLA_AHEAD):
        scores(0, hd)
    lax.fori_loop(0, qi, body, 0)
    tile(qi, True)
    out_t = jnp.concatenate([acc_ref[hd] / l_ref[hd:hd + 1, :] for hd in range(N_MLA_HEADS)], axis=0)
    o_ref[0] = out_t.T.astype(o_ref.dtype)


def _mla_attn(q, k, vt):
    b, s, w = q.shape
    nk = s // TQ_MLA
    return pl.pallas_call(
        _mla_attn_kernel,
        out_shape=jax.ShapeDtypeStruct((b, s, MLA_WIDTH), _BF16),
        grid=(b, nk),
        in_specs=[pl.BlockSpec((1, TQ_MLA, w), lambda bi, i: (bi, i, 0)),
                  pl.BlockSpec((1, s, w), lambda bi, i: (bi, 0, 0)),
                  pl.BlockSpec((1, nk, MLA_WIDTH, TQ_MLA), lambda bi, i: (bi, 0, 0, 0))],
        out_specs=pl.BlockSpec((1, TQ_MLA, MLA_WIDTH), lambda bi, i: (bi, i, 0)),
        scratch_shapes=[pltpu.VMEM((N_MLA_HEADS, TQ_MLA, TQ_MLA), _F32),
                        pltpu.VMEM((N_MLA_HEADS, TQ_MLA), _F32), pltpu.VMEM((N_MLA_HEADS, TQ_MLA), _F32),
                        pltpu.VMEM((N_MLA_HEADS, V_HEAD, TQ_MLA), _F32)],
        compiler_params=_params(("arbitrary", "arbitrary")),
        name="mla_attn",
    )(q, k, vt)


def _retention_kernel(q_ref, k_ref, v_ref, g_ref, intra_ref, kend_ref, qstart_ref, decay_ref,
                      o_ref, state_ref):
    @pl.when(pl.program_id(1) == 0)
    def _():
        state_ref[...] = jnp.zeros(state_ref.shape, _F32)

    L = RET_CHUNK
    for c in range(TS_RET // L):
        rows = slice(c * L, (c + 1) * L)
        for hd in range(N_RET_HEADS):
            cols = slice(hd * RET_DK, (hd + 1) * RET_DK)
            q = q_ref[0, rows, cols]
            k = k_ref[0, rows, cols]
            v = v_ref[0, rows, cols]
            state = state_ref[hd]
            scores = _dot_nt(q, k) * intra_ref[hd]
            inner = _dot(scores.astype(_BF16), v)
            cross = _dot(q, state.astype(_BF16)) * qstart_ref[hd]
            out = inner + cross
            v_dec = (v.astype(_F32) * kend_ref[hd]).astype(_BF16)
            state_ref[hd] = decay_ref[hd] * state + _dot_tn(k, v_dec)
            mu = jnp.mean(out, axis=-1, keepdims=True)
            cen = out - mu
            var = jnp.mean(cen * cen, axis=-1, keepdims=True)
            normed = cen * lax.rsqrt(var + EPS)
            g = g_ref[0, rows, cols].astype(_F32)
            o_ref[0, rows, cols] = (normed * (g * jax.nn.sigmoid(g))).astype(o_ref.dtype)


def _retention_tables():
    h, L = N_RET_HEADS, RET_CHUNK
    log_gamma = jnp.log(1.0 - 2.0 ** (-5.0 - jnp.arange(h, dtype=_F32)))
    j = jnp.arange(L, dtype=_F32)
    diff = j[:, None] - j[None, :]
    intra = jnp.where(diff[None] >= 0,
                      jnp.exp(jnp.maximum(diff, 0.0)[None] * log_gamma[:, None, None]), 0.0)
    rowb = lambda t: jnp.broadcast_to(t.T[:, :, None], (h, L, LANES))
    k_to_end = jnp.exp((L - 1 - j)[:, None] * log_gamma[None, :])
    q_from_start = jnp.exp((j + 1)[:, None] * log_gamma[None, :])
    chunk_decay = jnp.broadcast_to(jnp.exp(L * log_gamma)[:, None, None], (h, RET_DK, RET_DV))
    return intra, rowb(k_to_end), rowb(q_from_start), chunk_decay


def _retention(rq, rk, rv, rg):
    b, s, w = rq.shape
    tok = pl.BlockSpec((1, TS_RET, w), lambda bi, i: (bi, i, 0))
    tables = _retention_tables()
    return pl.pallas_call(
        _retention_kernel,
        out_shape=jax.ShapeDtypeStruct((b, s, w), _BF16),
        grid=(b, s // TS_RET),
        in_specs=[tok, tok, tok, tok] + [_const_spec(t.shape) for t in tables],
        out_specs=tok,
        scratch_shapes=[pltpu.VMEM((N_RET_HEADS, RET_DK, RET_DV), _F32)],
        compiler_params=_params(("arbitrary", "arbitrary")),
        name="retention",
    )(rq, rk, rv, rg, *tables)


def _mem_kv_kernel(mem_ref, g_ref, w_ref, kv_ref):
    mem_n = _rms(mem_ref[0], g_ref[...]).astype(_BF16)
    kv_ref[0] = _dot(mem_n, w_ref[...]).astype(_BF16)


def _mem_kv(mem, g_mem, w_xkv):
    b, m, d = mem.shape
    n = w_xkv.shape[1]
    return pl.pallas_call(
        _mem_kv_kernel,
        out_shape=jax.ShapeDtypeStruct((b, m, n), _BF16),
        grid=(b,),
        in_specs=[pl.BlockSpec((1, m, d), lambda bi: (bi, 0, 0)), _const_spec((1, d)), _const_spec((d, n))],
        out_specs=pl.BlockSpec((1, m, n), lambda bi: (bi, 0, 0)),
        compiler_params=_params(("arbitrary",)),
        name="mem_kv",
    )(mem, g_mem.reshape(1, d), w_xkv)


def _mix_xattn_kernel(x_ref, ymla_ref, yret_ref, wout_ref, gx_ref, wxq_ref, kv_ref, wxo_ref, o_ref):
    x1 = x_ref[0] + _dot(ymla_ref[0], wout_ref[:MLA_WIDTH, :]) + _dot(yret_ref[0], wout_ref[MLA_WIDTH:, :])
    h = _rms(x1, gx_ref[...]).astype(_BF16)
    q = (_dot(h, wxq_ref[...]) * (1.0 / math.sqrt(XATTN_HEAD))).astype(_BF16)
    heads = []
    for hd in range(N_XATTN_HEADS):
        cols = slice(hd * XATTN_HEAD, (hd + 1) * XATTN_HEAD)
        vcols = slice(D_MODEL + hd * XATTN_HEAD, D_MODEL + (hd + 1) * XATTN_HEAD)
        s = _dot_nt(q[:, cols], kv_ref[0, :, cols])
        p = jnp.exp(s - jnp.max(s, axis=-1, keepdims=True))
        o = _dot(p.astype(_BF16), kv_ref[0, :, vcols]) / jnp.sum(p, axis=-1, keepdims=True)
        heads.append(o.astype(_BF16))
    o_ref[0] = x1 + _dot(jnp.concatenate(heads, axis=1), wxo_ref[...])


def _mix_xattn(x, y_mla, y_ret, w_out, g_xattn, w_xq, kv_mem, w_xo):
    b, s, d = x.shape
    m, n = kv_mem.shape[1:]
    tok = lambda w: pl.BlockSpec((1, TM_MIX, w), lambda bi, i: (bi, i, 0))
    return pl.pallas_call(
        _mix_xattn_kernel,
        out_shape=jax.ShapeDtypeStruct((b, s, d), _F32),
        grid=(b, s // TM_MIX),
        in_specs=[tok(d), tok(MLA_WIDTH), tok(RET_WIDTH), _const_spec(w_out.shape), _const_spec((1, d)),
                  _const_spec(w_xq.shape), pl.BlockSpec((1, m, n), lambda bi, i: (bi, 0, 0)),
                  _const_spec(w_xo.shape)],
        out_specs=tok(d),
        compiler_params=_params(("arbitrary", "arbitrary")),
        name="mix_xattn",
    )(x, y_mla, y_ret, w_out, g_xattn.reshape(1, d), w_xq, kv_mem, w_xo)


def _conv_ffn_kernel(x_ref, g_ref, win_ref, cw_ref, cb_ref, wout_ref, gfin_ref, o_ref, gate_ref):
    tm = TM_FFN
    halo = SUBLANES

    @pl.when(pl.program_id(1) == 0)
    def _():
        gate_ref[:halo, :] = jnp.zeros((halo, D_FF), _F32)

    x = x_ref[0]
    h = _rms(x, g_ref[...]).astype(_BF16)
    gate = _dot(h, win_ref[:, :D_FF])
    up = _dot(h, win_ref[:, D_FF:])
    gate_ref[halo:, :] = gate
    conv = cb_ref[...] + gate * cw_ref[CONV_W - 1:CONV_W, :]
    for tap in range(CONV_W - 1):
        back = CONV_W - 1 - tap
        conv = conv + gate_ref[halo - back:halo - back + tm, :] * cw_ref[tap:tap + 1, :]
    act = (conv * jax.nn.sigmoid(conv) * up).astype(_BF16)
    x3 = x + _dot(act, wout_ref[...])
    o_ref[0] = _rms(x3, gfin_ref[...])
    gate_ref[:halo, :] = gate_ref[tm:, :]


def _conv_ffn(x, g_ffn, w_ffn_in, conv_w, conv_b, w_ffn_out, g_final):
    b, s, d = x.shape
    tok = pl.BlockSpec((1, TM_FFN, d), lambda bi, i: (bi, i, 0))
    return pl.pallas_call(
        _conv_ffn_kernel,
        out_shape=jax.ShapeDtypeStruct((b, s, d), _F32),
        grid=(b, s // TM_FFN),
        in_specs=[tok, _const_spec((1, d)), _const_spec(w_ffn_in.shape), _const_spec(conv_w.shape),
                  _const_spec((1, D_FF)), _const_spec(w_ffn_out.shape), _const_spec((1, d))],
        out_specs=tok,
        scratch_shapes=[pltpu.VMEM((TM_FFN + SUBLANES, D_FF), _F32)],
        compiler_params=_params(("arbitrary", "arbitrary")),
        name="conv_ffn",
    )(x, g_ffn.reshape(1, d), w_ffn_in, conv_w, conv_b.reshape(1, D_FF), w_ffn_out, g_final.reshape(1, d))


def kernel(x, mem, positions, g_mix, w_in, g_q_lat, w_uq, g_kv_lat, w_ukv, w_out, g_xattn, g_mem, w_xq,
           w_xkv, w_xo, g_ffn, w_ffn_in, conv_w, conv_b, w_ffn_out, g_final):
    assert w_in.shape[0] == 1, "one layer supported"
    l = 0
    cos_t, sin_t = _rope_tables(positions)
    win_p, wuq_p, wuk_p, wuvt_p = _permute_in_weights(w_in[l], w_uq[l], w_ukv[l])
    q, k, vt, rq, rk, rv, rg = _in_proj(x, g_mix[l], win_p, g_q_lat[l], wuq_p, g_kv_lat[l], wuk_p, wuvt_p,
                                        cos_t, sin_t)
    y_mla = _mla_attn(q, k, vt)
    y_ret = _retention(rq, rk, rv, rg)
    kv_mem = _mem_kv(mem, g_mem[l], w_xkv[l].astype(_BF16))
    x = _mix_xattn(x, y_mla, y_ret, w_out[l].astype(_BF16), g_xattn[l], w_xq[l].astype(_BF16), kv_mem,
                   w_xo[l].astype(_BF16))
    return _conv_ffn(x, g_ffn[l], w_ffn_in[l].astype(_BF16), conv_w[l], conv_b[l], w_ffn_out[l].astype(_BF16),
                     g_final)
```

```python
import functools
import math

import jax
import jax.numpy as jnp
from jax import lax
from jax.experimental import pallas as pl
from jax.experimental.pallas import tpu as pltpu

D_MODEL = 1024
EPS = 1e-6
ROPE_BASE = 10000.0
N_MLA_HEADS = 8
QK_NOPE = 64
QK_ROPE = 32
V_HEAD = 64
Q_LORA = 256
KV_LORA = 128
N_RET_HEADS = 4
RET_DK = 128
RET_DV = 128
RET_CHUNK = 128
MLA_WIDTH = N_MLA_HEADS * V_HEAD
RET_WIDTH = N_RET_HEADS * RET_DV
N_XATTN_HEADS = 4
XATTN_HEAD = D_MODEL // N_XATTN_HEADS
D_FF = 2816
CONV_W = 3

LANES = 128
SUBLANES = 8
VMEM_LIMIT = 56 * 1024 * 1024

HEAD_PAD = LANES
ROPE_HALF = QK_ROPE // 2
MLA_QK_WIDTH = N_MLA_HEADS * HEAD_PAD
IN_LAT = Q_LORA + KV_LORA + 2 * HEAD_PAD
IN_PERM = IN_LAT + 4 * RET_WIDTH

TM_TABLE = 1024
TM_IN = 512
TQ_MLA = 256
TS_RET = 512
TM_MIX = 512
TM_FFN = 256

_BF16 = jnp.bfloat16
_F32 = jnp.float32


def _dot(a, b):
    return jnp.dot(a, b, preferred_element_type=_F32)


def _dot_nt(a, b):
    return lax.dot_general(a, b, (((1,), (1,)), ((), ())), preferred_element_type=_F32)


def _dot_tn(a, b):
    return lax.dot_general(a, b, (((0,), (0,)), ((), ())), preferred_element_type=_F32)


def _rms(x, g):
    inv = lax.rsqrt(jnp.mean(x * x, axis=-1, keepdims=True) + EPS)
    return x * inv * g


def _const_spec(shape):
    nd = len(shape)
    return pl.BlockSpec(shape, lambda *_: (0,) * nd, pipeline_mode=pl.Buffered(1))


def _params(sem):
    return pltpu.CompilerParams(dimension_semantics=sem, vmem_limit_bytes=VMEM_LIMIT)


def _rope_table_kernel(pos_ref, inv_ref, sgn_ref, cos_ref, sin_ref):
    ang = pos_ref[...] * inv_ref[...]
    cos_ref[...] = jnp.cos(ang)
    sin_ref[...] = jnp.sin(ang) * sgn_ref[...]


def _rope_tables(positions):
    t = positions.size
    pos = positions.astype(_F32).reshape(t, 1)
    f_ret = 1.0 / (ROPE_BASE ** (jnp.arange(0, RET_DK, 2, dtype=_F32) / RET_DK))
    f_mla = 1.0 / (ROPE_BASE ** (jnp.arange(0, QK_ROPE, 2, dtype=_F32) / QK_ROPE))
    z = lambda n: jnp.zeros((n,), _F32)
    o = lambda n: jnp.ones((n,), _F32)
    inv = jnp.concatenate([f_ret, f_ret, z(QK_NOPE), f_mla, f_mla, z(HEAD_PAD - QK_NOPE - QK_ROPE)])
    sgn = jnp.concatenate([-o(RET_DK // 2), o(RET_DK // 2), z(QK_NOPE), -o(ROPE_HALF), o(ROPE_HALF),
                           z(HEAD_PAD - QK_NOPE - QK_ROPE)])
    w = 2 * LANES
    return pl.pallas_call(
        _rope_table_kernel,
        out_shape=(jax.ShapeDtypeStruct((t, w), _F32), jax.ShapeDtypeStruct((t, w), _F32)),
        grid=(t // TM_TABLE,),
        in_specs=[pl.BlockSpec((TM_TABLE, 1), lambda i: (i, 0)),
                  pl.BlockSpec((1, w), lambda i: (0, 0)),
                  pl.BlockSpec((1, w), lambda i: (0, 0))],
        out_specs=(pl.BlockSpec((TM_TABLE, w), lambda i: (i, 0)),
                   pl.BlockSpec((TM_TABLE, w), lambda i: (i, 0))),
        compiler_params=_params(("arbitrary",)),
        name="rope_tables",
    )(pos, inv.reshape(1, w), sgn.reshape(1, w))


def _in_proj_kernel(x_ref, gmix_ref, win_ref, gq_ref, wuq_ref, gkv_ref, wuk_ref, wuvt_ref, cos_ref, sin_ref,
                    q_ref, k_ref, vt_ref, rq_ref, rk_ref, rv_ref, rg_ref):
    h = _rms(x_ref[0], gmix_ref[...]).astype(_BF16)
    cos_r, sin_r = cos_ref[:, :LANES], sin_ref[:, :LANES]
    cos_m, sin_m = cos_ref[:, LANES:], sin_ref[:, LANES:]

    lat = _dot(h, win_ref[:, :IN_LAT])
    c_q = lat[:, :Q_LORA]
    c_kv = lat[:, Q_LORA:Q_LORA + KV_LORA]
    kr = lat[:, Q_LORA + KV_LORA:Q_LORA + KV_LORA + HEAD_PAD]
    kr_sw = lat[:, Q_LORA + KV_LORA + HEAD_PAD:IN_LAT]
    k_rope = kr * cos_m + kr_sw * sin_m

    cqn = _rms(c_q, gq_ref[...]).astype(_BF16)
    q = _dot(cqn, wuq_ref[:, :MLA_QK_WIDTH])
    q_sw = _dot(cqn, wuq_ref[:, MLA_QK_WIDTH:])
    scale = math.log2(math.e) / math.sqrt(QK_NOPE + QK_ROPE)
    for hd in range(N_MLA_HEADS):
        sl = slice(hd * HEAD_PAD, (hd + 1) * HEAD_PAD)
        q_ref[0, :, sl] = ((q[:, sl] * cos_m + q_sw[:, sl] * sin_m) * scale).astype(_BF16)

    ckvn = _rms(c_kv, gkv_ref[...]).astype(_BF16)
    k_nope = _dot(ckvn, wuk_ref[...])
    for hd in range(N_MLA_HEADS):
        sl = slice(hd * HEAD_PAD, (hd + 1) * HEAD_PAD)
        k_ref[0, :, sl] = (k_nope[:, sl] + k_rope).astype(_BF16)
    v_t = _dot_nt(wuvt_ref[...], ckvn).astype(_BF16)
    for j in range(TM_IN // TQ_MLA):
        vt_ref[0, j] = v_t[:, j * TQ_MLA:(j + 1) * TQ_MLA]

    def ret_rope(col0, out_ref, mult):
        r = _dot(h, win_ref[:, col0:col0 + RET_WIDTH])
        for hd in range(N_RET_HEADS):
            sl = slice(hd * RET_DK, (hd + 1) * RET_DK)
            rh = r[:, sl]
            roped = rh * cos_r + pltpu.roll(rh, RET_DK // 2, axis=1) * sin_r
            if mult is not None:
                roped = roped * mult
            out_ref[0, :, sl] = roped.astype(_BF16)

    ret_rope(IN_LAT, rq_ref, None)
    ret_rope(IN_LAT + RET_WIDTH, rk_ref, RET_DK ** -0.5)
    rv_ref[0] = _dot(h, win_ref[:, IN_LAT + 2 * RET_WIDTH:IN_LAT + 3 * RET_WIDTH]).astype(_BF16)
    rg_ref[0] = _dot(h, win_ref[:, IN_LAT + 3 * RET_WIDTH:IN_PERM]).astype(_BF16)


def _permute_in_weights(w_in, w_uq, w_ukv):
    d = w_in.shape[0]
    o_kr = Q_LORA + KV_LORA
    x1 = w_in[:, o_kr:o_kr + ROPE_HALF]
    x2 = w_in[:, o_kr + ROPE_HALF:o_kr + QK_ROPE]
    zl = jnp.zeros((d, QK_NOPE), w_in.dtype)
    zr = jnp.zeros((d, HEAD_PAD - QK_NOPE - QK_ROPE), w_in.dtype)
    win_p = jnp.concatenate([w_in[:, :o_kr], zl, x1, x2, zr, zl, x2, x1, zr, w_in[:, o_kr + QK_ROPE:]],
                            axis=1).astype(_BF16)

    wq = w_uq.reshape(Q_LORA, N_MLA_HEADS, QK_NOPE + QK_ROPE)
    nope, r1, r2 = wq[..., :QK_NOPE], wq[..., QK_NOPE:QK_NOPE + ROPE_HALF], wq[..., QK_NOPE + ROPE_HALF:]
    z32 = jnp.zeros((Q_LORA, N_MLA_HEADS, HEAD_PAD - QK_NOPE - QK_ROPE), w_uq.dtype)
    z64 = jnp.zeros((Q_LORA, N_MLA_HEADS, QK_NOPE), w_uq.dtype)
    main = jnp.concatenate([nope, r1, r2, z32], axis=-1).reshape(Q_LORA, MLA_QK_WIDTH)
    swap = jnp.concatenate([z64, r2, r1, z32], axis=-1).reshape(Q_LORA, MLA_QK_WIDTH)
    wuq_p = jnp.concatenate([main, swap], axis=1).astype(_BF16)

    wkv = w_ukv.reshape(KV_LORA, N_MLA_HEADS, QK_NOPE + V_HEAD)
    zk = jnp.zeros((KV_LORA, N_MLA_HEADS, HEAD_PAD - QK_NOPE), w_ukv.dtype)
    wk = jnp.concatenate([wkv[..., :QK_NOPE], zk], axis=-1).reshape(KV_LORA, MLA_QK_WIDTH)
    wv_t = wkv[..., QK_NOPE:].reshape(KV_LORA, MLA_WIDTH).T
    return win_p, wuq_p, wk.astype(_BF16), wv_t.astype(_BF16)


def _in_proj(x, g_mix, win_p, g_q_lat, wuq_p, g_kv_lat, wuk_p, wuvt_p, cos_t, sin_t):
    b, s, d = x.shape
    nt = s // TM_IN
    per = TM_IN // TQ_MLA
    tok = lambda w: pl.BlockSpec((1, TM_IN, w), lambda bi, i: (bi, i, 0))
    tab = pl.BlockSpec((TM_IN, 2 * LANES), lambda bi, i: (bi * nt + i, 0))
    bf = lambda w: jax.ShapeDtypeStruct((b, s, w), _BF16)
    return pl.pallas_call(
        _in_proj_kernel,
        out_shape=(bf(MLA_QK_WIDTH), bf(MLA_QK_WIDTH),
                   jax.ShapeDtypeStruct((b, s // TQ_MLA, MLA_WIDTH, TQ_MLA), _BF16),
                   bf(RET_WIDTH), bf(RET_WIDTH), bf(RET_WIDTH), bf(RET_WIDTH)),
        grid=(b, nt),
        in_specs=[tok(d), _const_spec((1, d)), _const_spec(win_p.shape), _const_spec((1, Q_LORA)),
                  _const_spec(wuq_p.shape), _const_spec((1, KV_LORA)), _const_spec(wuk_p.shape),
                  _const_spec(wuvt_p.shape), tab, tab],
        out_specs=(tok(MLA_QK_WIDTH), tok(MLA_QK_WIDTH),
                   pl.BlockSpec((1, per, MLA_WIDTH, TQ_MLA), lambda bi, i: (bi, i, 0, 0)),
                   tok(RET_WIDTH), tok(RET_WIDTH), tok(RET_WIDTH), tok(RET_WIDTH)),
        compiler_params=_params(("arbitrary", "arbitrary")),
        name="in_proj",
    )(x, g_mix.reshape(1, d), win_p, g_q_lat.reshape(1, Q_LORA), wuq_p, g_kv_lat.reshape(1, KV_LORA),
      wuk_p, wuvt_p, cos_t, sin_t)


_MASK_VALUE = -0.7 * float(jnp.finfo(jnp.float32).max)
_MLA_AHEAD = 4
_MLA_TILES_PER_ITER = 5


def _mla_attn_kernel(qlo_ref, qhi_ref, k_ref, vt_ref, o_ref, q_ref, s_ref, m_ref, l_ref, acc_ref):
    tq = TQ_MLA
    n_tiles = k_ref.shape[1] // tq
    lo = pl.program_id(1)
    hi = n_tiles - 1 - lo
    q_ref[0] = qlo_ref[0]
    q_ref[1] = qhi_ref[0]
    m_ref[...] = jnp.full(m_ref.shape, _MASK_VALUE, _F32)
    l_ref[...] = jnp.zeros(l_ref.shape, _F32)
    acc_ref[...] = jnp.zeros(acc_ref.shape, _F32)

    def step_args(t):
        sel = (t > lo).astype(jnp.int32)
        return sel, t - 1 - sel * lo

    def scores(sel, kk, hd):
        sl = slice(hd * HEAD_PAD, (hd + 1) * HEAD_PAD)
        ks = pl.multiple_of(kk * tq, tq)
        s_ref[hd] = _dot_nt(k_ref[0, pl.ds(ks, tq), sl], q_ref[sel, :, sl])

    def softmax(sel, hd, keep):
        row = slice(hd, hd + 1)

        def st():
            s = s_ref[hd]
            return s if keep is None else jnp.where(keep, s, _MASK_VALUE)

        m_prev = m_ref[sel, row, :]
        m_next = jnp.maximum(m_prev, jnp.max(st(), axis=0, keepdims=True))
        alpha = jnp.exp2(m_prev - m_next)
        p = jnp.exp2(st() - m_next)
        l_ref[sel, row, :] = alpha * l_ref[sel, row, :] + jnp.sum(p, axis=0, keepdims=True)
        m_ref[sel, row, :] = m_next
        return alpha, p.astype(_BF16)

    def accumulate(sel, kk, hd, alpha, p):
        vt = vt_ref[0, kk, hd * V_HEAD:(hd + 1) * V_HEAD, :]
        acc_ref[sel, hd] = acc_ref[sel, hd] * alpha + _dot(vt, p)

    def tile(sel, kk, diagonal, nxt):
        keep = None
        if diagonal:
            key = lax.broadcasted_iota(jnp.int32, (tq, tq), 0)
            qry = lax.broadcasted_iota(jnp.int32, (tq, tq), 1)
            keep = key <= qry
        pending = None
        for hd in range(N_MLA_HEADS):
            ahead = hd + _MLA_AHEAD
            if ahead < N_MLA_HEADS:
                scores(sel, kk, ahead)
            elif nxt is not None:
                scores(nxt[0], nxt[1], ahead - N_MLA_HEADS)
            current = softmax(sel, hd, keep)
            if pending is not None:
                accumulate(sel, kk, hd - 1, *pending)
            pending = current
        accumulate(sel, kk, N_MLA_HEADS - 1, *pending)

    for hd in range(_MLA_AHEAD):
        scores(0, lo, hd)
    tile(0, lo, True, step_args(1))

    def body(it, carry):
        for u in range(_MLA_TILES_PER_ITER):
            t = 1 + it * _MLA_TILES_PER_ITER + u
            last = t + 1 == n_tiles
            sel_n, kk_n = step_args(t + 1)
            tile(*step_args(t), False, (jnp.where(last, 1, sel_n), jnp.where(last, hi, kk_n)))
        return carry

    lax.fori_loop(0, (n_tiles - 1) // _MLA_TILES_PER_ITER, body, 0)
    tile(1, hi, True, None)

    for sel, qt in ((0, lo), (1, hi)):
        out_t = jnp.concatenate([acc_ref[sel, hd] / l_ref[sel, hd:hd + 1, :] for hd in range(N_MLA_HEADS)],
                                axis=0)
        o_ref[0, pl.ds(pl.multiple_of(qt * tq, tq), tq), :] = out_t.T.astype(o_ref.dtype)


def _mla_attn(q, k, vt):
    b, s, w = q.shape
    nk = s // TQ_MLA
    assert nk % 2 == 0 and (nk - 1) % _MLA_TILES_PER_ITER == 0
    return pl.pallas_call(
        _mla_attn_kernel,
        out_shape=jax.ShapeDtypeStruct((b, s, MLA_WIDTH), _BF16),
        grid=(b, nk // 2),
        in_specs=[pl.BlockSpec((1, TQ_MLA, w), lambda bi, i: (bi, i, 0)),
                  pl.BlockSpec((1, TQ_MLA, w), lambda bi, i: (bi, nk - 1 - i, 0)),
                  pl.BlockSpec((1, s, w), lambda bi, i: (bi, 0, 0)),
                  pl.BlockSpec((1, nk, MLA_WIDTH, TQ_MLA), lambda bi, i: (bi, 0, 0, 0))],
        out_specs=pl.BlockSpec((1, s, MLA_WIDTH), lambda bi, i: (bi, 0, 0)),
        scratch_shapes=[pltpu.VMEM((2, TQ_MLA, w), _BF16),
                        pltpu.VMEM((N_MLA_HEADS, TQ_MLA, TQ_MLA), _F32),
                        pltpu.VMEM((2, N_MLA_HEADS, TQ_MLA), _F32), pltpu.VMEM((2, N_MLA_HEADS, TQ_MLA), _F32),
                        pltpu.VMEM((2, N_MLA_HEADS, V_HEAD, TQ_MLA), _F32)],
        compiler_params=_params(("arbitrary", "arbitrary")),
        name="mla_attn",
    )(q, q, k, vt)


def _retention_kernel(q_ref, k_ref, v_ref, g_ref, intra_ref, kend_ref, qstart_ref, decay_ref,
                      o_ref, state_ref):
    @pl.when(pl.program_id(1) == 0)
    def _():
        state_ref[...] = jnp.zeros(state_ref.shape, _F32)

    L = RET_CHUNK
    for c in range(TS_RET // L):
        rows = slice(c * L, (c + 1) * L)
        for hd in range(N_RET_HEADS):
            cols = slice(hd * RET_DK, (hd + 1) * RET_DK)
            q = q_ref[0, rows, cols]
            k = k_ref[0, rows, cols]
            v = v_ref[0, rows, cols]
            state = state_ref[hd]
            scores = _dot_nt(q, k) * intra_ref[hd]
            inner = _dot(scores.astype(_BF16), v)
            cross = _dot(q, state.astype(_BF16)) * qstart_ref[hd]
            out = inner + cross
            v_dec = (v.astype(_F32) * kend_ref[hd]).astype(_BF16)
            state_ref[hd] = decay_ref[hd] * state + _dot_tn(k, v_dec)
            mu = jnp.mean(out, axis=-1, keepdims=True)
            cen = out - mu
            var = jnp.mean(cen * cen, axis=-1, keepdims=True)
            normed = cen * lax.rsqrt(var + EPS)
            g = g_ref[0, rows, cols].astype(_F32)
            o_ref[0, rows, cols] = (normed * (g * jax.nn.sigmoid(g))).astype(o_ref.dtype)


def _retention_tables():
    h, L = N_RET_HEADS, RET_CHUNK
    log_gamma = jnp.log(1.0 - 2.0 ** (-5.0 - jnp.arange(h, dtype=_F32)))
    j = jnp.arange(L, dtype=_F32)
    diff = j[:, None] - j[None, :]
    intra = jnp.where(diff[None] >= 0,
                      jnp.exp(jnp.maximum(diff, 0.0)[None] * log_gamma[:, None, None]), 0.0)
    rowb = lambda t: jnp.broadcast_to(t.T[:, :, None], (h, L, LANES))
    k_to_end = jnp.exp((L - 1 - j)[:, None] * log_gamma[None, :])
    q_from_start = jnp.exp((j + 1)[:, None] * log_gamma[None, :])
    chunk_decay = jnp.broadcast_to(jnp.exp(L * log_gamma)[:, None, None], (h, RET_DK, RET_DV))
    return intra, rowb(k_to_end), rowb(q_from_start), chunk_decay


def _retention(rq, rk, rv, rg):
    b, s, w = rq.shape
    tok = pl.BlockSpec((1, TS_RET, w), lambda bi, i: (bi, i, 0))
    tables = _retention_tables()
    return pl.pallas_call(
        _retention_kernel,
        out_shape=jax.ShapeDtypeStruct((b, s, w), _BF16),
        grid=(b, s // TS_RET),
        in_specs=[tok, tok, tok, tok] + [_const_spec(t.shape) for t in tables],
        out_specs=tok,
        scratch_shapes=[pltpu.VMEM((N_RET_HEADS, RET_DK, RET_DV), _F32)],
        compiler_params=_params(("arbitrary", "arbitrary")),
        name="retention",
    )(rq, rk, rv, rg, *tables)


def _mem_kv_kernel(mem_ref, g_ref, w_ref, kv_ref):
    mem_n = _rms(mem_ref[0], g_ref[...]).astype(_BF16)
    kv_ref[0] = _dot(mem_n, w_ref[...]).astype(_BF16)


def _mem_kv(mem, g_mem, w_xkv):
    b, m, d = mem.shape
    n = w_xkv.shape[1]
    return pl.pallas_call(
        _mem_kv_kernel,
        out_shape=jax.ShapeDtypeStruct((b, m, n), _BF16),
        grid=(b,),
        in_specs=[pl.BlockSpec((1, m, d), lambda bi: (bi, 0, 0)), _const_spec((1, d)), _const_spec((d, n))],
        out_specs=pl.BlockSpec((1, m, n), lambda bi: (bi, 0, 0)),
        compiler_params=_params(("arbitrary",)),
        name="mem_kv",
    )(mem, g_mem.reshape(1, d), w_xkv)


def _mix_xattn_kernel(x_ref, ymla_ref, yret_ref, wout_ref, gx_ref, wxq_ref, kv_ref, wxo_ref, o_ref):
    x1 = x_ref[0] + _dot(ymla_ref[0], wout_ref[:MLA_WIDTH, :]) + _dot(yret_ref[0], wout_ref[MLA_WIDTH:, :])
    h = _rms(x1, gx_ref[...]).astype(_BF16)
    q = (_dot(h, wxq_ref[...]) * (1.0 / math.sqrt(XATTN_HEAD))).astype(_BF16)
    heads = []
    for hd in range(N_XATTN_HEADS):
        cols = slice(hd * XATTN_HEAD, (hd + 1) * XATTN_HEAD)
        vcols = slice(D_MODEL + hd * XATTN_HEAD, D_MODEL + (hd + 1) * XATTN_HEAD)
        s = _dot_nt(q[:, cols], kv_ref[0, :, cols])
        p = jnp.exp(s - jnp.max(s, axis=-1, keepdims=True))
        o = _dot(p.astype(_BF16), kv_ref[0, :, vcols]) / jnp.sum(p, axis=-1, keepdims=True)
        heads.append(o.astype(_BF16))
    o_ref[0] = x1 + _dot(jnp.concatenate(heads, axis=1), wxo_ref[...])


def _mix_xattn(x, y_mla, y_ret, w_out, g_xattn, w_xq, kv_mem, w_xo):
    b, s, d = x.shape
    m, n = kv_mem.shape[1:]
    tok = lambda w: pl.BlockSpec((1, TM_MIX, w), lambda bi, i: (bi, i, 0))
    return pl.pallas_call(
        _mix_xattn_kernel,
        out_shape=jax.ShapeDtypeStruct((b, s, d), _F32),
        grid=(b, s // TM_MIX),
        in_specs=[tok(d), tok(MLA_WIDTH), tok(RET_WIDTH), _const_spec(w_out.shape), _const_spec((1, d)),
                  _const_spec(w_xq.shape), pl.BlockSpec((1, m, n), lambda bi, i: (bi, 0, 0)),
                  _const_spec(w_xo.shape)],
        out_specs=tok(d),
        compiler_params=_params(("arbitrary", "arbitrary")),
        name="mix_xattn",
    )(x, y_mla, y_ret, w_out, g_xattn.reshape(1, d), w_xq, kv_mem, w_xo)


def _conv_ffn_kernel(x_ref, g_ref, win_ref, cw_ref, cb_ref, wout_ref, gfin_ref, o_ref, gate_ref):
    tm = TM_FFN
    halo = SUBLANES

    @pl.when(pl.program_id(1) == 0)
    def _():
        gate_ref[:halo, :] = jnp.zeros((halo, D_FF), _F32)

    x = x_ref[0]
    h = _rms(x, g_ref[...]).astype(_BF16)
    gate = _dot(h, win_ref[:, :D_FF])
    up = _dot(h, win_ref[:, D_FF:])
    gate_ref[halo:, :] = gate
    conv = cb_ref[...] + gate * cw_ref[CONV_W - 1:CONV_W, :]
    for tap in range(CONV_W - 1):
        back = CONV_W - 1 - tap
        conv = conv + gate_ref[halo - back:halo - back + tm, :] * cw_ref[tap:tap + 1, :]
    act = (conv * jax.nn.sigmoid(conv) * up).astype(_BF16)
    x3 = x + _dot(act, wout_ref[...])
    o_ref[0] = _rms(x3, gfin_ref[...])
    gate_ref[:halo, :] = gate_ref[tm:, :]


def _conv_ffn(x, g_ffn, w_ffn_in, conv_w, conv_b, w_ffn_out, g_final):
    b, s, d = x.shape
    tok = pl.BlockSpec((1, TM_FFN, d), lambda bi, i: (bi, i, 0))
    return pl.pallas_call(
        _conv_ffn_kernel,
        out_shape=jax.ShapeDtypeStruct((b, s, d), _F32),
        grid=(b, s // TM_FFN),
        in_specs=[tok, _const_spec((1, d)), _const_spec(w_ffn_in.shape), _const_spec(conv_w.shape),
                  _const_spec((1, D_FF)), _const_spec(w_ffn_out.shape), _const_spec((1, d))],
        out_specs=tok,
        scratch_shapes=[pltpu.VMEM((TM_FFN + SUBLANES, D_FF), _F32)],
        compiler_params=_params(("arbitrary", "arbitrary")),
        name="conv_ffn",
    )(x, g_ffn.reshape(1, d), w_ffn_in, conv_w, conv_b.reshape(1, D_FF), w_ffn_out, g_final.reshape(1, d))


def kernel(x, mem, positions, g_mix, w_in, g_q_lat, w_uq, g_kv_lat, w_ukv, w_out, g_xattn, g_mem, w_xq,
           w_xkv, w_xo, g_ffn, w_ffn_in, conv_w, conv_b, w_ffn_out, g_final):
    assert w_in.shape[0] == 1, "one layer supported"
    l = 0
    cos_t, sin_t = _rope_tables(positions)
    win_p, wuq_p, wuk_p, wuvt_p = _permute_in_weights(w_in[l], w_uq[l], w_ukv[l])
    q, k, vt, rq, rk, rv, rg = _in_proj(x, g_mix[l], win_p, g_q_lat[l], wuq_p, g_kv_lat[l], wuk_p, wuvt_p,
                                        cos_t, sin_t)
    y_mla = _mla_attn(q, k, vt)
    y_ret = _retention(rq, rk, rv, rg)
    kv_mem = _mem_kv(mem, g_mem[l], w_xkv[l].astype(_BF16))
    x = _mix_xattn(x, y_mla, y_ret, w_out[l].astype(_BF16), g_xattn[l], w_xq[l].astype(_BF16), kv_mem,
                   w_xo[l].astype(_BF16))
    return _conv_ffn(x, g_ffn[l], w_ffn_in[l].astype(_BF16), conv_w[l], conv_b[l], w_ffn_out[l].astype(_BF16),
                     g_final)
```

```python
import functools
import math

import jax
import jax.numpy as jnp
from jax import lax
from jax.experimental import pallas as pl
from jax.experimental.pallas import tpu as pltpu

D_MODEL = 1024
EPS = 1e-6
ROPE_BASE = 10000.0
N_MLA_HEADS = 8
QK_NOPE = 64
QK_ROPE = 32
V_HEAD = 64
Q_LORA = 256
KV_LORA = 128
N_RET_HEADS = 4
RET_DK = 128
RET_DV = 128
RET_CHUNK = 128
MLA_WIDTH = N_MLA_HEADS * V_HEAD
RET_WIDTH = N_RET_HEADS * RET_DV
N_XATTN_HEADS = 4
XATTN_HEAD = D_MODEL // N_XATTN_HEADS
D_FF = 2816
CONV_W = 3

LANES = 128
SUBLANES = 8
VMEM_LIMIT = 56 * 1024 * 1024

HEAD_PAD = LANES
ROPE_HALF = QK_ROPE // 2
MLA_QK_WIDTH = N_MLA_HEADS * HEAD_PAD
IN_LAT = Q_LORA + KV_LORA + 2 * HEAD_PAD
IN_PERM = IN_LAT + 4 * RET_WIDTH

TM_TABLE = 1024
TM_IN = 512
TQ_MLA = 256
TS_RET = 512
TM_MIX = 512
TM_FFN = 256

_BF16 = jnp.bfloat16
_F32 = jnp.float32


def _dot(a, b):
    return jnp.dot(a, b, preferred_element_type=_F32)


def _dot_nt(a, b):
    return lax.dot_general(a, b, (((1,), (1,)), ((), ())), preferred_element_type=_F32)


def _dot_tn(a, b):
    return lax.dot_general(a, b, (((0,), (0,)), ((), ())), preferred_element_type=_F32)


def _rms(x, g):
    inv = lax.rsqrt(jnp.mean(x * x, axis=-1, keepdims=True) + EPS)
    return x * inv * g


def _const_spec(shape):
    nd = len(shape)
    return pl.BlockSpec(shape, lambda *_: (0,) * nd, pipeline_mode=pl.Buffered(1))


def _params(sem):
    return pltpu.CompilerParams(dimension_semantics=sem, vmem_limit_bytes=VMEM_LIMIT)


def _rope_table_kernel(pos_ref, inv_ref, cos_ref, sin_ref):
    ang = pos_ref[...] * inv_ref[...]
    cos_ref[...] = jnp.cos(ang)
    sin_ref[...] = jnp.sin(ang)


def _rope_tables(positions):
    t = positions.size
    pos = positions.astype(_F32).reshape(t, 1)
    f_ret = 1.0 / (ROPE_BASE ** (jnp.arange(0, RET_DK, 2, dtype=_F32) / RET_DK))
    f_mla = 1.0 / (ROPE_BASE ** (jnp.arange(0, QK_ROPE, 2, dtype=_F32) / QK_ROPE))
    inv = jnp.concatenate([f_ret, f_mla, f_mla, jnp.zeros((HEAD_PAD - QK_NOPE - QK_ROPE,), _F32)])
    return pl.pallas_call(
        _rope_table_kernel,
        out_shape=(jax.ShapeDtypeStruct((t, LANES), _F32), jax.ShapeDtypeStruct((t, LANES), _F32)),
        grid=(t // TM_TABLE,),
        in_specs=[pl.BlockSpec((TM_TABLE, 1), lambda i: (i, 0)),
                  pl.BlockSpec((1, LANES), lambda i: (0, 0))],
        out_specs=(pl.BlockSpec((TM_TABLE, LANES), lambda i: (i, 0)),
                   pl.BlockSpec((TM_TABLE, LANES), lambda i: (i, 0))),
        compiler_params=_params(("arbitrary",)),
        name="rope_tables",
    )(pos, inv.reshape(1, LANES))


def _in_proj_kernel(x_ref, gmix_ref, win_ref, gq_ref, wuq_ref, gkv_ref, wuk_ref, wuvt_ref, cos_ref, sin_ref,
                    q_ref, k_ref, vt_ref, rq_ref, rk_ref, rv_ref, rg_ref):
    h = _rms(x_ref[0], gmix_ref[...]).astype(_BF16)
    t_cos, t_sin = cos_ref[...], sin_ref[...]
    lane = lax.broadcasted_iota(jnp.int32, t_cos.shape, 1)
    low = lane < RET_DK // 2
    cos_r = jnp.where(low, t_cos, pltpu.roll(t_cos, RET_DK // 2, axis=1))
    sin_r = jnp.where(low, -t_sin, pltpu.roll(t_sin, RET_DK // 2, axis=1))
    cos_m = jnp.where(low, 1.0, t_cos)
    sin_m = jnp.where(low, 0.0, jnp.where(lane < QK_NOPE + ROPE_HALF, -t_sin, t_sin))

    lat = _dot(h, win_ref[:, :IN_LAT])
    c_q = lat[:, :Q_LORA]
    c_kv = lat[:, Q_LORA:Q_LORA + KV_LORA]
    kr = lat[:, Q_LORA + KV_LORA:Q_LORA + KV_LORA + HEAD_PAD]
    kr_sw = lat[:, Q_LORA + KV_LORA + HEAD_PAD:IN_LAT]
    k_rope = kr * cos_m + kr_sw * sin_m

    cqn = _rms(c_q, gq_ref[...]).astype(_BF16)
    q = _dot(cqn, wuq_ref[:, :MLA_QK_WIDTH])
    q_sw = _dot(cqn, wuq_ref[:, MLA_QK_WIDTH:])
    scale = math.log2(math.e) / math.sqrt(QK_NOPE + QK_ROPE)
    for hd in range(N_MLA_HEADS):
        sl = slice(hd * HEAD_PAD, (hd + 1) * HEAD_PAD)
        q_ref[0, :, sl] = ((q[:, sl] * cos_m + q_sw[:, sl] * sin_m) * scale).astype(_BF16)

    ckvn = _rms(c_kv, gkv_ref[...]).astype(_BF16)
    k_nope = _dot(ckvn, wuk_ref[...])
    for hd in range(N_MLA_HEADS):
        sl = slice(hd * HEAD_PAD, (hd + 1) * HEAD_PAD)
        k_ref[0, :, sl] = (k_nope[:, sl] + k_rope).astype(_BF16)
    v_t = _dot_nt(wuvt_ref[...], ckvn).astype(_BF16)
    for j in range(TM_IN // TQ_MLA):
        vt_ref[0, j] = v_t[:, j * TQ_MLA:(j + 1) * TQ_MLA]

    def ret_rope(col0, out_ref, mult):
        r = _dot(h, win_ref[:, col0:col0 + RET_WIDTH])
        for hd in range(N_RET_HEADS):
            sl = slice(hd * RET_DK, (hd + 1) * RET_DK)
            rh = r[:, sl]
            roped = rh * cos_r + pltpu.roll(rh, RET_DK // 2, axis=1) * sin_r
            if mult is not None:
                roped = roped * mult
            out_ref[0, :, sl] = roped.astype(_BF16)

    ret_rope(IN_LAT, rq_ref, None)
    ret_rope(IN_LAT + RET_WIDTH, rk_ref, RET_DK ** -0.5)
    rv_ref[0] = _dot(h, win_ref[:, IN_LAT + 2 * RET_WIDTH:IN_LAT + 3 * RET_WIDTH]).astype(_BF16)
    rg_ref[0] = _dot(h, win_ref[:, IN_LAT + 3 * RET_WIDTH:IN_PERM]).astype(_BF16)


def _permute_in_weights(w_in, w_uq, w_ukv):
    d = w_in.shape[0]
    o_kr = Q_LORA + KV_LORA
    x1 = w_in[:, o_kr:o_kr + ROPE_HALF]
    x2 = w_in[:, o_kr + ROPE_HALF:o_kr + QK_ROPE]
    zl = jnp.zeros((d, QK_NOPE), w_in.dtype)
    zr = jnp.zeros((d, HEAD_PAD - QK_NOPE - QK_ROPE), w_in.dtype)
    win_p = jnp.concatenate([w_in[:, :o_kr], zl, x1, x2, zr, zl, x2, x1, zr, w_in[:, o_kr + QK_ROPE:]],
                            axis=1).astype(_BF16)

    wq = w_uq.reshape(Q_LORA, N_MLA_HEADS, QK_NOPE + QK_ROPE)
    nope, r1, r2 = wq[..., :QK_NOPE], wq[..., QK_NOPE:QK_NOPE + ROPE_HALF], wq[..., QK_NOPE + ROPE_HALF:]
    z32 = jnp.zeros((Q_LORA, N_MLA_HEADS, HEAD_PAD - QK_NOPE - QK_ROPE), w_uq.dtype)
    z64 = jnp.zeros((Q_LORA, N_MLA_HEADS, QK_NOPE), w_uq.dtype)
    main = jnp.concatenate([nope, r1, r2, z32], axis=-1).reshape(Q_LORA, MLA_QK_WIDTH)
    swap = jnp.concatenate([z64, r2, r1, z32], axis=-1).reshape(Q_LORA, MLA_QK_WIDTH)
    wuq_p = jnp.concatenate([main, swap], axis=1).astype(_BF16)

    wkv = w_ukv.reshape(KV_LORA, N_MLA_HEADS, QK_NOPE + V_HEAD)
    zk = jnp.zeros((KV_LORA, N_MLA_HEADS, HEAD_PAD - QK_NOPE), w_ukv.dtype)
    wk = jnp.concatenate([wkv[..., :QK_NOPE], zk], axis=-1).reshape(KV_LORA, MLA_QK_WIDTH)
    wv_t = wkv[..., QK_NOPE:].reshape(KV_LORA, MLA_WIDTH).T
    return win_p, wuq_p, wk.astype(_BF16), wv_t.astype(_BF16)


def _in_proj(x, g_mix, win_p, g_q_lat, wuq_p, g_kv_lat, wuk_p, wuvt_p, cos_t, sin_t):
    b, s, d = x.shape
    nt = s // TM_IN
    per = TM_IN // TQ_MLA
    tok = lambda w: pl.BlockSpec((1, TM_IN, w), lambda bi, i: (bi, i, 0))
    tab = pl.BlockSpec((TM_IN, LANES), lambda bi, i: (bi * nt + i, 0))
    bf = lambda w: jax.ShapeDtypeStruct((b, s, w), _BF16)
    return pl.pallas_call(
        _in_proj_kernel,
        out_shape=(bf(MLA_QK_WIDTH), bf(MLA_QK_WIDTH),
                   jax.ShapeDtypeStruct((b, s // TQ_MLA, MLA_WIDTH, TQ_MLA), _BF16),
                   bf(RET_WIDTH), bf(RET_WIDTH), bf(RET_WIDTH), bf(RET_WIDTH)),
        grid=(b, nt),
        in_specs=[tok(d), _const_spec((1, d)), _const_spec(win_p.shape), _const_spec((1, Q_LORA)),
                  _const_spec(wuq_p.shape), _const_spec((1, KV_LORA)), _const_spec(wuk_p.shape),
                  _const_spec(wuvt_p.shape), tab, tab],
        out_specs=(tok(MLA_QK_WIDTH), tok(MLA_QK_WIDTH),
                   pl.BlockSpec((1, per, MLA_WIDTH, TQ_MLA), lambda bi, i: (bi, i, 0, 0)),
                   tok(RET_WIDTH), tok(RET_WIDTH), tok(RET_WIDTH), tok(RET_WIDTH)),
        compiler_params=_params(("arbitrary", "arbitrary")),
        name="in_proj",
    )(x, g_mix.reshape(1, d), win_p, g_q_lat.reshape(1, Q_LORA), wuq_p, g_kv_lat.reshape(1, KV_LORA),
      wuk_p, wuvt_p, cos_t, sin_t)


_MASK_VALUE = -0.7 * float(jnp.finfo(jnp.float32).max)
_MLA_AHEAD = 4
_MLA_TILES_PER_ITER = 5


def _mla_attn_kernel(qlo_ref, qhi_ref, k_ref, vt_ref, o_ref, q_ref, s_ref, m_ref, l_ref, acc_ref):
    tq = TQ_MLA
    n_tiles = k_ref.shape[1] // tq
    lo = pl.program_id(1)
    hi = n_tiles - 1 - lo
    q_ref[0] = qlo_ref[0]
    q_ref[1] = qhi_ref[0]
    m_ref[...] = jnp.full(m_ref.shape, _MASK_VALUE, _F32)
    l_ref[...] = jnp.zeros(l_ref.shape, _F32)
    acc_ref[...] = jnp.zeros(acc_ref.shape, _F32)

    def step_args(t):
        sel = (t > lo).astype(jnp.int32)
        return sel, t - 1 - sel * lo

    def scores(sel, kk, hd):
        sl = slice(hd * HEAD_PAD, (hd + 1) * HEAD_PAD)
        ks = pl.multiple_of(kk * tq, tq)
        s_ref[hd] = _dot_nt(k_ref[0, pl.ds(ks, tq), sl], q_ref[sel, :, sl])

    def softmax(sel, hd, keep):
        row = slice(hd, hd + 1)

        def st():
            s = s_ref[hd]
            return s if keep is None else jnp.where(keep, s, _MASK_VALUE)

        m_prev = m_ref[sel, row, :]
        m_next = jnp.maximum(m_prev, jnp.max(st(), axis=0, keepdims=True))
        alpha = jnp.exp2(m_prev - m_next)
        p = jnp.exp2(st() - m_next)
        l_ref[sel, row, :] = alpha * l_ref[sel, row, :] + jnp.sum(p, axis=0, keepdims=True)
        m_ref[sel, row, :] = m_next
        return alpha, p.astype(_BF16)

    def accumulate(sel, kk, hd, alpha, p):
        vt = vt_ref[0, kk, hd * V_HEAD:(hd + 1) * V_HEAD, :]
        acc_ref[sel, hd] = acc_ref[sel, hd] * alpha + _dot(vt, p)

    def tile(sel, kk, diagonal, nxt):
        keep = None
        if diagonal:
            key = lax.broadcasted_iota(jnp.int32, (tq, tq), 0)
            qry = lax.broadcasted_iota(jnp.int32, (tq, tq), 1)
            keep = key <= qry
        pending = None
        for hd in range(N_MLA_HEADS):
            ahead = hd + _MLA_AHEAD
            if ahead < N_MLA_HEADS:
                scores(sel, kk, ahead)
            elif nxt is not None:
                scores(nxt[0], nxt[1], ahead - N_MLA_HEADS)
            current = softmax(sel, hd, keep)
            if pending is not None:
                accumulate(sel, kk, hd - 1, *pending)
            pending = current
        accumulate(sel, kk, N_MLA_HEADS - 1, *pending)

    for hd in range(_MLA_AHEAD):
        scores(0, lo, hd)
    tile(0, lo, True, step_args(1))

    def body(it, carry):
        for u in range(_MLA_TILES_PER_ITER):
            t = 1 + it * _MLA_TILES_PER_ITER + u
            last = t + 1 == n_tiles
            sel_n, kk_n = step_args(t + 1)
            tile(*step_args(t), False, (jnp.where(last, 1, sel_n), jnp.where(last, hi, kk_n)))
        return carry

    lax.fori_loop(0, (n_tiles - 1) // _MLA_TILES_PER_ITER, body, 0)
    tile(1, hi, True, None)

    for sel, qt in ((0, lo), (1, hi)):
        out_t = jnp.concatenate([acc_ref[sel, hd] / l_ref[sel, hd:hd + 1, :] for hd in range(N_MLA_HEADS)],
                                axis=0)
        o_ref[0, pl.ds(pl.multiple_of(qt * tq, tq), tq), :] = out_t.T.astype(o_ref.dtype)


def _mla_attn(q, k, vt):
    b, s, w = q.shape
    nk = s // TQ_MLA
    assert nk % 2 == 0 and (nk - 1) % _MLA_TILES_PER_ITER == 0
    return pl.pallas_call(
        _mla_attn_kernel,
        out_shape=jax.ShapeDtypeStruct((b, s, MLA_WIDTH), _BF16),
        grid=(b, nk // 2),
        in_specs=[pl.BlockSpec((1, TQ_MLA, w), lambda bi, i: (bi, i, 0)),
                  pl.BlockSpec((1, TQ_MLA, w), lambda bi, i: (bi, nk - 1 - i, 0)),
                  pl.BlockSpec((1, s, w), lambda bi, i: (bi, 0, 0)),
                  pl.BlockSpec((1, nk, MLA_WIDTH, TQ_MLA), lambda bi, i: (bi, 0, 0, 0))],
        out_specs=pl.BlockSpec((1, s, MLA_WIDTH), lambda bi, i: (bi, 0, 0)),
        scratch_shapes=[pltpu.VMEM((2, TQ_MLA, w), _BF16),
                        pltpu.VMEM((N_MLA_HEADS, TQ_MLA, TQ_MLA), _F32),
                        pltpu.VMEM((2, N_MLA_HEADS, TQ_MLA), _F32), pltpu.VMEM((2, N_MLA_HEADS, TQ_MLA), _F32),
                        pltpu.VMEM((2, N_MLA_HEADS, V_HEAD, TQ_MLA), _F32)],
        compiler_params=_params(("arbitrary", "arbitrary")),
        name="mla_attn",
    )(q, q, k, vt)


def _retention_kernel(q_ref, k_ref, v_ref, g_ref, intra_ref, kend_ref, qstart_ref, decay_ref,
                      o_ref, state_ref):
    @pl.when(pl.program_id(1) == 0)
    def _():
        state_ref[...] = jnp.zeros(state_ref.shape, _F32)

    L = RET_CHUNK
    units = [(c, hd) for c in range(TS_RET // L) for hd in range(N_RET_HEADS)]
    rows = lambda c: slice(c * L, (c + 1) * L)
    cols = lambda hd: slice(hd * RET_DK, (hd + 1) * RET_DK)
    scores, chunk_kv = {}, {}
    for c, hd in units:
        scores[c, hd] = _dot_nt(q_ref[0, rows(c), cols(hd)], k_ref[0, rows(c), cols(hd)])
    for c, hd in units:
        v_dec = (v_ref[0, rows(c), cols(hd)].astype(_F32) * kend_ref[hd]).astype(_BF16)
        chunk_kv[c, hd] = _dot_tn(k_ref[0, rows(c), cols(hd)], v_dec)
    prev_state = {}
    for hd in range(N_RET_HEADS):
        state = state_ref[hd]
        for c in range(TS_RET // L):
            prev_state[c, hd] = state.astype(_BF16)
            state = decay_ref[hd] * state + chunk_kv[c, hd]
        state_ref[hd] = state
    inner, cross = {}, {}
    for c, hd in units:
        inner[c, hd] = _dot((scores[c, hd] * intra_ref[hd]).astype(_BF16), v_ref[0, rows(c), cols(hd)])
    for c, hd in units:
        cross[c, hd] = _dot(q_ref[0, rows(c), cols(hd)], prev_state[c, hd])
    for c, hd in units:
        out = inner[c, hd] + cross[c, hd] * qstart_ref[hd]
        mu = jnp.mean(out, axis=-1, keepdims=True)
        cen = out - mu
        var = jnp.mean(cen * cen, axis=-1, keepdims=True)
        normed = cen * lax.rsqrt(var + EPS)
        g = g_ref[0, rows(c), cols(hd)].astype(_F32)
        o_ref[0, rows(c), cols(hd)] = (normed * (g * jax.nn.sigmoid(g))).astype(o_ref.dtype)


def _retention_tables():
    h, L = N_RET_HEADS, RET_CHUNK
    log_gamma = jnp.log(1.0 - 2.0 ** (-5.0 - jnp.arange(h, dtype=_F32)))
    j = jnp.arange(L, dtype=_F32)
    diff = j[:, None] - j[None, :]
    intra = jnp.where(diff[None] >= 0,
                      jnp.exp(jnp.maximum(diff, 0.0)[None] * log_gamma[:, None, None]), 0.0)
    rowb = lambda t: jnp.broadcast_to(t.T[:, :, None], (h, L, LANES))
    k_to_end = jnp.exp((L - 1 - j)[:, None] * log_gamma[None, :])
    q_from_start = jnp.exp((j + 1)[:, None] * log_gamma[None, :])
    chunk_decay = jnp.broadcast_to(jnp.exp(L * log_gamma)[:, None, None], (h, RET_DK, RET_DV))
    return intra, rowb(k_to_end), rowb(q_from_start), chunk_decay


def _retention(rq, rk, rv, rg):
    b, s, w = rq.shape
    tok = pl.BlockSpec((1, TS_RET, w), lambda bi, i: (bi, i, 0))
    tables = _retention_tables()
    return pl.pallas_call(
        _retention_kernel,
        out_shape=jax.ShapeDtypeStruct((b, s, w), _BF16),
        grid=(b, s // TS_RET),
        in_specs=[tok, tok, tok, tok] + [_const_spec(t.shape) for t in tables],
        out_specs=tok,
        scratch_shapes=[pltpu.VMEM((N_RET_HEADS, RET_DK, RET_DV), _F32)],
        compiler_params=_params(("arbitrary", "arbitrary")),
        name="retention",
    )(rq, rk, rv, rg, *tables)


def _mem_kv_kernel(mem_ref, g_ref, w_ref, kv_ref):
    mem_n = _rms(mem_ref[0], g_ref[...]).astype(_BF16)
    kv_ref[0] = _dot(mem_n, w_ref[...]).astype(_BF16)


def _mem_kv(mem, g_mem, w_xkv):
    b, m, d = mem.shape
    n = w_xkv.shape[1]
    return pl.pallas_call(
        _mem_kv_kernel,
        out_shape=jax.ShapeDtypeStruct((b, m, n), _BF16),
        grid=(b,),
        in_specs=[pl.BlockSpec((1, m, d), lambda bi: (bi, 0, 0)), _const_spec((1, d)), _const_spec((d, n))],
        out_specs=pl.BlockSpec((1, m, n), lambda bi: (bi, 0, 0)),
        compiler_params=_params(("arbitrary",)),
        name="mem_kv",
    )(mem, g_mem.reshape(1, d), w_xkv)


def _mix_xattn_kernel(x_ref, ymla_ref, yret_ref, wout_ref, gx_ref, wxq_ref, kv_ref, wxo_ref, o_ref):
    x1 = x_ref[0] + _dot(ymla_ref[0], wout_ref[:MLA_WIDTH, :]) + _dot(yret_ref[0], wout_ref[MLA_WIDTH:, :])
    h = _rms(x1, gx_ref[...]).astype(_BF16)
    q = (_dot(h, wxq_ref[...]) * (1.0 / math.sqrt(XATTN_HEAD))).astype(_BF16)
    cols = lambda hd: slice(hd * XATTN_HEAD, (hd + 1) * XATTN_HEAD)
    vcols = lambda hd: slice(D_MODEL + hd * XATTN_HEAD, D_MODEL + (hd + 1) * XATTN_HEAD)
    scores = [_dot_nt(q[:, cols(hd)], kv_ref[0, :, cols(hd)]) for hd in range(N_XATTN_HEADS)]
    probs = [jnp.exp(s - jnp.max(s, axis=-1, keepdims=True)) for s in scores]
    heads = [(_dot(p.astype(_BF16), kv_ref[0, :, vcols(hd)]) / jnp.sum(p, axis=-1, keepdims=True)).astype(_BF16)
             for hd, p in enumerate(probs)]
    o_ref[0] = x1 + _dot(jnp.concatenate(heads, axis=1), wxo_ref[...])


def _mix_xattn(x, y_mla, y_ret, w_out, g_xattn, w_xq, kv_mem, w_xo):
    b, s, d = x.shape
    m, n = kv_mem.shape[1:]
    tok = lambda w: pl.BlockSpec((1, TM_MIX, w), lambda bi, i: (bi, i, 0))
    return pl.pallas_call(
        _mix_xattn_kernel,
        out_shape=jax.ShapeDtypeStruct((b, s, d), _F32),
        grid=(b, s // TM_MIX),
        in_specs=[tok(d), tok(MLA_WIDTH), tok(RET_WIDTH), _const_spec(w_out.shape), _const_spec((1, d)),
                  _const_spec(w_xq.shape), pl.BlockSpec((1, m, n), lambda bi, i: (bi, 0, 0)),
                  _const_spec(w_xo.shape)],
        out_specs=tok(d),
        compiler_params=_params(("arbitrary", "arbitrary")),
        name="mix_xattn",
    )(x, y_mla, y_ret, w_out, g_xattn.reshape(1, d), w_xq, kv_mem, w_xo)


def _conv_ffn_kernel(x_ref, g_ref, win_ref, cw_ref, cb_ref, wout_ref, gfin_ref, o_ref, gate_ref):
    tm = TM_FFN
    halo = SUBLANES

    @pl.when(pl.program_id(1) == 0)
    def _():
        gate_ref[:halo, :] = jnp.zeros((halo, D_FF), _F32)

    x = x_ref[0]
    h = _rms(x, g_ref[...]).astype(_BF16)
    gate = _dot(h, win_ref[:, :D_FF])
    up = _dot(h, win_ref[:, D_FF:])
    gate_ref[halo:, :] = gate
    conv = cb_ref[...] + gate * cw_ref[CONV_W - 1:CONV_W, :]
    for tap in range(CONV_W - 1):
        back = CONV_W - 1 - tap
        conv = conv + gate_ref[halo - back:halo - back + tm, :] * cw_ref[tap:tap + 1, :]
    act = (conv * jax.nn.sigmoid(conv) * up).astype(_BF16)
    x3 = x + _dot(act, wout_ref[...])
    o_ref[0] = _rms(x3, gfin_ref[...])
    gate_ref[:halo, :] = gate_ref[tm:, :]


def _conv_ffn(x, g_ffn, w_ffn_in, conv_w, conv_b, w_ffn_out, g_final):
    b, s, d = x.shape
    tok = pl.BlockSpec((1, TM_FFN, d), lambda bi, i: (bi, i, 0))
    return pl.pallas_call(
        _conv_ffn_kernel,
        out_shape=jax.ShapeDtypeStruct((b, s, d), _F32),
        grid=(b, s // TM_FFN),
        in_specs=[tok, _const_spec((1, d)), _const_spec(w_ffn_in.shape), _const_spec(conv_w.shape),
                  _const_spec((1, D_FF)), _const_spec(w_ffn_out.shape), _const_spec((1, d))],
        out_specs=tok,
        scratch_shapes=[pltpu.VMEM((TM_FFN + SUBLANES, D_FF), _F32)],
        compiler_params=_params(("arbitrary", "arbitrary")),
        name="conv_ffn",
    )(x, g_ffn.reshape(1, d), w_ffn_in, conv_w, conv_b.reshape(1, D_FF), w_ffn_out, g_final.reshape(1, d))


def kernel(x, mem, positions, g_mix, w_in, g_q_lat, w_uq, g_kv_lat, w_ukv, w_out, g_xattn, g_mem, w_xq,
           w_xkv, w_xo, g_ffn, w_ffn_in, conv_w, conv_b, w_ffn_out, g_final):
    assert w_in.shape[0] == 1, "one layer supported"
    l = 0
    cos_t, sin_t = _rope_tables(positions)
    win_p, wuq_p, wuk_p, wuvt_p = _permute_in_weights(w_in[l], w_uq[l], w_ukv[l])
    q, k, vt, rq, rk, rv, rg = _in_proj(x, g_mix[l], win_p, g_q_lat[l], wuq_p, g_kv_lat[l], wuk_p, wuvt_p,
                                        cos_t, sin_t)
    y_mla = _mla_attn(q, k, vt)
    y_ret = _retention(rq, rk, rv, rg)
    kv_mem = _mem_kv(mem, g_mem[l], w_xkv[l].astype(_BF16))
    x = _mix_xattn(x, y_mla, y_ret, w_out[l].astype(_BF16), g_xattn[l], w_xq[l].astype(_BF16), kv_mem,
                   w_xo[l].astype(_BF16))
    return _conv_ffn(x, g_ffn[l], w_ffn_in[l].astype(_BF16), conv_w[l], conv_b[l], w_ffn_out[l].astype(_BF16),
                     g_final)
```

```python
import functools
import math

import jax
import jax.numpy as jnp
from jax import lax
from jax.experimental import pallas as pl
from jax.experimental.pallas import tpu as pltpu

D_MODEL = 1024
EPS = 1e-6
ROPE_BASE = 10000.0
N_MLA_HEADS = 8
QK_NOPE = 64
QK_ROPE = 32
V_HEAD = 64
Q_LORA = 256
KV_LORA = 128
N_RET_HEADS = 4
RET_DK = 128
RET_DV = 128
RET_CHUNK = 128
MLA_WIDTH = N_MLA_HEADS * V_HEAD
RET_WIDTH = N_RET_HEADS * RET_DV
N_XATTN_HEADS = 4
XATTN_HEAD = D_MODEL // N_XATTN_HEADS
D_FF = 2816
CONV_W = 3

LANES = 128
SUBLANES = 8
VMEM_LIMIT = 56 * 1024 * 1024

HEAD_PAD = LANES
ROPE_HALF = QK_ROPE // 2
MLA_QK_WIDTH = N_MLA_HEADS * HEAD_PAD
IN_LAT = Q_LORA + KV_LORA + 2 * HEAD_PAD

TM_TABLE = 1024
TM_IN = 512
TQ_MLA = 256
TS_RET = 512
TM_MIX = 512
TM_FFN = 256

_BF16 = jnp.bfloat16
_F32 = jnp.float32


def _dot(a, b):
    return jnp.dot(a, b, preferred_element_type=_F32)


def _dot_nt(a, b):
    return lax.dot_general(a, b, (((1,), (1,)), ((), ())), preferred_element_type=_F32)


def _dot_tn(a, b):
    return lax.dot_general(a, b, (((0,), (0,)), ((), ())), preferred_element_type=_F32)


def _rms(x, g):
    inv = lax.rsqrt(jnp.mean(x * x, axis=-1, keepdims=True) + EPS)
    return x * inv * g


def _const_spec(shape):
    nd = len(shape)
    return pl.BlockSpec(shape, lambda *_: (0,) * nd, pipeline_mode=pl.Buffered(1))


def _params(sem):
    return pltpu.CompilerParams(dimension_semantics=sem, vmem_limit_bytes=VMEM_LIMIT)


def _rope_table_kernel(pos_ref, inv_ref, cos_ref, sin_ref):
    for r in range(TM_TABLE // LANES):
        col = jnp.broadcast_to(pos_ref[r:r + 1, :], (LANES, LANES)).T
        ang = col * inv_ref[...]
        cos_ref[r * LANES:(r + 1) * LANES, :] = jnp.cos(ang)
        sin_ref[r * LANES:(r + 1) * LANES, :] = jnp.sin(ang)


def _rope_tables(positions):
    t = positions.size
    pos = positions.astype(_F32).reshape(t // LANES, LANES)
    f_ret = 1.0 / (ROPE_BASE ** (jnp.arange(0, RET_DK, 2, dtype=_F32) / RET_DK))
    f_mla = 1.0 / (ROPE_BASE ** (jnp.arange(0, QK_ROPE, 2, dtype=_F32) / QK_ROPE))
    inv = jnp.concatenate([f_ret, f_mla, f_mla, jnp.zeros((HEAD_PAD - QK_NOPE - QK_ROPE,), _F32)])
    return pl.pallas_call(
        _rope_table_kernel,
        out_shape=(jax.ShapeDtypeStruct((t, LANES), _F32), jax.ShapeDtypeStruct((t, LANES), _F32)),
        grid=(t // TM_TABLE,),
        in_specs=[pl.BlockSpec((TM_TABLE // LANES, LANES), lambda i: (i, 0)),
                  pl.BlockSpec((1, LANES), lambda i: (0, 0))],
        out_specs=(pl.BlockSpec((TM_TABLE, LANES), lambda i: (i, 0)),
                   pl.BlockSpec((TM_TABLE, LANES), lambda i: (i, 0))),
        compiler_params=_params(("arbitrary",)),
        name="rope_tables",
    )(pos, inv.reshape(1, LANES))


def _in_proj_kernel(x_ref, gmix_ref, wlat_ref, wret_ref, gq_ref, wuq_ref, gkv_ref, wuk_ref, wuvt_ref, cos_ref, sin_ref,
                    q_ref, k_ref, vt_ref, rq_ref, rk_ref, rv_ref, rg_ref):
    h = _rms(x_ref[0], gmix_ref[...]).astype(_BF16)
    t_cos, t_sin = cos_ref[...], sin_ref[...]
    lane = lax.broadcasted_iota(jnp.int32, t_cos.shape, 1)
    low = lane < RET_DK // 2
    cos_r = jnp.where(low, t_cos, pltpu.roll(t_cos, RET_DK // 2, axis=1))
    sin_r = jnp.where(low, -t_sin, pltpu.roll(t_sin, RET_DK // 2, axis=1))
    cos_m = jnp.where(low, 1.0, t_cos)
    sin_m = jnp.where(low, 0.0, jnp.where(lane < QK_NOPE + ROPE_HALF, -t_sin, t_sin))

    lat = _dot(h, wlat_ref[...])
    c_q = lat[:, :Q_LORA]
    c_kv = lat[:, Q_LORA:Q_LORA + KV_LORA]
    kr = lat[:, Q_LORA + KV_LORA:Q_LORA + KV_LORA + HEAD_PAD]
    kr_sw = lat[:, Q_LORA + KV_LORA + HEAD_PAD:IN_LAT]
    k_rope = kr * cos_m + kr_sw * sin_m

    cqn = _rms(c_q, gq_ref[...]).astype(_BF16)
    q = _dot(cqn, wuq_ref[:, :MLA_QK_WIDTH])
    q_sw = _dot(cqn, wuq_ref[:, MLA_QK_WIDTH:])
    scale = math.log2(math.e) / math.sqrt(QK_NOPE + QK_ROPE)
    for hd in range(N_MLA_HEADS):
        sl = slice(hd * HEAD_PAD, (hd + 1) * HEAD_PAD)
        q_ref[0, :, sl] = ((q[:, sl] * cos_m + q_sw[:, sl] * sin_m) * scale).astype(_BF16)

    ckvn = _rms(c_kv, gkv_ref[...]).astype(_BF16)
    k_nope = _dot(ckvn, wuk_ref[...])
    for hd in range(N_MLA_HEADS):
        sl = slice(hd * HEAD_PAD, (hd + 1) * HEAD_PAD)
        k_ref[0, :, sl] = (k_nope[:, sl] + k_rope).astype(_BF16)
    v_t = _dot_nt(wuvt_ref[...], ckvn).astype(_BF16)
    for j in range(TM_IN // TQ_MLA):
        vt_ref[0, j] = v_t[:, j * TQ_MLA:(j + 1) * TQ_MLA]

    def ret_rope(col0, out_ref, mult):
        r = _dot(h, wret_ref[:, col0:col0 + RET_WIDTH])
        for hd in range(N_RET_HEADS):
            sl = slice(hd * RET_DK, (hd + 1) * RET_DK)
            rh = r[:, sl]
            roped = rh * cos_r + pltpu.roll(rh, RET_DK // 2, axis=1) * sin_r
            if mult is not None:
                roped = roped * mult
            out_ref[0, :, sl] = roped.astype(_BF16)

    ret_rope(0, rq_ref, None)
    ret_rope(RET_WIDTH, rk_ref, RET_DK ** -0.5)
    rv_ref[0] = _dot(h, wret_ref[:, 2 * RET_WIDTH:3 * RET_WIDTH]).astype(_BF16)
    rg_ref[0] = _dot(h, wret_ref[:, 3 * RET_WIDTH:]).astype(_BF16)


def _permute_in_weights(w_in, w_uq, w_ukv):
    d = w_in.shape[0]
    o_kr = Q_LORA + KV_LORA
    x1 = w_in[:, o_kr:o_kr + ROPE_HALF]
    x2 = w_in[:, o_kr + ROPE_HALF:o_kr + QK_ROPE]
    zl = jnp.zeros((d, QK_NOPE), w_in.dtype)
    zr = jnp.zeros((d, HEAD_PAD - QK_NOPE - QK_ROPE), w_in.dtype)
    wlat_p = jnp.concatenate([w_in[:, :o_kr], zl, x1, x2, zr, zl, x2, x1, zr], axis=1).astype(_BF16)
    wret_p = w_in[:, o_kr + QK_ROPE:].astype(_BF16)

    wq = w_uq.reshape(Q_LORA, N_MLA_HEADS, QK_NOPE + QK_ROPE)
    nope, r1, r2 = wq[..., :QK_NOPE], wq[..., QK_NOPE:QK_NOPE + ROPE_HALF], wq[..., QK_NOPE + ROPE_HALF:]
    z32 = jnp.zeros((Q_LORA, N_MLA_HEADS, HEAD_PAD - QK_NOPE - QK_ROPE), w_uq.dtype)
    z64 = jnp.zeros((Q_LORA, N_MLA_HEADS, QK_NOPE), w_uq.dtype)
    main = jnp.concatenate([nope, r1, r2, z32], axis=-1).reshape(Q_LORA, MLA_QK_WIDTH)
    swap = jnp.concatenate([z64, r2, r1, z32], axis=-1).reshape(Q_LORA, MLA_QK_WIDTH)
    wuq_p = jnp.concatenate([main, swap], axis=1).astype(_BF16)

    wkv = w_ukv.reshape(KV_LORA, N_MLA_HEADS, QK_NOPE + V_HEAD)
    zk = jnp.zeros((KV_LORA, N_MLA_HEADS, HEAD_PAD - QK_NOPE), w_ukv.dtype)
    wk = jnp.concatenate([wkv[..., :QK_NOPE], zk], axis=-1).reshape(KV_LORA, MLA_QK_WIDTH)
    wv_t = wkv[..., QK_NOPE:].reshape(KV_LORA, MLA_WIDTH).T
    return wlat_p, wret_p, wuq_p, wk.astype(_BF16), wv_t.astype(_BF16)


def _in_proj(x, g_mix, wlat_p, wret_p, g_q_lat, wuq_p, g_kv_lat, wuk_p, wuvt_p, cos_t, sin_t):
    b, s, d = x.shape
    nt = s // TM_IN
    per = TM_IN // TQ_MLA
    tok = lambda w: pl.BlockSpec((1, TM_IN, w), lambda bi, i: (bi, i, 0))
    tab = pl.BlockSpec((TM_IN, LANES), lambda bi, i: (bi * nt + i, 0))
    bf = lambda w: jax.ShapeDtypeStruct((b, s, w), _BF16)
    return pl.pallas_call(
        _in_proj_kernel,
        out_shape=(bf(MLA_QK_WIDTH), bf(MLA_QK_WIDTH),
                   jax.ShapeDtypeStruct((b, s // TQ_MLA, MLA_WIDTH, TQ_MLA), _BF16),
                   bf(RET_WIDTH), bf(RET_WIDTH), bf(RET_WIDTH), bf(RET_WIDTH)),
        grid=(b, nt),
        in_specs=[tok(d), _const_spec((1, d)), _const_spec(wlat_p.shape), _const_spec(wret_p.shape),
                  _const_spec((1, Q_LORA)),
                  _const_spec(wuq_p.shape), _const_spec((1, KV_LORA)), _const_spec(wuk_p.shape),
                  _const_spec(wuvt_p.shape), tab, tab],
        out_specs=(tok(MLA_QK_WIDTH), tok(MLA_QK_WIDTH),
                   pl.BlockSpec((1, per, MLA_WIDTH, TQ_MLA), lambda bi, i: (bi, i, 0, 0)),
                   tok(RET_WIDTH), tok(RET_WIDTH), tok(RET_WIDTH), tok(RET_WIDTH)),
        compiler_params=_params(("arbitrary", "arbitrary")),
        name="in_proj",
    )(x, g_mix.reshape(1, d), wlat_p, wret_p, g_q_lat.reshape(1, Q_LORA), wuq_p, g_kv_lat.reshape(1, KV_LORA),
      wuk_p, wuvt_p, cos_t, sin_t)


_MASK_VALUE = -0.7 * float(jnp.finfo(jnp.float32).max)
_MLA_AHEAD = 4
_MLA_TILES_PER_ITER = 5
_MLA_DENOM_ROWS = 16


def _mla_attn_kernel(qlo_ref, qhi_ref, k_ref, vt_ref, o_ref, q_ref, s_ref, m_ref, acc_ref):
    tq = TQ_MLA
    n_tiles = k_ref.shape[1] // tq
    lo = pl.program_id(1)
    hi = n_tiles - 1 - lo
    q_ref[0] = qlo_ref[0]
    q_ref[1] = qhi_ref[0]
    m_ref[...] = jnp.full(m_ref.shape, _MASK_VALUE, _F32)
    acc_ref[...] = jnp.zeros(acc_ref.shape, _F32)
    ones_rows = jnp.ones((_MLA_DENOM_ROWS, tq), _BF16)

    def step_args(t):
        sel = (t > lo).astype(jnp.int32)
        return sel, t - 1 - sel * lo

    def scores(sel, kk, hd):
        sl = slice(hd * HEAD_PAD, (hd + 1) * HEAD_PAD)
        ks = pl.multiple_of(kk * tq, tq)
        s_ref[hd] = _dot_nt(k_ref[0, pl.ds(ks, tq), sl], q_ref[sel, :, sl])

    def softmax(sel, hd, keep):
        row = slice(hd, hd + 1)

        def st():
            s = s_ref[hd]
            return s if keep is None else jnp.where(keep, s, _MASK_VALUE)

        m_prev = m_ref[sel, row, :]
        m_next = jnp.maximum(m_prev, jnp.max(st(), axis=0, keepdims=True))
        m_ref[sel, row, :] = m_next
        return jnp.exp2(m_prev - m_next), jnp.exp2(st() - m_next).astype(_BF16)

    def accumulate(sel, kk, hd, alpha, p):
        vt = jnp.concatenate([vt_ref[0, kk, hd * V_HEAD:(hd + 1) * V_HEAD, :], ones_rows], axis=0)
        acc_ref[sel, hd] = acc_ref[sel, hd] * alpha + _dot(vt, p)

    def tile(sel, kk, diagonal, nxt):
        keep = None
        if diagonal:
            key = lax.broadcasted_iota(jnp.int32, (tq, tq), 0)
            qry = lax.broadcasted_iota(jnp.int32, (tq, tq), 1)
            keep = key <= qry
        pending = None
        for hd in range(N_MLA_HEADS):
            ahead = hd + _MLA_AHEAD
            if ahead < N_MLA_HEADS:
                scores(sel, kk, ahead)
            elif nxt is not None:
                scores(nxt[0], nxt[1], ahead - N_MLA_HEADS)
            current = softmax(sel, hd, keep)
            if pending is not None:
                accumulate(sel, kk, hd - 1, *pending)
            pending = current
        accumulate(sel, kk, N_MLA_HEADS - 1, *pending)

    for hd in range(_MLA_AHEAD):
        scores(0, lo, hd)
    tile(0, lo, True, step_args(1))

    def body(it, carry):
        for u in range(_MLA_TILES_PER_ITER):
            t = 1 + it * _MLA_TILES_PER_ITER + u
            last = t + 1 == n_tiles
            sel_n, kk_n = step_args(t + 1)
            tile(*step_args(t), False, (jnp.where(last, 1, sel_n), jnp.where(last, hi, kk_n)))
        return carry

    lax.fori_loop(0, (n_tiles - 1) // _MLA_TILES_PER_ITER, body, 0)
    tile(1, hi, True, None)

    for sel, qt in ((0, lo), (1, hi)):
        out_t = jnp.concatenate([acc_ref[sel, hd, :V_HEAD, :] / acc_ref[sel, hd, V_HEAD:V_HEAD + 1, :]
                                 for hd in range(N_MLA_HEADS)], axis=0)
        o_ref[0, pl.ds(pl.multiple_of(qt * tq, tq), tq), :] = out_t.T.astype(o_ref.dtype)


def _mla_attn(q, k, vt):
    b, s, w = q.shape
    nk = s // TQ_MLA
    assert nk % 2 == 0 and (nk - 1) % _MLA_TILES_PER_ITER == 0
    return pl.pallas_call(
        _mla_attn_kernel,
        out_shape=jax.ShapeDtypeStruct((b, s, MLA_WIDTH), _BF16),
        grid=(b, nk // 2),
        in_specs=[pl.BlockSpec((1, TQ_MLA, w), lambda bi, i: (bi, i, 0)),
                  pl.BlockSpec((1, TQ_MLA, w), lambda bi, i: (bi, nk - 1 - i, 0)),
                  pl.BlockSpec((1, s, w), lambda bi, i: (bi, 0, 0)),
                  pl.BlockSpec((1, nk, MLA_WIDTH, TQ_MLA), lambda bi, i: (bi, 0, 0, 0))],
        out_specs=pl.BlockSpec((1, s, MLA_WIDTH), lambda bi, i: (bi, 0, 0)),
        scratch_shapes=[pltpu.VMEM((2, TQ_MLA, w), _BF16),
                        pltpu.VMEM((N_MLA_HEADS, TQ_MLA, TQ_MLA), _F32),
                        pltpu.VMEM((2, N_MLA_HEADS, TQ_MLA), _F32),
                        pltpu.VMEM((2, N_MLA_HEADS, V_HEAD + _MLA_DENOM_ROWS, TQ_MLA), _F32)],
        compiler_params=_params(("arbitrary", "arbitrary")),
        name="mla_attn",
    )(q, q, k, vt)


def _retention_kernel(q_ref, k_ref, v_ref, g_ref, intra_ref, kend_ref, qstart_ref, decay_ref,
                      o_ref, state_ref):
    @pl.when(pl.program_id(1) == 0)
    def _():
        state_ref[...] = jnp.zeros(state_ref.shape, _F32)

    L = RET_CHUNK
    units = [(c, hd) for c in range(TS_RET // L) for hd in range(N_RET_HEADS)]
    rows = lambda c: slice(c * L, (c + 1) * L)
    cols = lambda hd: slice(hd * RET_DK, (hd + 1) * RET_DK)
    scores, chunk_kv = {}, {}
    for c, hd in units:
        scores[c, hd] = _dot_nt(q_ref[0, rows(c), cols(hd)], k_ref[0, rows(c), cols(hd)])
    for c, hd in units:
        v_dec = (v_ref[0, rows(c), cols(hd)].astype(_F32) * kend_ref[hd]).astype(_BF16)
        chunk_kv[c, hd] = _dot_tn(k_ref[0, rows(c), cols(hd)], v_dec)
    prev_state = {}
    for hd in range(N_RET_HEADS):
        state = state_ref[hd]
        for c in range(TS_RET // L):
            prev_state[c, hd] = state.astype(_BF16)
            state = decay_ref[hd] * state + chunk_kv[c, hd]
        state_ref[hd] = state
    inner, cross = {}, {}
    for c, hd in units:
        inner[c, hd] = _dot((scores[c, hd] * intra_ref[hd]).astype(_BF16), v_ref[0, rows(c), cols(hd)])
    for c, hd in units:
        cross[c, hd] = _dot(q_ref[0, rows(c), cols(hd)], prev_state[c, hd])
    for c, hd in units:
        out = inner[c, hd] + cross[c, hd] * qstart_ref[hd]
        mu = jnp.mean(out, axis=-1, keepdims=True)
        cen = out - mu
        var = jnp.mean(cen * cen, axis=-1, keepdims=True)
        normed = cen * lax.rsqrt(var + EPS)
        g = g_ref[0, rows(c), cols(hd)].astype(_F32)
        o_ref[0, rows(c), cols(hd)] = (normed * (g * jax.nn.sigmoid(g))).astype(o_ref.dtype)


def _retention_tables():
    h, L = N_RET_HEADS, RET_CHUNK
    log_gamma = jnp.log(1.0 - 2.0 ** (-5.0 - jnp.arange(h, dtype=_F32)))
    j = jnp.arange(L, dtype=_F32)
    diff = j[:, None] - j[None, :]
    intra = jnp.where(diff[None] >= 0,
                      jnp.exp(jnp.maximum(diff, 0.0)[None] * log_gamma[:, None, None]), 0.0)
    rowb = lambda t: jnp.broadcast_to(t.T[:, :, None], (h, L, LANES))
    k_to_end = jnp.exp((L - 1 - j)[:, None] * log_gamma[None, :])
    q_from_start = jnp.exp((j + 1)[:, None] * log_gamma[None, :])
    chunk_decay = jnp.broadcast_to(jnp.exp(L * log_gamma)[:, None, None], (h, RET_DK, RET_DV))
    return intra, rowb(k_to_end), rowb(q_from_start), chunk_decay


def _retention(rq, rk, rv, rg):
    b, s, w = rq.shape
    tok = pl.BlockSpec((1, TS_RET, w), lambda bi, i: (bi, i, 0))
    tables = _retention_tables()
    return pl.pallas_call(
        _retention_kernel,
        out_shape=jax.ShapeDtypeStruct((b, s, w), _BF16),
        grid=(b, s // TS_RET),
        in_specs=[tok, tok, tok, tok] + [_const_spec(t.shape) for t in tables],
        out_specs=tok,
        scratch_shapes=[pltpu.VMEM((N_RET_HEADS, RET_DK, RET_DV), _F32)],
        compiler_params=_params(("arbitrary", "arbitrary")),
        name="retention",
    )(rq, rk, rv, rg, *tables)


def _mem_kv_kernel(mem_ref, g_ref, w_ref, kv_ref):
    mem_n = _rms(mem_ref[0], g_ref[...]).astype(_BF16)
    kv_ref[0] = _dot(mem_n, w_ref[...]).astype(_BF16)


def _mem_kv(mem, g_mem, w_xkv):
    b, m, d = mem.shape
    n = w_xkv.shape[1]
    return pl.pallas_call(
        _mem_kv_kernel,
        out_shape=jax.ShapeDtypeStruct((b, m, n), _BF16),
        grid=(b,),
        in_specs=[pl.BlockSpec((1, m, d), lambda bi: (bi, 0, 0)), _const_spec((1, d)), _const_spec((d, n))],
        out_specs=pl.BlockSpec((1, m, n), lambda bi: (bi, 0, 0)),
        compiler_params=_params(("arbitrary",)),
        name="mem_kv",
    )(mem, g_mem.reshape(1, d), w_xkv)


def _mix_xattn_kernel(x_ref, ymla_ref, yret_ref, wout_ref, gx_ref, wxq_ref, kv_ref, wxo_ref, o_ref):
    x1 = x_ref[0] + _dot(ymla_ref[0], wout_ref[:MLA_WIDTH, :]) + _dot(yret_ref[0], wout_ref[MLA_WIDTH:, :])
    h = _rms(x1, gx_ref[...]).astype(_BF16)
    q = (_dot(h, wxq_ref[...]) * (1.0 / math.sqrt(XATTN_HEAD))).astype(_BF16)
    cols = lambda hd: slice(hd * XATTN_HEAD, (hd + 1) * XATTN_HEAD)
    vcols = lambda hd: slice(D_MODEL + hd * XATTN_HEAD, D_MODEL + (hd + 1) * XATTN_HEAD)
    scores = [_dot_nt(q[:, cols(hd)], kv_ref[0, :, cols(hd)]) for hd in range(N_XATTN_HEADS)]
    probs = [jnp.exp(s - jnp.max(s, axis=-1, keepdims=True)) for s in scores]
    heads = [(_dot(p.astype(_BF16), kv_ref[0, :, vcols(hd)]) / jnp.sum(p, axis=-1, keepdims=True)).astype(_BF16)
             for hd, p in enumerate(probs)]
    o_ref[0] = x1 + _dot(jnp.concatenate(heads, axis=1), wxo_ref[...])


def _mix_xattn(x, y_mla, y_ret, w_out, g_xattn, w_xq, kv_mem, w_xo):
    b, s, d = x.shape
    m, n = kv_mem.shape[1:]
    tok = lambda w: pl.BlockSpec((1, TM_MIX, w), lambda bi, i: (bi, i, 0))
    return pl.pallas_call(
        _mix_xattn_kernel,
        out_shape=jax.ShapeDtypeStruct((b, s, d), _F32),
        grid=(b, s // TM_MIX),
        in_specs=[tok(d), tok(MLA_WIDTH), tok(RET_WIDTH), _const_spec(w_out.shape), _const_spec((1, d)),
                  _const_spec(w_xq.shape), pl.BlockSpec((1, m, n), lambda bi, i: (bi, 0, 0)),
                  _const_spec(w_xo.shape)],
        out_specs=tok(d),
        compiler_params=_params(("arbitrary", "arbitrary")),
        name="mix_xattn",
    )(x, y_mla, y_ret, w_out, g_xattn.reshape(1, d), w_xq, kv_mem, w_xo)


def _conv_ffn_kernel(x_ref, g_ref, win_ref, cw_ref, cb_ref, wout_ref, gfin_ref, o_ref, gate_ref):
    tm = TM_FFN
    halo = SUBLANES

    @pl.when(pl.program_id(1) == 0)
    def _():
        gate_ref[:halo, :] = jnp.zeros((halo, D_FF), _F32)

    x = x_ref[0]
    h = _rms(x, g_ref[...]).astype(_BF16)
    gate = _dot(h, win_ref[:, :D_FF])
    up = _dot(h, win_ref[:, D_FF:])
    gate_ref[halo:, :] = gate
    conv = cb_ref[...] + gate * cw_ref[CONV_W - 1:CONV_W, :]
    for tap in range(CONV_W - 1):
        back = CONV_W - 1 - tap
        conv = conv + gate_ref[halo - back:halo - back + tm, :] * cw_ref[tap:tap + 1, :]
    act = (conv * jax.nn.sigmoid(conv) * up).astype(_BF16)
    x3 = x + _dot(act, wout_ref[...])
    o_ref[0] = _rms(x3, gfin_ref[...])
    gate_ref[:halo, :] = gate_ref[tm:, :]


def _conv_ffn(x, g_ffn, w_ffn_in, conv_w, conv_b, w_ffn_out, g_final):
    b, s, d = x.shape
    tok = pl.BlockSpec((1, TM_FFN, d), lambda bi, i: (bi, i, 0))
    return pl.pallas_call(
        _conv_ffn_kernel,
        out_shape=jax.ShapeDtypeStruct((b, s, d), _F32),
        grid=(b, s // TM_FFN),
        in_specs=[tok, _const_spec((1, d)), _const_spec(w_ffn_in.shape), _const_spec(conv_w.shape),
                  _const_spec((1, D_FF)), _const_spec(w_ffn_out.shape), _const_spec((1, d))],
        out_specs=tok,
        scratch_shapes=[pltpu.VMEM((TM_FFN + SUBLANES, D_FF), _F32)],
        compiler_params=_params(("arbitrary", "arbitrary")),
        name="conv_ffn",
    )(x, g_ffn.reshape(1, d), w_ffn_in, conv_w, conv_b.reshape(1, D_FF), w_ffn_out, g_final.reshape(1, d))


def kernel(x, mem, positions, g_mix, w_in, g_q_lat, w_uq, g_kv_lat, w_ukv, w_out, g_xattn, g_mem, w_xq,
           w_xkv, w_xo, g_ffn, w_ffn_in, conv_w, conv_b, w_ffn_out, g_final):
    assert w_in.shape[0] == 1, "one layer supported"
    l = 0
    cos_t, sin_t = _rope_tables(positions)
    wlat_p, wret_p, wuq_p, wuk_p, wuvt_p = _permute_in_weights(w_in[l], w_uq[l], w_ukv[l])
    q, k, vt, rq, rk, rv, rg = _in_proj(x, g_mix[l], wlat_p, wret_p, g_q_lat[l], wuq_p, g_kv_lat[l], wuk_p, wuvt_p,
                                        cos_t, sin_t)
    y_mla = _mla_attn(q, k, vt)
    y_ret = _retention(rq, rk, rv, rg)
    kv_mem = _mem_kv(mem, g_mem[l], w_xkv[l].astype(_BF16))
    x = _mix_xattn(x, y_mla, y_ret, w_out[l].astype(_BF16), g_xattn[l], w_xq[l].astype(_BF16), kv_mem,
                   w_xo[l].astype(_BF16))
    return _conv_ffn(x, g_ffn[l], w_ffn_in[l].astype(_BF16), conv_w[l], conv_b[l], w_ffn_out[l].astype(_BF16),
                     g_final)
```

```python
import functools
import math

import jax
import jax.numpy as jnp
from jax import lax
from jax.experimental import pallas as pl
from jax.experimental.pallas import tpu as pltpu

D_MODEL = 1024
EPS = 1e-6
ROPE_BASE = 10000.0
N_MLA_HEADS = 8
QK_NOPE = 64
QK_ROPE = 32
V_HEAD = 64
Q_LORA = 256
KV_LORA = 128
N_RET_HEADS = 4
RET_DK = 128
RET_DV = 128
RET_CHUNK = 128
MLA_WIDTH = N_MLA_HEADS * V_HEAD
RET_WIDTH = N_RET_HEADS * RET_DV
N_XATTN_HEADS = 4
XATTN_HEAD = D_MODEL // N_XATTN_HEADS
D_FF = 2816
CONV_W = 3

LANES = 128
SUBLANES = 8
VMEM_LIMIT = 56 * 1024 * 1024

HEAD_PAD = LANES
ROPE_HALF = QK_ROPE // 2
MLA_QK_WIDTH = N_MLA_HEADS * HEAD_PAD
IN_LAT = Q_LORA + KV_LORA + HEAD_PAD

TM_TABLE = 1024
TM_IN = 512
TQ_MLA = 256
TS_RET = 512
TM_MIX = 1024
TM_FFN = 512

_BF16 = jnp.bfloat16
_F32 = jnp.float32


def _dot(a, b):
    return jnp.dot(a, b, preferred_element_type=_F32)


def _dot_nt(a, b):
    return lax.dot_general(a, b, (((1,), (1,)), ((), ())), preferred_element_type=_F32)


def _dot_tn(a, b):
    return lax.dot_general(a, b, (((0,), (0,)), ((), ())), preferred_element_type=_F32)


def _rms(x, g):
    inv = lax.rsqrt(jnp.mean(x * x, axis=-1, keepdims=True) + EPS)
    return x * inv * g


def _const_spec(shape):
    nd = len(shape)
    return pl.BlockSpec(shape, lambda *_: (0,) * nd, pipeline_mode=pl.Buffered(1))


def _params(sem):
    return pltpu.CompilerParams(dimension_semantics=sem, vmem_limit_bytes=VMEM_LIMIT)


def _rope_table_kernel(pos_ref, inv_ref, cos_ref, sin_ref):
    for r in range(TM_TABLE // LANES):
        col = jnp.broadcast_to(pos_ref[r:r + 1, :], (LANES, LANES)).T
        ang = col * inv_ref[...]
        cos_ref[r * LANES:(r + 1) * LANES, :] = jnp.cos(ang)
        sin_ref[r * LANES:(r + 1) * LANES, :] = jnp.sin(ang)


def _rope_tables(positions):
    t = positions.size
    pos = positions.astype(_F32).reshape(t // LANES, LANES)
    f_ret = 1.0 / (ROPE_BASE ** (jnp.arange(0, RET_DK, 2, dtype=_F32) / RET_DK))
    f_mla = 1.0 / (ROPE_BASE ** (jnp.arange(0, QK_ROPE, 2, dtype=_F32) / QK_ROPE))
    inv = jnp.concatenate([f_ret, f_mla, f_mla, jnp.zeros((HEAD_PAD - QK_NOPE - QK_ROPE,), _F32)])
    return pl.pallas_call(
        _rope_table_kernel,
        out_shape=(jax.ShapeDtypeStruct((t, LANES), _F32), jax.ShapeDtypeStruct((t, LANES), _F32)),
        grid=(t // TM_TABLE,),
        in_specs=[pl.BlockSpec((TM_TABLE // LANES, LANES), lambda i: (i, 0)),
                  pl.BlockSpec((1, LANES), lambda i: (0, 0))],
        out_specs=(pl.BlockSpec((TM_TABLE, LANES), lambda i: (i, 0)),
                   pl.BlockSpec((TM_TABLE, LANES), lambda i: (i, 0))),
        compiler_params=_params(("arbitrary",)),
        name="rope_tables",
    )(pos, inv.reshape(1, LANES))


def _in_proj_kernel(x_ref, gmix_ref, wlat_ref, wret_ref, gq_ref, wuq_ref, gkv_ref, wuk_ref, wuvt_ref, cos_ref, sin_ref,
                    q_ref, k_ref, vt_ref, rq_ref, rk_ref, rv_ref, rg_ref):
    h = _rms(x_ref[0], gmix_ref[...]).astype(_BF16)
    t_cos, t_sin = cos_ref[...], sin_ref[...]
    lane = lax.broadcasted_iota(jnp.int32, t_cos.shape, 1)
    low = lane < RET_DK // 2
    cos_r = jnp.where(low, t_cos, pltpu.roll(t_cos, RET_DK // 2, axis=1))
    sin_r = jnp.where(low, -t_sin, pltpu.roll(t_sin, RET_DK // 2, axis=1))
    cos_m = jnp.where(low, 1.0, t_cos)
    sin_m = jnp.where(low, 0.0, jnp.where(lane < QK_NOPE + ROPE_HALF, -t_sin, t_sin))

    first_half = lane < QK_NOPE + ROPE_HALF

    def swap_halves(t):
        return jnp.where(first_half, pltpu.roll(t, LANES - ROPE_HALF, axis=1), pltpu.roll(t, ROPE_HALF, axis=1))

    lat = _dot(h, wlat_ref[...])
    c_q = lat[:, :Q_LORA]
    c_kv = lat[:, Q_LORA:Q_LORA + KV_LORA]
    kr = lat[:, Q_LORA + KV_LORA:IN_LAT]
    k_rope = kr * cos_m + swap_halves(kr) * sin_m

    cqn = _rms(c_q, gq_ref[...]).astype(_BF16)
    q = _dot(cqn, wuq_ref[...])
    scale = math.log2(math.e) / math.sqrt(QK_NOPE + QK_ROPE)
    for hd in range(N_MLA_HEADS):
        sl = slice(hd * HEAD_PAD, (hd + 1) * HEAD_PAD)
        q_ref[0, :, sl] = ((q[:, sl] * cos_m + swap_halves(q[:, sl]) * sin_m) * scale).astype(_BF16)

    ckvn = _rms(c_kv, gkv_ref[...]).astype(_BF16)
    k_nope = _dot(ckvn, wuk_ref[...])
    for hd in range(N_MLA_HEADS):
        sl = slice(hd * HEAD_PAD, (hd + 1) * HEAD_PAD)
        k_ref[0, :, sl] = (k_nope[:, sl] + k_rope).astype(_BF16)
    v_t = _dot_nt(wuvt_ref[...], ckvn).astype(_BF16)
    for j in range(TM_IN // TQ_MLA):
        vt_ref[0, j] = v_t[:, j * TQ_MLA:(j + 1) * TQ_MLA]

    def ret_rope(col0, out_ref, mult):
        r = _dot(h, wret_ref[:, col0:col0 + RET_WIDTH])
        for hd in range(N_RET_HEADS):
            sl = slice(hd * RET_DK, (hd + 1) * RET_DK)
            rh = r[:, sl]
            roped = rh * cos_r + pltpu.roll(rh, RET_DK // 2, axis=1) * sin_r
            if mult is not None:
                roped = roped * mult
            out_ref[0, :, sl] = roped.astype(_BF16)

    ret_rope(0, rq_ref, None)
    ret_rope(RET_WIDTH, rk_ref, RET_DK ** -0.5)
    rv_ref[0] = _dot(h, wret_ref[:, 2 * RET_WIDTH:3 * RET_WIDTH]).astype(_BF16)
    rg_ref[0] = _dot(h, wret_ref[:, 3 * RET_WIDTH:]).astype(_BF16)


def _permute_in_weights(w_in, w_uq, w_ukv):
    d = w_in.shape[0]
    o_kr = Q_LORA + KV_LORA
    x1 = w_in[:, o_kr:o_kr + ROPE_HALF]
    x2 = w_in[:, o_kr + ROPE_HALF:o_kr + QK_ROPE]
    zl = jnp.zeros((d, QK_NOPE), w_in.dtype)
    zr = jnp.zeros((d, HEAD_PAD - QK_NOPE - QK_ROPE), w_in.dtype)
    wlat_p = jnp.concatenate([w_in[:, :o_kr], zl, x1, x2, zr], axis=1).astype(_BF16)
    wret_p = w_in[:, o_kr + QK_ROPE:].astype(_BF16)

    wq = w_uq.reshape(Q_LORA, N_MLA_HEADS, QK_NOPE + QK_ROPE)
    nope, r1, r2 = wq[..., :QK_NOPE], wq[..., QK_NOPE:QK_NOPE + ROPE_HALF], wq[..., QK_NOPE + ROPE_HALF:]
    z32 = jnp.zeros((Q_LORA, N_MLA_HEADS, HEAD_PAD - QK_NOPE - QK_ROPE), w_uq.dtype)
    wuq_p = jnp.concatenate([nope, r1, r2, z32], axis=-1).reshape(Q_LORA, MLA_QK_WIDTH).astype(_BF16)

    wkv = w_ukv.reshape(KV_LORA, N_MLA_HEADS, QK_NOPE + V_HEAD)
    zk = jnp.zeros((KV_LORA, N_MLA_HEADS, HEAD_PAD - QK_NOPE), w_ukv.dtype)
    wk = jnp.concatenate([wkv[..., :QK_NOPE], zk], axis=-1).reshape(KV_LORA, MLA_QK_WIDTH)
    wv_t = wkv[..., QK_NOPE:].reshape(KV_LORA, MLA_WIDTH).T
    return wlat_p, wret_p, wuq_p, wk.astype(_BF16), wv_t.astype(_BF16)


def _in_proj(x, g_mix, wlat_p, wret_p, g_q_lat, wuq_p, g_kv_lat, wuk_p, wuvt_p, cos_t, sin_t):
    b, s, d = x.shape
    nt = s // TM_IN
    per = TM_IN // TQ_MLA
    tok = lambda w: pl.BlockSpec((1, TM_IN, w), lambda bi, i: (bi, i, 0))
    tab = pl.BlockSpec((TM_IN, LANES), lambda bi, i: (bi * nt + i, 0))
    bf = lambda w: jax.ShapeDtypeStruct((b, s, w), _BF16)
    return pl.pallas_call(
        _in_proj_kernel,
        out_shape=(bf(MLA_QK_WIDTH), bf(MLA_QK_WIDTH),
                   jax.ShapeDtypeStruct((b, s // TQ_MLA, MLA_WIDTH, TQ_MLA), _BF16),
                   bf(RET_WIDTH), bf(RET_WIDTH), bf(RET_WIDTH), bf(RET_WIDTH)),
        grid=(b, nt),
        in_specs=[tok(d), _const_spec((1, d)), _const_spec(wlat_p.shape), _const_spec(wret_p.shape),
                  _const_spec((1, Q_LORA)),
                  _const_spec(wuq_p.shape), _const_spec((1, KV_LORA)), _const_spec(wuk_p.shape),
                  _const_spec(wuvt_p.shape), tab, tab],
        out_specs=(tok(MLA_QK_WIDTH), tok(MLA_QK_WIDTH),
                   pl.BlockSpec((1, per, MLA_WIDTH, TQ_MLA), lambda bi, i: (bi, i, 0, 0)),
                   tok(RET_WIDTH), tok(RET_WIDTH), tok(RET_WIDTH), tok(RET_WIDTH)),
        compiler_params=_params(("arbitrary", "arbitrary")),
        name="in_proj",
    )(x, g_mix.reshape(1, d), wlat_p, wret_p, g_q_lat.reshape(1, Q_LORA), wuq_p, g_kv_lat.reshape(1, KV_LORA),
      wuk_p, wuvt_p, cos_t, sin_t)


_MASK_VALUE = -0.7 * float(jnp.finfo(jnp.float32).max)
_MLA_AHEAD = 4
_MLA_TILES_PER_ITER = 5
_MLA_DENOM_ROWS = 16


def _mla_attn_kernel(qlo_ref, qhi_ref, k_ref, vt_ref, o_ref, q_ref, s_ref, m_ref, acc_ref):
    tq = TQ_MLA
    n_tiles = k_ref.shape[1] // tq
    lo = pl.program_id(1)
    hi = n_tiles - 1 - lo
    q_ref[0] = qlo_ref[0]
    q_ref[1] = qhi_ref[0]
    m_ref[...] = jnp.full(m_ref.shape, _MASK_VALUE, _F32)
    acc_ref[...] = jnp.zeros(acc_ref.shape, _F32)
    ones_rows = jnp.ones((_MLA_DENOM_ROWS, tq), _BF16)

    def step_args(t):
        sel = (t > lo).astype(jnp.int32)
        return sel, t - 1 - sel * lo

    def scores(sel, kk, hd):
        sl = slice(hd * HEAD_PAD, (hd + 1) * HEAD_PAD)
        ks = pl.multiple_of(kk * tq, tq)
        s_ref[hd] = _dot_nt(k_ref[0, pl.ds(ks, tq), sl], q_ref[sel, :, sl])

    def softmax(sel, hd, keep):
        row = slice(hd, hd + 1)

        def st():
            s = s_ref[hd]
            return s if keep is None else jnp.where(keep, s, _MASK_VALUE)

        m_prev = m_ref[sel, row, :]
        m_next = jnp.maximum(m_prev, jnp.max(st(), axis=0, keepdims=True))
        m_ref[sel, row, :] = m_next
        return jnp.exp2(m_prev - m_next), jnp.exp2(st() - m_next).astype(_BF16)

    def accumulate(sel, kk, hd, alpha, p):
        vt = jnp.concatenate([vt_ref[0, kk, hd * V_HEAD:(hd + 1) * V_HEAD, :], ones_rows], axis=0)
        acc_ref[sel, hd] = acc_ref[sel, hd] * alpha + _dot(vt, p)

    def tile(sel, kk, diagonal, nxt):
        keep = None
        if diagonal:
            key = lax.broadcasted_iota(jnp.int32, (tq, tq), 0)
            qry = lax.broadcasted_iota(jnp.int32, (tq, tq), 1)
            keep = key <= qry
        pending = None
        for hd in range(N_MLA_HEADS):
            ahead = hd + _MLA_AHEAD
            if ahead < N_MLA_HEADS:
                scores(sel, kk, ahead)
            elif nxt is not None:
                scores(nxt[0], nxt[1], ahead - N_MLA_HEADS)
            current = softmax(sel, hd, keep)
            if pending is not None:
                accumulate(sel, kk, hd - 1, *pending)
            pending = current
        accumulate(sel, kk, N_MLA_HEADS - 1, *pending)

    for hd in range(_MLA_AHEAD):
        scores(0, lo, hd)
    tile(0, lo, True, step_args(1))

    def body(it, carry):
        for u in range(_MLA_TILES_PER_ITER):
            t = 1 + it * _MLA_TILES_PER_ITER + u
            last = t + 1 == n_tiles
            sel_n, kk_n = step_args(t + 1)
            tile(*step_args(t), False, (jnp.where(last, 1, sel_n), jnp.where(last, hi, kk_n)))
        return carry

    lax.fori_loop(0, (n_tiles - 1) // _MLA_TILES_PER_ITER, body, 0)
    tile(1, hi, True, None)

    for sel, qt in ((0, lo), (1, hi)):
        out_t = jnp.concatenate([acc_ref[sel, hd, :V_HEAD, :] / acc_ref[sel, hd, V_HEAD:V_HEAD + 1, :]
                                 for hd in range(N_MLA_HEADS)], axis=0)
        o_ref[0, pl.ds(pl.multiple_of(qt * tq, tq), tq), :] = out_t.T.astype(o_ref.dtype)


def _mla_attn(q, k, vt):
    b, s, w = q.shape
    nk = s // TQ_MLA
    assert nk % 2 == 0 and (nk - 1) % _MLA_TILES_PER_ITER == 0
    return pl.pallas_call(
        _mla_attn_kernel,
        out_shape=jax.ShapeDtypeStruct((b, s, MLA_WIDTH), _BF16),
        grid=(b, nk // 2),
        in_specs=[pl.BlockSpec((1, TQ_MLA, w), lambda bi, i: (bi, i, 0)),
                  pl.BlockSpec((1, TQ_MLA, w), lambda bi, i: (bi, nk - 1 - i, 0)),
                  pl.BlockSpec((1, s, w), lambda bi, i: (bi, 0, 0)),
                  pl.BlockSpec((1, nk, MLA_WIDTH, TQ_MLA), lambda bi, i: (bi, 0, 0, 0))],
        out_specs=pl.BlockSpec((1, s, MLA_WIDTH), lambda bi, i: (bi, 0, 0)),
        scratch_shapes=[pltpu.VMEM((2, TQ_MLA, w), _BF16),
                        pltpu.VMEM((N_MLA_HEADS, TQ_MLA, TQ_MLA), _F32),
                        pltpu.VMEM((2, N_MLA_HEADS, TQ_MLA), _F32),
                        pltpu.VMEM((2, N_MLA_HEADS, V_HEAD + _MLA_DENOM_ROWS, TQ_MLA), _F32)],
        compiler_params=_params(("arbitrary", "arbitrary")),
        name="mla_attn",
    )(q, q, k, vt)


def _retention_kernel(q_ref, k_ref, v_ref, g_ref, intra_ref, kend_ref, qstart_ref, decay_ref,
                      o_ref, state_ref):
    @pl.when(pl.program_id(1) == 0)
    def _():
        state_ref[...] = jnp.zeros(state_ref.shape, _F32)

    L = RET_CHUNK
    units = [(c, hd) for c in range(TS_RET // L) for hd in range(N_RET_HEADS)]
    rows = lambda c: slice(c * L, (c + 1) * L)
    cols = lambda hd: slice(hd * RET_DK, (hd + 1) * RET_DK)
    scores, chunk_kv = {}, {}
    for c, hd in units:
        scores[c, hd] = _dot_nt(q_ref[0, rows(c), cols(hd)], k_ref[0, rows(c), cols(hd)])
    for c, hd in units:
        v_dec = (v_ref[0, rows(c), cols(hd)].astype(_F32) * kend_ref[hd]).astype(_BF16)
        chunk_kv[c, hd] = _dot_tn(k_ref[0, rows(c), cols(hd)], v_dec)
    prev_state = {}
    for hd in range(N_RET_HEADS):
        state = state_ref[hd]
        for c in range(TS_RET // L):
            prev_state[c, hd] = state.astype(_BF16)
            state = decay_ref[hd] * state + chunk_kv[c, hd]
        state_ref[hd] = state
    inner, cross = {}, {}
    for c, hd in units:
        inner[c, hd] = _dot((scores[c, hd] * intra_ref[hd]).astype(_BF16), v_ref[0, rows(c), cols(hd)])
    for c, hd in units:
        cross[c, hd] = _dot(q_ref[0, rows(c), cols(hd)], prev_state[c, hd])
    for c, hd in units:
        out = inner[c, hd] + cross[c, hd] * qstart_ref[hd]
        mu = jnp.mean(out, axis=-1, keepdims=True)
        cen = out - mu
        var = jnp.mean(cen * cen, axis=-1, keepdims=True)
        normed = cen * lax.rsqrt(var + EPS)
        g = g_ref[0, rows(c), cols(hd)].astype(_F32)
        o_ref[0, rows(c), cols(hd)] = (normed * (g * jax.nn.sigmoid(g))).astype(o_ref.dtype)


def _retention_tables():
    h, L = N_RET_HEADS, RET_CHUNK
    log_gamma = jnp.log(1.0 - 2.0 ** (-5.0 - jnp.arange(h, dtype=_F32)))
    j = jnp.arange(L, dtype=_F32)
    diff = j[:, None] - j[None, :]
    intra = jnp.where(diff[None] >= 0,
                      jnp.exp(jnp.maximum(diff, 0.0)[None] * log_gamma[:, None, None]), 0.0)
    rowb = lambda t: jnp.broadcast_to(t.T[:, :, None], (h, L, LANES))
    k_to_end = jnp.exp((L - 1 - j)[:, None] * log_gamma[None, :])
    q_from_start = jnp.exp((j + 1)[:, None] * log_gamma[None, :])
    chunk_decay = jnp.broadcast_to(jnp.exp(L * log_gamma)[:, None, None], (h, RET_DK, RET_DV))
    return intra, rowb(k_to_end), rowb(q_from_start), chunk_decay


def _retention(rq, rk, rv, rg):
    b, s, w = rq.shape
    tok = pl.BlockSpec((1, TS_RET, w), lambda bi, i: (bi, i, 0))
    tables = _retention_tables()
    return pl.pallas_call(
        _retention_kernel,
        out_shape=jax.ShapeDtypeStruct((b, s, w), _BF16),
        grid=(b, s // TS_RET),
        in_specs=[tok, tok, tok, tok] + [_const_spec(t.shape) for t in tables],
        out_specs=tok,
        scratch_shapes=[pltpu.VMEM((N_RET_HEADS, RET_DK, RET_DV), _F32)],
        compiler_params=_params(("arbitrary", "arbitrary")),
        name="retention",
    )(rq, rk, rv, rg, *tables)


def _mem_kv_kernel(mem_ref, g_ref, w_ref, kv_ref):
    mem_n = _rms(mem_ref[0], g_ref[...]).astype(_BF16)
    kv_ref[0] = _dot(mem_n, w_ref[...]).astype(_BF16)


def _mem_kv(mem, g_mem, w_xkv):
    b, m, d = mem.shape
    n = w_xkv.shape[1]
    return pl.pallas_call(
        _mem_kv_kernel,
        out_shape=jax.ShapeDtypeStruct((b, m, n), _BF16),
        grid=(b,),
        in_specs=[pl.BlockSpec((1, m, d), lambda bi: (bi, 0, 0)), _const_spec((1, d)), _const_spec((d, n))],
        out_specs=pl.BlockSpec((1, m, n), lambda bi: (bi, 0, 0)),
        compiler_params=_params(("arbitrary",)),
        name="mem_kv",
    )(mem, g_mem.reshape(1, d), w_xkv)


def _mix_xattn_kernel(x_ref, ymla_ref, yret_ref, wout_ref, gx_ref, wxq_ref, kv_ref, wxo_ref, o_ref):
    x1 = x_ref[0] + _dot(ymla_ref[0], wout_ref[:MLA_WIDTH, :]) + _dot(yret_ref[0], wout_ref[MLA_WIDTH:, :])
    h = _rms(x1, gx_ref[...]).astype(_BF16)
    q = (_dot(h, wxq_ref[...]) * (1.0 / math.sqrt(XATTN_HEAD))).astype(_BF16)
    cols = lambda hd: slice(hd * XATTN_HEAD, (hd + 1) * XATTN_HEAD)
    vcols = lambda hd: slice(D_MODEL + hd * XATTN_HEAD, D_MODEL + (hd + 1) * XATTN_HEAD)
    scores = [_dot_nt(q[:, cols(hd)], kv_ref[0, :, cols(hd)]) for hd in range(N_XATTN_HEADS)]
    probs = [jnp.exp(s - jnp.max(s, axis=-1, keepdims=True)) for s in scores]
    heads = [(_dot(p.astype(_BF16), kv_ref[0, :, vcols(hd)]) / jnp.sum(p, axis=-1, keepdims=True)).astype(_BF16)
             for hd, p in enumerate(probs)]
    o_ref[0] = x1 + _dot(jnp.concatenate(heads, axis=1), wxo_ref[...])


def _mix_xattn(x, y_mla, y_ret, w_out, g_xattn, w_xq, kv_mem, w_xo):
    b, s, d = x.shape
    m, n = kv_mem.shape[1:]
    tok = lambda w: pl.BlockSpec((1, TM_MIX, w), lambda bi, i: (bi, i, 0))
    return pl.pallas_call(
        _mix_xattn_kernel,
        out_shape=jax.ShapeDtypeStruct((b, s, d), _F32),
        grid=(b, s // TM_MIX),
        in_specs=[tok(d), tok(MLA_WIDTH), tok(RET_WIDTH), _const_spec(w_out.shape), _const_spec((1, d)),
                  _const_spec(w_xq.shape), pl.BlockSpec((1, m, n), lambda bi, i: (bi, 0, 0)),
                  _const_spec(w_xo.shape)],
        out_specs=tok(d),
        compiler_params=_params(("arbitrary", "arbitrary")),
        name="mix_xattn",
    )(x, y_mla, y_ret, w_out, g_xattn.reshape(1, d), w_xq, kv_mem, w_xo)


def _conv_ffn_kernel(x_ref, g_ref, win_ref, cw_ref, cb_ref, wout_ref, gfin_ref, o_ref, gate_ref):
    tm = TM_FFN
    halo = SUBLANES

    @pl.when(pl.program_id(1) == 0)
    def _():
        gate_ref[:halo, :] = jnp.zeros((halo, D_FF), _F32)

    x = x_ref[0]
    h = _rms(x, g_ref[...]).astype(_BF16)
    gate = _dot(h, win_ref[:, :D_FF])
    up = _dot(h, win_ref[:, D_FF:])
    gate_ref[halo:, :] = gate
    conv = cb_ref[...] + gate * cw_ref[CONV_W - 1:CONV_W, :]
    for tap in range(CONV_W - 1):
        back = CONV_W - 1 - tap
        conv = conv + gate_ref[halo - back:halo - back + tm, :] * cw_ref[tap:tap + 1, :]
    act = (conv * jax.nn.sigmoid(conv) * up).astype(_BF16)
    x3 = x + _dot(act, wout_ref[...])
    o_ref[0] = _rms(x3, gfin_ref[...])
    gate_ref[:halo, :] = gate_ref[tm:, :]


def _conv_ffn(x, g_ffn, w_ffn_in, conv_w, conv_b, w_ffn_out, g_final):
    b, s, d = x.shape
    tok = pl.BlockSpec((1, TM_FFN, d), lambda bi, i: (bi, i, 0))
    return pl.pallas_call(
        _conv_ffn_kernel,
        out_shape=jax.ShapeDtypeStruct((b, s, d), _F32),
        grid=(b, s // TM_FFN),
        in_specs=[tok, _const_spec((1, d)), _const_spec(w_ffn_in.shape), _const_spec(conv_w.shape),
                  _const_spec((1, D_FF)), _const_spec(w_ffn_out.shape), _const_spec((1, d))],
        out_specs=tok,
        scratch_shapes=[pltpu.VMEM((TM_FFN + SUBLANES, D_FF), _F32)],
        compiler_params=_params(("arbitrary", "arbitrary")),
        name="conv_ffn",
    )(x, g_ffn.reshape(1, d), w_ffn_in, conv_w, conv_b.reshape(1, D_FF), w_ffn_out, g_final.reshape(1, d))


def kernel(x, mem, positions, g_mix, w_in, g_q_lat, w_uq, g_kv_lat, w_ukv, w_out, g_xattn, g_mem, w_xq,
           w_xkv, w_xo, g_ffn, w_ffn_in, conv_w, conv_b, w_ffn_out, g_final):
    assert w_in.shape[0] == 1, "one layer supported"
    l = 0
    cos_t, sin_t = _rope_tables(positions)
    wlat_p, wret_p, wuq_p, wuk_p, wuvt_p = _permute_in_weights(w_in[l], w_uq[l], w_ukv[l])
    q, k, vt, rq, rk, rv, rg = _in_proj(x, g_mix[l], wlat_p, wret_p, g_q_lat[l], wuq_p, g_kv_lat[l], wuk_p, wuvt_p,
                                        cos_t, sin_t)
    y_mla = _mla_attn(q, k, vt)
    y_ret = _retention(rq, rk, rv, rg)
    kv_mem = _mem_kv(mem, g_mem[l], w_xkv[l].astype(_BF16))
    x = _mix_xattn(x, y_mla, y_ret, w_out[l].astype(_BF16), g_xattn[l], w_xq[l].astype(_BF16), kv_mem,
                   w_xo[l].astype(_BF16))
    return _conv_ffn(x, g_ffn[l], w_ffn_in[l].astype(_BF16), conv_w[l], conv_b[l], w_ffn_out[l].astype(_BF16),
                     g_final)
```

```python
import functools
import math

import jax
import jax.numpy as jnp
from jax import lax
from jax.experimental import pallas as pl
from jax.experimental.pallas import tpu as pltpu

D_MODEL = 1024
EPS = 1e-6
ROPE_BASE = 10000.0
N_MLA_HEADS = 8
QK_NOPE = 64
QK_ROPE = 32
V_HEAD = 64
Q_LORA = 256
KV_LORA = 128
N_RET_HEADS = 4
RET_DK = 128
RET_DV = 128
RET_CHUNK = 128
MLA_WIDTH = N_MLA_HEADS * V_HEAD
RET_WIDTH = N_RET_HEADS * RET_DV
N_XATTN_HEADS = 4
XATTN_HEAD = D_MODEL // N_XATTN_HEADS
D_FF = 2816
CONV_W = 3

LANES = 128
SUBLANES = 8
VMEM_LIMIT = 56 * 1024 * 1024

HEAD_PAD = LANES
ROPE_HALF = QK_ROPE // 2
MLA_QK_WIDTH = N_MLA_HEADS * HEAD_PAD
IN_LAT = Q_LORA + KV_LORA + HEAD_PAD

TM_TABLE = 1024
TM_IN = 512
TQ_MLA = 256
TM_MIX = 1024
TM_FFN = 512

_BF16 = jnp.bfloat16
_F32 = jnp.float32


def _dot(a, b):
    return jnp.dot(a, b, preferred_element_type=_F32)


def _dot_nt(a, b):
    return lax.dot_general(a, b, (((1,), (1,)), ((), ())), preferred_element_type=_F32)


def _dot_tn(a, b):
    return lax.dot_general(a, b, (((0,), (0,)), ((), ())), preferred_element_type=_F32)


def _rms(x, g):
    inv = lax.rsqrt(jnp.mean(x * x, axis=-1, keepdims=True) + EPS)
    return x * inv * g


def _const_spec(shape):
    nd = len(shape)
    return pl.BlockSpec(shape, lambda *_: (0,) * nd, pipeline_mode=pl.Buffered(1))


def _params(sem):
    return pltpu.CompilerParams(dimension_semantics=sem, vmem_limit_bytes=VMEM_LIMIT)


def _rope_table_kernel(pos_ref, inv_ref, cos_ref, sin_ref):
    for r in range(TM_TABLE // LANES):
        col = jnp.broadcast_to(pos_ref[r:r + 1, :], (LANES, LANES)).T
        ang = col * inv_ref[...]
        cos_ref[r * LANES:(r + 1) * LANES, :] = jnp.cos(ang)
        sin_ref[r * LANES:(r + 1) * LANES, :] = jnp.sin(ang)


def _rope_tables(positions):
    t = positions.size
    pos = positions.astype(_F32).reshape(t // LANES, LANES)
    f_ret = 1.0 / (ROPE_BASE ** (jnp.arange(0, RET_DK, 2, dtype=_F32) / RET_DK))
    f_mla = 1.0 / (ROPE_BASE ** (jnp.arange(0, QK_ROPE, 2, dtype=_F32) / QK_ROPE))
    inv = jnp.concatenate([f_ret, f_mla, f_mla, jnp.zeros((HEAD_PAD - QK_NOPE - QK_ROPE,), _F32)])
    return pl.pallas_call(
        _rope_table_kernel,
        out_shape=(jax.ShapeDtypeStruct((t, LANES), _F32), jax.ShapeDtypeStruct((t, LANES), _F32)),
        grid=(t // TM_TABLE,),
        in_specs=[pl.BlockSpec((TM_TABLE // LANES, LANES), lambda i: (i, 0)),
                  pl.BlockSpec((1, LANES), lambda i: (0, 0))],
        out_specs=(pl.BlockSpec((TM_TABLE, LANES), lambda i: (i, 0)),
                   pl.BlockSpec((TM_TABLE, LANES), lambda i: (i, 0))),
        compiler_params=_params(("arbitrary",)),
        name="rope_tables",
    )(pos, inv.reshape(1, LANES))


def _in_proj_kernel(x_ref, gmix_ref, wlat_ref, wret_ref, gq_ref, wuq_ref, gkv_ref, wuk_ref, wuvt_ref, cos_ref, sin_ref,
                    intra_ref, kend_ref, qstart_ref, decay_ref,
                    q_ref, k_ref, vt_ref, yret_ref, state_ref):
    h = _rms(x_ref[0], gmix_ref[...]).astype(_BF16)
    t_cos, t_sin = cos_ref[...], sin_ref[...]
    lane = lax.broadcasted_iota(jnp.int32, t_cos.shape, 1)
    low = lane < RET_DK // 2
    cos_r = jnp.where(low, t_cos, pltpu.roll(t_cos, RET_DK // 2, axis=1))
    sin_r = jnp.where(low, -t_sin, pltpu.roll(t_sin, RET_DK // 2, axis=1))
    cos_m = jnp.where(low, 1.0, t_cos)
    sin_m = jnp.where(low, 0.0, jnp.where(lane < QK_NOPE + ROPE_HALF, -t_sin, t_sin))

    first_half = lane < QK_NOPE + ROPE_HALF

    def swap_halves(t):
        return jnp.where(first_half, pltpu.roll(t, LANES - ROPE_HALF, axis=1), pltpu.roll(t, ROPE_HALF, axis=1))

    lat = _dot(h, wlat_ref[...])
    c_q = lat[:, :Q_LORA]
    c_kv = lat[:, Q_LORA:Q_LORA + KV_LORA]
    kr = lat[:, Q_LORA + KV_LORA:IN_LAT]
    k_rope = kr * cos_m + swap_halves(kr) * sin_m

    cqn = _rms(c_q, gq_ref[...]).astype(_BF16)
    q = _dot(cqn, wuq_ref[...])
    scale = math.log2(math.e) / math.sqrt(QK_NOPE + QK_ROPE)
    for hd in range(N_MLA_HEADS):
        sl = slice(hd * HEAD_PAD, (hd + 1) * HEAD_PAD)
        q_ref[0, :, sl] = ((q[:, sl] * cos_m + swap_halves(q[:, sl]) * sin_m) * scale).astype(_BF16)

    ckvn = _rms(c_kv, gkv_ref[...]).astype(_BF16)
    k_nope = _dot(ckvn, wuk_ref[...])
    for hd in range(N_MLA_HEADS):
        sl = slice(hd * HEAD_PAD, (hd + 1) * HEAD_PAD)
        k_ref[0, :, sl] = (k_nope[:, sl] + k_rope).astype(_BF16)
    v_t = _dot_nt(wuvt_ref[...], ckvn).astype(_BF16)
    for j in range(TM_IN // TQ_MLA):
        vt_ref[0, j] = v_t[:, j * TQ_MLA:(j + 1) * TQ_MLA]

    def ret_rope(col0, mult):
        r = _dot(h, wret_ref[:, col0:col0 + RET_WIDTH])
        heads = []
        for hd in range(N_RET_HEADS):
            rh = r[:, hd * RET_DK:(hd + 1) * RET_DK]
            roped = rh * cos_r + pltpu.roll(rh, RET_DK // 2, axis=1) * sin_r
            if mult is not None:
                roped = roped * mult
            heads.append(roped.astype(_BF16))
        return heads

    rq = ret_rope(0, None)
    rk = ret_rope(RET_WIDTH, RET_DK ** -0.5)
    rv = _dot(h, wret_ref[:, 2 * RET_WIDTH:3 * RET_WIDTH]).astype(_BF16)
    rg = _dot(h, wret_ref[:, 3 * RET_WIDTH:]).astype(_BF16)
    _retention_tile(rq, rk, rv, rg, intra_ref, kend_ref, qstart_ref, decay_ref, state_ref, yret_ref)


def _retention_tile(q_heads, k_heads, v, g, intra_ref, kend_ref, qstart_ref, decay_ref, state_ref, o_ref):
    @pl.when(pl.program_id(1) == 0)
    def _():
        state_ref[...] = jnp.zeros(state_ref.shape, _F32)

    L = RET_CHUNK
    n_chunks = v.shape[0] // L
    units = [(c, hd) for c in range(n_chunks) for hd in range(N_RET_HEADS)]
    rows = lambda c: slice(c * L, (c + 1) * L)
    cols = lambda hd: slice(hd * RET_DV, (hd + 1) * RET_DV)
    scores, chunk_kv = {}, {}
    for c, hd in units:
        scores[c, hd] = _dot_nt(q_heads[hd][rows(c)], k_heads[hd][rows(c)])
    for c, hd in units:
        v_dec = (v[rows(c), cols(hd)].astype(_F32) * kend_ref[hd]).astype(_BF16)
        chunk_kv[c, hd] = _dot_tn(k_heads[hd][rows(c)], v_dec)
    prev_state = {}
    for hd in range(N_RET_HEADS):
        state = state_ref[hd]
        for c in range(n_chunks):
            prev_state[c, hd] = state.astype(_BF16)
            state = decay_ref[hd] * state + chunk_kv[c, hd]
        state_ref[hd] = state
    inner, cross = {}, {}
    for c, hd in units:
        inner[c, hd] = _dot((scores[c, hd] * intra_ref[hd]).astype(_BF16), v[rows(c), cols(hd)])
    for c, hd in units:
        cross[c, hd] = _dot(q_heads[hd][rows(c)], prev_state[c, hd])
    for c, hd in units:
        out = inner[c, hd] + cross[c, hd] * qstart_ref[hd]
        mu = jnp.mean(out, axis=-1, keepdims=True)
        cen = out - mu
        var = jnp.mean(cen * cen, axis=-1, keepdims=True)
        normed = cen * lax.rsqrt(var + EPS)
        gate = g[rows(c), cols(hd)].astype(_F32)
        o_ref[0, rows(c), cols(hd)] = (normed * (gate * jax.nn.sigmoid(gate))).astype(o_ref.dtype)


def _permute_in_weights(w_in, w_uq, w_ukv):
    d = w_in.shape[0]
    o_kr = Q_LORA + KV_LORA
    x1 = w_in[:, o_kr:o_kr + ROPE_HALF]
    x2 = w_in[:, o_kr + ROPE_HALF:o_kr + QK_ROPE]
    zl = jnp.zeros((d, QK_NOPE), w_in.dtype)
    zr = jnp.zeros((d, HEAD_PAD - QK_NOPE - QK_ROPE), w_in.dtype)
    wlat_p = jnp.concatenate([w_in[:, :o_kr], zl, x1, x2, zr], axis=1).astype(_BF16)
    wret_p = w_in[:, o_kr + QK_ROPE:].astype(_BF16)

    wq = w_uq.reshape(Q_LORA, N_MLA_HEADS, QK_NOPE + QK_ROPE)
    nope, r1, r2 = wq[..., :QK_NOPE], wq[..., QK_NOPE:QK_NOPE + ROPE_HALF], wq[..., QK_NOPE + ROPE_HALF:]
    z32 = jnp.zeros((Q_LORA, N_MLA_HEADS, HEAD_PAD - QK_NOPE - QK_ROPE), w_uq.dtype)
    wuq_p = jnp.concatenate([nope, r1, r2, z32], axis=-1).reshape(Q_LORA, MLA_QK_WIDTH).astype(_BF16)

    wkv = w_ukv.reshape(KV_LORA, N_MLA_HEADS, QK_NOPE + V_HEAD)
    zk = jnp.zeros((KV_LORA, N_MLA_HEADS, HEAD_PAD - QK_NOPE), w_ukv.dtype)
    wk = jnp.concatenate([wkv[..., :QK_NOPE], zk], axis=-1).reshape(KV_LORA, MLA_QK_WIDTH)
    wv_t = wkv[..., QK_NOPE:].reshape(KV_LORA, MLA_WIDTH).T
    return wlat_p, wret_p, wuq_p, wk.astype(_BF16), wv_t.astype(_BF16)


def _in_proj(x, g_mix, wlat_p, wret_p, g_q_lat, wuq_p, g_kv_lat, wuk_p, wuvt_p, cos_t, sin_t):
    b, s, d = x.shape
    nt = s // TM_IN
    per = TM_IN // TQ_MLA
    tok = lambda w: pl.BlockSpec((1, TM_IN, w), lambda bi, i: (bi, i, 0))
    tab = pl.BlockSpec((TM_IN, LANES), lambda bi, i: (bi * nt + i, 0))
    bf = lambda w: jax.ShapeDtypeStruct((b, s, w), _BF16)
    ret_tables = _retention_tables()
    return pl.pallas_call(
        _in_proj_kernel,
        out_shape=(bf(MLA_QK_WIDTH), bf(MLA_QK_WIDTH),
                   jax.ShapeDtypeStruct((b, s // TQ_MLA, MLA_WIDTH, TQ_MLA), _BF16), bf(RET_WIDTH)),
        grid=(b, nt),
        in_specs=[tok(d), _const_spec((1, d)), _const_spec(wlat_p.shape), _const_spec(wret_p.shape),
                  _const_spec((1, Q_LORA)), _const_spec(wuq_p.shape), _const_spec((1, KV_LORA)),
                  _const_spec(wuk_p.shape), _const_spec(wuvt_p.shape), tab, tab]
                 + [_const_spec(t.shape) for t in ret_tables],
        out_specs=(tok(MLA_QK_WIDTH), tok(MLA_QK_WIDTH),
                   pl.BlockSpec((1, per, MLA_WIDTH, TQ_MLA), lambda bi, i: (bi, i, 0, 0)), tok(RET_WIDTH)),
        scratch_shapes=[pltpu.VMEM((N_RET_HEADS, RET_DK, RET_DV), _F32)],
        compiler_params=_params(("arbitrary", "arbitrary")),
        name="in_proj",
    )(x, g_mix.reshape(1, d), wlat_p, wret_p, g_q_lat.reshape(1, Q_LORA), wuq_p, g_kv_lat.reshape(1, KV_LORA),
      wuk_p, wuvt_p, cos_t, sin_t, *ret_tables)


_MASK_VALUE = -0.7 * float(jnp.finfo(jnp.float32).max)
_MLA_AHEAD = 4
_MLA_TILES_PER_ITER = 5
_MLA_DENOM_ROWS = 16


def _mla_attn_kernel(qlo_ref, qhi_ref, k_ref, vt_ref, o_ref, q_ref, s_ref, m_ref, acc_ref):
    tq = TQ_MLA
    n_tiles = k_ref.shape[1] // tq
    lo = pl.program_id(1)
    hi = n_tiles - 1 - lo
    q_ref[0] = qlo_ref[0]
    q_ref[1] = qhi_ref[0]
    m_ref[...] = jnp.full(m_ref.shape, _MASK_VALUE, _F32)
    acc_ref[...] = jnp.zeros(acc_ref.shape, _F32)
    ones_rows = jnp.ones((_MLA_DENOM_ROWS, tq), _BF16)

    def step_args(t):
        sel = (t > lo).astype(jnp.int32)
        return sel, t - 1 - sel * lo

    def scores(sel, kk, hd):
        sl = slice(hd * HEAD_PAD, (hd + 1) * HEAD_PAD)
        ks = pl.multiple_of(kk * tq, tq)
        s_ref[hd] = _dot_nt(k_ref[0, pl.ds(ks, tq), sl], q_ref[sel, :, sl])

    def softmax(sel, hd, keep):
        row = slice(hd, hd + 1)

        def st():
            s = s_ref[hd]
            return s if keep is None else jnp.where(keep, s, _MASK_VALUE)

        m_prev = m_ref[sel, row, :]
        m_next = jnp.maximum(m_prev, jnp.max(st(), axis=0, keepdims=True))
        m_ref[sel, row, :] = m_next
        return jnp.exp2(m_prev - m_next), jnp.exp2(st() - m_next).astype(_BF16)

    def accumulate(sel, kk, hd, alpha, p):
        vt = jnp.concatenate([vt_ref[0, kk, hd * V_HEAD:(hd + 1) * V_HEAD, :], ones_rows], axis=0)
        acc_ref[sel, hd] = acc_ref[sel, hd] * alpha + _dot(vt, p)

    def tile(sel, kk, diagonal, nxt):
        keep = None
        if diagonal:
            key = lax.broadcasted_iota(jnp.int32, (tq, tq), 0)
            qry = lax.broadcasted_iota(jnp.int32, (tq, tq), 1)
            keep = key <= qry
        pending = None
        for hd in range(N_MLA_HEADS):
            ahead = hd + _MLA_AHEAD
            if ahead < N_MLA_HEADS:
                scores(sel, kk, ahead)
            elif nxt is not None:
                scores(nxt[0], nxt[1], ahead - N_MLA_HEADS)
            current = softmax(sel, hd, keep)
            if pending is not None:
                accumulate(sel, kk, hd - 1, *pending)
            pending = current
        accumulate(sel, kk, N_MLA_HEADS - 1, *pending)

    for hd in range(_MLA_AHEAD):
        scores(0, lo, hd)
    tile(0, lo, True, step_args(1))

    def body(it, carry):
        for u in range(_MLA_TILES_PER_ITER):
            t = 1 + it * _MLA_TILES_PER_ITER + u
            last = t + 1 == n_tiles
            sel_n, kk_n = step_args(t + 1)
            tile(*step_args(t), False, (jnp.where(last, 1, sel_n), jnp.where(last, hi, kk_n)))
        return carry

    lax.fori_loop(0, (n_tiles - 1) // _MLA_TILES_PER_ITER, body, 0)
    tile(1, hi, True, None)

    for sel, qt in ((0, lo), (1, hi)):
        out_t = jnp.concatenate([acc_ref[sel, hd, :V_HEAD, :] / acc_ref[sel, hd, V_HEAD:V_HEAD + 1, :]
                                 for hd in range(N_MLA_HEADS)], axis=0)
        o_ref[0, pl.ds(pl.multiple_of(qt * tq, tq), tq), :] = out_t.T.astype(o_ref.dtype)


def _mla_attn(q, k, vt):
    b, s, w = q.shape
    nk = s // TQ_MLA
    assert nk % 2 == 0 and (nk - 1) % _MLA_TILES_PER_ITER == 0
    return pl.pallas_call(
        _mla_attn_kernel,
        out_shape=jax.ShapeDtypeStruct((b, s, MLA_WIDTH), _BF16),
        grid=(b, nk // 2),
        in_specs=[pl.BlockSpec((1, TQ_MLA, w), lambda bi, i: (bi, i, 0)),
                  pl.BlockSpec((1, TQ_MLA, w), lambda bi, i: (bi, nk - 1 - i, 0)),
                  pl.BlockSpec((1, s, w), lambda bi, i: (bi, 0, 0)),
                  pl.BlockSpec((1, nk, MLA_WIDTH, TQ_MLA), lambda bi, i: (bi, 0, 0, 0))],
        out_specs=pl.BlockSpec((1, s, MLA_WIDTH), lambda bi, i: (bi, 0, 0)),
        scratch_shapes=[pltpu.VMEM((2, TQ_MLA, w), _BF16),
                        pltpu.VMEM((N_MLA_HEADS, TQ_MLA, TQ_MLA), _F32),
                        pltpu.VMEM((2, N_MLA_HEADS, TQ_MLA), _F32),
                        pltpu.VMEM((2, N_MLA_HEADS, V_HEAD + _MLA_DENOM_ROWS, TQ_MLA), _F32)],
        compiler_params=_params(("arbitrary", "arbitrary")),
        name="mla_attn",
    )(q, q, k, vt)


def _retention_tables():
    h, L = N_RET_HEADS, RET_CHUNK
    log_gamma = jnp.log(1.0 - 2.0 ** (-5.0 - jnp.arange(h, dtype=_F32)))
    j = jnp.arange(L, dtype=_F32)
    diff = j[:, None] - j[None, :]
    intra = jnp.where(diff[None] >= 0,
                      jnp.exp(jnp.maximum(diff, 0.0)[None] * log_gamma[:, None, None]), 0.0)
    rowb = lambda t: jnp.broadcast_to(t.T[:, :, None], (h, L, LANES))
    k_to_end = jnp.exp((L - 1 - j)[:, None] * log_gamma[None, :])
    q_from_start = jnp.exp((j + 1)[:, None] * log_gamma[None, :])
    chunk_decay = jnp.broadcast_to(jnp.exp(L * log_gamma)[:, None, None], (h, RET_DK, RET_DV))
    return intra, rowb(k_to_end), rowb(q_from_start), chunk_decay


def _mem_kv_kernel(mem_ref, g_ref, w_ref, kv_ref):
    mem_n = _rms(mem_ref[0], g_ref[...]).astype(_BF16)
    kv_ref[0] = _dot(mem_n, w_ref[...]).astype(_BF16)


def _mem_kv(mem, g_mem, w_xkv):
    b, m, d = mem.shape
    n = w_xkv.shape[1]
    return pl.pallas_call(
        _mem_kv_kernel,
        out_shape=jax.ShapeDtypeStruct((b, m, n), _BF16),
        grid=(b,),
        in_specs=[pl.BlockSpec((1, m, d), lambda bi: (bi, 0, 0)), _const_spec((1, d)), _const_spec((d, n))],
        out_specs=pl.BlockSpec((1, m, n), lambda bi: (bi, 0, 0)),
        compiler_params=_params(("arbitrary",)),
        name="mem_kv",
    )(mem, g_mem.reshape(1, d), w_xkv)


def _mix_xattn_kernel(x_ref, ymla_ref, yret_ref, wout_ref, gx_ref, wxq_ref, kv_ref, wxo_ref, o_ref):
    x1 = x_ref[0] + _dot(ymla_ref[0], wout_ref[:MLA_WIDTH, :]) + _dot(yret_ref[0], wout_ref[MLA_WIDTH:, :])
    h = _rms(x1, gx_ref[...]).astype(_BF16)
    q = (_dot(h, wxq_ref[...]) * (1.0 / math.sqrt(XATTN_HEAD))).astype(_BF16)
    cols = lambda hd: slice(hd * XATTN_HEAD, (hd + 1) * XATTN_HEAD)
    vcols = lambda hd: slice(D_MODEL + hd * XATTN_HEAD, D_MODEL + (hd + 1) * XATTN_HEAD)
    scores = [_dot_nt(q[:, cols(hd)], kv_ref[0, :, cols(hd)]) for hd in range(N_XATTN_HEADS)]
    probs = [jnp.exp(s - jnp.max(s, axis=-1, keepdims=True)) for s in scores]
    heads = [(_dot(p.astype(_BF16), kv_ref[0, :, vcols(hd)]) / jnp.sum(p, axis=-1, keepdims=True)).astype(_BF16)
             for hd, p in enumerate(probs)]
    o_ref[0] = x1 + _dot(jnp.concatenate(heads, axis=1), wxo_ref[...])


def _mix_xattn(x, y_mla, y_ret, w_out, g_xattn, w_xq, kv_mem, w_xo):
    b, s, d = x.shape
    m, n = kv_mem.shape[1:]
    tok = lambda w: pl.BlockSpec((1, TM_MIX, w), lambda bi, i: (bi, i, 0))
    return pl.pallas_call(
        _mix_xattn_kernel,
        out_shape=jax.ShapeDtypeStruct((b, s, d), _F32),
        grid=(b, s // TM_MIX),
        in_specs=[tok(d), tok(MLA_WIDTH), tok(RET_WIDTH), _const_spec(w_out.shape), _const_spec((1, d)),
                  _const_spec(w_xq.shape), pl.BlockSpec((1, m, n), lambda bi, i: (bi, 0, 0)),
                  _const_spec(w_xo.shape)],
        out_specs=tok(d),
        compiler_params=_params(("arbitrary", "arbitrary")),
        name="mix_xattn",
    )(x, y_mla, y_ret, w_out, g_xattn.reshape(1, d), w_xq, kv_mem, w_xo)


def _conv_ffn_kernel(x_ref, g_ref, win_ref, cw_ref, cb_ref, wout_ref, gfin_ref, o_ref, gate_ref):
    tm = TM_FFN
    halo = SUBLANES

    @pl.when(pl.program_id(1) == 0)
    def _():
        gate_ref[:halo, :] = jnp.zeros((halo, D_FF), _F32)

    x = x_ref[0]
    h = _rms(x, g_ref[...]).astype(_BF16)
    gate = _dot(h, win_ref[:, :D_FF])
    up = _dot(h, win_ref[:, D_FF:])
    gate_ref[halo:, :] = gate
    conv = cb_ref[...] + gate * cw_ref[CONV_W - 1:CONV_W, :]
    for tap in range(CONV_W - 1):
        back = CONV_W - 1 - tap
        conv = conv + gate_ref[halo - back:halo - back + tm, :] * cw_ref[tap:tap + 1, :]
    act = (conv * jax.nn.sigmoid(conv) * up).astype(_BF16)
    x3 = x + _dot(act, wout_ref[...])
    o_ref[0] = _rms(x3, gfin_ref[...])
    gate_ref[:halo, :] = gate_ref[tm:, :]


def _conv_ffn(x, g_ffn, w_ffn_in, conv_w, conv_b, w_ffn_out, g_final):
    b, s, d = x.shape
    tok = pl.BlockSpec((1, TM_FFN, d), lambda bi, i: (bi, i, 0))
    return pl.pallas_call(
        _conv_ffn_kernel,
        out_shape=jax.ShapeDtypeStruct((b, s, d), _F32),
        grid=(b, s // TM_FFN),
        in_specs=[tok, _const_spec((1, d)), _const_spec(w_ffn_in.shape), _const_spec(conv_w.shape),
                  _const_spec((1, D_FF)), _const_spec(w_ffn_out.shape), _const_spec((1, d))],
        out_specs=tok,
        scratch_shapes=[pltpu.VMEM((TM_FFN + SUBLANES, D_FF), _F32)],
        compiler_params=_params(("arbitrary", "arbitrary")),
        name="conv_ffn",
    )(x, g_ffn.reshape(1, d), w_ffn_in, conv_w, conv_b.reshape(1, D_FF), w_ffn_out, g_final.reshape(1, d))


def kernel(x, mem, positions, g_mix, w_in, g_q_lat, w_uq, g_kv_lat, w_ukv, w_out, g_xattn, g_mem, w_xq,
           w_xkv, w_xo, g_ffn, w_ffn_in, conv_w, conv_b, w_ffn_out, g_final):
    assert w_in.shape[0] == 1, "one layer supported"
    l = 0
    cos_t, sin_t = _rope_tables(positions)
    wlat_p, wret_p, wuq_p, wuk_p, wuvt_p = _permute_in_weights(w_in[l], w_uq[l], w_ukv[l])
    q, k, vt, y_ret = _in_proj(x, g_mix[l], wlat_p, wret_p, g_q_lat[l], wuq_p, g_kv_lat[l], wuk_p, wuvt_p,
                               cos_t, sin_t)
    y_mla = _mla_attn(q, k, vt)
    kv_mem = _mem_kv(mem, g_mem[l], w_xkv[l].astype(_BF16))
    x = _mix_xattn(x, y_mla, y_ret, w_out[l].astype(_BF16), g_xattn[l], w_xq[l].astype(_BF16), kv_mem,
                   w_xo[l].astype(_BF16))
    return _conv_ffn(x, g_ffn[l], w_ffn_in[l].astype(_BF16), conv_w[l], conv_b[l], w_ffn_out[l].astype(_BF16),
                     g_final)
```

```python
import functools
import math

import jax
import jax.numpy as jnp
from jax import lax
from jax.experimental import pallas as pl
from jax.experimental.pallas import tpu as pltpu

D_MODEL = 1024
EPS = 1e-6
ROPE_BASE = 10000.0
N_MLA_HEADS = 8
QK_NOPE = 64
QK_ROPE = 32
V_HEAD = 64
Q_LORA = 256
KV_LORA = 128
N_RET_HEADS = 4
RET_DK = 128
RET_DV = 128
RET_CHUNK = 128
MLA_WIDTH = N_MLA_HEADS * V_HEAD
RET_WIDTH = N_RET_HEADS * RET_DV
N_XATTN_HEADS = 4
XATTN_HEAD = D_MODEL // N_XATTN_HEADS
D_FF = 2816
CONV_W = 3

LANES = 128
SUBLANES = 8
_BF16_ROWS = 16
VMEM_LIMIT = 56 * 1024 * 1024

HEAD_PAD = LANES
ROPE_HALF = QK_ROPE // 2
MLA_QK_WIDTH = N_MLA_HEADS * HEAD_PAD
IN_LAT = Q_LORA + KV_LORA + HEAD_PAD

TM_TABLE = 1024
TM_IN = 512
TQ_MLA = 256
TM_MIX = 1024
TM_FFN = 512

_BF16 = jnp.bfloat16
_F32 = jnp.float32


def _dot(a, b):
    return jnp.dot(a, b, preferred_element_type=_F32)


def _dot_nt(a, b):
    return lax.dot_general(a, b, (((1,), (1,)), ((), ())), preferred_element_type=_F32)


def _dot_tn(a, b):
    return lax.dot_general(a, b, (((0,), (0,)), ((), ())), preferred_element_type=_F32)


def _rms(x, g):
    inv = lax.rsqrt(jnp.mean(x * x, axis=-1, keepdims=True) + EPS)
    return x * inv * g


def _const_spec(shape):
    nd = len(shape)
    return pl.BlockSpec(shape, lambda *_: (0,) * nd, pipeline_mode=pl.Buffered(1))


def _params(sem):
    return pltpu.CompilerParams(dimension_semantics=sem, vmem_limit_bytes=VMEM_LIMIT)


def _with_casts(body, n_in, n_out, n_cast):
    def kern(*refs):
        ins, rest = refs[:n_in], refs[n_in:]
        cast_in, rest = rest[:n_cast], rest[n_cast:]
        outs, rest = rest[:n_out], rest[n_out:]
        cast_out, scratch = rest[:n_cast], rest[n_cast:]
        for src, dst in zip(cast_in, cast_out):
            dst[...] = src[...].astype(dst.dtype)
        body(*ins, *outs, *scratch)
    return kern


def _cast_plumbing(weights, grid):
    steps = math.prod(grid)
    if len(grid) == 1:
        block_of = lambda i: (i, 0)
    else:
        block_of = lambda bi, i: (bi * grid[1] + i, 0)
    in_specs, out_specs, out_shapes = [], [], []
    for w in weights:
        rows, cols = w.shape
        assert rows % steps == 0 and (rows // steps) % _BF16_ROWS == 0, (w.shape, steps)
        spec = pl.BlockSpec((rows // steps, cols), block_of)
        in_specs.append(spec)
        out_specs.append(spec)
        out_shapes.append(jax.ShapeDtypeStruct(w.shape, _BF16))
    return in_specs, out_specs, out_shapes


def _rope_table_kernel(pos_ref, inv_ref, win_ref, cos_ref, sin_ref, wlat_ref, wret_ref):
    for r in range(TM_TABLE // LANES):
        col = jnp.broadcast_to(pos_ref[r:r + 1, :], (LANES, LANES)).T
        ang = col * inv_ref[...]
        cos_ref[r * LANES:(r + 1) * LANES, :] = jnp.cos(ang)
        sin_ref[r * LANES:(r + 1) * LANES, :] = jnp.sin(ang)
    o_kr = Q_LORA + KV_LORA
    w = win_ref[...]
    around_kr = w[:, o_kr:o_kr + LANES]
    lane = lax.broadcasted_iota(jnp.int32, around_kr.shape, 1)
    kr_block = jnp.where((lane >= QK_NOPE) & (lane < QK_NOPE + QK_ROPE), pltpu.roll(around_kr, QK_NOPE, axis=1), 0.0)
    wlat_ref[...] = jnp.concatenate([w[:, :o_kr], kr_block], axis=1).astype(_BF16)
    wret_ref[...] = w[:, o_kr + QK_ROPE:].astype(_BF16)


def _rope_tables(positions, w_in):
    t = positions.size
    steps = t // TM_TABLE
    d, n_in = w_in.shape
    rows = d // steps
    assert d % steps == 0 and rows % _BF16_ROWS == 0
    pos = positions.astype(_F32).reshape(t // LANES, LANES)
    f_ret = 1.0 / (ROPE_BASE ** (jnp.arange(0, RET_DK, 2, dtype=_F32) / RET_DK))
    f_mla = 1.0 / (ROPE_BASE ** (jnp.arange(0, QK_ROPE, 2, dtype=_F32) / QK_ROPE))
    inv = jnp.concatenate([f_ret, f_mla, f_mla, jnp.zeros((HEAD_PAD - QK_NOPE - QK_ROPE,), _F32)])
    n_ret = n_in - Q_LORA - KV_LORA - QK_ROPE
    return pl.pallas_call(
        _rope_table_kernel,
        out_shape=(jax.ShapeDtypeStruct((t, LANES), _F32), jax.ShapeDtypeStruct((t, LANES), _F32),
                   jax.ShapeDtypeStruct((d, IN_LAT), _BF16), jax.ShapeDtypeStruct((d, n_ret), _BF16)),
        grid=(steps,),
        in_specs=[pl.BlockSpec((TM_TABLE // LANES, LANES), lambda i: (i, 0)),
                  pl.BlockSpec((1, LANES), lambda i: (0, 0)),
                  pl.BlockSpec((rows, n_in), lambda i: (i, 0))],
        out_specs=(pl.BlockSpec((TM_TABLE, LANES), lambda i: (i, 0)),
                   pl.BlockSpec((TM_TABLE, LANES), lambda i: (i, 0)),
                   pl.BlockSpec((rows, IN_LAT), lambda i: (i, 0)),
                   pl.BlockSpec((rows, n_ret), lambda i: (i, 0))),
        compiler_params=_params(("arbitrary",)),
        name="rope_tables",
    )(pos, inv.reshape(1, LANES), w_in)


def _in_proj_kernel(x_ref, gmix_ref, wlat_ref, wret_ref, gq_ref, wuq_ref, gkv_ref, wuk_ref, wuvt_ref, cos_ref, sin_ref,
                    intra_ref, kend_ref, qstart_ref, decay_ref,
                    q_ref, k_ref, vt_ref, yret_ref, state_ref):
    h = _rms(x_ref[0], gmix_ref[...]).astype(_BF16)
    t_cos, t_sin = cos_ref[...], sin_ref[...]
    lane = lax.broadcasted_iota(jnp.int32, t_cos.shape, 1)
    low = lane < RET_DK // 2
    cos_r = jnp.where(low, t_cos, pltpu.roll(t_cos, RET_DK // 2, axis=1))
    sin_r = jnp.where(low, -t_sin, pltpu.roll(t_sin, RET_DK // 2, axis=1))
    cos_m = jnp.where(low, 1.0, t_cos)
    sin_m = jnp.where(low, 0.0, jnp.where(lane < QK_NOPE + ROPE_HALF, -t_sin, t_sin))

    first_half = lane < QK_NOPE + ROPE_HALF

    def swap_halves(t):
        return jnp.where(first_half, pltpu.roll(t, LANES - ROPE_HALF, axis=1), pltpu.roll(t, ROPE_HALF, axis=1))

    lat = _dot(h, wlat_ref[...])
    c_q = lat[:, :Q_LORA]
    c_kv = lat[:, Q_LORA:Q_LORA + KV_LORA]
    kr = lat[:, Q_LORA + KV_LORA:IN_LAT]
    k_rope = kr * cos_m + swap_halves(kr) * sin_m

    cqn = _rms(c_q, gq_ref[...]).astype(_BF16)
    q = _dot(cqn, wuq_ref[...])
    scale = math.log2(math.e) / math.sqrt(QK_NOPE + QK_ROPE)
    for hd in range(N_MLA_HEADS):
        sl = slice(hd * HEAD_PAD, (hd + 1) * HEAD_PAD)
        q_ref[0, :, sl] = ((q[:, sl] * cos_m + swap_halves(q[:, sl]) * sin_m) * scale).astype(_BF16)

    ckvn = _rms(c_kv, gkv_ref[...]).astype(_BF16)
    k_nope = _dot(ckvn, wuk_ref[...])
    for hd in range(N_MLA_HEADS):
        sl = slice(hd * HEAD_PAD, (hd + 1) * HEAD_PAD)
        k_ref[0, :, sl] = (k_nope[:, sl] + k_rope).astype(_BF16)
    v_t = _dot_nt(wuvt_ref[...], ckvn).astype(_BF16)
    for j in range(TM_IN // TQ_MLA):
        vt_ref[0, j] = v_t[:, j * TQ_MLA:(j + 1) * TQ_MLA]

    def ret_rope(col0, mult):
        r = _dot(h, wret_ref[:, col0:col0 + RET_WIDTH])
        heads = []
        for hd in range(N_RET_HEADS):
            rh = r[:, hd * RET_DK:(hd + 1) * RET_DK]
            roped = rh * cos_r + pltpu.roll(rh, RET_DK // 2, axis=1) * sin_r
            if mult is not None:
                roped = roped * mult
            heads.append(roped.astype(_BF16))
        return heads

    rq = ret_rope(0, None)
    rk = ret_rope(RET_WIDTH, RET_DK ** -0.5)
    rv = _dot(h, wret_ref[:, 2 * RET_WIDTH:3 * RET_WIDTH]).astype(_BF16)
    rg = _dot(h, wret_ref[:, 3 * RET_WIDTH:]).astype(_BF16)
    _retention_tile(rq, rk, rv, rg, intra_ref, kend_ref, qstart_ref, decay_ref, state_ref, yret_ref)


def _retention_tile(q_heads, k_heads, v, g, intra_ref, kend_ref, qstart_ref, decay_ref, state_ref, o_ref):
    @pl.when(pl.program_id(1) == 0)
    def _():
        state_ref[...] = jnp.zeros(state_ref.shape, _F32)

    L = RET_CHUNK
    n_chunks = v.shape[0] // L
    units = [(c, hd) for c in range(n_chunks) for hd in range(N_RET_HEADS)]
    rows = lambda c: slice(c * L, (c + 1) * L)
    cols = lambda hd: slice(hd * RET_DV, (hd + 1) * RET_DV)
    scores, chunk_kv = {}, {}
    for c, hd in units:
        scores[c, hd] = _dot_nt(q_heads[hd][rows(c)], k_heads[hd][rows(c)])
    for c, hd in units:
        v_dec = (v[rows(c), cols(hd)].astype(_F32) * kend_ref[hd]).astype(_BF16)
        chunk_kv[c, hd] = _dot_tn(k_heads[hd][rows(c)], v_dec)
    prev_state = {}
    for hd in range(N_RET_HEADS):
        state = state_ref[hd]
        for c in range(n_chunks):
            prev_state[c, hd] = state.astype(_BF16)
            state = decay_ref[hd] * state + chunk_kv[c, hd]
        state_ref[hd] = state
    inner, cross = {}, {}
    for c, hd in units:
        inner[c, hd] = _dot((scores[c, hd] * intra_ref[hd]).astype(_BF16), v[rows(c), cols(hd)])
    for c, hd in units:
        cross[c, hd] = _dot(q_heads[hd][rows(c)], prev_state[c, hd])
    for c, hd in units:
        out = inner[c, hd] + cross[c, hd] * qstart_ref[hd]
        mu = jnp.mean(out, axis=-1, keepdims=True)
        cen = out - mu
        var = jnp.mean(cen * cen, axis=-1, keepdims=True)
        normed = cen * lax.rsqrt(var + EPS)
        gate = g[rows(c), cols(hd)].astype(_F32)
        o_ref[0, rows(c), cols(hd)] = (normed * (gate * jax.nn.sigmoid(gate))).astype(o_ref.dtype)


def _permute_up_weights(w_uq, w_ukv):
    wq = w_uq.reshape(Q_LORA, N_MLA_HEADS, QK_NOPE + QK_ROPE)
    nope, r1, r2 = wq[..., :QK_NOPE], wq[..., QK_NOPE:QK_NOPE + ROPE_HALF], wq[..., QK_NOPE + ROPE_HALF:]
    z32 = jnp.zeros((Q_LORA, N_MLA_HEADS, HEAD_PAD - QK_NOPE - QK_ROPE), w_uq.dtype)
    wuq_p = jnp.concatenate([nope, r1, r2, z32], axis=-1).reshape(Q_LORA, MLA_QK_WIDTH).astype(_BF16)

    wkv = w_ukv.reshape(KV_LORA, N_MLA_HEADS, QK_NOPE + V_HEAD)
    zk = jnp.zeros((KV_LORA, N_MLA_HEADS, HEAD_PAD - QK_NOPE), w_ukv.dtype)
    wk = jnp.concatenate([wkv[..., :QK_NOPE], zk], axis=-1).reshape(KV_LORA, MLA_QK_WIDTH)
    wv_t = wkv[..., QK_NOPE:].reshape(KV_LORA, MLA_WIDTH).T
    return wuq_p, wk.astype(_BF16), wv_t.astype(_BF16)


def _in_proj(x, g_mix, wlat_p, wret_p, g_q_lat, wuq_p, g_kv_lat, wuk_p, wuvt_p, cos_t, sin_t, later_weights):
    b, s, d = x.shape
    nt = s // TM_IN
    per = TM_IN // TQ_MLA
    tok = lambda w: pl.BlockSpec((1, TM_IN, w), lambda bi, i: (bi, i, 0))
    tab = pl.BlockSpec((TM_IN, LANES), lambda bi, i: (bi * nt + i, 0))
    bf = lambda w: jax.ShapeDtypeStruct((b, s, w), _BF16)
    ret_tables = _retention_tables()
    cast_in, cast_out, cast_shapes = _cast_plumbing(later_weights, (b, nt))
    n_in = 11 + len(ret_tables)
    return pl.pallas_call(
        _with_casts(_in_proj_kernel, n_in, 4, len(later_weights)),
        out_shape=(bf(MLA_QK_WIDTH), bf(MLA_QK_WIDTH),
                   jax.ShapeDtypeStruct((b, s // TQ_MLA, MLA_WIDTH, TQ_MLA), _BF16), bf(RET_WIDTH), *cast_shapes),
        grid=(b, nt),
        in_specs=[tok(d), _const_spec((1, d)), _const_spec(wlat_p.shape), _const_spec(wret_p.shape),
                  _const_spec((1, Q_LORA)), _const_spec(wuq_p.shape), _const_spec((1, KV_LORA)),
                  _const_spec(wuk_p.shape), _const_spec(wuvt_p.shape), tab, tab]
                 + [_const_spec(t.shape) for t in ret_tables] + cast_in,
        out_specs=(tok(MLA_QK_WIDTH), tok(MLA_QK_WIDTH),
                   pl.BlockSpec((1, per, MLA_WIDTH, TQ_MLA), lambda bi, i: (bi, i, 0, 0)), tok(RET_WIDTH), *cast_out),
        scratch_shapes=[pltpu.VMEM((N_RET_HEADS, RET_DK, RET_DV), _F32)],
        compiler_params=_params(("arbitrary", "arbitrary")),
        name="in_proj",
    )(x, g_mix.reshape(1, d), wlat_p, wret_p, g_q_lat.reshape(1, Q_LORA), wuq_p, g_kv_lat.reshape(1, KV_LORA),
      wuk_p, wuvt_p, cos_t, sin_t, *ret_tables, *later_weights)


_MASK_VALUE = -0.7 * float(jnp.finfo(jnp.float32).max)
_MLA_AHEAD = 4
_MLA_TILES_PER_ITER = 5
_MLA_DENOM_ROWS = 16


def _mla_attn_kernel(qlo_ref, qhi_ref, k_ref, vt_ref, o_ref, q_ref, s_ref, m_ref, acc_ref):
    tq = TQ_MLA
    n_tiles = k_ref.shape[1] // tq
    lo = pl.program_id(1)
    hi = n_tiles - 1 - lo
    q_ref[0] = qlo_ref[0]
    q_ref[1] = qhi_ref[0]
    m_ref[...] = jnp.full(m_ref.shape, _MASK_VALUE, _F32)
    acc_ref[...] = jnp.zeros(acc_ref.shape, _F32)
    ones_rows = jnp.ones((_MLA_DENOM_ROWS, tq), _BF16)

    def step_args(t):
        sel = (t > lo).astype(jnp.int32)
        return sel, t - 1 - sel * lo

    def scores(sel, kk, hd):
        sl = slice(hd * HEAD_PAD, (hd + 1) * HEAD_PAD)
        ks = pl.multiple_of(kk * tq, tq)
        s_ref[hd] = _dot_nt(k_ref[0, pl.ds(ks, tq), sl], q_ref[sel, :, sl])

    def softmax(sel, hd, keep):
        row = slice(hd, hd + 1)

        def st():
            s = s_ref[hd]
            return s if keep is None else jnp.where(keep, s, _MASK_VALUE)

        m_prev = m_ref[sel, row, :]
        m_next = jnp.maximum(m_prev, jnp.max(st(), axis=0, keepdims=True))
        m_ref[sel, row, :] = m_next
        return jnp.exp2(m_prev - m_next), jnp.exp2(st() - m_next).astype(_BF16)

    def accumulate(sel, kk, hd, alpha, p):
        vt = jnp.concatenate([vt_ref[0, kk, hd * V_HEAD:(hd + 1) * V_HEAD, :], ones_rows], axis=0)
        acc_ref[sel, hd] = acc_ref[sel, hd] * alpha + _dot(vt, p)

    def tile(sel, kk, diagonal, nxt):
        keep = None
        if diagonal:
            key = lax.broadcasted_iota(jnp.int32, (tq, tq), 0)
            qry = lax.broadcasted_iota(jnp.int32, (tq, tq), 1)
            keep = key <= qry
        pending = None
        for hd in range(N_MLA_HEADS):
            ahead = hd + _MLA_AHEAD
            if ahead < N_MLA_HEADS:
                scores(sel, kk, ahead)
            elif nxt is not None:
                scores(nxt[0], nxt[1], ahead - N_MLA_HEADS)
            current = softmax(sel, hd, keep)
            if pending is not None:
                accumulate(sel, kk, hd - 1, *pending)
            pending = current
        accumulate(sel, kk, N_MLA_HEADS - 1, *pending)

    for hd in range(_MLA_AHEAD):
        scores(0, lo, hd)
    tile(0, lo, True, step_args(1))

    def body(it, carry):
        for u in range(_MLA_TILES_PER_ITER):
            t = 1 + it * _MLA_TILES_PER_ITER + u
            last = t + 1 == n_tiles
            sel_n, kk_n = step_args(t + 1)
            tile(*step_args(t), False, (jnp.where(last, 1, sel_n), jnp.where(last, hi, kk_n)))
        return carry

    lax.fori_loop(0, (n_tiles - 1) // _MLA_TILES_PER_ITER, body, 0)
    tile(1, hi, True, None)

    for sel, qt in ((0, lo), (1, hi)):
        out_t = jnp.concatenate([acc_ref[sel, hd, :V_HEAD, :] / acc_ref[sel, hd, V_HEAD:V_HEAD + 1, :]
                                 for hd in range(N_MLA_HEADS)], axis=0)
        o_ref[0, pl.ds(pl.multiple_of(qt * tq, tq), tq), :] = out_t.T.astype(o_ref.dtype)


def _mla_attn(q, k, vt, later_weights):
    b, s, w = q.shape
    nk = s // TQ_MLA
    assert nk % 2 == 0 and (nk - 1) % _MLA_TILES_PER_ITER == 0
    cast_in, cast_out, cast_shapes = _cast_plumbing(later_weights, (b, nk // 2))
    return pl.pallas_call(
        _with_casts(_mla_attn_kernel, 4, 1, len(later_weights)),
        out_shape=(jax.ShapeDtypeStruct((b, s, MLA_WIDTH), _BF16), *cast_shapes),
        grid=(b, nk // 2),
        in_specs=[pl.BlockSpec((1, TQ_MLA, w), lambda bi, i: (bi, i, 0)),
                  pl.BlockSpec((1, TQ_MLA, w), lambda bi, i: (bi, nk - 1 - i, 0)),
                  pl.BlockSpec((1, s, w), lambda bi, i: (bi, 0, 0)),
                  pl.BlockSpec((1, nk, MLA_WIDTH, TQ_MLA), lambda bi, i: (bi, 0, 0, 0))] + cast_in,
        out_specs=(pl.BlockSpec((1, s, MLA_WIDTH), lambda bi, i: (bi, 0, 0)), *cast_out),
        scratch_shapes=[pltpu.VMEM((2, TQ_MLA, w), _BF16),
                        pltpu.VMEM((N_MLA_HEADS, TQ_MLA, TQ_MLA), _F32),
                        pltpu.VMEM((2, N_MLA_HEADS, TQ_MLA), _F32),
                        pltpu.VMEM((2, N_MLA_HEADS, V_HEAD + _MLA_DENOM_ROWS, TQ_MLA), _F32)],
        compiler_params=_params(("arbitrary", "arbitrary")),
        name="mla_attn",
    )(q, q, k, vt, *later_weights)


def _retention_tables():
    h, L = N_RET_HEADS, RET_CHUNK
    log_gamma = jnp.log(1.0 - 2.0 ** (-5.0 - jnp.arange(h, dtype=_F32)))
    j = jnp.arange(L, dtype=_F32)
    diff = j[:, None] - j[None, :]
    intra = jnp.where(diff[None] >= 0,
                      jnp.exp(jnp.maximum(diff, 0.0)[None] * log_gamma[:, None, None]), 0.0)
    rowb = lambda t: jnp.broadcast_to(t.T[:, :, None], (h, L, LANES))
    k_to_end = jnp.exp((L - 1 - j)[:, None] * log_gamma[None, :])
    q_from_start = jnp.exp((j + 1)[:, None] * log_gamma[None, :])
    chunk_decay = jnp.broadcast_to(jnp.exp(L * log_gamma)[:, None, None], (h, RET_DK, RET_DV))
    return intra, rowb(k_to_end), rowb(q_from_start), chunk_decay


def _mem_kv_kernel(mem_ref, g_ref, w_ref, kv_ref):
    mem_n = _rms(mem_ref[0], g_ref[...]).astype(_BF16)
    kv_ref[0] = _dot(mem_n, w_ref[...]).astype(_BF16)


def _mem_kv(mem, g_mem, w_xkv):
    b, m, d = mem.shape
    n = w_xkv.shape[1]
    return pl.pallas_call(
        _mem_kv_kernel,
        out_shape=jax.ShapeDtypeStruct((b, m, n), _BF16),
        grid=(b,),
        in_specs=[pl.BlockSpec((1, m, d), lambda bi: (bi, 0, 0)), _const_spec((1, d)), _const_spec((d, n))],
        out_specs=pl.BlockSpec((1, m, n), lambda bi: (bi, 0, 0)),
        compiler_params=_params(("arbitrary",)),
        name="mem_kv",
    )(mem, g_mem.reshape(1, d), w_xkv)


def _mix_xattn_kernel(x_ref, ymla_ref, yret_ref, wout_ref, gx_ref, wxq_ref, kv_ref, wxo_ref, o_ref):
    x1 = x_ref[0] + _dot(ymla_ref[0], wout_ref[:MLA_WIDTH, :]) + _dot(yret_ref[0], wout_ref[MLA_WIDTH:, :])
    h = _rms(x1, gx_ref[...]).astype(_BF16)
    q = (_dot(h, wxq_ref[...]) * (1.0 / math.sqrt(XATTN_HEAD))).astype(_BF16)
    cols = lambda hd: slice(hd * XATTN_HEAD, (hd + 1) * XATTN_HEAD)
    vcols = lambda hd: slice(D_MODEL + hd * XATTN_HEAD, D_MODEL + (hd + 1) * XATTN_HEAD)
    scores = [_dot_nt(q[:, cols(hd)], kv_ref[0, :, cols(hd)]) for hd in range(N_XATTN_HEADS)]
    probs = [jnp.exp(s - jnp.max(s, axis=-1, keepdims=True)) for s in scores]
    heads = [(_dot(p.astype(_BF16), kv_ref[0, :, vcols(hd)]) / jnp.sum(p, axis=-1, keepdims=True)).astype(_BF16)
             for hd, p in enumerate(probs)]
    o_ref[0] = x1 + _dot(jnp.concatenate(heads, axis=1), wxo_ref[...])


def _mix_xattn(x, y_mla, y_ret, w_out, g_xattn, w_xq, kv_mem, w_xo, later_weights):
    b, s, d = x.shape
    m, n = kv_mem.shape[1:]
    tok = lambda w: pl.BlockSpec((1, TM_MIX, w), lambda bi, i: (bi, i, 0))
    cast_in, cast_out, cast_shapes = _cast_plumbing(later_weights, (b, s // TM_MIX))
    return pl.pallas_call(
        _with_casts(_mix_xattn_kernel, 8, 1, len(later_weights)),
        out_shape=(jax.ShapeDtypeStruct((b, s, d), _F32), *cast_shapes),
        grid=(b, s // TM_MIX),
        in_specs=[tok(d), tok(MLA_WIDTH), tok(RET_WIDTH), _const_spec(w_out.shape), _const_spec((1, d)),
                  _const_spec(w_xq.shape), pl.BlockSpec((1, m, n), lambda bi, i: (bi, 0, 0)),
                  _const_spec(w_xo.shape)] + cast_in,
        out_specs=(tok(d), *cast_out),
        compiler_params=_params(("arbitrary", "arbitrary")),
        name="mix_xattn",
    )(x, y_mla, y_ret, w_out, g_xattn.reshape(1, d), w_xq, kv_mem, w_xo, *later_weights)


def _conv_ffn_kernel(x_ref, g_ref, win_ref, cw_ref, cb_ref, wout_ref, gfin_ref, o_ref, gate_ref):
    tm = TM_FFN
    halo = SUBLANES

    @pl.when(pl.program_id(1) == 0)
    def _():
        gate_ref[:halo, :] = jnp.zeros((halo, D_FF), _F32)

    x = x_ref[0]
    h = _rms(x, g_ref[...]).astype(_BF16)
    gate = _dot(h, win_ref[:, :D_FF])
    up = _dot(h, win_ref[:, D_FF:])
    gate_ref[halo:, :] = gate
    conv = cb_ref[...] + gate * cw_ref[CONV_W - 1:CONV_W, :]
    for tap in range(CONV_W - 1):
        back = CONV_W - 1 - tap
        conv = conv + gate_ref[halo - back:halo - back + tm, :] * cw_ref[tap:tap + 1, :]
    act = (conv * jax.nn.sigmoid(conv) * up).astype(_BF16)
    x3 = x + _dot(act, wout_ref[...])
    o_ref[0] = _rms(x3, gfin_ref[...])
    gate_ref[:halo, :] = gate_ref[tm:, :]


def _conv_ffn(x, g_ffn, w_ffn_in, conv_w, conv_b, w_ffn_out, g_final):
    b, s, d = x.shape
    tok = pl.BlockSpec((1, TM_FFN, d), lambda bi, i: (bi, i, 0))
    return pl.pallas_call(
        _conv_ffn_kernel,
        out_shape=jax.ShapeDtypeStruct((b, s, d), _F32),
        grid=(b, s // TM_FFN),
        in_specs=[tok, _const_spec((1, d)), _const_spec(w_ffn_in.shape), _const_spec(conv_w.shape),
                  _const_spec((1, D_FF)), _const_spec(w_ffn_out.shape), _const_spec((1, d))],
        out_specs=tok,
        scratch_shapes=[pltpu.VMEM((TM_FFN + SUBLANES, D_FF), _F32)],
        compiler_params=_params(("arbitrary", "arbitrary")),
        name="conv_ffn",
    )(x, g_ffn.reshape(1, d), w_ffn_in, conv_w, conv_b.reshape(1, D_FF), w_ffn_out, g_final.reshape(1, d))


def kernel(x, mem, positions, g_mix, w_in, g_q_lat, w_uq, g_kv_lat, w_ukv, w_out, g_xattn, g_mem, w_xq,
           w_xkv, w_xo, g_ffn, w_ffn_in, conv_w, conv_b, w_ffn_out, g_final):
    assert w_in.shape[0] == 1, "one layer supported"
    l = 0
    cos_t, sin_t, wlat_p, wret_p = _rope_tables(positions, w_in[l])
    wuq_p, wuk_p, wuvt_p = _permute_up_weights(w_uq[l], w_ukv[l])
    q, k, vt, y_ret, w_out_b, w_xq_b, w_xo_b, w_xkv_b = _in_proj(
        x, g_mix[l], wlat_p, wret_p, g_q_lat[l], wuq_p, g_kv_lat[l], wuk_p, wuvt_p, cos_t, sin_t,
        (w_out[l], w_xq[l], w_xo[l], w_xkv[l]))
    y_mla, w_ffn_in_b = _mla_attn(q, k, vt, (w_ffn_in[l],))
    kv_mem = _mem_kv(mem, g_mem[l], w_xkv_b)
    x, w_ffn_out_b = _mix_xattn(x, y_mla, y_ret, w_out_b, g_xattn[l], w_xq_b, kv_mem, w_xo_b, (w_ffn_out[l],))
    return _conv_ffn(x, g_ffn[l], w_ffn_in_b, conv_w[l], conv_b[l], w_ffn_out_b, g_final)
```

```python
import functools
import math

import jax
import jax.numpy as jnp
from jax import lax
from jax.experimental import pallas as pl
from jax.experimental.pallas import tpu as pltpu

D_MODEL = 1024
EPS = 1e-6
ROPE_BASE = 10000.0
N_MLA_HEADS = 8
QK_NOPE = 64
QK_ROPE = 32
V_HEAD = 64
Q_LORA = 256
KV_LORA = 128
N_RET_HEADS = 4
RET_DK = 128
RET_DV = 128
RET_CHUNK = 128
MLA_WIDTH = N_MLA_HEADS * V_HEAD
RET_WIDTH = N_RET_HEADS * RET_DV
N_XATTN_HEADS = 4
XATTN_HEAD = D_MODEL // N_XATTN_HEADS
D_FF = 2816
CONV_W = 3

LANES = 128
SUBLANES = 8
_BF16_ROWS = 16
VMEM_LIMIT = 56 * 1024 * 1024

HEAD_PAD = LANES
ROPE_HALF = QK_ROPE // 2
MLA_QK_WIDTH = N_MLA_HEADS * HEAD_PAD
IN_LAT = Q_LORA + KV_LORA + HEAD_PAD

TM_TABLE = 1024
TM_IN = 512
TQ_MLA = 256
TM_MIX = 1024
TM_FFN = 512

_BF16 = jnp.bfloat16
_F32 = jnp.float32


def _dot(a, b):
    return jnp.dot(a, b, preferred_element_type=_F32)


def _dot_nt(a, b):
    return lax.dot_general(a, b, (((1,), (1,)), ((), ())), preferred_element_type=_F32)


def _dot_tn(a, b):
    return lax.dot_general(a, b, (((0,), (0,)), ((), ())), preferred_element_type=_F32)


def _rms(x, g):
    inv = lax.rsqrt(jnp.mean(x * x, axis=-1, keepdims=True) + EPS)
    return x * inv * g


def _const_spec(shape):
    nd = len(shape)
    return pl.BlockSpec(shape, lambda *_: (0,) * nd, pipeline_mode=pl.Buffered(1))


def _params(sem):
    return pltpu.CompilerParams(dimension_semantics=sem, vmem_limit_bytes=VMEM_LIMIT)


def _with_casts(body, n_in, n_out, n_cast):
    def kern(*refs):
        ins, rest = refs[:n_in], refs[n_in:]
        cast_in, rest = rest[:n_cast], rest[n_cast:]
        outs, rest = rest[:n_out], rest[n_out:]
        cast_out, scratch = rest[:n_cast], rest[n_cast:]
        for src, dst in zip(cast_in, cast_out):
            dst[...] = src[...].astype(dst.dtype)
        body(*ins, *outs, *scratch)
    return kern


def _cast_plumbing(weights, grid):
    steps = math.prod(grid)
    if len(grid) == 1:
        block_of = lambda i: (i, 0)
    else:
        block_of = lambda bi, i: (bi * grid[1] + i, 0)
    in_specs, out_specs, out_shapes = [], [], []
    for w in weights:
        rows, cols = w.shape
        assert rows % steps == 0 and (rows // steps) % _BF16_ROWS == 0, (w.shape, steps)
        spec = pl.BlockSpec((rows // steps, cols), block_of)
        in_specs.append(spec)
        out_specs.append(spec)
        out_shapes.append(jax.ShapeDtypeStruct(w.shape, _BF16))
    return in_specs, out_specs, out_shapes


def _rope_table_kernel(pos_ref, inv_ref, wq_ref, wk_ref, wv_ref, wg_ref, cos_ref, sin_ref,
                       wq_out, wk_out, wv_out, wg_out):
    for r in range(TM_TABLE // LANES):
        col = jnp.broadcast_to(pos_ref[r:r + 1, :], (LANES, LANES)).T
        ang = col * inv_ref[...]
        cos_ref[r * LANES:(r + 1) * LANES, :] = jnp.cos(ang)
        sin_ref[r * LANES:(r + 1) * LANES, :] = jnp.sin(ang)
    for src, dst in ((wq_ref, wq_out), (wk_ref, wk_out), (wv_ref, wv_out), (wg_ref, wg_out)):
        dst[...] = src[...].astype(_BF16)


def _rope_tables(positions, w_in_t):
    t = positions.size
    steps = t // TM_TABLE
    n_in, d = w_in_t.shape
    row0 = Q_LORA + KV_LORA + QK_ROPE
    rows = RET_WIDTH // steps
    assert RET_WIDTH % steps == 0 and rows % _BF16_ROWS == 0 and row0 % rows == 0
    assert n_in == row0 + 4 * RET_WIDTH
    pos = positions.astype(_F32).reshape(t // LANES, LANES)
    f_ret = 1.0 / (ROPE_BASE ** (jnp.arange(0, RET_DK, 2, dtype=_F32) / RET_DK))
    f_mla = 1.0 / (ROPE_BASE ** (jnp.arange(0, QK_ROPE, 2, dtype=_F32) / QK_ROPE))
    inv = jnp.concatenate([f_ret, f_mla, f_mla, jnp.zeros((HEAD_PAD - QK_NOPE - QK_ROPE,), _F32)])
    group = lambda j: pl.BlockSpec((rows, d), lambda i: (row0 // rows + j * steps + i, 0))
    table = pl.BlockSpec((TM_TABLE, LANES), lambda i: (i, 0))
    w_out = pl.BlockSpec((rows, d), lambda i: (i, 0))
    return pl.pallas_call(
        _rope_table_kernel,
        out_shape=(jax.ShapeDtypeStruct((t, LANES), _F32), jax.ShapeDtypeStruct((t, LANES), _F32))
                  + (jax.ShapeDtypeStruct((RET_WIDTH, d), _BF16),) * 4,
        grid=(steps,),
        in_specs=[pl.BlockSpec((TM_TABLE // LANES, LANES), lambda i: (i, 0)),
                  pl.BlockSpec((1, LANES), lambda i: (0, 0)), group(0), group(1), group(2), group(3)],
        out_specs=(table, table, w_out, w_out, w_out, w_out),
        compiler_params=_params(("arbitrary",)),
        name="rope_tables",
    )(pos, inv.reshape(1, LANES), w_in_t, w_in_t, w_in_t, w_in_t)


def _in_proj_kernel(x_ref, gmix_ref, wlat_ref, wrq_ref, wrk_ref, wrv_ref, wrg_ref, gq_ref, wuq_ref, gkv_ref, wuk_ref, wuvt_ref, cos_ref, sin_ref,
                    intra_ref, kend_ref, qstart_ref, decay_ref,
                    q_ref, k_ref, vt_ref, yret_ref, state_ref):
    h = _rms(x_ref[0], gmix_ref[...]).astype(_BF16)
    t_cos, t_sin = cos_ref[...], sin_ref[...]
    lane = lax.broadcasted_iota(jnp.int32, t_cos.shape, 1)
    low = lane < RET_DK // 2
    cos_r = jnp.where(low, t_cos, pltpu.roll(t_cos, RET_DK // 2, axis=1))
    sin_r = jnp.where(low, -t_sin, pltpu.roll(t_sin, RET_DK // 2, axis=1))
    cos_m = jnp.where(low, 1.0, t_cos)
    sin_m = jnp.where(low, 0.0, jnp.where(lane < QK_NOPE + ROPE_HALF, -t_sin, t_sin))

    first_half = lane < QK_NOPE + ROPE_HALF

    def swap_halves(t):
        return jnp.where(first_half, pltpu.roll(t, LANES - ROPE_HALF, axis=1), pltpu.roll(t, ROPE_HALF, axis=1))

    lat = _dot_nt(h, wlat_ref[...])
    c_q = lat[:, :Q_LORA]
    c_kv = lat[:, Q_LORA:Q_LORA + KV_LORA]
    kr = lat[:, Q_LORA + KV_LORA:IN_LAT]
    k_rope = kr * cos_m + swap_halves(kr) * sin_m

    cqn = _rms(c_q, gq_ref[...]).astype(_BF16)
    q = _dot(cqn, wuq_ref[...])
    scale = math.log2(math.e) / math.sqrt(QK_NOPE + QK_ROPE)
    for hd in range(N_MLA_HEADS):
        sl = slice(hd * HEAD_PAD, (hd + 1) * HEAD_PAD)
        q_ref[0, :, sl] = ((q[:, sl] * cos_m + swap_halves(q[:, sl]) * sin_m) * scale).astype(_BF16)

    ckvn = _rms(c_kv, gkv_ref[...]).astype(_BF16)
    k_nope = _dot(ckvn, wuk_ref[...])
    for hd in range(N_MLA_HEADS):
        sl = slice(hd * HEAD_PAD, (hd + 1) * HEAD_PAD)
        k_ref[0, :, sl] = (k_nope[:, sl] + k_rope).astype(_BF16)
    v_t = _dot_nt(wuvt_ref[...], ckvn).astype(_BF16)
    for j in range(TM_IN // TQ_MLA):
        vt_ref[0, j] = v_t[:, j * TQ_MLA:(j + 1) * TQ_MLA]

    def ret_rope(w_ref, mult):
        r = _dot_nt(h, w_ref[...])
        heads = []
        for hd in range(N_RET_HEADS):
            rh = r[:, hd * RET_DK:(hd + 1) * RET_DK]
            roped = rh * cos_r + pltpu.roll(rh, RET_DK // 2, axis=1) * sin_r
            if mult is not None:
                roped = roped * mult
            heads.append(roped.astype(_BF16))
        return heads

    rq = ret_rope(wrq_ref, None)
    rk = ret_rope(wrk_ref, RET_DK ** -0.5)
    rv = _dot_nt(h, wrv_ref[...]).astype(_BF16)
    rg = _dot_nt(h, wrg_ref[...]).astype(_BF16)
    _retention_tile(rq, rk, rv, rg, intra_ref, kend_ref, qstart_ref, decay_ref, state_ref, yret_ref)


def _retention_tile(q_heads, k_heads, v, g, intra_ref, kend_ref, qstart_ref, decay_ref, state_ref, o_ref):
    @pl.when(pl.program_id(1) == 0)
    def _():
        state_ref[...] = jnp.zeros(state_ref.shape, _F32)

    L = RET_CHUNK
    n_chunks = v.shape[0] // L
    units = [(c, hd) for c in range(n_chunks) for hd in range(N_RET_HEADS)]
    rows = lambda c: slice(c * L, (c + 1) * L)
    cols = lambda hd: slice(hd * RET_DV, (hd + 1) * RET_DV)
    scores, chunk_kv = {}, {}
    for c, hd in units:
        scores[c, hd] = _dot_nt(q_heads[hd][rows(c)], k_heads[hd][rows(c)])
    for c, hd in units:
        v_dec = (v[rows(c), cols(hd)].astype(_F32) * kend_ref[hd]).astype(_BF16)
        chunk_kv[c, hd] = _dot_tn(k_heads[hd][rows(c)], v_dec)
    prev_state = {}
    for hd in range(N_RET_HEADS):
        state = state_ref[hd]
        for c in range(n_chunks):
            prev_state[c, hd] = state.astype(_BF16)
            state = decay_ref[hd] * state + chunk_kv[c, hd]
        state_ref[hd] = state
    inner, cross = {}, {}
    for c, hd in units:
        inner[c, hd] = _dot((scores[c, hd] * intra_ref[hd]).astype(_BF16), v[rows(c), cols(hd)])
    for c, hd in units:
        cross[c, hd] = _dot(q_heads[hd][rows(c)], prev_state[c, hd])
    for c, hd in units:
        out = inner[c, hd] + cross[c, hd] * qstart_ref[hd]
        mu = jnp.mean(out, axis=-1, keepdims=True)
        cen = out - mu
        var = jnp.mean(cen * cen, axis=-1, keepdims=True)
        normed = cen * lax.rsqrt(var + EPS)
        gate = g[rows(c), cols(hd)].astype(_F32)
        o_ref[0, rows(c), cols(hd)] = (normed * (gate * jax.nn.sigmoid(gate))).astype(o_ref.dtype)


def _permute_small_weights(w_in_t, w_uq, w_ukv):
    o_kr = Q_LORA + KV_LORA
    d = w_in_t.shape[1]
    wlat_t = jnp.concatenate([w_in_t[:o_kr], jnp.zeros((QK_NOPE, d), w_in_t.dtype), w_in_t[o_kr:o_kr + QK_ROPE],
                              jnp.zeros((HEAD_PAD - QK_NOPE - QK_ROPE, d), w_in_t.dtype)], axis=0).astype(_BF16)
    wq = w_uq.reshape(Q_LORA, N_MLA_HEADS, QK_NOPE + QK_ROPE)
    nope, r1, r2 = wq[..., :QK_NOPE], wq[..., QK_NOPE:QK_NOPE + ROPE_HALF], wq[..., QK_NOPE + ROPE_HALF:]
    z32 = jnp.zeros((Q_LORA, N_MLA_HEADS, HEAD_PAD - QK_NOPE - QK_ROPE), w_uq.dtype)
    wuq_p = jnp.concatenate([nope, r1, r2, z32], axis=-1).reshape(Q_LORA, MLA_QK_WIDTH).astype(_BF16)

    wkv = w_ukv.reshape(KV_LORA, N_MLA_HEADS, QK_NOPE + V_HEAD)
    zk = jnp.zeros((KV_LORA, N_MLA_HEADS, HEAD_PAD - QK_NOPE), w_ukv.dtype)
    wk = jnp.concatenate([wkv[..., :QK_NOPE], zk], axis=-1).reshape(KV_LORA, MLA_QK_WIDTH)
    wv_t = wkv[..., QK_NOPE:].reshape(KV_LORA, MLA_WIDTH).T
    return wlat_t, wuq_p, wk.astype(_BF16), wv_t.astype(_BF16)


def _in_proj(x, g_mix, wlat_t, wret_t, g_q_lat, wuq_p, g_kv_lat, wuk_p, wuvt_p, cos_t, sin_t, later_weights):
    b, s, d = x.shape
    nt = s // TM_IN
    per = TM_IN // TQ_MLA
    tok = lambda w: pl.BlockSpec((1, TM_IN, w), lambda bi, i: (bi, i, 0))
    tab = pl.BlockSpec((TM_IN, LANES), lambda bi, i: (bi * nt + i, 0))
    bf = lambda w: jax.ShapeDtypeStruct((b, s, w), _BF16)
    ret_tables = _retention_tables()
    cast_in, cast_out, cast_shapes = _cast_plumbing(later_weights, (b, nt))
    n_in = 14 + len(ret_tables)
    return pl.pallas_call(
        _with_casts(_in_proj_kernel, n_in, 4, len(later_weights)),
        out_shape=(bf(MLA_QK_WIDTH), bf(MLA_QK_WIDTH),
                   jax.ShapeDtypeStruct((b, s // TQ_MLA, MLA_WIDTH, TQ_MLA), _BF16), bf(RET_WIDTH), *cast_shapes),
        grid=(b, nt),
        in_specs=[tok(d), _const_spec((1, d)), _const_spec(wlat_t.shape)] + [_const_spec(w.shape) for w in wret_t]
                 + [_const_spec((1, Q_LORA)), _const_spec(wuq_p.shape), _const_spec((1, KV_LORA)),
                  _const_spec(wuk_p.shape), _const_spec(wuvt_p.shape), tab, tab]
                 + [_const_spec(t.shape) for t in ret_tables] + cast_in,
        out_specs=(tok(MLA_QK_WIDTH), tok(MLA_QK_WIDTH),
                   pl.BlockSpec((1, per, MLA_WIDTH, TQ_MLA), lambda bi, i: (bi, i, 0, 0)), tok(RET_WIDTH), *cast_out),
        scratch_shapes=[pltpu.VMEM((N_RET_HEADS, RET_DK, RET_DV), _F32)],
        compiler_params=_params(("arbitrary", "arbitrary")),
        name="in_proj",
    )(x, g_mix.reshape(1, d), wlat_t, *wret_t, g_q_lat.reshape(1, Q_LORA), wuq_p, g_kv_lat.reshape(1, KV_LORA),
      wuk_p, wuvt_p, cos_t, sin_t, *ret_tables, *later_weights)


_MASK_VALUE = -0.7 * float(jnp.finfo(jnp.float32).max)
_MLA_AHEAD = 4
_MLA_TILES_PER_ITER = 5
_MLA_DENOM_ROWS = 16


def _mla_attn_kernel(qlo_ref, qhi_ref, k_ref, vt_ref, o_ref, q_ref, s_ref, m_ref, acc_ref):
    tq = TQ_MLA
    n_tiles = k_ref.shape[1] // tq
    lo = pl.program_id(1)
    hi = n_tiles - 1 - lo
    q_ref[0] = qlo_ref[0]
    q_ref[1] = qhi_ref[0]
    m_ref[...] = jnp.full(m_ref.shape, _MASK_VALUE, _F32)
    acc_ref[...] = jnp.zeros(acc_ref.shape, _F32)
    ones_rows = jnp.ones((_MLA_DENOM_ROWS, tq), _BF16)

    def step_args(t):
        sel = (t > lo).astype(jnp.int32)
        return sel, t - 1 - sel * lo

    def scores(sel, kk, hd):
        sl = slice(hd * HEAD_PAD, (hd + 1) * HEAD_PAD)
        ks = pl.multiple_of(kk * tq, tq)
        s_ref[hd] = _dot_nt(k_ref[0, pl.ds(ks, tq), sl], q_ref[sel, :, sl])

    def softmax(sel, hd, keep):
        row = slice(hd, hd + 1)

        def st():
            s = s_ref[hd]
            return s if keep is None else jnp.where(keep, s, _MASK_VALUE)

        m_prev = m_ref[sel, row, :]
        m_next = jnp.maximum(m_prev, jnp.max(st(), axis=0, keepdims=True))
        m_ref[sel, row, :] = m_next
        return jnp.exp2(m_prev - m_next), jnp.exp2(st() - m_next).astype(_BF16)

    def accumulate(sel, kk, hd, alpha, p):
        vt = jnp.concatenate([vt_ref[0, kk, hd * V_HEAD:(hd + 1) * V_HEAD, :], ones_rows], axis=0)
        acc_ref[sel, hd] = acc_ref[sel, hd] * alpha + _dot(vt, p)

    def tile(sel, kk, diagonal, nxt):
        keep = None
        if diagonal:
            key = lax.broadcasted_iota(jnp.int32, (tq, tq), 0)
            qry = lax.broadcasted_iota(jnp.int32, (tq, tq), 1)
            keep = key <= qry
        pending = None
        for hd in range(N_MLA_HEADS):
            ahead = hd + _MLA_AHEAD
            if ahead < N_MLA_HEADS:
                scores(sel, kk, ahead)
            elif nxt is not None:
                scores(nxt[0], nxt[1], ahead - N_MLA_HEADS)
            current = softmax(sel, hd, keep)
            if pending is not None:
                accumulate(sel, kk, hd - 1, *pending)
            pending = current
        accumulate(sel, kk, N_MLA_HEADS - 1, *pending)

    for hd in range(_MLA_AHEAD):
        scores(0, lo, hd)
    tile(0, lo, True, step_args(1))

    def body(it, carry):
        for u in range(_MLA_TILES_PER_ITER):
            t = 1 + it * _MLA_TILES_PER_ITER + u
            last = t + 1 == n_tiles
            sel_n, kk_n = step_args(t + 1)
            tile(*step_args(t), False, (jnp.where(last, 1, sel_n), jnp.where(last, hi, kk_n)))
        return carry

    lax.fori_loop(0, (n_tiles - 1) // _MLA_TILES_PER_ITER, body, 0)
    tile(1, hi, True, None)

    for sel, qt in ((0, lo), (1, hi)):
        out_t = jnp.concatenate([acc_ref[sel, hd, :V_HEAD, :] / acc_ref[sel, hd, V_HEAD:V_HEAD + 1, :]
                                 for hd in range(N_MLA_HEADS)], axis=0)
        o_ref[0, pl.ds(pl.multiple_of(qt * tq, tq), tq), :] = out_t.T.astype(o_ref.dtype)


def _mla_attn(q, k, vt, later_weights):
    b, s, w = q.shape
    nk = s // TQ_MLA
    assert nk % 2 == 0 and (nk - 1) % _MLA_TILES_PER_ITER == 0
    cast_in, cast_out, cast_shapes = _cast_plumbing(later_weights, (b, nk // 2))
    return pl.pallas_call(
        _with_casts(_mla_attn_kernel, 4, 1, len(later_weights)),
        out_shape=(jax.ShapeDtypeStruct((b, s, MLA_WIDTH), _BF16), *cast_shapes),
        grid=(b, nk // 2),
        in_specs=[pl.BlockSpec((1, TQ_MLA, w), lambda bi, i: (bi, i, 0)),
                  pl.BlockSpec((1, TQ_MLA, w), lambda bi, i: (bi, nk - 1 - i, 0)),
                  pl.BlockSpec((1, s, w), lambda bi, i: (bi, 0, 0)),
                  pl.BlockSpec((1, nk, MLA_WIDTH, TQ_MLA), lambda bi, i: (bi, 0, 0, 0))] + cast_in,
        out_specs=(pl.BlockSpec((1, s, MLA_WIDTH), lambda bi, i: (bi, 0, 0)), *cast_out),
        scratch_shapes=[pltpu.VMEM((2, TQ_MLA, w), _BF16),
                        pltpu.VMEM((N_MLA_HEADS, TQ_MLA, TQ_MLA), _F32),
                        pltpu.VMEM((2, N_MLA_HEADS, TQ_MLA), _F32),
                        pltpu.VMEM((2, N_MLA_HEADS, V_HEAD + _MLA_DENOM_ROWS, TQ_MLA), _F32)],
        compiler_params=_params(("arbitrary", "arbitrary")),
        name="mla_attn",
    )(q, q, k, vt, *later_weights)


def _retention_tables():
    h, L = N_RET_HEADS, RET_CHUNK
    log_gamma = jnp.log(1.0 - 2.0 ** (-5.0 - jnp.arange(h, dtype=_F32)))
    j = jnp.arange(L, dtype=_F32)
    diff = j[:, None] - j[None, :]
    intra = jnp.where(diff[None] >= 0,
                      jnp.exp(jnp.maximum(diff, 0.0)[None] * log_gamma[:, None, None]), 0.0)
    rowb = lambda t: jnp.broadcast_to(t.T[:, :, None], (h, L, LANES))
    k_to_end = jnp.exp((L - 1 - j)[:, None] * log_gamma[None, :])
    q_from_start = jnp.exp((j + 1)[:, None] * log_gamma[None, :])
    chunk_decay = jnp.broadcast_to(jnp.exp(L * log_gamma)[:, None, None], (h, RET_DK, RET_DV))
    return intra, rowb(k_to_end), rowb(q_from_start), chunk_decay


def _mem_kv_kernel(mem_ref, g_ref, w_ref, kv_ref):
    mem_n = _rms(mem_ref[0], g_ref[...]).astype(_BF16)
    kv_ref[0] = _dot(mem_n, w_ref[...]).astype(_BF16)


def _mem_kv(mem, g_mem, w_xkv):
    b, m, d = mem.shape
    n = w_xkv.shape[1]
    return pl.pallas_call(
        _mem_kv_kernel,
        out_shape=jax.ShapeDtypeStruct((b, m, n), _BF16),
        grid=(b,),
        in_specs=[pl.BlockSpec((1, m, d), lambda bi: (bi, 0, 0)), _const_spec((1, d)), _const_spec((d, n))],
        out_specs=pl.BlockSpec((1, m, n), lambda bi: (bi, 0, 0)),
        compiler_params=_params(("arbitrary",)),
        name="mem_kv",
    )(mem, g_mem.reshape(1, d), w_xkv)


def _mix_xattn_kernel(x_ref, ymla_ref, yret_ref, wout_ref, gx_ref, wxq_ref, kv_ref, wxo_ref, o_ref):
    x1 = x_ref[0] + _dot(ymla_ref[0], wout_ref[:MLA_WIDTH, :]) + _dot(yret_ref[0], wout_ref[MLA_WIDTH:, :])
    h = _rms(x1, gx_ref[...]).astype(_BF16)
    q = (_dot(h, wxq_ref[...]) * (1.0 / math.sqrt(XATTN_HEAD))).astype(_BF16)
    cols = lambda hd: slice(hd * XATTN_HEAD, (hd + 1) * XATTN_HEAD)
    vcols = lambda hd: slice(D_MODEL + hd * XATTN_HEAD, D_MODEL + (hd + 1) * XATTN_HEAD)
    scores = [_dot_nt(q[:, cols(hd)], kv_ref[0, :, cols(hd)]) for hd in range(N_XATTN_HEADS)]
    probs = [jnp.exp(s - jnp.max(s, axis=-1, keepdims=True)) for s in scores]
    heads = [(_dot(p.astype(_BF16), kv_ref[0, :, vcols(hd)]) / jnp.sum(p, axis=-1, keepdims=True)).astype(_BF16)
             for hd, p in enumerate(probs)]
    o_ref[0] = x1 + _dot(jnp.concatenate(heads, axis=1), wxo_ref[...])


def _mix_xattn(x, y_mla, y_ret, w_out, g_xattn, w_xq, kv_mem, w_xo, later_weights):
    b, s, d = x.shape
    m, n = kv_mem.shape[1:]
    tok = lambda w: pl.BlockSpec((1, TM_MIX, w), lambda bi, i: (bi, i, 0))
    cast_in, cast_out, cast_shapes = _cast_plumbing(later_weights, (b, s // TM_MIX))
    return pl.pallas_call(
        _with_casts(_mix_xattn_kernel, 8, 1, len(later_weights)),
        out_shape=(jax.ShapeDtypeStruct((b, s, d), _F32), *cast_shapes),
        grid=(b, s // TM_MIX),
        in_specs=[tok(d), tok(MLA_WIDTH), tok(RET_WIDTH), _const_spec(w_out.shape), _const_spec((1, d)),
                  _const_spec(w_xq.shape), pl.BlockSpec((1, m, n), lambda bi, i: (bi, 0, 0)),
                  _const_spec(w_xo.shape)] + cast_in,
        out_specs=(tok(d), *cast_out),
        compiler_params=_params(("arbitrary", "arbitrary")),
        name="mix_xattn",
    )(x, y_mla, y_ret, w_out, g_xattn.reshape(1, d), w_xq, kv_mem, w_xo, *later_weights)


def _conv_ffn_kernel(x_ref, g_ref, win_ref, cw_ref, cb_ref, wout_ref, gfin_ref, o_ref, gate_ref):
    tm = TM_FFN
    halo = SUBLANES

    @pl.when(pl.program_id(1) == 0)
    def _():
        gate_ref[:halo, :] = jnp.zeros((halo, D_FF), _F32)

    x = x_ref[0]
    h = _rms(x, g_ref[...]).astype(_BF16)
    gate = _dot(h, win_ref[:, :D_FF])
    up = _dot(h, win_ref[:, D_FF:])
    gate_ref[halo:, :] = gate
    conv = cb_ref[...] + gate * cw_ref[CONV_W - 1:CONV_W, :]
    for tap in range(CONV_W - 1):
        back = CONV_W - 1 - tap
        conv = conv + gate_ref[halo - back:halo - back + tm, :] * cw_ref[tap:tap + 1, :]
    act = (conv * jax.nn.sigmoid(conv) * up).astype(_BF16)
    x3 = x + _dot(act, wout_ref[...])
    o_ref[0] = _rms(x3, gfin_ref[...])
    gate_ref[:halo, :] = gate_ref[tm:, :]


def _conv_ffn(x, g_ffn, w_ffn_in, conv_w, conv_b, w_ffn_out, g_final):
    b, s, d = x.shape
    tok = pl.BlockSpec((1, TM_FFN, d), lambda bi, i: (bi, i, 0))
    return pl.pallas_call(
        _conv_ffn_kernel,
        out_shape=jax.ShapeDtypeStruct((b, s, d), _F32),
        grid=(b, s // TM_FFN),
        in_specs=[tok, _const_spec((1, d)), _const_spec(w_ffn_in.shape), _const_spec(conv_w.shape),
                  _const_spec((1, D_FF)), _const_spec(w_ffn_out.shape), _const_spec((1, d))],
        out_specs=tok,
        scratch_shapes=[pltpu.VMEM((TM_FFN + SUBLANES, D_FF), _F32)],
        compiler_params=_params(("arbitrary", "arbitrary")),
        name="conv_ffn",
    )(x, g_ffn.reshape(1, d), w_ffn_in, conv_w, conv_b.reshape(1, D_FF), w_ffn_out, g_final.reshape(1, d))


def kernel(x, mem, positions, g_mix, w_in, g_q_lat, w_uq, g_kv_lat, w_ukv, w_out, g_xattn, g_mem, w_xq,
           w_xkv, w_xo, g_ffn, w_ffn_in, conv_w, conv_b, w_ffn_out, g_final):
    assert w_in.shape[0] == 1, "one layer supported"
    l = 0
    w_in_t = w_in[l].T
    cos_t, sin_t, *wret_t = _rope_tables(positions, w_in_t)
    wlat_t, wuq_p, wuk_p, wuvt_p = _permute_small_weights(w_in_t, w_uq[l], w_ukv[l])
    q, k, vt, y_ret, w_out_b, w_xq_b, w_xo_b, w_xkv_b = _in_proj(
        x, g_mix[l], wlat_t, wret_t, g_q_lat[l], wuq_p, g_kv_lat[l], wuk_p, wuvt_p, cos_t, sin_t,
        (w_out[l], w_xq[l], w_xo[l], w_xkv[l]))
    y_mla, w_ffn_in_b = _mla_attn(q, k, vt, (w_ffn_in[l],))
    kv_mem = _mem_kv(mem, g_mem[l], w_xkv_b)
    x, w_ffn_out_b = _mix_xattn(x, y_mla, y_ret, w_out_b, g_xattn[l], w_xq_b, kv_mem, w_xo_b, (w_ffn_out[l],))
    return _conv_ffn(x, g_ffn[l], w_ffn_in_b, conv_w[l], conv_b[l], w_ffn_out_b, g_final)
```

```python
import functools
import math

import jax
import jax.numpy as jnp
from jax import lax
from jax.experimental import pallas as pl
from jax.experimental.pallas import tpu as pltpu

D_MODEL = 1024
EPS = 1e-6
ROPE_BASE = 10000.0
N_MLA_HEADS = 8
QK_NOPE = 64
QK_ROPE = 32
V_HEAD = 64
Q_LORA = 256
KV_LORA = 128
N_RET_HEADS = 4
RET_DK = 128
RET_DV = 128
RET_CHUNK = 128
MLA_WIDTH = N_MLA_HEADS * V_HEAD
RET_WIDTH = N_RET_HEADS * RET_DV
N_XATTN_HEADS = 4
XATTN_HEAD = D_MODEL // N_XATTN_HEADS
D_FF = 2816
CONV_W = 3

LANES = 128
SUBLANES = 8
_BF16_ROWS = 16
VMEM_LIMIT = 56 * 1024 * 1024

HEAD_PAD = LANES
ROPE_HALF = QK_ROPE // 2
MLA_QK_WIDTH = N_MLA_HEADS * HEAD_PAD
IN_LAT = Q_LORA + KV_LORA + HEAD_PAD

TM_TABLE = 1024
TM_IN = 512
TQ_MLA = 256
TM_MIX = 1024
TM_FFN = 512

_BF16 = jnp.bfloat16
_F32 = jnp.float32


def _dot(a, b):
    return jnp.dot(a, b, preferred_element_type=_F32)


def _dot_nt(a, b):
    return lax.dot_general(a, b, (((1,), (1,)), ((), ())), preferred_element_type=_F32)


def _dot_tn(a, b):
    return lax.dot_general(a, b, (((0,), (0,)), ((), ())), preferred_element_type=_F32)


def _rms(x, g):
    inv = lax.rsqrt(jnp.mean(x * x, axis=-1, keepdims=True) + EPS)
    return x * inv * g


def _const_spec(shape):
    nd = len(shape)
    return pl.BlockSpec(shape, lambda *_: (0,) * nd, pipeline_mode=pl.Buffered(1))


def _params(sem):
    return pltpu.CompilerParams(dimension_semantics=sem, vmem_limit_bytes=VMEM_LIMIT)


def _with_casts(body, n_in, n_out, n_cast):
    def kern(*refs):
        ins, rest = refs[:n_in], refs[n_in:]
        cast_in, rest = rest[:n_cast], rest[n_cast:]
        outs, rest = rest[:n_out], rest[n_out:]
        cast_out, scratch = rest[:n_cast], rest[n_cast:]
        for src, dst in zip(cast_in, cast_out):
            dst[...] = src[...].astype(dst.dtype)
        body(*ins, *outs, *scratch)
    return kern


def _cast_plumbing(weights, grid):
    steps = math.prod(grid)
    if len(grid) == 1:
        block_of = lambda i: (i, 0)
    else:
        block_of = lambda bi, i: (bi * grid[1] + i, 0)
    in_specs, out_specs, out_shapes = [], [], []
    for w in weights:
        rows, cols = w.shape
        assert rows % steps == 0 and (rows // steps) % _BF16_ROWS == 0, (w.shape, steps)
        spec = pl.BlockSpec((rows // steps, cols), block_of)
        in_specs.append(spec)
        out_specs.append(spec)
        out_shapes.append(jax.ShapeDtypeStruct(w.shape, _BF16))
    return in_specs, out_specs, out_shapes


def _rope_table_kernel(pos_ref, inv_ref, wq_ref, wk_ref, wv_ref, wg_ref, cos_ref, sin_ref,
                       wq_out, wk_out, wv_out, wg_out):
    for r in range(TM_TABLE // LANES):
        col = jnp.broadcast_to(pos_ref[r:r + 1, :], (LANES, LANES)).T
        ang = col * inv_ref[...]
        cos_ref[r * LANES:(r + 1) * LANES, :] = jnp.cos(ang)
        sin_ref[r * LANES:(r + 1) * LANES, :] = jnp.sin(ang)
    for src, dst in ((wq_ref, wq_out), (wk_ref, wk_out), (wv_ref, wv_out), (wg_ref, wg_out)):
        dst[...] = src[...].astype(_BF16)


def _rope_tables(positions, w_in_t):
    t = positions.size
    steps = t // TM_TABLE
    n_in, d = w_in_t.shape
    row0 = Q_LORA + KV_LORA + QK_ROPE
    rows = RET_WIDTH // steps
    assert RET_WIDTH % steps == 0 and rows % _BF16_ROWS == 0 and row0 % rows == 0
    assert n_in == row0 + 4 * RET_WIDTH
    pos = positions.astype(_F32).reshape(t // LANES, LANES)
    f_ret = 1.0 / (ROPE_BASE ** (jnp.arange(0, RET_DK, 2, dtype=_F32) / RET_DK))
    f_mla = 1.0 / (ROPE_BASE ** (jnp.arange(0, QK_ROPE, 2, dtype=_F32) / QK_ROPE))
    inv = jnp.concatenate([f_ret, f_mla, f_mla, jnp.zeros((HEAD_PAD - QK_NOPE - QK_ROPE,), _F32)])
    group = lambda j: pl.BlockSpec((rows, d), lambda i: (row0 // rows + j * steps + i, 0))
    table = pl.BlockSpec((TM_TABLE, LANES), lambda i: (i, 0))
    w_out = pl.BlockSpec((rows, d), lambda i: (i, 0))
    return pl.pallas_call(
        _rope_table_kernel,
        out_shape=(jax.ShapeDtypeStruct((t, LANES), _F32), jax.ShapeDtypeStruct((t, LANES), _F32))
                  + (jax.ShapeDtypeStruct((RET_WIDTH, d), _BF16),) * 4,
        grid=(steps,),
        in_specs=[pl.BlockSpec((TM_TABLE // LANES, LANES), lambda i: (i, 0)),
                  pl.BlockSpec((1, LANES), lambda i: (0, 0)), group(0), group(1), group(2), group(3)],
        out_specs=(table, table, w_out, w_out, w_out, w_out),
        compiler_params=_params(("arbitrary",)),
        name="rope_tables",
    )(pos, inv.reshape(1, LANES), w_in_t, w_in_t, w_in_t, w_in_t)


def _in_proj_kernel(x_ref, gmix_ref, wlat_ref, wrq_ref, wrk_ref, wrv_ref, wrg_ref, gq_ref, wuq_ref, gkv_ref, wuk_ref, wuvt_ref, cos_ref, sin_ref,
                    intra_ref, kend_ref, qstart_ref, decay_ref,
                    q_ref, k_ref, vt_ref, yret_ref, state_ref):
    h = _rms(x_ref[0], gmix_ref[...]).astype(_BF16)
    t_cos, t_sin = cos_ref[...], sin_ref[...]
    lane = lax.broadcasted_iota(jnp.int32, t_cos.shape, 1)
    low = lane < RET_DK // 2
    cos_r = jnp.where(low, t_cos, pltpu.roll(t_cos, RET_DK // 2, axis=1))
    sin_r = jnp.where(low, -t_sin, pltpu.roll(t_sin, RET_DK // 2, axis=1))
    cos_m = jnp.where(low, 1.0, t_cos)
    sin_m = jnp.where(low, 0.0, jnp.where(lane < QK_NOPE + ROPE_HALF, -t_sin, t_sin))

    first_half = lane < QK_NOPE + ROPE_HALF

    def swap_halves(t):
        return jnp.where(first_half, pltpu.roll(t, LANES - ROPE_HALF, axis=1), pltpu.roll(t, ROPE_HALF, axis=1))

    lat = _dot_nt(h, wlat_ref[...])
    c_q = lat[:, :Q_LORA]
    c_kv = lat[:, Q_LORA:Q_LORA + KV_LORA]
    kr = lat[:, Q_LORA + KV_LORA:IN_LAT]
    k_rope = kr * cos_m + swap_halves(kr) * sin_m

    cqn = _rms(c_q, gq_ref[...]).astype(_BF16)
    q = _dot(cqn, wuq_ref[...])
    scale = math.log2(math.e) / math.sqrt(QK_NOPE + QK_ROPE)
    for hd in range(N_MLA_HEADS):
        sl = slice(hd * HEAD_PAD, (hd + 1) * HEAD_PAD)
        q_ref[0, :, sl] = ((q[:, sl] * cos_m + swap_halves(q[:, sl]) * sin_m) * scale).astype(_BF16)

    ckvn = _rms(c_kv, gkv_ref[...]).astype(_BF16)
    k_nope = _dot(ckvn, wuk_ref[...])
    for hd in range(N_MLA_HEADS):
        sl = slice(hd * HEAD_PAD, (hd + 1) * HEAD_PAD)
        k_ref[0, :, sl] = (k_nope[:, sl] + k_rope).astype(_BF16)
    v_t = _dot_nt(wuvt_ref[...], ckvn).astype(_BF16)
    for j in range(TM_IN // TQ_MLA):
        vt_ref[0, j] = v_t[:, j * TQ_MLA:(j + 1) * TQ_MLA]

    def ret_rope(w_ref, mult):
        r = _dot_nt(h, w_ref[...])
        heads = []
        for hd in range(N_RET_HEADS):
            rh = r[:, hd * RET_DK:(hd + 1) * RET_DK]
            roped = rh * cos_r + pltpu.roll(rh, RET_DK // 2, axis=1) * sin_r
            if mult is not None:
                roped = roped * mult
            heads.append(roped.astype(_BF16))
        return heads

    rq = ret_rope(wrq_ref, None)
    rk = ret_rope(wrk_ref, RET_DK ** -0.5)
    rv = _dot_nt(h, wrv_ref[...]).astype(_BF16)
    rg = _dot_nt(h, wrg_ref[...]).astype(_BF16)
    _retention_tile(rq, rk, rv, rg, intra_ref, kend_ref, qstart_ref, decay_ref, state_ref, yret_ref)


def _retention_tile(q_heads, k_heads, v, g, intra_ref, kend_ref, qstart_ref, decay_ref, state_ref, o_ref):
    @pl.when(pl.program_id(1) == 0)
    def _():
        state_ref[...] = jnp.zeros(state_ref.shape, _F32)

    L = RET_CHUNK
    n_chunks = v.shape[0] // L
    units = [(c, hd) for c in range(n_chunks) for hd in range(N_RET_HEADS)]
    rows = lambda c: slice(c * L, (c + 1) * L)
    cols = lambda hd: slice(hd * RET_DV, (hd + 1) * RET_DV)
    scores, chunk_kv = {}, {}
    for c, hd in units:
        scores[c, hd] = _dot_nt(q_heads[hd][rows(c)], k_heads[hd][rows(c)])
    for c, hd in units:
        v_dec = (v[rows(c), cols(hd)].astype(_F32) * kend_ref[hd]).astype(_BF16)
        chunk_kv[c, hd] = _dot_tn(k_heads[hd][rows(c)], v_dec)
    prev_state = {}
    for hd in range(N_RET_HEADS):
        state = state_ref[hd]
        for c in range(n_chunks):
            prev_state[c, hd] = state.astype(_BF16)
            state = decay_ref[hd] * state + chunk_kv[c, hd]
        state_ref[hd] = state
    inner, cross = {}, {}
    for c, hd in units:
        inner[c, hd] = _dot((scores[c, hd] * intra_ref[hd]).astype(_BF16), v[rows(c), cols(hd)])
    for c, hd in units:
        cross[c, hd] = _dot(q_heads[hd][rows(c)], prev_state[c, hd])
    for c, hd in units:
        out = inner[c, hd] + cross[c, hd] * qstart_ref[hd]
        mu = jnp.mean(out, axis=-1, keepdims=True)
        cen = out - mu
        var = jnp.mean(cen * cen, axis=-1, keepdims=True)
        normed = cen * lax.rsqrt(var + EPS)
        gate = g[rows(c), cols(hd)].astype(_F32)
        o_ref[0, rows(c), cols(hd)] = (normed * (gate * jax.nn.sigmoid(gate))).astype(o_ref.dtype)


def _permute_small_weights(w_in_t, w_uq, w_ukv):
    o_kr = Q_LORA + KV_LORA
    d = w_in_t.shape[1]
    wlat_t = jnp.concatenate([w_in_t[:o_kr], jnp.zeros((QK_NOPE, d), w_in_t.dtype), w_in_t[o_kr:o_kr + QK_ROPE],
                              jnp.zeros((HEAD_PAD - QK_NOPE - QK_ROPE, d), w_in_t.dtype)], axis=0).astype(_BF16)
    wq = w_uq.reshape(Q_LORA, N_MLA_HEADS, QK_NOPE + QK_ROPE)
    nope, r1, r2 = wq[..., :QK_NOPE], wq[..., QK_NOPE:QK_NOPE + ROPE_HALF], wq[..., QK_NOPE + ROPE_HALF:]
    z32 = jnp.zeros((Q_LORA, N_MLA_HEADS, HEAD_PAD - QK_NOPE - QK_ROPE), w_uq.dtype)
    wuq_p = jnp.concatenate([nope, r1, r2, z32], axis=-1).reshape(Q_LORA, MLA_QK_WIDTH).astype(_BF16)

    wkv = w_ukv.reshape(KV_LORA, N_MLA_HEADS, QK_NOPE + V_HEAD)
    zk = jnp.zeros((KV_LORA, N_MLA_HEADS, HEAD_PAD - QK_NOPE), w_ukv.dtype)
    wk = jnp.concatenate([wkv[..., :QK_NOPE], zk], axis=-1).reshape(KV_LORA, MLA_QK_WIDTH)
    wv_t = wkv[..., QK_NOPE:].reshape(KV_LORA, MLA_WIDTH).T
    return wlat_t, wuq_p, wk.astype(_BF16), wv_t.astype(_BF16)


def _in_proj(x, g_mix, wlat_t, wret_t, g_q_lat, wuq_p, g_kv_lat, wuk_p, wuvt_p, cos_t, sin_t, later_weights):
    b, s, d = x.shape
    nt = s // TM_IN
    per = TM_IN // TQ_MLA
    tok = lambda w: pl.BlockSpec((1, TM_IN, w), lambda bi, i: (bi, i, 0))
    tab = pl.BlockSpec((TM_IN, LANES), lambda bi, i: (bi * nt + i, 0))
    bf = lambda w: jax.ShapeDtypeStruct((b, s, w), _BF16)
    ret_tables = _retention_tables()
    cast_in, cast_out, cast_shapes = _cast_plumbing(later_weights, (b, nt))
    n_in = 14 + len(ret_tables)
    return pl.pallas_call(
        _with_casts(_in_proj_kernel, n_in, 4, len(later_weights)),
        out_shape=(bf(MLA_QK_WIDTH), bf(MLA_QK_WIDTH),
                   jax.ShapeDtypeStruct((b, s // TQ_MLA, MLA_WIDTH, TQ_MLA), _BF16), bf(RET_WIDTH), *cast_shapes),
        grid=(b, nt),
        in_specs=[tok(d), _const_spec((1, d)), _const_spec(wlat_t.shape)] + [_const_spec(w.shape) for w in wret_t]
                 + [_const_spec((1, Q_LORA)), _const_spec(wuq_p.shape), _const_spec((1, KV_LORA)),
                  _const_spec(wuk_p.shape), _const_spec(wuvt_p.shape), tab, tab]
                 + [_const_spec(t.shape) for t in ret_tables] + cast_in,
        out_specs=(tok(MLA_QK_WIDTH), tok(MLA_QK_WIDTH),
                   pl.BlockSpec((1, per, MLA_WIDTH, TQ_MLA), lambda bi, i: (bi, i, 0, 0)), tok(RET_WIDTH), *cast_out),
        scratch_shapes=[pltpu.VMEM((N_RET_HEADS, RET_DK, RET_DV), _F32)],
        compiler_params=_params(("arbitrary", "arbitrary")),
        name="in_proj",
    )(x, g_mix.reshape(1, d), wlat_t, *wret_t, g_q_lat.reshape(1, Q_LORA), wuq_p, g_kv_lat.reshape(1, KV_LORA),
      wuk_p, wuvt_p, cos_t, sin_t, *ret_tables, *later_weights)


_MASK_VALUE = -0.7 * float(jnp.finfo(jnp.float32).max)
_MLA_AHEAD = 4
_MLA_TILES_PER_ITER = 5
_MLA_DENOM_ROWS = 16


def _mla_attn_kernel(qlo_ref, qhi_ref, k_ref, vt_ref, o_ref, q_ref, s_ref, m_ref, acc_ref):
    tq = TQ_MLA
    n_tiles = k_ref.shape[1] // tq
    lo = pl.program_id(1)
    hi = n_tiles - 1 - lo
    q_ref[0] = qlo_ref[0]
    q_ref[1] = qhi_ref[0]
    m_ref[...] = jnp.full(m_ref.shape, _MASK_VALUE, _F32)
    acc_ref[...] = jnp.zeros(acc_ref.shape, _F32)
    ones_rows = jnp.ones((_MLA_DENOM_ROWS, tq), _BF16)

    def step_args(t):
        sel = (t > lo).astype(jnp.int32)
        return sel, t - 1 - sel * lo

    def scores(sel, kk, hd):
        sl = slice(hd * HEAD_PAD, (hd + 1) * HEAD_PAD)
        ks = pl.multiple_of(kk * tq, tq)
        s_ref[hd, :tq, :] = _dot_nt(k_ref[0, pl.ds(ks, tq), sl], q_ref[sel, :, sl])

    def softmax(sel, hd, keep):
        row = slice(hd, hd + 1)

        def st():
            s = s_ref[hd, :tq, :]
            return s if keep is None else jnp.where(keep, s, _MASK_VALUE)

        m_prev = m_ref[sel, row, :]
        m_next = jnp.maximum(m_prev, jnp.max(st(), axis=0, keepdims=True))
        m_ref[sel, row, :] = m_next
        return jnp.exp2(m_prev - m_next), jnp.exp2(st() - m_next).astype(_BF16)

    def accumulate(sel, kk, hd, alpha, p):
        vt = jnp.concatenate([vt_ref[0, kk, hd * V_HEAD:(hd + 1) * V_HEAD, :], ones_rows], axis=0)
        acc_ref[sel, hd] = acc_ref[sel, hd] * alpha + _dot(vt, p)

    def tile(sel, kk, diagonal, nxt):
        keep = None
        if diagonal:
            key = lax.broadcasted_iota(jnp.int32, (tq, tq), 0)
            qry = lax.broadcasted_iota(jnp.int32, (tq, tq), 1)
            keep = key <= qry
        pending = None
        for hd in range(N_MLA_HEADS):
            ahead = hd + _MLA_AHEAD
            if ahead < N_MLA_HEADS:
                scores(sel, kk, ahead)
            elif nxt is not None:
                scores(nxt[0], nxt[1], ahead - N_MLA_HEADS)
            current = softmax(sel, hd, keep)
            if pending is not None:
                accumulate(sel, kk, hd - 1, *pending)
            pending = current
        accumulate(sel, kk, N_MLA_HEADS - 1, *pending)

    for hd in range(_MLA_AHEAD):
        scores(0, lo, hd)
    tile(0, lo, True, step_args(1))

    def body(it, carry):
        for u in range(_MLA_TILES_PER_ITER):
            t = 1 + it * _MLA_TILES_PER_ITER + u
            last = t + 1 == n_tiles
            sel_n, kk_n = step_args(t + 1)
            tile(*step_args(t), False, (jnp.where(last, 1, sel_n), jnp.where(last, hi, kk_n)))
        return carry

    lax.fori_loop(0, (n_tiles - 1) // _MLA_TILES_PER_ITER, body, 0)
    tile(1, hi, True, None)

    for sel, qt in ((0, lo), (1, hi)):
        out_t = jnp.concatenate([acc_ref[sel, hd, :V_HEAD, :] / acc_ref[sel, hd, V_HEAD:V_HEAD + 1, :]
                                 for hd in range(N_MLA_HEADS)], axis=0)
        o_ref[0, pl.ds(pl.multiple_of(qt * tq, tq), tq), :] = out_t.T.astype(o_ref.dtype)


def _mla_attn(q, k, vt, later_weights):
    b, s, w = q.shape
    nk = s // TQ_MLA
    assert nk % 2 == 0 and (nk - 1) % _MLA_TILES_PER_ITER == 0
    cast_in, cast_out, cast_shapes = _cast_plumbing(later_weights, (b, nk // 2))
    return pl.pallas_call(
        _with_casts(_mla_attn_kernel, 4, 1, len(later_weights)),
        out_shape=(jax.ShapeDtypeStruct((b, s, MLA_WIDTH), _BF16), *cast_shapes),
        grid=(b, nk // 2),
        in_specs=[pl.BlockSpec((1, TQ_MLA, w), lambda bi, i: (bi, i, 0)),
                  pl.BlockSpec((1, TQ_MLA, w), lambda bi, i: (bi, nk - 1 - i, 0)),
                  pl.BlockSpec((1, s, w), lambda bi, i: (bi, 0, 0)),
                  pl.BlockSpec((1, nk, MLA_WIDTH, TQ_MLA), lambda bi, i: (bi, 0, 0, 0))] + cast_in,
        out_specs=(pl.BlockSpec((1, s, MLA_WIDTH), lambda bi, i: (bi, 0, 0)), *cast_out),
        scratch_shapes=[pltpu.VMEM((2, TQ_MLA, w), _BF16),
                        pltpu.VMEM((N_MLA_HEADS, TQ_MLA + SUBLANES, TQ_MLA), _F32),
                        pltpu.VMEM((2, N_MLA_HEADS, TQ_MLA), _F32),
                        pltpu.VMEM((2, N_MLA_HEADS, V_HEAD + _MLA_DENOM_ROWS, TQ_MLA), _F32)],
        compiler_params=_params(("arbitrary", "arbitrary")),
        name="mla_attn",
    )(q, q, k, vt, *later_weights)


def _retention_tables():
    h, L = N_RET_HEADS, RET_CHUNK
    log_gamma = jnp.log(1.0 - 2.0 ** (-5.0 - jnp.arange(h, dtype=_F32)))
    j = jnp.arange(L, dtype=_F32)
    diff = j[:, None] - j[None, :]
    intra = jnp.where(diff[None] >= 0,
                      jnp.exp(jnp.maximum(diff, 0.0)[None] * log_gamma[:, None, None]), 0.0)
    rowb = lambda t: jnp.broadcast_to(t.T[:, :, None], (h, L, LANES))
    k_to_end = jnp.exp((L - 1 - j)[:, None] * log_gamma[None, :])
    q_from_start = jnp.exp((j + 1)[:, None] * log_gamma[None, :])
    chunk_decay = jnp.broadcast_to(jnp.exp(L * log_gamma)[:, None, None], (h, RET_DK, RET_DV))
    return intra, rowb(k_to_end), rowb(q_from_start), chunk_decay


def _mem_kv_kernel(mem_ref, g_ref, w_ref, kv_ref):
    mem_n = _rms(mem_ref[0], g_ref[...]).astype(_BF16)
    kv_ref[0] = _dot(mem_n, w_ref[...]).astype(_BF16)


def _mem_kv(mem, g_mem, w_xkv):
    b, m, d = mem.shape
    n = w_xkv.shape[1]
    return pl.pallas_call(
        _mem_kv_kernel,
        out_shape=jax.ShapeDtypeStruct((b, m, n), _BF16),
        grid=(b,),
        in_specs=[pl.BlockSpec((1, m, d), lambda bi: (bi, 0, 0)), _const_spec((1, d)), _const_spec((d, n))],
        out_specs=pl.BlockSpec((1, m, n), lambda bi: (bi, 0, 0)),
        compiler_params=_params(("arbitrary",)),
        name="mem_kv",
    )(mem, g_mem.reshape(1, d), w_xkv)


def _mix_xattn_kernel(x_ref, ymla_ref, yret_ref, wout_ref, gx_ref, wxq_ref, kv_ref, wxo_ref, o_ref):
    x1 = x_ref[0] + _dot(ymla_ref[0], wout_ref[:MLA_WIDTH, :]) + _dot(yret_ref[0], wout_ref[MLA_WIDTH:, :])
    h = _rms(x1, gx_ref[...]).astype(_BF16)
    q = (_dot(h, wxq_ref[...]) * (1.0 / math.sqrt(XATTN_HEAD))).astype(_BF16)
    cols = lambda hd: slice(hd * XATTN_HEAD, (hd + 1) * XATTN_HEAD)
    vcols = lambda hd: slice(D_MODEL + hd * XATTN_HEAD, D_MODEL + (hd + 1) * XATTN_HEAD)
    scores = [_dot_nt(q[:, cols(hd)], kv_ref[0, :, cols(hd)]) for hd in range(N_XATTN_HEADS)]
    probs = [jnp.exp(s - jnp.max(s, axis=-1, keepdims=True)) for s in scores]
    heads = [(_dot(p.astype(_BF16), kv_ref[0, :, vcols(hd)]) / jnp.sum(p, axis=-1, keepdims=True)).astype(_BF16)
             for hd, p in enumerate(probs)]
    o_ref[0] = x1 + _dot(jnp.concatenate(heads, axis=1), wxo_ref[...])


def _mix_xattn(x, y_mla, y_ret, w_out, g_xattn, w_xq, kv_mem, w_xo, later_weights):
    b, s, d = x.shape
    m, n = kv_mem.shape[1:]
    tok = lambda w: pl.BlockSpec((1, TM_MIX, w), lambda bi, i: (bi, i, 0))
    cast_in, cast_out, cast_shapes = _cast_plumbing(later_weights, (b, s // TM_MIX))
    return pl.pallas_call(
        _with_casts(_mix_xattn_kernel, 8, 1, len(later_weights)),
        out_shape=(jax.ShapeDtypeStruct((b, s, d), _F32), *cast_shapes),
        grid=(b, s // TM_MIX),
        in_specs=[tok(d), tok(MLA_WIDTH), tok(RET_WIDTH), _const_spec(w_out.shape), _const_spec((1, d)),
                  _const_spec(w_xq.shape), pl.BlockSpec((1, m, n), lambda bi, i: (bi, 0, 0)),
                  _const_spec(w_xo.shape)] + cast_in,
        out_specs=(tok(d), *cast_out),
        compiler_params=_params(("arbitrary", "arbitrary")),
        name="mix_xattn",
    )(x, y_mla, y_ret, w_out, g_xattn.reshape(1, d), w_xq, kv_mem, w_xo, *later_weights)


def _conv_ffn_kernel(x_ref, g_ref, win_ref, cw_ref, cb_ref, wout_ref, gfin_ref, o_ref, gate_ref):
    tm = TM_FFN
    halo = SUBLANES

    @pl.when(pl.program_id(1) == 0)
    def _():
        gate_ref[:halo, :] = jnp.zeros((halo, D_FF), _F32)

    x = x_ref[0]
    h = _rms(x, g_ref[...]).astype(_BF16)
    gate = _dot(h, win_ref[:, :D_FF])
    up = _dot(h, win_ref[:, D_FF:])
    gate_ref[halo:, :] = gate
    conv = cb_ref[...] + gate * cw_ref[CONV_W - 1:CONV_W, :]
    for tap in range(CONV_W - 1):
        back = CONV_W - 1 - tap
        conv = conv + gate_ref[halo - back:halo - back + tm, :] * cw_ref[tap:tap + 1, :]
    act = (conv * jax.nn.sigmoid(conv) * up).astype(_BF16)
    x3 = x + _dot(act, wout_ref[...])
    o_ref[0] = _rms(x3, gfin_ref[...])
    gate_ref[:halo, :] = gate_ref[tm:, :]


def _conv_ffn(x, g_ffn, w_ffn_in, conv_w, conv_b, w_ffn_out, g_final):
    b, s, d = x.shape
    tok = pl.BlockSpec((1, TM_FFN, d), lambda bi, i: (bi, i, 0))
    return pl.pallas_call(
        _conv_ffn_kernel,
        out_shape=jax.ShapeDtypeStruct((b, s, d), _F32),
        grid=(b, s // TM_FFN),
        in_specs=[tok, _const_spec((1, d)), _const_spec(w_ffn_in.shape), _const_spec(conv_w.shape),
                  _const_spec((1, D_FF)), _const_spec(w_ffn_out.shape), _const_spec((1, d))],
        out_specs=tok,
        scratch_shapes=[pltpu.VMEM((TM_FFN + SUBLANES, D_FF), _F32)],
        compiler_params=_params(("arbitrary", "arbitrary")),
        name="conv_ffn",
    )(x, g_ffn.reshape(1, d), w_ffn_in, conv_w, conv_b.reshape(1, D_FF), w_ffn_out, g_final.reshape(1, d))


def kernel(x, mem, positions, g_mix, w_in, g_q_lat, w_uq, g_kv_lat, w_ukv, w_out, g_xattn, g_mem, w_xq,
           w_xkv, w_xo, g_ffn, w_ffn_in, conv_w, conv_b, w_ffn_out, g_final):
    assert w_in.shape[0] == 1, "one layer supported"
    l = 0
    w_in_t = w_in[l].T
    cos_t, sin_t, *wret_t = _rope_tables(positions, w_in_t)
    wlat_t, wuq_p, wuk_p, wuvt_p = _permute_small_weights(w_in_t, w_uq[l], w_ukv[l])
    q, k, vt, y_ret, w_out_b, w_xq_b, w_xo_b, w_xkv_b = _in_proj(
        x, g_mix[l], wlat_t, wret_t, g_q_lat[l], wuq_p, g_kv_lat[l], wuk_p, wuvt_p, cos_t, sin_t,
        (w_out[l], w_xq[l], w_xo[l], w_xkv[l]))
    y_mla, w_ffn_in_b = _mla_attn(q, k, vt, (w_ffn_in[l],))
    kv_mem = _mem_kv(mem, g_mem[l], w_xkv_b)
    x, w_ffn_out_b = _mix_xattn(x, y_mla, y_ret, w_out_b, g_xattn[l], w_xq_b, kv_mem, w_xo_b, (w_ffn_out[l],))
    return _conv_ffn(x, g_ffn[l], w_ffn_in_b, conv_w[l], conv_b[l], w_ffn_out_b, g_final)
```

```python
import functools
import math

import jax
import jax.numpy as jnp
from jax import lax
from jax.experimental import pallas as pl
from jax.experimental.pallas import tpu as pltpu

D_MODEL = 1024
EPS = 1e-6
ROPE_BASE = 10000.0
N_MLA_HEADS = 8
QK_NOPE = 64
QK_ROPE = 32
V_HEAD = 64
Q_LORA = 256
KV_LORA = 128
N_RET_HEADS = 4
RET_DK = 128
RET_DV = 128
RET_CHUNK = 128
MLA_WIDTH = N_MLA_HEADS * V_HEAD
RET_WIDTH = N_RET_HEADS * RET_DV
N_XATTN_HEADS = 4
XATTN_HEAD = D_MODEL // N_XATTN_HEADS
D_FF = 2816
CONV_W = 3

LANES = 128
SUBLANES = 8
_BF16_ROWS = 16
VMEM_LIMIT = 56 * 1024 * 1024

HEAD_PAD = LANES
ROPE_HALF = QK_ROPE // 2
MLA_QK_WIDTH = N_MLA_HEADS * HEAD_PAD
IN_LAT = Q_LORA + KV_LORA + HEAD_PAD

TM_TABLE = 1024
TM_IN = 512
TQ_MLA = 256
TM_MIX = 1024
TM_FFN = 512

_BF16 = jnp.bfloat16
_F32 = jnp.float32


def _dot(a, b):
    return jnp.dot(a, b, preferred_element_type=_F32)


def _dot_nt(a, b):
    return lax.dot_general(a, b, (((1,), (1,)), ((), ())), preferred_element_type=_F32)


def _dot_tn(a, b):
    return lax.dot_general(a, b, (((0,), (0,)), ((), ())), preferred_element_type=_F32)


def _rms(x, g):
    inv = lax.rsqrt(jnp.mean(x * x, axis=-1, keepdims=True) + EPS)
    return x * inv * g


def _const_spec(shape):
    nd = len(shape)
    return pl.BlockSpec(shape, lambda *_: (0,) * nd, pipeline_mode=pl.Buffered(1))


def _params(sem):
    return pltpu.CompilerParams(dimension_semantics=sem, vmem_limit_bytes=VMEM_LIMIT)


def _with_casts(body, n_in, n_out, n_cast):
    def kern(*refs):
        ins, rest = refs[:n_in], refs[n_in:]
        cast_in, rest = rest[:n_cast], rest[n_cast:]
        outs, rest = rest[:n_out], rest[n_out:]
        cast_out, scratch = rest[:n_cast], rest[n_cast:]
        for src, dst in zip(cast_in, cast_out):
            dst[...] = src[...].astype(dst.dtype)
        body(*ins, *outs, *scratch)
    return kern


def _cast_plumbing(weights, grid):
    steps = math.prod(grid)
    if len(grid) == 1:
        block_of = lambda i: (i, 0)
    else:
        block_of = lambda bi, i: (bi * grid[1] + i, 0)
    in_specs, out_specs, out_shapes = [], [], []
    for w in weights:
        rows, cols = w.shape
        assert rows % steps == 0 and (rows // steps) % _BF16_ROWS == 0, (w.shape, steps)
        spec = pl.BlockSpec((rows // steps, cols), block_of)
        in_specs.append(spec)
        out_specs.append(spec)
        out_shapes.append(jax.ShapeDtypeStruct(w.shape, _BF16))
    return in_specs, out_specs, out_shapes


def _prep_kernel(pos_ref, inv_ref, wq_ref, wk_ref, wv_ref, wg_ref, mem_ref, gmem_ref, wxkv_ref,
                 cos_ref, sin_ref, wq_out, wk_out, wv_out, wg_out, kv_ref, *, steps_per_batch):
    for r in range(TM_TABLE // LANES):
        col = jnp.broadcast_to(pos_ref[r:r + 1, :], (LANES, LANES)).T
        ang = col * inv_ref[...]
        cos_ref[r * LANES:(r + 1) * LANES, :] = jnp.cos(ang)
        sin_ref[r * LANES:(r + 1) * LANES, :] = jnp.sin(ang)
    for src, dst in ((wq_ref, wq_out), (wk_ref, wk_out), (wv_ref, wv_out), (wg_ref, wg_out)):
        dst[...] = src[...].astype(_BF16)

    @pl.when(pl.program_id(0) % steps_per_batch == 0)
    def _():
        mem_n = _rms(mem_ref[0], gmem_ref[...]).astype(_BF16)
        kv_ref[0] = _dot(mem_n, wxkv_ref[...].astype(_BF16)).astype(_BF16)


def _prep(positions, w_in_t, mem, g_mem, w_xkv):
    t = positions.size
    steps = t // TM_TABLE
    n_in, d = w_in_t.shape
    b, m, _ = mem.shape
    n_kv = w_xkv.shape[1]
    per_batch = steps // b
    assert steps % b == 0
    row0 = Q_LORA + KV_LORA + QK_ROPE
    rows = RET_WIDTH // steps
    assert RET_WIDTH % steps == 0 and rows % _BF16_ROWS == 0 and row0 % rows == 0
    assert n_in == row0 + 4 * RET_WIDTH
    pos = positions.astype(_F32).reshape(t // LANES, LANES)
    f_ret = 1.0 / (ROPE_BASE ** (jnp.arange(0, RET_DK, 2, dtype=_F32) / RET_DK))
    f_mla = 1.0 / (ROPE_BASE ** (jnp.arange(0, QK_ROPE, 2, dtype=_F32) / QK_ROPE))
    inv = jnp.concatenate([f_ret, f_mla, f_mla, jnp.zeros((HEAD_PAD - QK_NOPE - QK_ROPE,), _F32)])
    group = lambda j: pl.BlockSpec((rows, d), lambda i: (row0 // rows + j * steps + i, 0))
    table = pl.BlockSpec((TM_TABLE, LANES), lambda i: (i, 0))
    w_out = pl.BlockSpec((rows, d), lambda i: (i, 0))
    return pl.pallas_call(
        functools.partial(_prep_kernel, steps_per_batch=per_batch),
        out_shape=(jax.ShapeDtypeStruct((t, LANES), _F32), jax.ShapeDtypeStruct((t, LANES), _F32))
                  + (jax.ShapeDtypeStruct((RET_WIDTH, d), _BF16),) * 4
                  + (jax.ShapeDtypeStruct((b, m, n_kv), _BF16),),
        grid=(steps,),
        in_specs=[pl.BlockSpec((TM_TABLE // LANES, LANES), lambda i: (i, 0)),
                  pl.BlockSpec((1, LANES), lambda i: (0, 0)), group(0), group(1), group(2), group(3),
                  pl.BlockSpec((1, m, d), lambda i: (i // per_batch, 0, 0)), _const_spec((1, d)),
                  _const_spec(w_xkv.shape)],
        out_specs=(table, table, w_out, w_out, w_out, w_out,
                   pl.BlockSpec((1, m, n_kv), lambda i: (i // per_batch, 0, 0))),
        compiler_params=_params(("arbitrary",)),
        name="prep",
    )(pos, inv.reshape(1, LANES), w_in_t, w_in_t, w_in_t, w_in_t, mem, g_mem.reshape(1, d), w_xkv)


def _in_proj_kernel(x_ref, gmix_ref, wlat_ref, wrq_ref, wrk_ref, wrv_ref, wrg_ref, gq_ref, wuq_ref, gkv_ref, wuk_ref, wuvt_ref, cos_ref, sin_ref,
                    intra_ref, kend_ref, qstart_ref, decay_ref,
                    q_ref, k_ref, vt_ref, yret_ref, state_ref):
    h = _rms(x_ref[0], gmix_ref[...]).astype(_BF16)
    t_cos, t_sin = cos_ref[...], sin_ref[...]
    lane = lax.broadcasted_iota(jnp.int32, t_cos.shape, 1)
    low = lane < RET_DK // 2
    cos_r = jnp.where(low, t_cos, pltpu.roll(t_cos, RET_DK // 2, axis=1))
    sin_r = jnp.where(low, -t_sin, pltpu.roll(t_sin, RET_DK // 2, axis=1))
    cos_m = jnp.where(low, 1.0, t_cos)
    sin_m = jnp.where(low, 0.0, jnp.where(lane < QK_NOPE + ROPE_HALF, -t_sin, t_sin))

    first_half = lane < QK_NOPE + ROPE_HALF

    def swap_halves(t):
        return jnp.where(first_half, pltpu.roll(t, LANES - ROPE_HALF, axis=1), pltpu.roll(t, ROPE_HALF, axis=1))

    lat = _dot_nt(h, wlat_ref[...])
    c_q = lat[:, :Q_LORA]
    c_kv = lat[:, Q_LORA:Q_LORA + KV_LORA]
    kr = lat[:, Q_LORA + KV_LORA:IN_LAT]
    k_rope = kr * cos_m + swap_halves(kr) * sin_m

    cqn = _rms(c_q, gq_ref[...]).astype(_BF16)
    q = _dot(cqn, wuq_ref[...])
    scale = math.log2(math.e) / math.sqrt(QK_NOPE + QK_ROPE)
    for hd in range(N_MLA_HEADS):
        sl = slice(hd * HEAD_PAD, (hd + 1) * HEAD_PAD)
        q_ref[0, :, sl] = ((q[:, sl] * cos_m + swap_halves(q[:, sl]) * sin_m) * scale).astype(_BF16)

    ckvn = _rms(c_kv, gkv_ref[...]).astype(_BF16)
    k_nope = _dot(ckvn, wuk_ref[...])
    for hd in range(N_MLA_HEADS):
        sl = slice(hd * HEAD_PAD, (hd + 1) * HEAD_PAD)
        k_ref[0, :, sl] = (k_nope[:, sl] + k_rope).astype(_BF16)
    v_t = _dot_nt(wuvt_ref[...], ckvn).astype(_BF16)
    for j in range(TM_IN // TQ_MLA):
        vt_ref[0, j] = v_t[:, j * TQ_MLA:(j + 1) * TQ_MLA]

    def ret_rope(w_ref, mult):
        r = _dot_nt(h, w_ref[...])
        heads = []
        for hd in range(N_RET_HEADS):
            rh = r[:, hd * RET_DK:(hd + 1) * RET_DK]
            roped = rh * cos_r + pltpu.roll(rh, RET_DK // 2, axis=1) * sin_r
            if mult is not None:
                roped = roped * mult
            heads.append(roped.astype(_BF16))
        return heads

    rq = ret_rope(wrq_ref, None)
    rk = ret_rope(wrk_ref, RET_DK ** -0.5)
    rv = _dot_nt(h, wrv_ref[...]).astype(_BF16)
    rg = _dot_nt(h, wrg_ref[...]).astype(_BF16)
    _retention_tile(rq, rk, rv, rg, intra_ref, kend_ref, qstart_ref, decay_ref, state_ref, yret_ref)


def _retention_tile(q_heads, k_heads, v, g, intra_ref, kend_ref, qstart_ref, decay_ref, state_ref, o_ref):
    @pl.when(pl.program_id(1) == 0)
    def _():
        state_ref[...] = jnp.zeros(state_ref.shape, _F32)

    L = RET_CHUNK
    n_chunks = v.shape[0] // L
    units = [(c, hd) for c in range(n_chunks) for hd in range(N_RET_HEADS)]
    rows = lambda c: slice(c * L, (c + 1) * L)
    cols = lambda hd: slice(hd * RET_DV, (hd + 1) * RET_DV)
    scores, chunk_kv = {}, {}
    for c, hd in units:
        scores[c, hd] = _dot_nt(q_heads[hd][rows(c)], k_heads[hd][rows(c)])
    for c, hd in units:
        v_dec = (v[rows(c), cols(hd)].astype(_F32) * kend_ref[hd]).astype(_BF16)
        chunk_kv[c, hd] = _dot_tn(k_heads[hd][rows(c)], v_dec)
    prev_state = {}
    for hd in range(N_RET_HEADS):
        state = state_ref[hd]
        for c in range(n_chunks):
            prev_state[c, hd] = state.astype(_BF16)
            state = decay_ref[hd] * state + chunk_kv[c, hd]
        state_ref[hd] = state
    inner, cross = {}, {}
    for c, hd in units:
        inner[c, hd] = _dot((scores[c, hd] * intra_ref[hd]).astype(_BF16), v[rows(c), cols(hd)])
    for c, hd in units:
        cross[c, hd] = _dot(q_heads[hd][rows(c)], prev_state[c, hd])
    for c, hd in units:
        out = inner[c, hd] + cross[c, hd] * qstart_ref[hd]
        mu = jnp.mean(out, axis=-1, keepdims=True)
        cen = out - mu
        var = jnp.mean(cen * cen, axis=-1, keepdims=True)
        normed = cen * lax.rsqrt(var + EPS)
        gate = g[rows(c), cols(hd)].astype(_F32)
        o_ref[0, rows(c), cols(hd)] = (normed * (gate * jax.nn.sigmoid(gate))).astype(o_ref.dtype)


def _permute_small_weights(w_in_t, w_uq, w_ukv):
    o_kr = Q_LORA + KV_LORA
    d = w_in_t.shape[1]
    wlat_t = jnp.concatenate([w_in_t[:o_kr], jnp.zeros((QK_NOPE, d), w_in_t.dtype), w_in_t[o_kr:o_kr + QK_ROPE],
                              jnp.zeros((HEAD_PAD - QK_NOPE - QK_ROPE, d), w_in_t.dtype)], axis=0).astype(_BF16)
    wq = w_uq.reshape(Q_LORA, N_MLA_HEADS, QK_NOPE + QK_ROPE)
    nope, r1, r2 = wq[..., :QK_NOPE], wq[..., QK_NOPE:QK_NOPE + ROPE_HALF], wq[..., QK_NOPE + ROPE_HALF:]
    z32 = jnp.zeros((Q_LORA, N_MLA_HEADS, HEAD_PAD - QK_NOPE - QK_ROPE), w_uq.dtype)
    wuq_p = jnp.concatenate([nope, r1, r2, z32], axis=-1).reshape(Q_LORA, MLA_QK_WIDTH).astype(_BF16)

    wkv = w_ukv.reshape(KV_LORA, N_MLA_HEADS, QK_NOPE + V_HEAD)
    zk = jnp.zeros((KV_LORA, N_MLA_HEADS, HEAD_PAD - QK_NOPE), w_ukv.dtype)
    wk = jnp.concatenate([wkv[..., :QK_NOPE], zk], axis=-1).reshape(KV_LORA, MLA_QK_WIDTH)
    wv_t = wkv[..., QK_NOPE:].reshape(KV_LORA, MLA_WIDTH).T
    return wlat_t, wuq_p, wk.astype(_BF16), wv_t.astype(_BF16)


def _in_proj(x, g_mix, wlat_t, wret_t, g_q_lat, wuq_p, g_kv_lat, wuk_p, wuvt_p, cos_t, sin_t, later_weights):
    b, s, d = x.shape
    nt = s // TM_IN
    per = TM_IN // TQ_MLA
    tok = lambda w: pl.BlockSpec((1, TM_IN, w), lambda bi, i: (bi, i, 0))
    tab = pl.BlockSpec((TM_IN, LANES), lambda bi, i: (bi * nt + i, 0))
    bf = lambda w: jax.ShapeDtypeStruct((b, s, w), _BF16)
    ret_tables = _retention_tables()
    cast_in, cast_out, cast_shapes = _cast_plumbing(later_weights, (b, nt))
    n_in = 14 + len(ret_tables)
    return pl.pallas_call(
        _with_casts(_in_proj_kernel, n_in, 4, len(later_weights)),
        out_shape=(bf(MLA_QK_WIDTH), bf(MLA_QK_WIDTH),
                   jax.ShapeDtypeStruct((b, s // TQ_MLA, MLA_WIDTH, TQ_MLA), _BF16), bf(RET_WIDTH), *cast_shapes),
        grid=(b, nt),
        in_specs=[tok(d), _const_spec((1, d)), _const_spec(wlat_t.shape)] + [_const_spec(w.shape) for w in wret_t]
                 + [_const_spec((1, Q_LORA)), _const_spec(wuq_p.shape), _const_spec((1, KV_LORA)),
                  _const_spec(wuk_p.shape), _const_spec(wuvt_p.shape), tab, tab]
                 + [_const_spec(t.shape) for t in ret_tables] + cast_in,
        out_specs=(tok(MLA_QK_WIDTH), tok(MLA_QK_WIDTH),
                   pl.BlockSpec((1, per, MLA_WIDTH, TQ_MLA), lambda bi, i: (bi, i, 0, 0)), tok(RET_WIDTH), *cast_out),
        scratch_shapes=[pltpu.VMEM((N_RET_HEADS, RET_DK, RET_DV), _F32)],
        compiler_params=_params(("arbitrary", "arbitrary")),
        name="in_proj",
    )(x, g_mix.reshape(1, d), wlat_t, *wret_t, g_q_lat.reshape(1, Q_LORA), wuq_p, g_kv_lat.reshape(1, KV_LORA),
      wuk_p, wuvt_p, cos_t, sin_t, *ret_tables, *later_weights)


_MASK_VALUE = -0.7 * float(jnp.finfo(jnp.float32).max)
_MLA_AHEAD = 4
_MLA_TILES_PER_ITER = 5
_MLA_DENOM_ROWS = 16


def _mla_attn_kernel(qlo_ref, qhi_ref, k_ref, vt_ref, o_ref, q_ref, s_ref, m_ref, acc_ref):
    tq = TQ_MLA
    n_tiles = k_ref.shape[1] // tq
    lo = pl.program_id(1)
    hi = n_tiles - 1 - lo
    q_ref[0] = qlo_ref[0]
    q_ref[1] = qhi_ref[0]
    m_ref[...] = jnp.full(m_ref.shape, _MASK_VALUE, _F32)
    acc_ref[...] = jnp.zeros(acc_ref.shape, _F32)
    ones_rows = jnp.ones((_MLA_DENOM_ROWS, tq), _BF16)

    def step_args(t):
        sel = (t > lo).astype(jnp.int32)
        return sel, t - 1 - sel * lo

    def scores(sel, kk, hd):
        sl = slice(hd * HEAD_PAD, (hd + 1) * HEAD_PAD)
        ks = pl.multiple_of(kk * tq, tq)
        s_ref[hd] = _dot_nt(k_ref[0, pl.ds(ks, tq), sl], q_ref[sel, :, sl])

    def softmax(sel, hd, keep):
        row = slice(hd, hd + 1)

        def st():
            s = s_ref[hd]
            return s if keep is None else jnp.where(keep, s, _MASK_VALUE)

        m_prev = m_ref[sel, row, :]
        m_next = jnp.maximum(m_prev, jnp.max(st(), axis=0, keepdims=True))
        m_ref[sel, row, :] = m_next
        return jnp.exp2(m_prev - m_next), jnp.exp2(st() - m_next).astype(_BF16)

    def accumulate(sel, kk, hd, alpha, p):
        vt = jnp.concatenate([vt_ref[0, kk, hd * V_HEAD:(hd + 1) * V_HEAD, :], ones_rows], axis=0)
        acc_ref[sel, hd] = acc_ref[sel, hd] * alpha + _dot(vt, p)

    def tile(sel, kk, diagonal, nxt):
        keep = None
        if diagonal:
            key = lax.broadcasted_iota(jnp.int32, (tq, tq), 0)
            qry = lax.broadcasted_iota(jnp.int32, (tq, tq), 1)
            keep = key <= qry
        pending = None
        for hd in range(N_MLA_HEADS):
            ahead = hd + _MLA_AHEAD
            if ahead < N_MLA_HEADS:
                scores(sel, kk, ahead)
            elif nxt is not None:
                scores(nxt[0], nxt[1], ahead - N_MLA_HEADS)
            current = softmax(sel, hd, keep)
            if pending is not None:
                accumulate(sel, kk, hd - 1, *pending)
            pending = current
        accumulate(sel, kk, N_MLA_HEADS - 1, *pending)

    for hd in range(_MLA_AHEAD):
        scores(0, lo, hd)
    tile(0, lo, True, step_args(1))

    def body(it, carry):
        for u in range(_MLA_TILES_PER_ITER):
            t = 1 + it * _MLA_TILES_PER_ITER + u
            last = t + 1 == n_tiles
            sel_n, kk_n = step_args(t + 1)
            tile(*step_args(t), False, (jnp.where(last, 1, sel_n), jnp.where(last, hi, kk_n)))
        return carry

    lax.fori_loop(0, (n_tiles - 1) // _MLA_TILES_PER_ITER, body, 0)
    tile(1, hi, True, None)

    for sel, qt in ((0, lo), (1, hi)):
        out_t = jnp.concatenate([acc_ref[sel, hd, :V_HEAD, :] / acc_ref[sel, hd, V_HEAD:V_HEAD + 1, :]
                                 for hd in range(N_MLA_HEADS)], axis=0)
        o_ref[0, pl.ds(pl.multiple_of(qt * tq, tq), tq), :] = out_t.T.astype(o_ref.dtype)


def _mla_attn(q, k, vt, later_weights):
    b, s, w = q.shape
    nk = s // TQ_MLA
    assert nk % 2 == 0 and (nk - 1) % _MLA_TILES_PER_ITER == 0
    cast_in, cast_out, cast_shapes = _cast_plumbing(later_weights, (b, nk // 2))
    return pl.pallas_call(
        _with_casts(_mla_attn_kernel, 4, 1, len(later_weights)),
        out_shape=(jax.ShapeDtypeStruct((b, s, MLA_WIDTH), _BF16), *cast_shapes),
        grid=(b, nk // 2),
        in_specs=[pl.BlockSpec((1, TQ_MLA, w), lambda bi, i: (bi, i, 0)),
                  pl.BlockSpec((1, TQ_MLA, w), lambda bi, i: (bi, nk - 1 - i, 0)),
                  pl.BlockSpec((1, s, w), lambda bi, i: (bi, 0, 0)),
                  pl.BlockSpec((1, nk, MLA_WIDTH, TQ_MLA), lambda bi, i: (bi, 0, 0, 0))] + cast_in,
        out_specs=(pl.BlockSpec((1, s, MLA_WIDTH), lambda bi, i: (bi, 0, 0)), *cast_out),
        scratch_shapes=[pltpu.VMEM((2, TQ_MLA, w), _BF16),
                        pltpu.VMEM((N_MLA_HEADS, TQ_MLA, TQ_MLA), _F32),
                        pltpu.VMEM((2, N_MLA_HEADS, TQ_MLA), _F32),
                        pltpu.VMEM((2, N_MLA_HEADS, V_HEAD + _MLA_DENOM_ROWS, TQ_MLA), _F32)],
        compiler_params=_params(("arbitrary", "arbitrary")),
        name="mla_attn",
    )(q, q, k, vt, *later_weights)


def _retention_tables():
    h, L = N_RET_HEADS, RET_CHUNK
    log_gamma = jnp.log(1.0 - 2.0 ** (-5.0 - jnp.arange(h, dtype=_F32)))
    j = jnp.arange(L, dtype=_F32)
    diff = j[:, None] - j[None, :]
    intra = jnp.where(diff[None] >= 0,
                      jnp.exp(jnp.maximum(diff, 0.0)[None] * log_gamma[:, None, None]), 0.0)
    rowb = lambda t: jnp.broadcast_to(t.T[:, :, None], (h, L, LANES))
    k_to_end = jnp.exp((L - 1 - j)[:, None] * log_gamma[None, :])
    q_from_start = jnp.exp((j + 1)[:, None] * log_gamma[None, :])
    chunk_decay = jnp.broadcast_to(jnp.exp(L * log_gamma)[:, None, None], (h, RET_DK, RET_DV))
    return intra, rowb(k_to_end), rowb(q_from_start), chunk_decay


def _mix_xattn_kernel(x_ref, ymla_ref, yret_ref, wout_ref, gx_ref, wxq_ref, kv_ref, wxo_ref, o_ref):
    x1 = x_ref[0] + _dot(ymla_ref[0], wout_ref[:MLA_WIDTH, :]) + _dot(yret_ref[0], wout_ref[MLA_WIDTH:, :])
    h = _rms(x1, gx_ref[...]).astype(_BF16)
    q = (_dot(h, wxq_ref[...]) * (1.0 / math.sqrt(XATTN_HEAD))).astype(_BF16)
    cols = lambda hd: slice(hd * XATTN_HEAD, (hd + 1) * XATTN_HEAD)
    vcols = lambda hd: slice(D_MODEL + hd * XATTN_HEAD, D_MODEL + (hd + 1) * XATTN_HEAD)
    scores = [_dot_nt(q[:, cols(hd)], kv_ref[0, :, cols(hd)]) for hd in range(N_XATTN_HEADS)]
    probs = [jnp.exp(s - jnp.max(s, axis=-1, keepdims=True)) for s in scores]
    heads = [(_dot(p.astype(_BF16), kv_ref[0, :, vcols(hd)]) / jnp.sum(p, axis=-1, keepdims=True)).astype(_BF16)
             for hd, p in enumerate(probs)]
    o_ref[0] = x1 + _dot(jnp.concatenate(heads, axis=1), wxo_ref[...])


def _mix_xattn(x, y_mla, y_ret, w_out, g_xattn, w_xq, kv_mem, w_xo, later_weights):
    b, s, d = x.shape
    m, n = kv_mem.shape[1:]
    tok = lambda w: pl.BlockSpec((1, TM_MIX, w), lambda bi, i: (bi, i, 0))
    cast_in, cast_out, cast_shapes = _cast_plumbing(later_weights, (b, s // TM_MIX))
    return pl.pallas_call(
        _with_casts(_mix_xattn_kernel, 8, 1, len(later_weights)),
        out_shape=(jax.ShapeDtypeStruct((b, s, d), _F32), *cast_shapes),
        grid=(b, s // TM_MIX),
        in_specs=[tok(d), tok(MLA_WIDTH), tok(RET_WIDTH), _const_spec(w_out.shape), _const_spec((1, d)),
                  _const_spec(w_xq.shape), pl.BlockSpec((1, m, n), lambda bi, i: (bi, 0, 0)),
                  _const_spec(w_xo.shape)] + cast_in,
        out_specs=(tok(d), *cast_out),
        compiler_params=_params(("arbitrary", "arbitrary")),
        name="mix_xattn",
    )(x, y_mla, y_ret, w_out, g_xattn.reshape(1, d), w_xq, kv_mem, w_xo, *later_weights)


def _conv_ffn_kernel(x_ref, g_ref, win_ref, cw_ref, cb_ref, wout_ref, gfin_ref, o_ref, gate_ref):
    halo = SUBLANES
    half = TM_FFN // 2

    @pl.when(pl.program_id(1) == 0)
    def _():
        gate_ref[:halo, :] = jnp.zeros((halo, D_FF), _F32)

    rows = [slice(0, half), slice(half, TM_FFN)]
    xs = [x_ref[0, r, :] for r in rows]
    hs = [_rms(x, g_ref[...]).astype(_BF16) for x in xs]

    def project(i):
        gate = _dot(hs[i], win_ref[:, :D_FF])
        up = _dot(hs[i], win_ref[:, D_FF:])
        gate_ref[halo + i * half:halo + (i + 1) * half, :] = gate
        return gate, up

    def activate(i, gate, up):
        conv = cb_ref[...] + gate * cw_ref[CONV_W - 1:CONV_W, :]
        for tap in range(CONV_W - 1):
            back = CONV_W - 1 - tap
            lo = halo + i * half - back
            conv = conv + gate_ref[lo:lo + half, :] * cw_ref[tap:tap + 1, :]
        return (conv * jax.nn.sigmoid(conv) * up).astype(_BF16)

    def finish(i, act):
        x3 = xs[i] + _dot(act, wout_ref[...])
        o_ref[0, rows[i], :] = _rms(x3, gfin_ref[...])

    g0, u0 = project(0)
    g1, u1 = project(1)
    a0 = activate(0, g0, u0)
    finish(0, a0)
    a1 = activate(1, g1, u1)
    finish(1, a1)
    gate_ref[:halo, :] = gate_ref[TM_FFN:, :]


def _conv_ffn(x, g_ffn, w_ffn_in, conv_w, conv_b, w_ffn_out, g_final):
    b, s, d = x.shape
    tok = pl.BlockSpec((1, TM_FFN, d), lambda bi, i: (bi, i, 0))
    return pl.pallas_call(
        _conv_ffn_kernel,
        out_shape=jax.ShapeDtypeStruct((b, s, d), _F32),
        grid=(b, s // TM_FFN),
        in_specs=[tok, _const_spec((1, d)), _const_spec(w_ffn_in.shape), _const_spec(conv_w.shape),
                  _const_spec((1, D_FF)), _const_spec(w_ffn_out.shape), _const_spec((1, d))],
        out_specs=tok,
        scratch_shapes=[pltpu.VMEM((TM_FFN + SUBLANES, D_FF), _F32)],
        compiler_params=_params(("arbitrary", "arbitrary")),
        name="conv_ffn",
    )(x, g_ffn.reshape(1, d), w_ffn_in, conv_w, conv_b.reshape(1, D_FF), w_ffn_out, g_final.reshape(1, d))


def kernel(x, mem, positions, g_mix, w_in, g_q_lat, w_uq, g_kv_lat, w_ukv, w_out, g_xattn, g_mem, w_xq,
           w_xkv, w_xo, g_ffn, w_ffn_in, conv_w, conv_b, w_ffn_out, g_final):
    assert w_in.shape[0] == 1, "one layer supported"
    l = 0
    w_in_t = w_in[l].T
    cos_t, sin_t, *wret_t, kv_mem = _prep(positions, w_in_t, mem, g_mem[l], w_xkv[l])
    wlat_t, wuq_p, wuk_p, wuvt_p = _permute_small_weights(w_in_t, w_uq[l], w_ukv[l])
    q, k, vt, y_ret, w_out_b, w_xq_b, w_xo_b = _in_proj(
        x, g_mix[l], wlat_t, wret_t, g_q_lat[l], wuq_p, g_kv_lat[l], wuk_p, wuvt_p, cos_t, sin_t,
        (w_out[l], w_xq[l], w_xo[l]))
    y_mla, w_ffn_in_b = _mla_attn(q, k, vt, (w_ffn_in[l],))
    x, w_ffn_out_b = _mix_xattn(x, y_mla, y_ret, w_out_b, g_xattn[l], w_xq_b, kv_mem, w_xo_b, (w_ffn_out[l],))
    return _conv_ffn(x, g_ffn[l], w_ffn_in_b, conv_w[l], conv_b[l], w_ffn_out_b, g_final)
```

```python
import functools
import math

import jax
import jax.numpy as jnp
from jax import lax
from jax.experimental import pallas as pl
from jax.experimental.pallas import tpu as pltpu

D_MODEL = 1024
EPS = 1e-6
ROPE_BASE = 10000.0
N_MLA_HEADS = 8
QK_NOPE = 64
QK_ROPE = 32
V_HEAD = 64
Q_LORA = 256
KV_LORA = 128
N_RET_HEADS = 4
RET_DK = 128
RET_DV = 128
RET_CHUNK = 128
MLA_WIDTH = N_MLA_HEADS * V_HEAD
RET_WIDTH = N_RET_HEADS * RET_DV
N_XATTN_HEADS = 4
XATTN_HEAD = D_MODEL // N_XATTN_HEADS
D_FF = 2816
CONV_W = 3

LANES = 128
SUBLANES = 8
_BF16_ROWS = 16
VMEM_LIMIT = 56 * 1024 * 1024

HEAD_PAD = LANES
ROPE_HALF = QK_ROPE // 2
MLA_QK_WIDTH = N_MLA_HEADS * HEAD_PAD
IN_LAT = Q_LORA + KV_LORA + HEAD_PAD

TM_TABLE = 1024
TM_IN = 512
TQ_MLA = 256
TM_MIX = 1024
TM_FFN = 512

_BF16 = jnp.bfloat16
_F32 = jnp.float32


def _dot(a, b):
    return jnp.dot(a, b, preferred_element_type=_F32)


def _dot_nt(a, b):
    return lax.dot_general(a, b, (((1,), (1,)), ((), ())), preferred_element_type=_F32)


def _dot_tn(a, b):
    return lax.dot_general(a, b, (((0,), (0,)), ((), ())), preferred_element_type=_F32)


def _rms(x, g):
    inv = lax.rsqrt(jnp.mean(x * x, axis=-1, keepdims=True) + EPS)
    return x * inv * g


def _const_spec(shape):
    nd = len(shape)
    return pl.BlockSpec(shape, lambda *_: (0,) * nd, pipeline_mode=pl.Buffered(1))


def _params(sem):
    return pltpu.CompilerParams(dimension_semantics=sem, vmem_limit_bytes=VMEM_LIMIT)


def _with_casts(body, n_in, n_out, n_cast):
    def kern(*refs):
        ins, rest = refs[:n_in], refs[n_in:]
        cast_in, rest = rest[:n_cast], rest[n_cast:]
        outs, rest = rest[:n_out], rest[n_out:]
        cast_out, scratch = rest[:n_cast], rest[n_cast:]
        for src, dst in zip(cast_in, cast_out):
            dst[...] = src[...].astype(dst.dtype)
        body(*ins, *outs, *scratch)
    return kern


def _cast_plumbing(weights, grid):
    steps = math.prod(grid)
    if len(grid) == 1:
        block_of = lambda i: (i, 0)
    else:
        block_of = lambda bi, i: (bi * grid[1] + i, 0)
    in_specs, out_specs, out_shapes = [], [], []
    for w in weights:
        rows, cols = w.shape
        assert rows % steps == 0 and (rows // steps) % _BF16_ROWS == 0, (w.shape, steps)
        spec = pl.BlockSpec((rows // steps, cols), block_of)
        in_specs.append(spec)
        out_specs.append(spec)
        out_shapes.append(jax.ShapeDtypeStruct(w.shape, _BF16))
    return in_specs, out_specs, out_shapes


def _prep_kernel(pos_ref, inv_ref, wq_ref, wk_ref, wv_ref, wg_ref, mem_ref, gmem_ref, wxkv_ref,
                 cos_ref, sin_ref, wq_out, wk_out, wv_out, wg_out, kv_ref, *, steps_per_batch):
    half_rows = TM_IN // 2 // LANES
    lane = lax.broadcasted_iota(jnp.int32, (LANES, LANES), 1)
    column = lambda r: jnp.broadcast_to(pos_ref[r:r + 1, :], (LANES, LANES)).T
    for tile in range(TM_TABLE // TM_IN):
        for rr in range(half_rows):
            r0 = tile * 2 * half_rows + rr
            pos = jnp.where(lane < RET_DK // 2, column(r0), column(r0 + half_rows))
            ang = pos * inv_ref[...]
            out = slice((tile * half_rows + rr) * LANES, (tile * half_rows + rr + 1) * LANES)
            cos_ref[out, :] = jnp.cos(ang)
            sin_ref[out, :] = jnp.sin(ang)
    for src, dst in ((wq_ref, wq_out), (wk_ref, wk_out), (wv_ref, wv_out), (wg_ref, wg_out)):
        dst[...] = src[...].astype(_BF16)

    @pl.when(pl.program_id(0) % steps_per_batch == 0)
    def _():
        mem_n = _rms(mem_ref[0], gmem_ref[...]).astype(_BF16)
        kv_ref[0] = _dot(mem_n, wxkv_ref[...].astype(_BF16)).astype(_BF16)


def _prep(positions, w_in_t, mem, g_mem, w_xkv):
    t = positions.size
    steps = t // TM_TABLE
    n_in, d = w_in_t.shape
    b, m, _ = mem.shape
    n_kv = w_xkv.shape[1]
    per_batch = steps // b
    assert steps % b == 0
    row0 = Q_LORA + KV_LORA + QK_ROPE
    rows = RET_WIDTH // steps
    assert RET_WIDTH % steps == 0 and rows % _BF16_ROWS == 0 and row0 % rows == 0
    assert n_in == row0 + 4 * RET_WIDTH
    pos = positions.astype(_F32).reshape(t // LANES, LANES)
    f_ret = 1.0 / (ROPE_BASE ** (jnp.arange(0, RET_DK, 2, dtype=_F32) / RET_DK))
    inv = jnp.concatenate([f_ret, f_ret])
    group = lambda j: pl.BlockSpec((rows, d), lambda i: (row0 // rows + j * steps + i, 0))
    table = pl.BlockSpec((TM_TABLE // 2, LANES), lambda i: (i, 0))
    w_out = pl.BlockSpec((rows, d), lambda i: (i, 0))
    return pl.pallas_call(
        functools.partial(_prep_kernel, steps_per_batch=per_batch),
        out_shape=(jax.ShapeDtypeStruct((t // 2, LANES), _F32),) * 2
                  + (jax.ShapeDtypeStruct((RET_WIDTH, d), _BF16),) * 4
                  + (jax.ShapeDtypeStruct((b, m, n_kv), _BF16),),
        grid=(steps,),
        in_specs=[pl.BlockSpec((TM_TABLE // LANES, LANES), lambda i: (i, 0)),
                  pl.BlockSpec((1, LANES), lambda i: (0, 0)), group(0), group(1), group(2), group(3),
                  pl.BlockSpec((1, m, d), lambda i: (i // per_batch, 0, 0)), _const_spec((1, d)),
                  _const_spec(w_xkv.shape)],
        out_specs=(table, table, w_out, w_out, w_out, w_out,
                   pl.BlockSpec((1, m, n_kv), lambda i: (i // per_batch, 0, 0))),
        compiler_params=_params(("arbitrary",)),
        name="prep",
    )(pos, inv.reshape(1, LANES), w_in_t, w_in_t, w_in_t, w_in_t, mem, g_mem.reshape(1, d), w_xkv)


def _in_proj_kernel(x_ref, gmix_ref, wlat_ref, wrq_ref, wrk_ref, wrv_ref, wrg_ref, gq_ref, wuq_ref, gkv_ref, wuk_ref, wuvt_ref, cos_ref, sin_ref,
                    intra_ref, kend_ref, qstart_ref, decay_ref,
                    q_ref, k_ref, vt_ref, yret_ref, state_ref):
    h = _rms(x_ref[0], gmix_ref[...]).astype(_BF16)
    lane = lax.broadcasted_iota(jnp.int32, (TM_IN, LANES), 1)
    low = lane < RET_DK // 2

    lo = lax.broadcasted_iota(jnp.int32, (TM_IN // 2, LANES), 1) < RET_DK // 2

    def unpack(t):
        t_roll = pltpu.roll(t, RET_DK // 2, axis=1)
        return jnp.concatenate([jnp.where(lo, t, t_roll), jnp.where(lo, t_roll, t)], axis=0)

    cos_r, sin_full = unpack(cos_ref[...]), unpack(sin_ref[...])
    sin_r = jnp.where(low, -sin_full, sin_full)
    ratio = (RET_DK // 2) // ROPE_HALF
    rope_lane = (lane >= QK_NOPE) & (lane < QK_NOPE + QK_ROPE)
    src = jnp.where(lane < QK_NOPE + ROPE_HALF, lane - QK_NOPE, lane - QK_NOPE - ROPE_HALF) * ratio
    src = jnp.where(rope_lane, src, 0)
    g_cos = jnp.take_along_axis(cos_r, src, axis=1)
    g_sin = jnp.take_along_axis(sin_full, src, axis=1)
    cos_m = jnp.where(rope_lane, g_cos, 1.0)
    sin_m = jnp.where(rope_lane, jnp.where(lane < QK_NOPE + ROPE_HALF, -g_sin, g_sin), 0.0)

    first_half = lane < QK_NOPE + ROPE_HALF

    def swap_halves(t):
        return jnp.where(first_half, pltpu.roll(t, LANES - ROPE_HALF, axis=1), pltpu.roll(t, ROPE_HALF, axis=1))

    lat = _dot_nt(h, wlat_ref[...])
    c_q = lat[:, :Q_LORA]
    c_kv = lat[:, Q_LORA:Q_LORA + KV_LORA]
    kr = lat[:, Q_LORA + KV_LORA:IN_LAT]
    k_rope = kr * cos_m + swap_halves(kr) * sin_m

    cqn = _rms(c_q, gq_ref[...]).astype(_BF16)
    q = _dot(cqn, wuq_ref[...])
    scale = math.log2(math.e) / math.sqrt(QK_NOPE + QK_ROPE)
    for hd in range(N_MLA_HEADS):
        sl = slice(hd * HEAD_PAD, (hd + 1) * HEAD_PAD)
        q_ref[0, :, sl] = ((q[:, sl] * cos_m + swap_halves(q[:, sl]) * sin_m) * scale).astype(_BF16)

    ckvn = _rms(c_kv, gkv_ref[...]).astype(_BF16)
    k_nope = _dot(ckvn, wuk_ref[...])
    for hd in range(N_MLA_HEADS):
        sl = slice(hd * HEAD_PAD, (hd + 1) * HEAD_PAD)
        k_ref[0, :, sl] = (k_nope[:, sl] + k_rope).astype(_BF16)
    v_t = _dot_nt(wuvt_ref[...], ckvn).astype(_BF16)
    for j in range(TM_IN // TQ_MLA):
        vt_ref[0, j] = v_t[:, j * TQ_MLA:(j + 1) * TQ_MLA]

    def ret_rope(w_ref, mult):
        r = _dot_nt(h, w_ref[...])
        heads = []
        for hd in range(N_RET_HEADS):
            rh = r[:, hd * RET_DK:(hd + 1) * RET_DK]
            roped = rh * cos_r + pltpu.roll(rh, RET_DK // 2, axis=1) * sin_r
            if mult is not None:
                roped = roped * mult
            heads.append(roped.astype(_BF16))
        return heads

    rq = ret_rope(wrq_ref, None)
    rk = ret_rope(wrk_ref, RET_DK ** -0.5)
    rv = _dot_nt(h, wrv_ref[...]).astype(_BF16)
    rg = _dot_nt(h, wrg_ref[...]).astype(_BF16)
    _retention_tile(rq, rk, rv, rg, intra_ref, kend_ref, qstart_ref, decay_ref, state_ref, yret_ref)


def _retention_tile(q_heads, k_heads, v, g, intra_ref, kend_ref, qstart_ref, decay_ref, state_ref, o_ref):
    @pl.when(pl.program_id(1) == 0)
    def _():
        state_ref[...] = jnp.zeros(state_ref.shape, _F32)

    L = RET_CHUNK
    n_chunks = v.shape[0] // L
    units = [(c, hd) for c in range(n_chunks) for hd in range(N_RET_HEADS)]
    rows = lambda c: slice(c * L, (c + 1) * L)
    cols = lambda hd: slice(hd * RET_DV, (hd + 1) * RET_DV)
    scores, chunk_kv = {}, {}
    for c, hd in units:
        scores[c, hd] = _dot_nt(q_heads[hd][rows(c)], k_heads[hd][rows(c)])
    for c, hd in units:
        v_dec = (v[rows(c), cols(hd)].astype(_F32) * kend_ref[hd]).astype(_BF16)
        chunk_kv[c, hd] = _dot_tn(k_heads[hd][rows(c)], v_dec)
    prev_state = {}
    for hd in range(N_RET_HEADS):
        state = state_ref[hd]
        for c in range(n_chunks):
            prev_state[c, hd] = state.astype(_BF16)
            state = decay_ref[hd] * state + chunk_kv[c, hd]
        state_ref[hd] = state
    inner, cross = {}, {}
    for c, hd in units:
        inner[c, hd] = _dot((scores[c, hd] * intra_ref[hd]).astype(_BF16), v[rows(c), cols(hd)])
    for c, hd in units:
        cross[c, hd] = _dot(q_heads[hd][rows(c)], prev_state[c, hd])
    for c, hd in units:
        out = inner[c, hd] + cross[c, hd] * qstart_ref[hd]
        mu = jnp.mean(out, axis=-1, keepdims=True)
        cen = out - mu
        var = jnp.mean(cen * cen, axis=-1, keepdims=True)
        normed = cen * lax.rsqrt(var + EPS)
        gate = g[rows(c), cols(hd)].astype(_F32)
        o_ref[0, rows(c), cols(hd)] = (normed * (gate * jax.nn.sigmoid(gate))).astype(o_ref.dtype)


def _permute_small_weights(w_in_t, w_uq, w_ukv):
    o_kr = Q_LORA + KV_LORA
    d = w_in_t.shape[1]
    wlat_t = jnp.concatenate([w_in_t[:o_kr], jnp.zeros((QK_NOPE, d), w_in_t.dtype), w_in_t[o_kr:o_kr + QK_ROPE],
                              jnp.zeros((HEAD_PAD - QK_NOPE - QK_ROPE, d), w_in_t.dtype)], axis=0).astype(_BF16)
    wq = w_uq.reshape(Q_LORA, N_MLA_HEADS, QK_NOPE + QK_ROPE)
    nope, r1, r2 = wq[..., :QK_NOPE], wq[..., QK_NOPE:QK_NOPE + ROPE_HALF], wq[..., QK_NOPE + ROPE_HALF:]
    z32 = jnp.zeros((Q_LORA, N_MLA_HEADS, HEAD_PAD - QK_NOPE - QK_ROPE), w_uq.dtype)
    wuq_p = jnp.concatenate([nope, r1, r2, z32], axis=-1).reshape(Q_LORA, MLA_QK_WIDTH).astype(_BF16)

    wkv = w_ukv.reshape(KV_LORA, N_MLA_HEADS, QK_NOPE + V_HEAD)
    zk = jnp.zeros((KV_LORA, N_MLA_HEADS, HEAD_PAD - QK_NOPE), w_ukv.dtype)
    wk = jnp.concatenate([wkv[..., :QK_NOPE], zk], axis=-1).reshape(KV_LORA, MLA_QK_WIDTH)
    wv_t = wkv[..., QK_NOPE:].reshape(KV_LORA, MLA_WIDTH).T
    return wlat_t, wuq_p, wk.astype(_BF16), wv_t.astype(_BF16)


def _in_proj(x, g_mix, wlat_t, wret_t, g_q_lat, wuq_p, g_kv_lat, wuk_p, wuvt_p, cos_t, sin_t, later_weights):
    b, s, d = x.shape
    nt = s // TM_IN
    per = TM_IN // TQ_MLA
    tok = lambda w: pl.BlockSpec((1, TM_IN, w), lambda bi, i: (bi, i, 0))
    tab = pl.BlockSpec((TM_IN // 2, LANES), lambda bi, i: (bi * nt + i, 0))
    bf = lambda w: jax.ShapeDtypeStruct((b, s, w), _BF16)
    ret_tables = _retention_tables()
    cast_in, cast_out, cast_shapes = _cast_plumbing(later_weights, (b, nt))
    n_in = 14 + len(ret_tables)
    return pl.pallas_call(
        _with_casts(_in_proj_kernel, n_in, 4, len(later_weights)),
        out_shape=(bf(MLA_QK_WIDTH), bf(MLA_QK_WIDTH),
                   jax.ShapeDtypeStruct((b, s // TQ_MLA, MLA_WIDTH, TQ_MLA), _BF16), bf(RET_WIDTH), *cast_shapes),
        grid=(b, nt),
        in_specs=[tok(d), _const_spec((1, d)), _const_spec(wlat_t.shape)] + [_const_spec(w.shape) for w in wret_t]
                 + [_const_spec((1, Q_LORA)), _const_spec(wuq_p.shape), _const_spec((1, KV_LORA)),
                  _const_spec(wuk_p.shape), _const_spec(wuvt_p.shape), tab, tab]
                 + [_const_spec(t.shape) for t in ret_tables] + cast_in,
        out_specs=(tok(MLA_QK_WIDTH), tok(MLA_QK_WIDTH),
                   pl.BlockSpec((1, per, MLA_WIDTH, TQ_MLA), lambda bi, i: (bi, i, 0, 0)), tok(RET_WIDTH), *cast_out),
        scratch_shapes=[pltpu.VMEM((N_RET_HEADS, RET_DK, RET_DV), _F32)],
        compiler_params=_params(("arbitrary", "arbitrary")),
        name="in_proj",
    )(x, g_mix.reshape(1, d), wlat_t, *wret_t, g_q_lat.reshape(1, Q_LORA), wuq_p, g_kv_lat.reshape(1, KV_LORA),
      wuk_p, wuvt_p, cos_t, sin_t, *ret_tables, *later_weights)


_MASK_VALUE = -0.7 * float(jnp.finfo(jnp.float32).max)
_MLA_AHEAD = 4
_MLA_TILES_PER_ITER = 5
_MLA_DENOM_ROWS = 16


def _mla_attn_kernel(qlo_ref, qhi_ref, k_ref, vt_ref, o_ref, q_ref, s_ref, m_ref, acc_ref):
    tq = TQ_MLA
    n_tiles = k_ref.shape[1] // tq
    lo = pl.program_id(1)
    hi = n_tiles - 1 - lo
    q_ref[0] = qlo_ref[0]
    q_ref[1] = qhi_ref[0]
    m_ref[...] = jnp.full(m_ref.shape, _MASK_VALUE, _F32)
    acc_ref[...] = jnp.zeros(acc_ref.shape, _F32)
    ones_rows = jnp.ones((_MLA_DENOM_ROWS, tq), _BF16)

    def step_args(t):
        sel = (t > lo).astype(jnp.int32)
        return sel, t - 1 - sel * lo

    def scores(sel, kk, hd):
        sl = slice(hd * HEAD_PAD, (hd + 1) * HEAD_PAD)
        ks = pl.multiple_of(kk * tq, tq)
        s_ref[hd] = _dot_nt(k_ref[0, pl.ds(ks, tq), sl], q_ref[sel, :, sl])

    def softmax(sel, hd, keep):
        row = slice(hd, hd + 1)

        def st():
            s = s_ref[hd]
            return s if keep is None else jnp.where(keep, s, _MASK_VALUE)

        m_prev = m_ref[sel, row, :]
        m_next = jnp.maximum(m_prev, jnp.max(st(), axis=0, keepdims=True))
        m_ref[sel, row, :] = m_next
        return jnp.exp2(m_prev - m_next), jnp.exp2(st() - m_next).astype(_BF16)

    def accumulate(sel, kk, hd, alpha, p):
        vt = jnp.concatenate([vt_ref[0, kk, hd * V_HEAD:(hd + 1) * V_HEAD, :], ones_rows], axis=0)
        acc_ref[sel, hd] = acc_ref[sel, hd] * alpha + _dot(vt, p)

    def tile(sel, kk, diagonal, nxt):
        keep = None
        if diagonal:
            key = lax.broadcasted_iota(jnp.int32, (tq, tq), 0)
            qry = lax.broadcasted_iota(jnp.int32, (tq, tq), 1)
            keep = key <= qry
        pending = None
        for hd in range(N_MLA_HEADS):
            ahead = hd + _MLA_AHEAD
            if ahead < N_MLA_HEADS:
                scores(sel, kk, ahead)
            elif nxt is not None:
                scores(nxt[0], nxt[1], ahead - N_MLA_HEADS)
            current = softmax(sel, hd, keep)
            if pending is not None:
                accumulate(sel, kk, hd - 1, *pending)
            pending = current
        accumulate(sel, kk, N_MLA_HEADS - 1, *pending)

    for hd in range(_MLA_AHEAD):
        scores(0, lo, hd)
    tile(0, lo, True, step_args(1))

    def body(it, carry):
        for u in range(_MLA_TILES_PER_ITER):
            t = 1 + it * _MLA_TILES_PER_ITER + u
            last = t + 1 == n_tiles
            sel_n, kk_n = step_args(t + 1)
            tile(*step_args(t), False, (jnp.where(last, 1, sel_n), jnp.where(last, hi, kk_n)))
        return carry

    lax.fori_loop(0, (n_tiles - 1) // _MLA_TILES_PER_ITER, body, 0)
    tile(1, hi, True, None)

    for sel, qt in ((0, lo), (1, hi)):
        out_t = jnp.concatenate([acc_ref[sel, hd, :V_HEAD, :] / acc_ref[sel, hd, V_HEAD:V_HEAD + 1, :]
                                 for hd in range(N_MLA_HEADS)], axis=0)
        o_ref[0, pl.ds(pl.multiple_of(qt * tq, tq), tq), :] = out_t.T.astype(o_ref.dtype)


def _mla_attn(q, k, vt, later_weights):
    b, s, w = q.shape
    nk = s // TQ_MLA
    assert nk % 2 == 0 and (nk - 1) % _MLA_TILES_PER_ITER == 0
    cast_in, cast_out, cast_shapes = _cast_plumbing(later_weights, (b, nk // 2))
    return pl.pallas_call(
        _with_casts(_mla_attn_kernel, 4, 1, len(later_weights)),
        out_shape=(jax.ShapeDtypeStruct((b, s, MLA_WIDTH), _BF16), *cast_shapes),
        grid=(b, nk // 2),
        in_specs=[pl.BlockSpec((1, TQ_MLA, w), lambda bi, i: (bi, i, 0)),
                  pl.BlockSpec((1, TQ_MLA, w), lambda bi, i: (bi, nk - 1 - i, 0)),
                  pl.BlockSpec((1, s, w), lambda bi, i: (bi, 0, 0)),
                  pl.BlockSpec((1, nk, MLA_WIDTH, TQ_MLA), lambda bi, i: (bi, 0, 0, 0))] + cast_in,
        out_specs=(pl.BlockSpec((1, s, MLA_WIDTH), lambda bi, i: (bi, 0, 0)), *cast_out),
        scratch_shapes=[pltpu.VMEM((2, TQ_MLA, w), _BF16),
                        pltpu.VMEM((N_MLA_HEADS, TQ_MLA, TQ_MLA), _F32),
                        pltpu.VMEM((2, N_MLA_HEADS, TQ_MLA), _F32),
                        pltpu.VMEM((2, N_MLA_HEADS, V_HEAD + _MLA_DENOM_ROWS, TQ_MLA), _F32)],
        compiler_params=_params(("arbitrary", "arbitrary")),
        name="mla_attn",
    )(q, q, k, vt, *later_weights)


def _retention_tables():
    h, L = N_RET_HEADS, RET_CHUNK
    log_gamma = jnp.log(1.0 - 2.0 ** (-5.0 - jnp.arange(h, dtype=_F32)))
    j = jnp.arange(L, dtype=_F32)
    diff = j[:, None] - j[None, :]
    intra = jnp.where(diff[None] >= 0,
                      jnp.exp(jnp.maximum(diff, 0.0)[None] * log_gamma[:, None, None]), 0.0)
    rowb = lambda t: jnp.broadcast_to(t.T[:, :, None], (h, L, LANES))
    k_to_end = jnp.exp((L - 1 - j)[:, None] * log_gamma[None, :])
    q_from_start = jnp.exp((j + 1)[:, None] * log_gamma[None, :])
    chunk_decay = jnp.broadcast_to(jnp.exp(L * log_gamma)[:, None, None], (h, RET_DK, RET_DV))
    return intra, rowb(k_to_end), rowb(q_from_start), chunk_decay


def _mix_xattn_kernel(x_ref, ymla_ref, yret_ref, wout_ref, gx_ref, wxq_ref, kv_ref, wxo_ref, o_ref):
    x1 = x_ref[0] + _dot(ymla_ref[0], wout_ref[:MLA_WIDTH, :]) + _dot(yret_ref[0], wout_ref[MLA_WIDTH:, :])
    h = _rms(x1, gx_ref[...]).astype(_BF16)
    q = (_dot(h, wxq_ref[...]) * (1.0 / math.sqrt(XATTN_HEAD))).astype(_BF16)
    cols = lambda hd: slice(hd * XATTN_HEAD, (hd + 1) * XATTN_HEAD)
    vcols = lambda hd: slice(D_MODEL + hd * XATTN_HEAD, D_MODEL + (hd + 1) * XATTN_HEAD)
    scores = [_dot_nt(q[:, cols(hd)], kv_ref[0, :, cols(hd)]) for hd in range(N_XATTN_HEADS)]
    probs = [jnp.exp(s - jnp.max(s, axis=-1, keepdims=True)) for s in scores]
    heads = [(_dot(p.astype(_BF16), kv_ref[0, :, vcols(hd)]) / jnp.sum(p, axis=-1, keepdims=True)).astype(_BF16)
             for hd, p in enumerate(probs)]
    o_ref[0] = x1 + _dot(jnp.concatenate(heads, axis=1), wxo_ref[...])


def _mix_xattn(x, y_mla, y_ret, w_out, g_xattn, w_xq, kv_mem, w_xo, later_weights):
    b, s, d = x.shape
    m, n = kv_mem.shape[1:]
    tok = lambda w: pl.BlockSpec((1, TM_MIX, w), lambda bi, i: (bi, i, 0))
    cast_in, cast_out, cast_shapes = _cast_plumbing(later_weights, (b, s // TM_MIX))
    return pl.pallas_call(
        _with_casts(_mix_xattn_kernel, 8, 1, len(later_weights)),
        out_shape=(jax.ShapeDtypeStruct((b, s, d), _F32), *cast_shapes),
        grid=(b, s // TM_MIX),
        in_specs=[tok(d), tok(MLA_WIDTH), tok(RET_WIDTH), _const_spec(w_out.shape), _const_spec((1, d)),
                  _const_spec(w_xq.shape), pl.BlockSpec((1, m, n), lambda bi, i: (bi, 0, 0)),
                  _const_spec(w_xo.shape)] + cast_in,
        out_specs=(tok(d), *cast_out),
        compiler_params=_params(("arbitrary", "arbitrary")),
        name="mix_xattn",
    )(x, y_mla, y_ret, w_out, g_xattn.reshape(1, d), w_xq, kv_mem, w_xo, *later_weights)


def _conv_ffn_kernel(x_ref, g_ref, win_ref, cw_ref, cb_ref, wout_ref, gfin_ref, o_ref, gate_ref):
    halo = SUBLANES
    half = TM_FFN // 2

    @pl.when(pl.program_id(1) == 0)
    def _():
        gate_ref[:halo, :] = jnp.zeros((halo, D_FF), _F32)

    rows = [slice(0, half), slice(half, TM_FFN)]
    xs = [x_ref[0, r, :] for r in rows]
    hs = [_rms(x, g_ref[...]).astype(_BF16) for x in xs]

    def project(i):
        gate = _dot(hs[i], win_ref[:, :D_FF])
        up = _dot(hs[i], win_ref[:, D_FF:])
        gate_ref[halo + i * half:halo + (i + 1) * half, :] = gate
        return gate, up

    def activate(i, gate, up):
        conv = cb_ref[...] + gate * cw_ref[CONV_W - 1:CONV_W, :]
        for tap in range(CONV_W - 1):
            back = CONV_W - 1 - tap
            lo = halo + i * half - back
            conv = conv + gate_ref[lo:lo + half, :] * cw_ref[tap:tap + 1, :]
        return (conv * jax.nn.sigmoid(conv) * up).astype(_BF16)

    def finish(i, act):
        x3 = xs[i] + _dot(act, wout_ref[...])
        o_ref[0, rows[i], :] = _rms(x3, gfin_ref[...])

    g0, u0 = project(0)
    g1, u1 = project(1)
    a0 = activate(0, g0, u0)
    finish(0, a0)
    a1 = activate(1, g1, u1)
    finish(1, a1)
    gate_ref[:halo, :] = gate_ref[TM_FFN:, :]


def _conv_ffn(x, g_ffn, w_ffn_in, conv_w, conv_b, w_ffn_out, g_final):
    b, s, d = x.shape
    tok = pl.BlockSpec((1, TM_FFN, d), lambda bi, i: (bi, i, 0))
    return pl.pallas_call(
        _conv_ffn_kernel,
        out_shape=jax.ShapeDtypeStruct((b, s, d), _F32),
        grid=(b, s // TM_FFN),
        in_specs=[tok, _const_spec((1, d)), _const_spec(w_ffn_in.shape), _const_spec(conv_w.shape),
                  _const_spec((1, D_FF)), _const_spec(w_ffn_out.shape), _const_spec((1, d))],
        out_specs=tok,
        scratch_shapes=[pltpu.VMEM((TM_FFN + SUBLANES, D_FF), _F32)],
        compiler_params=_params(("arbitrary", "arbitrary")),
        name="conv_ffn",
    )(x, g_ffn.reshape(1, d), w_ffn_in, conv_w, conv_b.reshape(1, D_FF), w_ffn_out, g_final.reshape(1, d))


def kernel(x, mem, positions, g_mix, w_in, g_q_lat, w_uq, g_kv_lat, w_ukv, w_out, g_xattn, g_mem, w_xq,
           w_xkv, w_xo, g_ffn, w_ffn_in, conv_w, conv_b, w_ffn_out, g_final):
    assert w_in.shape[0] == 1, "one layer supported"
    l = 0
    w_in_t = w_in[l].T
    cos_t, sin_t, *wret_t, kv_mem = _prep(positions, w_in_t, mem, g_mem[l], w_xkv[l])
    wlat_t, wuq_p, wuk_p, wuvt_p = _permute_small_weights(w_in_t, w_uq[l], w_ukv[l])
    q, k, vt, y_ret, w_out_b, w_xq_b, w_xo_b = _in_proj(
        x, g_mix[l], wlat_t, wret_t, g_q_lat[l], wuq_p, g_kv_lat[l], wuk_p, wuvt_p, cos_t, sin_t,
        (w_out[l], w_xq[l], w_xo[l]))
    y_mla, w_ffn_in_b = _mla_attn(q, k, vt, (w_ffn_in[l],))
    x, w_ffn_out_b = _mix_xattn(x, y_mla, y_ret, w_out_b, g_xattn[l], w_xq_b, kv_mem, w_xo_b, (w_ffn_out[l],))
    return _conv_ffn(x, g_ffn[l], w_ffn_in_b, conv_w[l], conv_b[l], w_ffn_out_b, g_final)
```

```python
import functools
import math

import jax
import jax.numpy as jnp
from jax import lax
from jax.experimental import pallas as pl
from jax.experimental.pallas import tpu as pltpu

D_MODEL = 1024
EPS = 1e-6
ROPE_BASE = 10000.0
N_MLA_HEADS = 8
QK_NOPE = 64
QK_ROPE = 32
V_HEAD = 64
Q_LORA = 256
KV_LORA = 128
N_RET_HEADS = 4
RET_DK = 128
RET_DV = 128
RET_CHUNK = 128
MLA_WIDTH = N_MLA_HEADS * V_HEAD
RET_WIDTH = N_RET_HEADS * RET_DV
N_XATTN_HEADS = 4
XATTN_HEAD = D_MODEL // N_XATTN_HEADS
D_FF = 2816
CONV_W = 3

LANES = 128
SUBLANES = 8
_BF16_ROWS = 16
VMEM_LIMIT = 56 * 1024 * 1024

HEAD_PAD = LANES
ROPE_HALF = QK_ROPE // 2
MLA_QK_WIDTH = N_MLA_HEADS * HEAD_PAD
IN_LAT = Q_LORA + KV_LORA + HEAD_PAD

TM_TABLE = 1024
TM_IN = 512
TQ_MLA = 256
TM_MIX = 1024
TM_FFN = 512

_BF16 = jnp.bfloat16
_F32 = jnp.float32


def _dot(a, b):
    return jnp.dot(a, b, preferred_element_type=_F32)


def _dot_nt(a, b):
    return lax.dot_general(a, b, (((1,), (1,)), ((), ())), preferred_element_type=_F32)


def _dot_tn(a, b):
    return lax.dot_general(a, b, (((0,), (0,)), ((), ())), preferred_element_type=_F32)


def _rms(x, g):
    inv = lax.rsqrt(jnp.mean(x * x, axis=-1, keepdims=True) + EPS)
    return x * inv * g


def _const_spec(shape):
    nd = len(shape)
    return pl.BlockSpec(shape, lambda *_: (0,) * nd, pipeline_mode=pl.Buffered(1))


def _params(sem):
    return pltpu.CompilerParams(dimension_semantics=sem, vmem_limit_bytes=VMEM_LIMIT)


def _with_casts(body, n_in, n_out, n_cast):
    def kern(*refs):
        ins, rest = refs[:n_in], refs[n_in:]
        cast_in, rest = rest[:n_cast], rest[n_cast:]
        outs, rest = rest[:n_out], rest[n_out:]
        cast_out, scratch = rest[:n_cast], rest[n_cast:]
        for src, dst in zip(cast_in, cast_out):
            dst[...] = src[...].astype(dst.dtype)
        body(*ins, *outs, *scratch)
    return kern


def _cast_plumbing(weights, grid):
    steps = math.prod(grid)
    if len(grid) == 1:
        block_of = lambda i: (i, 0)
    else:
        block_of = lambda bi, i: (bi * grid[1] + i, 0)
    in_specs, out_specs, out_shapes = [], [], []
    for w in weights:
        rows, cols = w.shape
        assert rows % steps == 0 and (rows // steps) % _BF16_ROWS == 0, (w.shape, steps)
        spec = pl.BlockSpec((rows // steps, cols), block_of)
        in_specs.append(spec)
        out_specs.append(spec)
        out_shapes.append(jax.ShapeDtypeStruct(w.shape, _BF16))
    return in_specs, out_specs, out_shapes


def _prep_kernel(pos_ref, inv_ref, wq_ref, wk_ref, wv_ref, wg_ref, mem_ref, gmem_ref, wxkv_ref,
                 cos_ref, sin_ref, wq_out, wk_out, wv_out, wg_out, kv_ref, *, steps_per_batch):
    half_rows = TM_IN // 2 // LANES
    lane = lax.broadcasted_iota(jnp.int32, (LANES, LANES), 1)
    column = lambda r: jnp.broadcast_to(pos_ref[r:r + 1, :], (LANES, LANES)).T
    for tile in range(TM_TABLE // TM_IN):
        for rr in range(half_rows):
            r0 = tile * 2 * half_rows + rr
            pos = jnp.where(lane < RET_DK // 2, column(r0), column(r0 + half_rows))
            ang = pos * inv_ref[...]
            out = slice((tile * half_rows + rr) * LANES, (tile * half_rows + rr + 1) * LANES)
            cos_ref[out, :] = jnp.cos(ang)
            sin_ref[out, :] = jnp.sin(ang)
    for src, dst in ((wq_ref, wq_out), (wk_ref, wk_out), (wv_ref, wv_out), (wg_ref, wg_out)):
        dst[...] = src[...].astype(_BF16)

    @pl.when(pl.program_id(0) % steps_per_batch == 0)
    def _():
        mem_n = _rms(mem_ref[0], gmem_ref[...]).astype(_BF16)
        kv_ref[0] = _dot(mem_n, wxkv_ref[...].astype(_BF16)).astype(_BF16)


def _prep(positions, w_in_t, mem, g_mem, w_xkv):
    t = positions.size
    steps = t // TM_TABLE
    n_in, d = w_in_t.shape
    b, m, _ = mem.shape
    n_kv = w_xkv.shape[1]
    per_batch = steps // b
    assert steps % b == 0
    row0 = Q_LORA + KV_LORA + QK_ROPE
    rows = RET_WIDTH // steps
    assert RET_WIDTH % steps == 0 and rows % _BF16_ROWS == 0 and row0 % rows == 0
    assert n_in == row0 + 4 * RET_WIDTH
    pos = positions.astype(_F32).reshape(t // LANES, LANES)
    f_ret = 1.0 / (ROPE_BASE ** (jnp.arange(0, RET_DK, 2, dtype=_F32) / RET_DK))
    inv = jnp.concatenate([f_ret, f_ret])
    group = lambda j: pl.BlockSpec((rows, d), lambda i: (row0 // rows + j * steps + i, 0))
    table = pl.BlockSpec((TM_TABLE // 2, LANES), lambda i: (i, 0))
    w_out = pl.BlockSpec((rows, d), lambda i: (i, 0))
    return pl.pallas_call(
        functools.partial(_prep_kernel, steps_per_batch=per_batch),
        out_shape=(jax.ShapeDtypeStruct((t // 2, LANES), _F32),) * 2
                  + (jax.ShapeDtypeStruct((RET_WIDTH, d), _BF16),) * 4
                  + (jax.ShapeDtypeStruct((b, m, n_kv), _BF16),),
        grid=(steps,),
        in_specs=[pl.BlockSpec((TM_TABLE // LANES, LANES), lambda i: (i, 0)),
                  pl.BlockSpec((1, LANES), lambda i: (0, 0)), group(0), group(1), group(2), group(3),
                  pl.BlockSpec((1, m, d), lambda i: (i // per_batch, 0, 0)), _const_spec((1, d)),
                  _const_spec(w_xkv.shape)],
        out_specs=(table, table, w_out, w_out, w_out, w_out,
                   pl.BlockSpec((1, m, n_kv), lambda i: (i // per_batch, 0, 0))),
        compiler_params=_params(("arbitrary",)),
        name="prep",
    )(pos, inv.reshape(1, LANES), w_in_t, w_in_t, w_in_t, w_in_t, mem, g_mem.reshape(1, d), w_xkv)


def _in_proj_kernel(x_ref, gmix_ref, wlat_ref, wrq_ref, wrk_ref, wrv_ref, wrg_ref, gq_ref, wuq_ref, gkv_ref, wuk_ref, wuvt_ref, cos_ref, sin_ref,
                    intra_ref, kend_ref, qstart_ref, decay_ref,
                    q_ref, k_ref, vt_ref, yret_ref, state_ref):
    h = _rms(x_ref[0], gmix_ref[...]).astype(_BF16)
    lane = lax.broadcasted_iota(jnp.int32, (TM_IN, LANES), 1)
    low = lane < RET_DK // 2

    lo = lax.broadcasted_iota(jnp.int32, (TM_IN // 2, LANES), 1) < RET_DK // 2

    def unpack(t):
        t_roll = pltpu.roll(t, RET_DK // 2, axis=1)
        return jnp.concatenate([jnp.where(lo, t, t_roll), jnp.where(lo, t_roll, t)], axis=0)

    cos_r, sin_full = unpack(cos_ref[...]), unpack(sin_ref[...])
    sin_r = jnp.where(low, -sin_full, sin_full)
    ratio = (RET_DK // 2) // ROPE_HALF
    rope_lane = (lane >= QK_NOPE) & (lane < QK_NOPE + QK_ROPE)
    src = jnp.where(lane < QK_NOPE + ROPE_HALF, lane - QK_NOPE, lane - QK_NOPE - ROPE_HALF) * ratio
    src = jnp.where(rope_lane, src, 0)
    g_cos = jnp.take_along_axis(cos_r, src, axis=1)
    g_sin = jnp.take_along_axis(sin_full, src, axis=1)
    cos_m = jnp.where(rope_lane, g_cos, 1.0)
    sin_m = jnp.where(rope_lane, jnp.where(lane < QK_NOPE + ROPE_HALF, -g_sin, g_sin), 0.0)

    first_half = lane < QK_NOPE + ROPE_HALF

    def swap_halves(t):
        return jnp.where(first_half, pltpu.roll(t, LANES - ROPE_HALF, axis=1), pltpu.roll(t, ROPE_HALF, axis=1))

    lat = _dot_nt(h, wlat_ref[...])
    c_q = lat[:, :Q_LORA]
    c_kv = lat[:, Q_LORA:Q_LORA + KV_LORA]
    kr = lat[:, Q_LORA + KV_LORA:IN_LAT]
    k_rope = kr * cos_m + swap_halves(kr) * sin_m

    cqn = _rms(c_q, gq_ref[...]).astype(_BF16)
    q = _dot(cqn, wuq_ref[...])
    scale = math.log2(math.e) / math.sqrt(QK_NOPE + QK_ROPE)
    for hd in range(N_MLA_HEADS):
        sl = slice(hd * HEAD_PAD, (hd + 1) * HEAD_PAD)
        q_ref[0, :, sl] = ((q[:, sl] * cos_m + swap_halves(q[:, sl]) * sin_m) * scale).astype(_BF16)

    ckvn = _rms(c_kv, gkv_ref[...]).astype(_BF16)
    k_nope = _dot(ckvn, wuk_ref[...])
    for hd in range(N_MLA_HEADS):
        sl = slice(hd * HEAD_PAD, (hd + 1) * HEAD_PAD)
        k_ref[0, :, sl] = (k_nope[:, sl] + k_rope).astype(_BF16)
    v_t = _dot_nt(wuvt_ref[...], ckvn).astype(_BF16)
    for j in range(TM_IN // TQ_MLA):
        vt_ref[0, j] = v_t[:, j * TQ_MLA:(j + 1) * TQ_MLA]

    def ret_rope(w_ref, mult):
        r = _dot_nt(h, w_ref[...])
        heads = []
        for hd in range(N_RET_HEADS):
            rh = r[:, hd * RET_DK:(hd + 1) * RET_DK]
            roped = rh * cos_r + pltpu.roll(rh, RET_DK // 2, axis=1) * sin_r
            if mult is not None:
                roped = roped * mult
            heads.append(roped.astype(_BF16))
        return heads

    rq = ret_rope(wrq_ref, None)
    rk = ret_rope(wrk_ref, RET_DK ** -0.5)
    rv = _dot_nt(h, wrv_ref[...]).astype(_BF16)
    rg = _dot_nt(h, wrg_ref[...]).astype(_BF16)
    _retention_tile(rq, rk, rv, rg, intra_ref, kend_ref, qstart_ref, decay_ref, state_ref, yret_ref)


def _retention_tile(q_heads, k_heads, v, g, intra_ref, kend_ref, qstart_ref, decay_ref, state_ref, o_ref):
    @pl.when(pl.program_id(1) == 0)
    def _():
        state_ref[...] = jnp.zeros(state_ref.shape, _F32)

    L = RET_CHUNK
    n_chunks = v.shape[0] // L
    units = [(c, hd) for c in range(n_chunks) for hd in range(N_RET_HEADS)]
    rows = lambda c: slice(c * L, (c + 1) * L)
    cols = lambda hd: slice(hd * RET_DV, (hd + 1) * RET_DV)
    scores, chunk_kv = {}, {}
    for c, hd in units:
        scores[c, hd] = _dot_nt(q_heads[hd][rows(c)], k_heads[hd][rows(c)])
    for c, hd in units:
        v_dec = (v[rows(c), cols(hd)].astype(_F32) * kend_ref[hd]).astype(_BF16)
        chunk_kv[c, hd] = _dot_tn(k_heads[hd][rows(c)], v_dec)
    prev_state = {}
    for hd in range(N_RET_HEADS):
        state = state_ref[hd]
        for c in range(n_chunks):
            prev_state[c, hd] = state.astype(_BF16)
            state = decay_ref[hd] * state + chunk_kv[c, hd]
        state_ref[hd] = state
    inner, cross = {}, {}
    for c, hd in units:
        inner[c, hd] = _dot((scores[c, hd] * intra_ref[hd]).astype(_BF16), v[rows(c), cols(hd)])
    for c, hd in units:
        cross[c, hd] = _dot(q_heads[hd][rows(c)], prev_state[c, hd])
    for c, hd in units:
        out = inner[c, hd] + cross[c, hd] * qstart_ref[hd]
        mu = jnp.mean(out, axis=-1, keepdims=True)
        cen = out - mu
        var = jnp.mean(cen * cen, axis=-1, keepdims=True)
        normed = cen * lax.rsqrt(var + EPS)
        gate = g[rows(c), cols(hd)].astype(_F32)
        o_ref[0, rows(c), cols(hd)] = (normed * (gate * jax.nn.sigmoid(gate))).astype(o_ref.dtype)


def _permute_small_weights(w_in_t, w_uq, w_ukv):
    o_kr = Q_LORA + KV_LORA
    d = w_in_t.shape[1]
    wlat_t = jnp.concatenate([w_in_t[:o_kr], jnp.zeros((QK_NOPE, d), w_in_t.dtype), w_in_t[o_kr:o_kr + QK_ROPE],
                              jnp.zeros((HEAD_PAD - QK_NOPE - QK_ROPE, d), w_in_t.dtype)], axis=0).astype(_BF16)
    wq = w_uq.reshape(Q_LORA, N_MLA_HEADS, QK_NOPE + QK_ROPE)
    nope, r1, r2 = wq[..., :QK_NOPE], wq[..., QK_NOPE:QK_NOPE + ROPE_HALF], wq[..., QK_NOPE + ROPE_HALF:]
    z32 = jnp.zeros((Q_LORA, N_MLA_HEADS, HEAD_PAD - QK_NOPE - QK_ROPE), w_uq.dtype)
    wuq_p = jnp.concatenate([nope, r1, r2, z32], axis=-1).reshape(Q_LORA, MLA_QK_WIDTH).astype(_BF16)

    wkv = w_ukv.reshape(KV_LORA, N_MLA_HEADS, QK_NOPE + V_HEAD)
    zk = jnp.zeros((KV_LORA, N_MLA_HEADS, HEAD_PAD - QK_NOPE), w_ukv.dtype)
    wk = jnp.concatenate([wkv[..., :QK_NOPE], zk], axis=-1).reshape(KV_LORA, MLA_QK_WIDTH)
    wv_t = wkv[..., QK_NOPE:].reshape(KV_LORA, MLA_WIDTH).T
    return wlat_t, wuq_p, wk.astype(_BF16), wv_t.astype(_BF16)


def _in_proj(x, g_mix, wlat_t, wret_t, g_q_lat, wuq_p, g_kv_lat, wuk_p, wuvt_p, cos_t, sin_t, later_weights):
    b, s, d = x.shape
    nt = s // TM_IN
    per = TM_IN // TQ_MLA
    tok = lambda w: pl.BlockSpec((1, TM_IN, w), lambda bi, i: (bi, i, 0))
    tab = pl.BlockSpec((TM_IN // 2, LANES), lambda bi, i: (bi * nt + i, 0))
    bf = lambda w: jax.ShapeDtypeStruct((b, s, w), _BF16)
    ret_tables = _retention_tables()
    cast_in, cast_out, cast_shapes = _cast_plumbing(later_weights, (b, nt))
    n_in = 14 + len(ret_tables)
    return pl.pallas_call(
        _with_casts(_in_proj_kernel, n_in, 4, len(later_weights)),
        out_shape=(bf(MLA_QK_WIDTH), bf(MLA_QK_WIDTH),
                   jax.ShapeDtypeStruct((b, s // TQ_MLA, MLA_WIDTH, TQ_MLA), _BF16), bf(RET_WIDTH), *cast_shapes),
        grid=(b, nt),
        in_specs=[tok(d), _const_spec((1, d)), _const_spec(wlat_t.shape)] + [_const_spec(w.shape) for w in wret_t]
                 + [_const_spec((1, Q_LORA)), _const_spec(wuq_p.shape), _const_spec((1, KV_LORA)),
                  _const_spec(wuk_p.shape), _const_spec(wuvt_p.shape), tab, tab]
                 + [_const_spec(t.shape) for t in ret_tables] + cast_in,
        out_specs=(tok(MLA_QK_WIDTH), tok(MLA_QK_WIDTH),
                   pl.BlockSpec((1, per, MLA_WIDTH, TQ_MLA), lambda bi, i: (bi, i, 0, 0)), tok(RET_WIDTH), *cast_out),
        scratch_shapes=[pltpu.VMEM((N_RET_HEADS, RET_DK, RET_DV), _F32)],
        compiler_params=_params(("arbitrary", "arbitrary")),
        name="in_proj",
    )(x, g_mix.reshape(1, d), wlat_t, *wret_t, g_q_lat.reshape(1, Q_LORA), wuq_p, g_kv_lat.reshape(1, KV_LORA),
      wuk_p, wuvt_p, cos_t, sin_t, *ret_tables, *later_weights)


_MASK_VALUE = -0.7 * float(jnp.finfo(jnp.float32).max)
_MLA_AHEAD = 6
_MLA_TILES_PER_ITER = 5
_MLA_DENOM_ROWS = 16


def _mla_attn_kernel(qlo_ref, qhi_ref, k_ref, vt_ref, o_ref, q_ref, s_ref, m_ref, acc_ref):
    tq = TQ_MLA
    n_tiles = k_ref.shape[1] // tq
    lo = pl.program_id(1)
    hi = n_tiles - 1 - lo
    q_ref[0] = qlo_ref[0]
    q_ref[1] = qhi_ref[0]
    m_ref[...] = jnp.full(m_ref.shape, _MASK_VALUE, _F32)
    acc_ref[...] = jnp.zeros(acc_ref.shape, _F32)
    ones_rows = jnp.ones((_MLA_DENOM_ROWS, tq), _BF16)

    def step_args(t):
        sel = (t > lo).astype(jnp.int32)
        return sel, t - 1 - sel * lo

    def scores(sel, kk, hd):
        sl = slice(hd * HEAD_PAD, (hd + 1) * HEAD_PAD)
        ks = pl.multiple_of(kk * tq, tq)
        s_ref[hd] = _dot_nt(k_ref[0, pl.ds(ks, tq), sl], q_ref[sel, :, sl])

    def softmax(sel, hd, keep):
        row = slice(hd, hd + 1)

        def st():
            s = s_ref[hd]
            return s if keep is None else jnp.where(keep, s, _MASK_VALUE)

        m_prev = m_ref[sel, row, :]
        m_next = jnp.maximum(m_prev, jnp.max(st(), axis=0, keepdims=True))
        m_ref[sel, row, :] = m_next
        return jnp.exp2(m_prev - m_next), jnp.exp2(st() - m_next).astype(_BF16)

    def accumulate(sel, kk, hd, alpha, p):
        vt = jnp.concatenate([vt_ref[0, kk, hd * V_HEAD:(hd + 1) * V_HEAD, :], ones_rows], axis=0)
        acc_ref[sel, hd] = acc_ref[sel, hd] * alpha + _dot(vt, p)

    def tile(sel, kk, diagonal, nxt):
        keep = None
        if diagonal:
            key = lax.broadcasted_iota(jnp.int32, (tq, tq), 0)
            qry = lax.broadcasted_iota(jnp.int32, (tq, tq), 1)
            keep = key <= qry
        pending = None
        for hd in range(N_MLA_HEADS):
            ahead = hd + _MLA_AHEAD
            if ahead < N_MLA_HEADS:
                scores(sel, kk, ahead)
            elif nxt is not None:
                scores(nxt[0], nxt[1], ahead - N_MLA_HEADS)
            current = softmax(sel, hd, keep)
            if pending is not None:
                accumulate(sel, kk, hd - 1, *pending)
            pending = current
        accumulate(sel, kk, N_MLA_HEADS - 1, *pending)

    for hd in range(_MLA_AHEAD):
        scores(0, lo, hd)
    tile(0, lo, True, step_args(1))

    def body(it, carry):
        for u in range(_MLA_TILES_PER_ITER):
            t = 1 + it * _MLA_TILES_PER_ITER + u
            last = t + 1 == n_tiles
            sel_n, kk_n = step_args(t + 1)
            tile(*step_args(t), False, (jnp.where(last, 1, sel_n), jnp.where(last, hi, kk_n)))
        return carry

    lax.fori_loop(0, (n_tiles - 1) // _MLA_TILES_PER_ITER, body, 0)
    tile(1, hi, True, None)

    for sel, qt in ((0, lo), (1, hi)):
        out_t = jnp.concatenate([acc_ref[sel, hd, :V_HEAD, :] / acc_ref[sel, hd, V_HEAD:V_HEAD + 1, :]
                                 for hd in range(N_MLA_HEADS)], axis=0)
        o_ref[0, pl.ds(pl.multiple_of(qt * tq, tq), tq), :] = out_t.T.astype(o_ref.dtype)


def _mla_attn(q, k, vt, later_weights):
    b, s, w = q.shape
    nk = s // TQ_MLA
    assert nk % 2 == 0 and (nk - 1) % _MLA_TILES_PER_ITER == 0
    cast_in, cast_out, cast_shapes = _cast_plumbing(later_weights, (b, nk // 2))
    return pl.pallas_call(
        _with_casts(_mla_attn_kernel, 4, 1, len(later_weights)),
        out_shape=(jax.ShapeDtypeStruct((b, s, MLA_WIDTH), _BF16), *cast_shapes),
        grid=(b, nk // 2),
        in_specs=[pl.BlockSpec((1, TQ_MLA, w), lambda bi, i: (bi, i, 0)),
                  pl.BlockSpec((1, TQ_MLA, w), lambda bi, i: (bi, nk - 1 - i, 0)),
                  pl.BlockSpec((1, s, w), lambda bi, i: (bi, 0, 0)),
                  pl.BlockSpec((1, nk, MLA_WIDTH, TQ_MLA), lambda bi, i: (bi, 0, 0, 0))] + cast_in,
        out_specs=(pl.BlockSpec((1, s, MLA_WIDTH), lambda bi, i: (bi, 0, 0)), *cast_out),
        scratch_shapes=[pltpu.VMEM((2, TQ_MLA, w), _BF16),
                        pltpu.VMEM((N_MLA_HEADS, TQ_MLA, TQ_MLA), _F32),
                        pltpu.VMEM((2, N_MLA_HEADS, TQ_MLA), _F32),
                        pltpu.VMEM((2, N_MLA_HEADS, V_HEAD + _MLA_DENOM_ROWS, TQ_MLA), _F32)],
        compiler_params=_params(("arbitrary", "arbitrary")),
        name="mla_attn",
    )(q, q, k, vt, *later_weights)


def _retention_tables():
    h, L = N_RET_HEADS, RET_CHUNK
    log_gamma = jnp.log(1.0 - 2.0 ** (-5.0 - jnp.arange(h, dtype=_F32)))
    j = jnp.arange(L, dtype=_F32)
    diff = j[:, None] - j[None, :]
    intra = jnp.where(diff[None] >= 0,
                      jnp.exp(jnp.maximum(diff, 0.0)[None] * log_gamma[:, None, None]), 0.0)
    rowb = lambda t: jnp.broadcast_to(t.T[:, :, None], (h, L, LANES))
    k_to_end = jnp.exp((L - 1 - j)[:, None] * log_gamma[None, :])
    q_from_start = jnp.exp((j + 1)[:, None] * log_gamma[None, :])
    chunk_decay = jnp.broadcast_to(jnp.exp(L * log_gamma)[:, None, None], (h, RET_DK, RET_DV))
    return intra, rowb(k_to_end), rowb(q_from_start), chunk_decay


def _mix_xattn_kernel(x_ref, ymla_ref, yret_ref, wout_ref, gx_ref, wxq_ref, kv_ref, wxo_ref, o_ref):
    x1 = x_ref[0] + _dot(ymla_ref[0], wout_ref[:MLA_WIDTH, :]) + _dot(yret_ref[0], wout_ref[MLA_WIDTH:, :])
    h = _rms(x1, gx_ref[...]).astype(_BF16)
    q = (_dot(h, wxq_ref[...]) * (1.0 / math.sqrt(XATTN_HEAD))).astype(_BF16)
    cols = lambda hd: slice(hd * XATTN_HEAD, (hd + 1) * XATTN_HEAD)
    vcols = lambda hd: slice(D_MODEL + hd * XATTN_HEAD, D_MODEL + (hd + 1) * XATTN_HEAD)
    scores = [_dot_nt(q[:, cols(hd)], kv_ref[0, :, cols(hd)]) for hd in range(N_XATTN_HEADS)]
    probs = [jnp.exp(s - jnp.max(s, axis=-1, keepdims=True)) for s in scores]
    heads = [(_dot(p.astype(_BF16), kv_ref[0, :, vcols(hd)]) / jnp.sum(p, axis=-1, keepdims=True)).astype(_BF16)
             for hd, p in enumerate(probs)]
    o_ref[0] = x1 + _dot(jnp.concatenate(heads, axis=1), wxo_ref[...])


def _mix_xattn(x, y_mla, y_ret, w_out, g_xattn, w_xq, kv_mem, w_xo, later_weights):
    b, s, d = x.shape
    m, n = kv_mem.shape[1:]
    tok = lambda w: pl.BlockSpec((1, TM_MIX, w), lambda bi, i: (bi, i, 0))
    cast_in, cast_out, cast_shapes = _cast_plumbing(later_weights, (b, s // TM_MIX))
    return pl.pallas_call(
        _with_casts(_mix_xattn_kernel, 8, 1, len(later_weights)),
        out_shape=(jax.ShapeDtypeStruct((b, s, d), _F32), *cast_shapes),
        grid=(b, s // TM_MIX),
        in_specs=[tok(d), tok(MLA_WIDTH), tok(RET_WIDTH), _const_spec(w_out.shape), _const_spec((1, d)),
                  _const_spec(w_xq.shape), pl.BlockSpec((1, m, n), lambda bi, i: (bi, 0, 0)),
                  _const_spec(w_xo.shape)] + cast_in,
        out_specs=(tok(d), *cast_out),
        compiler_params=_params(("arbitrary", "arbitrary")),
        name="mix_xattn",
    )(x, y_mla, y_ret, w_out, g_xattn.reshape(1, d), w_xq, kv_mem, w_xo, *later_weights)


def _conv_ffn_kernel(x_ref, g_ref, win_ref, cw_ref, cb_ref, wout_ref, gfin_ref, o_ref, gate_ref):
    halo = SUBLANES
    half = TM_FFN // 2

    @pl.when(pl.program_id(1) == 0)
    def _():
        gate_ref[:halo, :] = jnp.zeros((halo, D_FF), _F32)

    rows = [slice(0, half), slice(half, TM_FFN)]
    xs = [x_ref[0, r, :] for r in rows]
    hs = [_rms(x, g_ref[...]).astype(_BF16) for x in xs]

    def project(i):
        gate = _dot(hs[i], win_ref[:, :D_FF])
        up = _dot(hs[i], win_ref[:, D_FF:])
        gate_ref[halo + i * half:halo + (i + 1) * half, :] = gate
        return gate, up

    def activate(i, gate, up):
        conv = cb_ref[...] + gate * cw_ref[CONV_W - 1:CONV_W, :]
        for tap in range(CONV_W - 1):
            back = CONV_W - 1 - tap
            lo = halo + i * half - back
            conv = conv + gate_ref[lo:lo + half, :] * cw_ref[tap:tap + 1, :]
        return (conv * jax.nn.sigmoid(conv) * up).astype(_BF16)

    def finish(i, act):
        x3 = xs[i] + _dot(act, wout_ref[...])
        o_ref[0, rows[i], :] = _rms(x3, gfin_ref[...])

    g0, u0 = project(0)
    g1, u1 = project(1)
    a0 = activate(0, g0, u0)
    finish(0, a0)
    a1 = activate(1, g1, u1)
    finish(1, a1)
    gate_ref[:halo, :] = gate_ref[TM_FFN:, :]


def _conv_ffn(x, g_ffn, w_ffn_in, conv_w, conv_b, w_ffn_out, g_final):
    b, s, d = x.shape
    tok = pl.BlockSpec((1, TM_FFN, d), lambda bi, i: (bi, i, 0))
    return pl.pallas_call(
        _conv_ffn_kernel,
        out_shape=jax.ShapeDtypeStruct((b, s, d), _F32),
        grid=(b, s // TM_FFN),
        in_specs=[tok, _const_spec((1, d)), _const_spec(w_ffn_in.shape), _const_spec(conv_w.shape),
                  _const_spec((1, D_FF)), _const_spec(w_ffn_out.shape), _const_spec((1, d))],
        out_specs=tok,
        scratch_shapes=[pltpu.VMEM((TM_FFN + SUBLANES, D_FF), _F32)],
        compiler_params=_params(("arbitrary", "arbitrary")),
        name="conv_ffn",
    )(x, g_ffn.reshape(1, d), w_ffn_in, conv_w, conv_b.reshape(1, D_FF), w_ffn_out, g_final.reshape(1, d))


def kernel(x, mem, positions, g_mix, w_in, g_q_lat, w_uq, g_kv_lat, w_ukv, w_out, g_xattn, g_mem, w_xq,
           w_xkv, w_xo, g_ffn, w_ffn_in, conv_w, conv_b, w_ffn_out, g_final):
    assert w_in.shape[0] == 1, "one layer supported"
    l = 0
    w_in_t = w_in[l].T
    cos_t, sin_t, *wret_t, kv_mem = _prep(positions, w_in_t, mem, g_mem[l], w_xkv[l])
    wlat_t, wuq_p, wuk_p, wuvt_p = _permute_small_weights(w_in_t, w_uq[l], w_ukv[l])
    q, k, vt, y_ret, w_out_b, w_xq_b, w_xo_b = _in_proj(
        x, g_mix[l], wlat_t, wret_t, g_q_lat[l], wuq_p, g_kv_lat[l], wuk_p, wuvt_p, cos_t, sin_t,
        (w_out[l], w_xq[l], w_xo[l]))
    y_mla, w_ffn_in_b = _mla_attn(q, k, vt, (w_ffn_in[l],))
    x, w_ffn_out_b = _mix_xattn(x, y_mla, y_ret, w_out_b, g_xattn[l], w_xq_b, kv_mem, w_xo_b, (w_ffn_out[l],))
    return _conv_ffn(x, g_ffn[l], w_ffn_in_b, conv_w[l], conv_b[l], w_ffn_out_b, g_final)
```

```python
import functools
import math

import jax
import jax.numpy as jnp
from jax import lax
from jax.experimental import pallas as pl
from jax.experimental.pallas import tpu as pltpu

D_MODEL = 1024
EPS = 1e-6
ROPE_BASE = 10000.0
N_MLA_HEADS = 8
QK_NOPE = 64
QK_ROPE = 32
V_HEAD = 64
Q_LORA = 256
KV_LORA = 128
N_RET_HEADS = 4
RET_DK = 128
RET_DV = 128
RET_CHUNK = 128
MLA_WIDTH = N_MLA_HEADS * V_HEAD
RET_WIDTH = N_RET_HEADS * RET_DV
N_XATTN_HEADS = 4
XATTN_HEAD = D_MODEL // N_XATTN_HEADS
D_FF = 2816
CONV_W = 3

LANES = 128
SUBLANES = 8
_BF16_ROWS = 16
VMEM_LIMIT = 56 * 1024 * 1024

HEAD_PAD = LANES
ROPE_HALF = QK_ROPE // 2
MLA_QK_WIDTH = N_MLA_HEADS * HEAD_PAD
IN_LAT = Q_LORA + KV_LORA + HEAD_PAD

TM_TABLE = 1024
TM_IN = 512
TQ_MLA = 256
TM_MIX = 1024
TM_FFN = 512

_BF16 = jnp.bfloat16
_F32 = jnp.float32


def _dot(a, b):
    return jnp.dot(a, b, preferred_element_type=_F32)


def _dot_nt(a, b):
    return lax.dot_general(a, b, (((1,), (1,)), ((), ())), preferred_element_type=_F32)


def _dot_tn(a, b):
    return lax.dot_general(a, b, (((0,), (0,)), ((), ())), preferred_element_type=_F32)


def _rms(x, g):
    inv = lax.rsqrt(jnp.mean(x * x, axis=-1, keepdims=True) + EPS)
    return x * inv * g


def _const_spec(shape):
    nd = len(shape)
    return pl.BlockSpec(shape, lambda *_: (0,) * nd, pipeline_mode=pl.Buffered(1))


def _params(sem):
    return pltpu.CompilerParams(dimension_semantics=sem, vmem_limit_bytes=VMEM_LIMIT)


def _with_casts(body, n_in, n_out, n_cast):
    def kern(*refs):
        ins, rest = refs[:n_in], refs[n_in:]
        cast_in, rest = rest[:n_cast], rest[n_cast:]
        outs, rest = rest[:n_out], rest[n_out:]
        cast_out, scratch = rest[:n_cast], rest[n_cast:]
        for src, dst in zip(cast_in, cast_out):
            dst[...] = src[...].astype(dst.dtype)
        body(*ins, *outs, *scratch)
    return kern


def _cast_plumbing(weights, grid):
    steps = math.prod(grid)
    if len(grid) == 1:
        block_of = lambda i: (i, 0)
    else:
        block_of = lambda bi, i: (bi * grid[1] + i, 0)
    in_specs, out_specs, out_shapes = [], [], []
    for w in weights:
        rows, cols = w.shape
        assert rows % steps == 0 and (rows // steps) % _BF16_ROWS == 0, (w.shape, steps)
        spec = pl.BlockSpec((rows // steps, cols), block_of)
        in_specs.append(spec)
        out_specs.append(spec)
        out_shapes.append(jax.ShapeDtypeStruct(w.shape, _BF16))
    return in_specs, out_specs, out_shapes


def _prep_kernel(pos_ref, inv_ref, wq_ref, wk_ref, wv_ref, wg_ref, mem_ref, gmem_ref, wxkv_ref,
                 cos_ref, sin_ref, wq_out, wk_out, wv_out, wg_out, kv_ref, *, steps_per_batch):
    half_rows = TM_IN // 2 // LANES
    lane = lax.broadcasted_iota(jnp.int32, (LANES, LANES), 1)
    column = lambda r: jnp.broadcast_to(pos_ref[r:r + 1, :], (LANES, LANES)).T
    for tile in range(TM_TABLE // TM_IN):
        for rr in range(half_rows):
            r0 = tile * 2 * half_rows + rr
            pos = jnp.where(lane < RET_DK // 2, column(r0), column(r0 + half_rows))
            ang = pos * inv_ref[...]
            out = slice((tile * half_rows + rr) * LANES, (tile * half_rows + rr + 1) * LANES)
            cos_ref[out, :] = jnp.cos(ang)
            sin_ref[out, :] = jnp.sin(ang)
    for src, dst in ((wq_ref, wq_out), (wk_ref, wk_out), (wv_ref, wv_out), (wg_ref, wg_out)):
        dst[...] = src[...].astype(_BF16)

    @pl.when(pl.program_id(0) % steps_per_batch == 0)
    def _():
        mem_n = _rms(mem_ref[0], gmem_ref[...]).astype(_BF16)
        kv_ref[0] = _dot(mem_n, wxkv_ref[...].astype(_BF16)).astype(_BF16)


def _prep(positions, w_in_t, mem, g_mem, w_xkv):
    t = positions.size
    steps = t // TM_TABLE
    n_in, d = w_in_t.shape
    b, m, _ = mem.shape
    n_kv = w_xkv.shape[1]
    per_batch = steps // b
    assert steps % b == 0
    row0 = Q_LORA + KV_LORA + QK_ROPE
    rows = RET_WIDTH // steps
    assert RET_WIDTH % steps == 0 and rows % _BF16_ROWS == 0 and row0 % rows == 0
    assert n_in == row0 + 4 * RET_WIDTH
    pos = positions.astype(_F32).reshape(t // LANES, LANES)
    f_ret = 1.0 / (ROPE_BASE ** (jnp.arange(0, RET_DK, 2, dtype=_F32) / RET_DK))
    inv = jnp.concatenate([f_ret, f_ret])
    group = lambda j: pl.BlockSpec((rows, d), lambda i: (row0 // rows + j * steps + i, 0))
    table = pl.BlockSpec((TM_TABLE // 2, LANES), lambda i: (i, 0))
    w_out = pl.BlockSpec((rows, d), lambda i: (i, 0))
    return pl.pallas_call(
        functools.partial(_prep_kernel, steps_per_batch=per_batch),
        out_shape=(jax.ShapeDtypeStruct((t // 2, LANES), _F32),) * 2
                  + (jax.ShapeDtypeStruct((RET_WIDTH, d), _BF16),) * 4
                  + (jax.ShapeDtypeStruct((b, m, n_kv), _BF16),),
        grid=(steps,),
        in_specs=[pl.BlockSpec((TM_TABLE // LANES, LANES), lambda i: (i, 0)),
                  pl.BlockSpec((1, LANES), lambda i: (0, 0)), group(0), group(1), group(2), group(3),
                  pl.BlockSpec((1, m, d), lambda i: (i // per_batch, 0, 0)), _const_spec((1, d)),
                  _const_spec(w_xkv.shape)],
        out_specs=(table, table, w_out, w_out, w_out, w_out,
                   pl.BlockSpec((1, m, n_kv), lambda i: (i // per_batch, 0, 0))),
        compiler_params=_params(("arbitrary",)),
        name="prep",
    )(pos, inv.reshape(1, LANES), w_in_t, w_in_t, w_in_t, w_in_t, mem, g_mem.reshape(1, d), w_xkv)


def _in_proj_kernel(x_ref, gmix_ref, wlat_ref, wrq_ref, wrk_ref, wrv_ref, wrg_ref, gq_ref, wuq_ref, gkv_ref, wuk_ref, wuvt_ref, cos_ref, sin_ref,
                    intra_ref, kend_ref, qstart_ref, decay_ref,
                    q_ref, k_ref, vt_ref, yret_ref, state_ref):
    h = _rms(x_ref[0], gmix_ref[...]).astype(_BF16)
    lane = lax.broadcasted_iota(jnp.int32, (TM_IN, LANES), 1)
    low = lane < RET_DK // 2

    lo = lax.broadcasted_iota(jnp.int32, (TM_IN // 2, LANES), 1) < RET_DK // 2

    def unpack(t):
        t_roll = pltpu.roll(t, RET_DK // 2, axis=1)
        return jnp.concatenate([jnp.where(lo, t, t_roll), jnp.where(lo, t_roll, t)], axis=0)

    cos_r, sin_full = unpack(cos_ref[...]), unpack(sin_ref[...])
    sin_r = jnp.where(low, -sin_full, sin_full)
    ratio = (RET_DK // 2) // ROPE_HALF
    rope_lane = (lane >= QK_NOPE) & (lane < QK_NOPE + QK_ROPE)
    src = jnp.where(lane < QK_NOPE + ROPE_HALF, lane - QK_NOPE, lane - QK_NOPE - ROPE_HALF) * ratio
    src = jnp.where(rope_lane, src, 0)
    g_cos = jnp.take_along_axis(cos_r, src, axis=1)
    g_sin = jnp.take_along_axis(sin_full, src, axis=1)
    cos_m = jnp.where(rope_lane, g_cos, 1.0)
    sin_m = jnp.where(rope_lane, jnp.where(lane < QK_NOPE + ROPE_HALF, -g_sin, g_sin), 0.0)

    first_half = lane < QK_NOPE + ROPE_HALF

    def swap_halves(t):
        return jnp.where(first_half, pltpu.roll(t, LANES - ROPE_HALF, axis=1), pltpu.roll(t, ROPE_HALF, axis=1))

    lat = _dot_nt(h, wlat_ref[...])
    c_q = lat[:, :Q_LORA]
    c_kv = lat[:, Q_LORA:Q_LORA + KV_LORA]
    kr = lat[:, Q_LORA + KV_LORA:IN_LAT]
    k_rope = kr * cos_m + swap_halves(kr) * sin_m

    cqn = _rms(c_q, gq_ref[...]).astype(_BF16)
    q = _dot(cqn, wuq_ref[...])
    scale = math.log2(math.e) / math.sqrt(QK_NOPE + QK_ROPE)
    for hd in range(N_MLA_HEADS):
        sl = slice(hd * HEAD_PAD, (hd + 1) * HEAD_PAD)
        q_ref[0, :, sl] = ((q[:, sl] * cos_m + swap_halves(q[:, sl]) * sin_m) * scale).astype(_BF16)

    ckvn = _rms(c_kv, gkv_ref[...]).astype(_BF16)
    k_nope = _dot(ckvn, wuk_ref[...])
    for hd in range(N_MLA_HEADS):
        sl = slice(hd * HEAD_PAD, (hd + 1) * HEAD_PAD)
        k_ref[0, :, sl] = (k_nope[:, sl] + k_rope).astype(_BF16)
    v_t = _dot_nt(wuvt_ref[...], ckvn).astype(_BF16)
    for j in range(TM_IN // TQ_MLA):
        vt_ref[0, j] = v_t[:, j * TQ_MLA:(j + 1) * TQ_MLA]

    def ret_rope(w_ref, mult):
        r = _dot_nt(h, w_ref[...])
        heads = []
        for hd in range(N_RET_HEADS):
            rh = r[:, hd * RET_DK:(hd + 1) * RET_DK]
            roped = rh * cos_r + pltpu.roll(rh, RET_DK // 2, axis=1) * sin_r
            if mult is not None:
                roped = roped * mult
            heads.append(roped.astype(_BF16))
        return heads

    rq = ret_rope(wrq_ref, None)
    rk = ret_rope(wrk_ref, RET_DK ** -0.5)
    rv = _dot_nt(h, wrv_ref[...]).astype(_BF16)
    rg = _dot_nt(h, wrg_ref[...]).astype(_BF16)
    _retention_tile(rq, rk, rv, rg, intra_ref, kend_ref, qstart_ref, decay_ref, state_ref, yret_ref)


def _retention_tile(q_heads, k_heads, v, g, intra_ref, kend_ref, qstart_ref, decay_ref, state_ref, o_ref):
    @pl.when(pl.program_id(1) == 0)
    def _():
        state_ref[...] = jnp.zeros(state_ref.shape, _F32)

    L = RET_CHUNK
    n_chunks = v.shape[0] // L
    units = [(c, hd) for c in range(n_chunks) for hd in range(N_RET_HEADS)]
    rows = lambda c: slice(c * L, (c + 1) * L)
    cols = lambda hd: slice(hd * RET_DV, (hd + 1) * RET_DV)
    scores, chunk_kv = {}, {}
    for c, hd in units:
        scores[c, hd] = _dot_nt(q_heads[hd][rows(c)], k_heads[hd][rows(c)])
    for c, hd in units:
        v_dec = (v[rows(c), cols(hd)].astype(_F32) * kend_ref[hd]).astype(_BF16)
        chunk_kv[c, hd] = _dot_tn(k_heads[hd][rows(c)], v_dec)
    prev_state = {}
    for hd in range(N_RET_HEADS):
        state = state_ref[hd]
        for c in range(n_chunks):
            prev_state[c, hd] = state.astype(_BF16)
            state = decay_ref[hd] * state + chunk_kv[c, hd]
        state_ref[hd] = state
    inner, cross = {}, {}
    for c, hd in units:
        inner[c, hd] = _dot((scores[c, hd] * intra_ref[hd]).astype(_BF16), v[rows(c), cols(hd)])
    for c, hd in units:
        cross[c, hd] = _dot(q_heads[hd][rows(c)], prev_state[c, hd])
    for c, hd in units:
        out = inner[c, hd] + cross[c, hd] * qstart_ref[hd]
        mu = jnp.mean(out, axis=-1, keepdims=True)
        cen = out - mu
        var = jnp.mean(cen * cen, axis=-1, keepdims=True)
        normed = cen * lax.rsqrt(var + EPS)
        gate = g[rows(c), cols(hd)].astype(_F32)
        o_ref[0, rows(c), cols(hd)] = (normed * (gate * jax.nn.sigmoid(gate))).astype(o_ref.dtype)


def _permute_small_weights(w_in_t, w_uq, w_ukv):
    o_kr = Q_LORA + KV_LORA
    d = w_in_t.shape[1]
    wlat_t = jnp.concatenate([w_in_t[:o_kr], jnp.zeros((QK_NOPE, d), w_in_t.dtype), w_in_t[o_kr:o_kr + QK_ROPE],
                              jnp.zeros((HEAD_PAD - QK_NOPE - QK_ROPE, d), w_in_t.dtype)], axis=0).astype(_BF16)
    wq = w_uq.reshape(Q_LORA, N_MLA_HEADS, QK_NOPE + QK_ROPE)
    nope, r1, r2 = wq[..., :QK_NOPE], wq[..., QK_NOPE:QK_NOPE + ROPE_HALF], wq[..., QK_NOPE + ROPE_HALF:]
    z32 = jnp.zeros((Q_LORA, N_MLA_HEADS, HEAD_PAD - QK_NOPE - QK_ROPE), w_uq.dtype)
    wuq_p = jnp.concatenate([nope, r1, r2, z32], axis=-1).reshape(Q_LORA, MLA_QK_WIDTH).astype(_BF16)

    wkv = w_ukv.reshape(KV_LORA, N_MLA_HEADS, QK_NOPE + V_HEAD)
    zk = jnp.zeros((KV_LORA, N_MLA_HEADS, HEAD_PAD - QK_NOPE), w_ukv.dtype)
    wk = jnp.concatenate([wkv[..., :QK_NOPE], zk], axis=-1).reshape(KV_LORA, MLA_QK_WIDTH)
    wv_t = wkv[..., QK_NOPE:].reshape(KV_LORA, MLA_WIDTH).T
    return wlat_t, wuq_p, wk.astype(_BF16), wv_t.astype(_BF16)


def _in_proj(x, g_mix, wlat_t, wret_t, g_q_lat, wuq_p, g_kv_lat, wuk_p, wuvt_p, cos_t, sin_t, later_weights):
    b, s, d = x.shape
    nt = s // TM_IN
    per = TM_IN // TQ_MLA
    tok = lambda w: pl.BlockSpec((1, TM_IN, w), lambda bi, i: (bi, i, 0))
    tab = pl.BlockSpec((TM_IN // 2, LANES), lambda bi, i: (bi * nt + i, 0))
    bf = lambda w: jax.ShapeDtypeStruct((b, s, w), _BF16)
    ret_tables = _retention_tables()
    cast_in, cast_out, cast_shapes = _cast_plumbing(later_weights, (b, nt))
    in_specs = ([tok(d), _const_spec((1, d)), _const_spec(wlat_t.shape)] + [_const_spec(w.shape) for w in wret_t]
                + [_const_spec((1, Q_LORA)), _const_spec(wuq_p.shape), _const_spec((1, KV_LORA)),
                   _const_spec(wuk_p.shape), _const_spec(wuvt_p.shape), tab, tab]
                + [_const_spec(t.shape) for t in ret_tables])
    out_specs = (tok(MLA_QK_WIDTH), tok(MLA_QK_WIDTH),
                 pl.BlockSpec((1, per, MLA_WIDTH, TQ_MLA), lambda bi, i: (bi, i, 0, 0)), tok(RET_WIDTH))
    return pl.pallas_call(
        _with_casts(_in_proj_kernel, len(in_specs), len(out_specs), len(later_weights)),
        out_shape=(bf(MLA_QK_WIDTH), bf(MLA_QK_WIDTH),
                   jax.ShapeDtypeStruct((b, s // TQ_MLA, MLA_WIDTH, TQ_MLA), _BF16), bf(RET_WIDTH), *cast_shapes),
        grid=(b, nt),
        in_specs=in_specs + cast_in,
        out_specs=(*out_specs, *cast_out),
        scratch_shapes=[pltpu.VMEM((N_RET_HEADS, RET_DK, RET_DV), _F32)],
        compiler_params=_params(("arbitrary", "arbitrary")),
        name="in_proj",
    )(x, g_mix.reshape(1, d), wlat_t, *wret_t, g_q_lat.reshape(1, Q_LORA), wuq_p, g_kv_lat.reshape(1, KV_LORA),
      wuk_p, wuvt_p, cos_t, sin_t, *ret_tables, *later_weights)


_MASK_VALUE = -0.7 * float(jnp.finfo(jnp.float32).max)
_MLA_AHEAD = 4
_MLA_TILES_PER_ITER = 5
_MLA_DENOM_ROWS = 16


def _mla_attn_kernel(qlo_ref, qhi_ref, k_ref, vt_ref, o_ref, q_ref, s_ref, m_ref, acc_ref):
    tq = TQ_MLA
    n_tiles = k_ref.shape[1] // tq
    lo = pl.program_id(1)
    hi = n_tiles - 1 - lo
    q_ref[0] = qlo_ref[0]
    q_ref[1] = qhi_ref[0]
    m_ref[...] = jnp.full(m_ref.shape, _MASK_VALUE, _F32)
    acc_ref[...] = jnp.zeros(acc_ref.shape, _F32)
    ones_rows = jnp.ones((_MLA_DENOM_ROWS, tq), _BF16)

    def step_args(t):
        sel = (t > lo).astype(jnp.int32)
        return sel, t - 1 - sel * lo

    def scores(sel, kk, hd):
        sl = slice(hd * HEAD_PAD, (hd + 1) * HEAD_PAD)
        ks = pl.multiple_of(kk * tq, tq)
        s_ref[hd] = _dot_nt(k_ref[0, pl.ds(ks, tq), sl], q_ref[sel, :, sl])

    def softmax(sel, hd, keep):
        row = slice(hd, hd + 1)

        def st():
            s = s_ref[hd]
            return s if keep is None else jnp.where(keep, s, _MASK_VALUE)

        m_prev = m_ref[sel, row, :]
        m_next = jnp.maximum(m_prev, jnp.max(st(), axis=0, keepdims=True))
        m_ref[sel, row, :] = m_next
        return jnp.exp2(m_prev - m_next), jnp.exp2(st() - m_next).astype(_BF16)

    def accumulate(sel, kk, hd, alpha, p):
        vt = jnp.concatenate([vt_ref[0, kk, hd * V_HEAD:(hd + 1) * V_HEAD, :], ones_rows], axis=0)
        acc_ref[sel, hd] = acc_ref[sel, hd] * alpha + _dot(vt, p)

    def tile(sel, kk, diagonal, nxt):
        keep = None
        if diagonal:
            key = lax.broadcasted_iota(jnp.int32, (tq, tq), 0)
            qry = lax.broadcasted_iota(jnp.int32, (tq, tq), 1)
            keep = key <= qry
        pending = None
        for hd in range(N_MLA_HEADS):
            ahead = hd + _MLA_AHEAD
            if ahead < N_MLA_HEADS:
                scores(sel, kk, ahead)
            elif nxt is not None:
                scores(nxt[0], nxt[1], ahead - N_MLA_HEADS)
            current = softmax(sel, hd, keep)
            if pending is not None:
                accumulate(sel, kk, hd - 1, *pending)
            pending = current
        accumulate(sel, kk, N_MLA_HEADS - 1, *pending)

    for hd in range(_MLA_AHEAD):
        scores(0, lo, hd)
    tile(0, lo, True, step_args(1))

    def body(it, carry):
        for u in range(_MLA_TILES_PER_ITER):
            t = 1 + it * _MLA_TILES_PER_ITER + u
            last = t + 1 == n_tiles
            sel_n, kk_n = step_args(t + 1)
            tile(*step_args(t), False, (jnp.where(last, 1, sel_n), jnp.where(last, hi, kk_n)))
        return carry

    lax.fori_loop(0, (n_tiles - 1) // _MLA_TILES_PER_ITER, body, 0)
    tile(1, hi, True, None)

    for sel, qt in ((0, lo), (1, hi)):
        out_t = jnp.concatenate([acc_ref[sel, hd, :V_HEAD, :] / acc_ref[sel, hd, V_HEAD:V_HEAD + 1, :]
                                 for hd in range(N_MLA_HEADS)], axis=0)
        o_ref[0, pl.ds(pl.multiple_of(qt * tq, tq), tq), :] = out_t.T.astype(o_ref.dtype)


def _mla_attn(q, k, vt, later_weights):
    b, s, w = q.shape
    nk = s // TQ_MLA
    assert nk % 2 == 0 and (nk - 1) % _MLA_TILES_PER_ITER == 0
    cast_in, cast_out, cast_shapes = _cast_plumbing(later_weights, (b, nk // 2))
    in_specs = [pl.BlockSpec((1, TQ_MLA, w), lambda bi, i: (bi, i, 0)),
                pl.BlockSpec((1, TQ_MLA, w), lambda bi, i: (bi, nk - 1 - i, 0)),
                pl.BlockSpec((1, s, w), lambda bi, i: (bi, 0, 0)),
                pl.BlockSpec((1, nk, MLA_WIDTH, TQ_MLA), lambda bi, i: (bi, 0, 0, 0))]
    return pl.pallas_call(
        _with_casts(_mla_attn_kernel, len(in_specs), 1, len(later_weights)),
        out_shape=(jax.ShapeDtypeStruct((b, s, MLA_WIDTH), _BF16), *cast_shapes),
        grid=(b, nk // 2),
        in_specs=in_specs + cast_in,
        out_specs=(pl.BlockSpec((1, s, MLA_WIDTH), lambda bi, i: (bi, 0, 0)), *cast_out),
        scratch_shapes=[pltpu.VMEM((2, TQ_MLA, w), _BF16),
                        pltpu.VMEM((N_MLA_HEADS, TQ_MLA, TQ_MLA), _F32),
                        pltpu.VMEM((2, N_MLA_HEADS, TQ_MLA), _F32),
                        pltpu.VMEM((2, N_MLA_HEADS, V_HEAD + _MLA_DENOM_ROWS, TQ_MLA), _F32)],
        compiler_params=_params(("arbitrary", "arbitrary")),
        name="mla_attn",
    )(q, q, k, vt, *later_weights)


def _retention_tables():
    h, L = N_RET_HEADS, RET_CHUNK
    log_gamma = jnp.log(1.0 - 2.0 ** (-5.0 - jnp.arange(h, dtype=_F32)))
    j = jnp.arange(L, dtype=_F32)
    diff = j[:, None] - j[None, :]
    intra = jnp.where(diff[None] >= 0,
                      jnp.exp(jnp.maximum(diff, 0.0)[None] * log_gamma[:, None, None]), 0.0)
    rowb = lambda t: jnp.broadcast_to(t.T[:, :, None], (h, L, LANES))
    k_to_end = jnp.exp((L - 1 - j)[:, None] * log_gamma[None, :])
    q_from_start = jnp.exp((j + 1)[:, None] * log_gamma[None, :])
    chunk_decay = jnp.broadcast_to(jnp.exp(L * log_gamma)[:, None, None], (h, RET_DK, RET_DV))
    return intra, rowb(k_to_end), rowb(q_from_start), chunk_decay


def _mix_xattn_kernel(x_ref, ymla_ref, yret_ref, wout_ref, gx_ref, wxq_ref, kv_ref, wxo_ref, o_ref):
    half = TM_MIX // 2
    rows = [slice(0, half), slice(half, TM_MIX)]
    cols = lambda hd: slice(hd * XATTN_HEAD, (hd + 1) * XATTN_HEAD)
    vcols = lambda hd: slice(D_MODEL + hd * XATTN_HEAD, D_MODEL + (hd + 1) * XATTN_HEAD)

    def out_proj(r):
        return (x_ref[0, r, :] + _dot(ymla_ref[0, r, :], wout_ref[:MLA_WIDTH, :])
                + _dot(yret_ref[0, r, :], wout_ref[MLA_WIDTH:, :]))

    def queries(x1):
        h = _rms(x1, gx_ref[...]).astype(_BF16)
        return (_dot(h, wxq_ref[...]) * (1.0 / math.sqrt(XATTN_HEAD))).astype(_BF16)

    def scores(q):
        return [_dot_nt(q[:, cols(hd)], kv_ref[0, :, cols(hd)]) for hd in range(N_XATTN_HEADS)]

    def attend(ss):
        probs = [jnp.exp(s - jnp.max(s, axis=-1, keepdims=True)) for s in ss]
        heads = [(_dot(p.astype(_BF16), kv_ref[0, :, vcols(hd)]) / jnp.sum(p, axis=-1, keepdims=True)).astype(_BF16)
                 for hd, p in enumerate(probs)]
        return jnp.concatenate(heads, axis=1)

    x1 = [out_proj(r) for r in rows]
    q0 = queries(x1[0])
    q1 = queries(x1[1])
    s0 = scores(q0)
    s1 = scores(q1)
    a0 = attend(s0)
    o_ref[0, rows[0], :] = x1[0] + _dot(a0, wxo_ref[...])
    a1 = attend(s1)
    o_ref[0, rows[1], :] = x1[1] + _dot(a1, wxo_ref[...])


def _mix_xattn(x, y_mla, y_ret, w_out, g_xattn, w_xq, kv_mem, w_xo, later_weights):
    b, s, d = x.shape
    m, n = kv_mem.shape[1:]
    tok = lambda w: pl.BlockSpec((1, TM_MIX, w), lambda bi, i: (bi, i, 0))
    cast_in, cast_out, cast_shapes = _cast_plumbing(later_weights, (b, s // TM_MIX))
    in_specs = [tok(d), tok(MLA_WIDTH), tok(RET_WIDTH), _const_spec(w_out.shape), _const_spec((1, d)),
                _const_spec(w_xq.shape), pl.BlockSpec((1, m, n), lambda bi, i: (bi, 0, 0)),
                _const_spec(w_xo.shape)]
    return pl.pallas_call(
        _with_casts(_mix_xattn_kernel, len(in_specs), 1, len(later_weights)),
        out_shape=(jax.ShapeDtypeStruct((b, s, d), _F32), *cast_shapes),
        grid=(b, s // TM_MIX),
        in_specs=in_specs + cast_in,
        out_specs=(tok(d), *cast_out),
        compiler_params=_params(("arbitrary", "arbitrary")),
        name="mix_xattn",
    )(x, y_mla, y_ret, w_out, g_xattn.reshape(1, d), w_xq, kv_mem, w_xo, *later_weights)


def _conv_ffn_kernel(x_ref, g_ref, win_ref, cw_ref, cb_ref, wout_ref, gfin_ref, o_ref, gate_ref):
    halo = SUBLANES
    half = TM_FFN // 2

    @pl.when(pl.program_id(1) == 0)
    def _():
        gate_ref[:halo, :] = jnp.zeros((halo, D_FF), _F32)

    rows = [slice(0, half), slice(half, TM_FFN)]
    xs = [x_ref[0, r, :] for r in rows]
    hs = [_rms(x, g_ref[...]).astype(_BF16) for x in xs]

    def project(i):
        gate = _dot(hs[i], win_ref[:, :D_FF])
        up = _dot(hs[i], win_ref[:, D_FF:])
        gate_ref[halo + i * half:halo + (i + 1) * half, :] = gate
        return gate, up

    def activate(i, gate, up):
        conv = cb_ref[...] + gate * cw_ref[CONV_W - 1:CONV_W, :]
        for tap in range(CONV_W - 1):
            back = CONV_W - 1 - tap
            lo = halo + i * half - back
            conv = conv + gate_ref[lo:lo + half, :] * cw_ref[tap:tap + 1, :]
        return (conv * jax.nn.sigmoid(conv) * up).astype(_BF16)

    def finish(i, act):
        x3 = xs[i] + _dot(act, wout_ref[...])
        o_ref[0, rows[i], :] = _rms(x3, gfin_ref[...])

    g0, u0 = project(0)
    g1, u1 = project(1)
    a0 = activate(0, g0, u0)
    finish(0, a0)
    a1 = activate(1, g1, u1)
    finish(1, a1)
    gate_ref[:halo, :] = gate_ref[TM_FFN:, :]


def _conv_ffn(x, g_ffn, w_ffn_in, conv_w, conv_b, w_ffn_out, g_final):
    b, s, d = x.shape
    tok = pl.BlockSpec((1, TM_FFN, d), lambda bi, i: (bi, i, 0))
    return pl.pallas_call(
        _conv_ffn_kernel,
        out_shape=jax.ShapeDtypeStruct((b, s, d), _F32),
        grid=(b, s // TM_FFN),
        in_specs=[tok, _const_spec((1, d)), _const_spec(w_ffn_in.shape), _const_spec(conv_w.shape),
                  _const_spec((1, D_FF)), _const_spec(w_ffn_out.shape), _const_spec((1, d))],
        out_specs=tok,
        scratch_shapes=[pltpu.VMEM((TM_FFN + SUBLANES, D_FF), _F32)],
        compiler_params=_params(("arbitrary", "arbitrary")),
        name="conv_ffn",
    )(x, g_ffn.reshape(1, d), w_ffn_in, conv_w, conv_b.reshape(1, D_FF), w_ffn_out, g_final.reshape(1, d))


def kernel(x, mem, positions, g_mix, w_in, g_q_lat, w_uq, g_kv_lat, w_ukv, w_out, g_xattn, g_mem, w_xq,
           w_xkv, w_xo, g_ffn, w_ffn_in, conv_w, conv_b, w_ffn_out, g_final):
    assert w_in.shape[0] == 1, "one layer supported"
    l = 0
    w_in_t = w_in[l].T
    cos_t, sin_t, *wret_t, kv_mem = _prep(positions, w_in_t, mem, g_mem[l], w_xkv[l])
    wlat_t, wuq_p, wuk_p, wuvt_p = _permute_small_weights(w_in_t, w_uq[l], w_ukv[l])
    q, k, vt, y_ret, w_out_b, w_xq_b, w_xo_b = _in_proj(
        x, g_mix[l], wlat_t, wret_t, g_q_lat[l], wuq_p, g_kv_lat[l], wuk_p, wuvt_p, cos_t, sin_t,
        (w_out[l], w_xq[l], w_xo[l]))
    y_mla, w_ffn_in_b = _mla_attn(q, k, vt, (w_ffn_in[l],))
    x, w_ffn_out_b = _mix_xattn(x, y_mla, y_ret, w_out_b, g_xattn[l], w_xq_b, kv_mem, w_xo_b, (w_ffn_out[l],))
    return _conv_ffn(x, g_ffn[l], w_ffn_in_b, conv_w[l], conv_b[l], w_ffn_out_b, g_final)
```

```python
import functools
import math

import jax
import jax.numpy as jnp
from jax import lax
from jax.experimental import pallas as pl
from jax.experimental.pallas import tpu as pltpu

D_MODEL = 1024
EPS = 1e-6
ROPE_BASE = 10000.0
N_MLA_HEADS = 8
QK_NOPE = 64
QK_ROPE = 32
V_HEAD = 64
Q_LORA = 256
KV_LORA = 128
N_RET_HEADS = 4
RET_DK = 128
RET_DV = 128
RET_CHUNK = 128
MLA_WIDTH = N_MLA_HEADS * V_HEAD
RET_WIDTH = N_RET_HEADS * RET_DV
N_XATTN_HEADS = 4
XATTN_HEAD = D_MODEL // N_XATTN_HEADS
D_FF = 2816
CONV_W = 3

LANES = 128
SUBLANES = 8
_BF16_ROWS = 16
VMEM_LIMIT = 56 * 1024 * 1024

HEAD_PAD = LANES
ROPE_HALF = QK_ROPE // 2
MLA_QK_WIDTH = N_MLA_HEADS * HEAD_PAD
IN_LAT = Q_LORA + KV_LORA + HEAD_PAD

TM_TABLE = 1024
TM_IN = 1024
TQ_MLA = 256
TM_MIX = 1024
TM_FFN = 512

_BF16 = jnp.bfloat16
_F32 = jnp.float32


def _dot(a, b):
    return jnp.dot(a, b, preferred_element_type=_F32)


def _dot_nt(a, b):
    return lax.dot_general(a, b, (((1,), (1,)), ((), ())), preferred_element_type=_F32)


def _dot_tn(a, b):
    return lax.dot_general(a, b, (((0,), (0,)), ((), ())), preferred_element_type=_F32)


def _rms(x, g):
    inv = lax.rsqrt(jnp.mean(x * x, axis=-1, keepdims=True) + EPS)
    return x * inv * g


def _const_spec(shape):
    nd = len(shape)
    return pl.BlockSpec(shape, lambda *_: (0,) * nd, pipeline_mode=pl.Buffered(1))


def _params(sem):
    return pltpu.CompilerParams(dimension_semantics=sem, vmem_limit_bytes=VMEM_LIMIT)


def _with_casts(body, n_in, n_out, n_cast):
    def kern(*refs):
        ins, rest = refs[:n_in], refs[n_in:]
        cast_in, rest = rest[:n_cast], rest[n_cast:]
        outs, rest = rest[:n_out], rest[n_out:]
        cast_out, scratch = rest[:n_cast], rest[n_cast:]
        for src, dst in zip(cast_in, cast_out):
            dst[...] = src[...].astype(dst.dtype)
        body(*ins, *outs, *scratch)
    return kern


def _cast_plumbing(weights, grid):
    steps = math.prod(grid)
    if len(grid) == 1:
        block_of = lambda i: (i, 0)
    else:
        block_of = lambda bi, i: (bi * grid[1] + i, 0)
    in_specs, out_specs, out_shapes = [], [], []
    for w in weights:
        rows, cols = w.shape
        assert rows % steps == 0 and (rows // steps) % _BF16_ROWS == 0, (w.shape, steps)
        spec = pl.BlockSpec((rows // steps, cols), block_of)
        in_specs.append(spec)
        out_specs.append(spec)
        out_shapes.append(jax.ShapeDtypeStruct(w.shape, _BF16))
    return in_specs, out_specs, out_shapes


def _prep_kernel(pos_ref, inv_ref, wq_ref, wk_ref, wv_ref, wg_ref, mem_ref, gmem_ref, wxkv_ref,
                 cos_ref, sin_ref, wq_out, wk_out, wv_out, wg_out, kv_ref, *, steps_per_batch):
    half_rows = TM_IN // 2 // LANES
    lane = lax.broadcasted_iota(jnp.int32, (LANES, LANES), 1)
    column = lambda r: jnp.broadcast_to(pos_ref[r:r + 1, :], (LANES, LANES)).T
    for tile in range(TM_TABLE // TM_IN):
        for rr in range(half_rows):
            r0 = tile * 2 * half_rows + rr
            pos = jnp.where(lane < RET_DK // 2, column(r0), column(r0 + half_rows))
            ang = pos * inv_ref[...]
            out = slice((tile * half_rows + rr) * LANES, (tile * half_rows + rr + 1) * LANES)
            cos_ref[out, :] = jnp.cos(ang)
            sin_ref[out, :] = jnp.sin(ang)
    for src, dst in ((wq_ref, wq_out), (wk_ref, wk_out), (wv_ref, wv_out), (wg_ref, wg_out)):
        dst[...] = src[...].astype(_BF16)

    @pl.when(pl.program_id(0) % steps_per_batch == 0)
    def _():
        mem_n = _rms(mem_ref[0], gmem_ref[...]).astype(_BF16)
        kv_ref[0] = _dot(mem_n, wxkv_ref[...].astype(_BF16)).astype(_BF16)


def _prep(positions, w_in_t, mem, g_mem, w_xkv):
    t = positions.size
    steps = t // TM_TABLE
    n_in, d = w_in_t.shape
    b, m, _ = mem.shape
    n_kv = w_xkv.shape[1]
    per_batch = steps // b
    assert steps % b == 0
    row0 = Q_LORA + KV_LORA + QK_ROPE
    rows = RET_WIDTH // steps
    assert RET_WIDTH % steps == 0 and rows % _BF16_ROWS == 0 and row0 % rows == 0
    assert n_in == row0 + 4 * RET_WIDTH
    pos = positions.astype(_F32).reshape(t // LANES, LANES)
    f_ret = 1.0 / (ROPE_BASE ** (jnp.arange(0, RET_DK, 2, dtype=_F32) / RET_DK))
    inv = jnp.concatenate([f_ret, f_ret])
    group = lambda j: pl.BlockSpec((rows, d), lambda i: (row0 // rows + j * steps + i, 0))
    table = pl.BlockSpec((TM_TABLE // 2, LANES), lambda i: (i, 0))
    w_out = pl.BlockSpec((rows, d), lambda i: (i, 0))
    return pl.pallas_call(
        functools.partial(_prep_kernel, steps_per_batch=per_batch),
        out_shape=(jax.ShapeDtypeStruct((t // 2, LANES), _F32),) * 2
                  + (jax.ShapeDtypeStruct((RET_WIDTH, d), _BF16),) * 4
                  + (jax.ShapeDtypeStruct((b, m, n_kv), _BF16),),
        grid=(steps,),
        in_specs=[pl.BlockSpec((TM_TABLE // LANES, LANES), lambda i: (i, 0)),
                  pl.BlockSpec((1, LANES), lambda i: (0, 0)), group(0), group(1), group(2), group(3),
                  pl.BlockSpec((1, m, d), lambda i: (i // per_batch, 0, 0)), _const_spec((1, d)),
                  _const_spec(w_xkv.shape)],
        out_specs=(table, table, w_out, w_out, w_out, w_out,
                   pl.BlockSpec((1, m, n_kv), lambda i: (i // per_batch, 0, 0))),
        compiler_params=_params(("arbitrary",)),
        name="prep",
    )(pos, inv.reshape(1, LANES), w_in_t, w_in_t, w_in_t, w_in_t, mem, g_mem.reshape(1, d), w_xkv)


def _in_proj_kernel(x_ref, gmix_ref, wlat_ref, wrq_ref, wrk_ref, wrv_ref, wrg_ref, gq_ref, wuq_ref, gkv_ref, wuk_ref, wuvt_ref, cos_ref, sin_ref,
                    intra_ref, kend_ref, qstart_ref, decay_ref,
                    q_ref, k_ref, vt_ref, yret_ref, state_ref):
    h = _rms(x_ref[0], gmix_ref[...]).astype(_BF16)
    lane = lax.broadcasted_iota(jnp.int32, (TM_IN, LANES), 1)
    low = lane < RET_DK // 2

    lo = lax.broadcasted_iota(jnp.int32, (TM_IN // 2, LANES), 1) < RET_DK // 2

    def unpack(t):
        t_roll = pltpu.roll(t, RET_DK // 2, axis=1)
        return jnp.concatenate([jnp.where(lo, t, t_roll), jnp.where(lo, t_roll, t)], axis=0)

    cos_r, sin_full = unpack(cos_ref[...]), unpack(sin_ref[...])
    sin_r = jnp.where(low, -sin_full, sin_full)
    ratio = (RET_DK // 2) // ROPE_HALF
    rope_lane = (lane >= QK_NOPE) & (lane < QK_NOPE + QK_ROPE)
    src = jnp.where(lane < QK_NOPE + ROPE_HALF, lane - QK_NOPE, lane - QK_NOPE - ROPE_HALF) * ratio
    src = jnp.where(rope_lane, src, 0)
    g_cos = jnp.take_along_axis(cos_r, src, axis=1)
    g_sin = jnp.take_along_axis(sin_full, src, axis=1)
    cos_m = jnp.where(rope_lane, g_cos, 1.0)
    sin_m = jnp.where(rope_lane, jnp.where(lane < QK_NOPE + ROPE_HALF, -g_sin, g_sin), 0.0)

    first_half = lane < QK_NOPE + ROPE_HALF

    def swap_halves(t):
        return jnp.where(first_half, pltpu.roll(t, LANES - ROPE_HALF, axis=1), pltpu.roll(t, ROPE_HALF, axis=1))

    lat = _dot_nt(h, wlat_ref[...])
    c_q = lat[:, :Q_LORA]
    c_kv = lat[:, Q_LORA:Q_LORA + KV_LORA]
    kr = lat[:, Q_LORA + KV_LORA:IN_LAT]
    k_rope = kr * cos_m + swap_halves(kr) * sin_m

    cqn = _rms(c_q, gq_ref[...]).astype(_BF16)
    q = _dot(cqn, wuq_ref[...])
    scale = math.log2(math.e) / math.sqrt(QK_NOPE + QK_ROPE)
    for hd in range(N_MLA_HEADS):
        sl = slice(hd * HEAD_PAD, (hd + 1) * HEAD_PAD)
        q_ref[0, :, sl] = ((q[:, sl] * cos_m + swap_halves(q[:, sl]) * sin_m) * scale).astype(_BF16)

    ckvn = _rms(c_kv, gkv_ref[...]).astype(_BF16)
    k_nope = _dot(ckvn, wuk_ref[...])
    for hd in range(N_MLA_HEADS):
        sl = slice(hd * HEAD_PAD, (hd + 1) * HEAD_PAD)
        k_ref[0, :, sl] = (k_nope[:, sl] + k_rope).astype(_BF16)
    v_t = _dot_nt(wuvt_ref[...], ckvn).astype(_BF16)
    for j in range(TM_IN // TQ_MLA):
        vt_ref[0, j] = v_t[:, j * TQ_MLA:(j + 1) * TQ_MLA]

    def ret_rope(w_ref, mult):
        r = _dot_nt(h, w_ref[...])
        heads = []
        for hd in range(N_RET_HEADS):
            rh = r[:, hd * RET_DK:(hd + 1) * RET_DK]
            roped = rh * cos_r + pltpu.roll(rh, RET_DK // 2, axis=1) * sin_r
            if mult is not None:
                roped = roped * mult
            heads.append(roped.astype(_BF16))
        return heads

    rq = ret_rope(wrq_ref, None)
    rk = ret_rope(wrk_ref, RET_DK ** -0.5)
    rv = _dot_nt(h, wrv_ref[...]).astype(_BF16)
    rg = _dot_nt(h, wrg_ref[...]).astype(_BF16)
    _retention_tile(rq, rk, rv, rg, intra_ref, kend_ref, qstart_ref, decay_ref, state_ref, yret_ref)


def _retention_tile(q_heads, k_heads, v, g, intra_ref, kend_ref, qstart_ref, decay_ref, state_ref, o_ref):
    @pl.when(pl.program_id(1) == 0)
    def _():
        state_ref[...] = jnp.zeros(state_ref.shape, _F32)

    L = RET_CHUNK
    n_chunks = v.shape[0] // L
    units = [(c, hd) for c in range(n_chunks) for hd in range(N_RET_HEADS)]
    rows = lambda c: slice(c * L, (c + 1) * L)
    cols = lambda hd: slice(hd * RET_DV, (hd + 1) * RET_DV)
    scores, chunk_kv = {}, {}
    for c, hd in units:
        scores[c, hd] = _dot_nt(q_heads[hd][rows(c)], k_heads[hd][rows(c)])
    for c, hd in units:
        v_dec = (v[rows(c), cols(hd)].astype(_F32) * kend_ref[hd]).astype(_BF16)
        chunk_kv[c, hd] = _dot_tn(k_heads[hd][rows(c)], v_dec)
    prev_state = {}
    for hd in range(N_RET_HEADS):
        state = state_ref[hd]
        for c in range(n_chunks):
            prev_state[c, hd] = state.astype(_BF16)
            state = decay_ref[hd] * state + chunk_kv[c, hd]
        state_ref[hd] = state
    inner, cross = {}, {}
    for c, hd in units:
        inner[c, hd] = _dot((scores[c, hd] * intra_ref[hd]).astype(_BF16), v[rows(c), cols(hd)])
    for c, hd in units:
        cross[c, hd] = _dot(q_heads[hd][rows(c)], prev_state[c, hd])
    for c, hd in units:
        out = inner[c, hd] + cross[c, hd] * qstart_ref[hd]
        mu = jnp.mean(out, axis=-1, keepdims=True)
        cen = out - mu
        var = jnp.mean(cen * cen, axis=-1, keepdims=True)
        normed = cen * lax.rsqrt(var + EPS)
        gate = g[rows(c), cols(hd)].astype(_F32)
        o_ref[0, rows(c), cols(hd)] = (normed * (gate * jax.nn.sigmoid(gate))).astype(o_ref.dtype)


def _permute_small_weights(w_in_t, w_uq, w_ukv):
    o_kr = Q_LORA + KV_LORA
    d = w_in_t.shape[1]
    wlat_t = jnp.concatenate([w_in_t[:o_kr], jnp.zeros((QK_NOPE, d), w_in_t.dtype), w_in_t[o_kr:o_kr + QK_ROPE],
                              jnp.zeros((HEAD_PAD - QK_NOPE - QK_ROPE, d), w_in_t.dtype)], axis=0).astype(_BF16)
    wq = w_uq.reshape(Q_LORA, N_MLA_HEADS, QK_NOPE + QK_ROPE)
    nope, r1, r2 = wq[..., :QK_NOPE], wq[..., QK_NOPE:QK_NOPE + ROPE_HALF], wq[..., QK_NOPE + ROPE_HALF:]
    z32 = jnp.zeros((Q_LORA, N_MLA_HEADS, HEAD_PAD - QK_NOPE - QK_ROPE), w_uq.dtype)
    wuq_p = jnp.concatenate([nope, r1, r2, z32], axis=-1).reshape(Q_LORA, MLA_QK_WIDTH).astype(_BF16)

    wkv = w_ukv.reshape(KV_LORA, N_MLA_HEADS, QK_NOPE + V_HEAD)
    zk = jnp.zeros((KV_LORA, N_MLA_HEADS, HEAD_PAD - QK_NOPE), w_ukv.dtype)
    wk = jnp.concatenate([wkv[..., :QK_NOPE], zk], axis=-1).reshape(KV_LORA, MLA_QK_WIDTH)
    wv_t = wkv[..., QK_NOPE:].reshape(KV_LORA, MLA_WIDTH).T
    return wlat_t, wuq_p, wk.astype(_BF16), wv_t.astype(_BF16)


def _in_proj(x, g_mix, wlat_t, wret_t, g_q_lat, wuq_p, g_kv_lat, wuk_p, wuvt_p, cos_t, sin_t, later_weights):
    b, s, d = x.shape
    nt = s // TM_IN
    per = TM_IN // TQ_MLA
    tok = lambda w: pl.BlockSpec((1, TM_IN, w), lambda bi, i: (bi, i, 0))
    tab = pl.BlockSpec((TM_IN // 2, LANES), lambda bi, i: (bi * nt + i, 0))
    bf = lambda w: jax.ShapeDtypeStruct((b, s, w), _BF16)
    ret_tables = _retention_tables()
    cast_in, cast_out, cast_shapes = _cast_plumbing(later_weights, (b, nt))
    in_specs = ([tok(d), _const_spec((1, d)), _const_spec(wlat_t.shape)] + [_const_spec(w.shape) for w in wret_t]
                + [_const_spec((1, Q_LORA)), _const_spec(wuq_p.shape), _const_spec((1, KV_LORA)),
                   _const_spec(wuk_p.shape), _const_spec(wuvt_p.shape), tab, tab]
                + [_const_spec(t.shape) for t in ret_tables])
    out_specs = (tok(MLA_QK_WIDTH), tok(MLA_QK_WIDTH),
                 pl.BlockSpec((1, per, MLA_WIDTH, TQ_MLA), lambda bi, i: (bi, i, 0, 0)), tok(RET_WIDTH))
    return pl.pallas_call(
        _with_casts(_in_proj_kernel, len(in_specs), len(out_specs), len(later_weights)),
        out_shape=(bf(MLA_QK_WIDTH), bf(MLA_QK_WIDTH),
                   jax.ShapeDtypeStruct((b, s // TQ_MLA, MLA_WIDTH, TQ_MLA), _BF16), bf(RET_WIDTH), *cast_shapes),
        grid=(b, nt),
        in_specs=in_specs + cast_in,
        out_specs=(*out_specs, *cast_out),
        scratch_shapes=[pltpu.VMEM((N_RET_HEADS, RET_DK, RET_DV), _F32)],
        compiler_params=_params(("arbitrary", "arbitrary")),
        name="in_proj",
    )(x, g_mix.reshape(1, d), wlat_t, *wret_t, g_q_lat.reshape(1, Q_LORA), wuq_p, g_kv_lat.reshape(1, KV_LORA),
      wuk_p, wuvt_p, cos_t, sin_t, *ret_tables, *later_weights)


_MASK_VALUE = -0.7 * float(jnp.finfo(jnp.float32).max)
_MLA_AHEAD = 4
_MLA_TILES_PER_ITER = 5
_MLA_DENOM_ROWS = 16


def _mla_attn_kernel(qlo_ref, qhi_ref, k_ref, vt_ref, o_ref, q_ref, s_ref, m_ref, acc_ref):
    tq = TQ_MLA
    n_tiles = k_ref.shape[1] // tq
    lo = pl.program_id(1)
    hi = n_tiles - 1 - lo
    q_ref[0] = qlo_ref[0]
    q_ref[1] = qhi_ref[0]
    m_ref[...] = jnp.full(m_ref.shape, _MASK_VALUE, _F32)
    acc_ref[...] = jnp.zeros(acc_ref.shape, _F32)
    ones_rows = jnp.ones((_MLA_DENOM_ROWS, tq), _BF16)

    def step_args(t):
        sel = (t > lo).astype(jnp.int32)
        return sel, t - 1 - sel * lo

    def scores(sel, kk, hd):
        sl = slice(hd * HEAD_PAD, (hd + 1) * HEAD_PAD)
        ks = pl.multiple_of(kk * tq, tq)
        s_ref[hd] = _dot_nt(k_ref[0, pl.ds(ks, tq), sl], q_ref[sel, :, sl])

    def softmax(sel, hd, keep):
        row = slice(hd, hd + 1)

        def st():
            s = s_ref[hd]
            return s if keep is None else jnp.where(keep, s, _MASK_VALUE)

        m_prev = m_ref[sel, row, :]
        m_next = jnp.maximum(m_prev, jnp.max(st(), axis=0, keepdims=True))
        m_ref[sel, row, :] = m_next
        return jnp.exp2(m_prev - m_next), jnp.exp2(st() - m_next).astype(_BF16)

    def accumulate(sel, kk, hd, alpha, p):
        vt = jnp.concatenate([vt_ref[0, kk, hd * V_HEAD:(hd + 1) * V_HEAD, :], ones_rows], axis=0)
        acc_ref[sel, hd] = acc_ref[sel, hd] * alpha + _dot(vt, p)

    def tile(sel, kk, diagonal, nxt):
        keep = None
        if diagonal:
            key = lax.broadcasted_iota(jnp.int32, (tq, tq), 0)
            qry = lax.broadcasted_iota(jnp.int32, (tq, tq), 1)
            keep = key <= qry
        pending = None
        for hd in range(N_MLA_HEADS):
            ahead = hd + _MLA_AHEAD
            if ahead < N_MLA_HEADS:
                scores(sel, kk, ahead)
            elif nxt is not None:
                scores(nxt[0], nxt[1], ahead - N_MLA_HEADS)
            current = softmax(sel, hd, keep)
            if pending is not None:
                accumulate(sel, kk, hd - 1, *pending)
            pending = current
        accumulate(sel, kk, N_MLA_HEADS - 1, *pending)

    for hd in range(_MLA_AHEAD):
        scores(0, lo, hd)
    tile(0, lo, True, step_args(1))

    def body(it, carry):
        for u in range(_MLA_TILES_PER_ITER):
            t = 1 + it * _MLA_TILES_PER_ITER + u
            last = t + 1 == n_tiles
            sel_n, kk_n = step_args(t + 1)
            tile(*step_args(t), False, (jnp.where(last, 1, sel_n), jnp.where(last, hi, kk_n)))
        return carry

    lax.fori_loop(0, (n_tiles - 1) // _MLA_TILES_PER_ITER, body, 0)
    tile(1, hi, True, None)

    for sel, qt in ((0, lo), (1, hi)):
        out_t = jnp.concatenate([acc_ref[sel, hd, :V_HEAD, :] / acc_ref[sel, hd, V_HEAD:V_HEAD + 1, :]
                                 for hd in range(N_MLA_HEADS)], axis=0)
        o_ref[0, pl.ds(pl.multiple_of(qt * tq, tq), tq), :] = out_t.T.astype(o_ref.dtype)


def _mla_attn(q, k, vt, later_weights):
    b, s, w = q.shape
    nk = s // TQ_MLA
    assert nk % 2 == 0 and (nk - 1) % _MLA_TILES_PER_ITER == 0
    cast_in, cast_out, cast_shapes = _cast_plumbing(later_weights, (b, nk // 2))
    in_specs = [pl.BlockSpec((1, TQ_MLA, w), lambda bi, i: (bi, i, 0)),
                pl.BlockSpec((1, TQ_MLA, w), lambda bi, i: (bi, nk - 1 - i, 0)),
                pl.BlockSpec((1, s, w), lambda bi, i: (bi, 0, 0)),
                pl.BlockSpec((1, nk, MLA_WIDTH, TQ_MLA), lambda bi, i: (bi, 0, 0, 0))]
    return pl.pallas_call(
        _with_casts(_mla_attn_kernel, len(in_specs), 1, len(later_weights)),
        out_shape=(jax.ShapeDtypeStruct((b, s, MLA_WIDTH), _BF16), *cast_shapes),
        grid=(b, nk // 2),
        in_specs=in_specs + cast_in,
        out_specs=(pl.BlockSpec((1, s, MLA_WIDTH), lambda bi, i: (bi, 0, 0)), *cast_out),
        scratch_shapes=[pltpu.VMEM((2, TQ_MLA, w), _BF16),
                        pltpu.VMEM((N_MLA_HEADS, TQ_MLA, TQ_MLA), _F32),
                        pltpu.VMEM((2, N_MLA_HEADS, TQ_MLA), _F32),
                        pltpu.VMEM((2, N_MLA_HEADS, V_HEAD + _MLA_DENOM_ROWS, TQ_MLA), _F32)],
        compiler_params=_params(("arbitrary", "arbitrary")),
        name="mla_attn",
    )(q, q, k, vt, *later_weights)


def _retention_tables():
    h, L = N_RET_HEADS, RET_CHUNK
    log_gamma = jnp.log(1.0 - 2.0 ** (-5.0 - jnp.arange(h, dtype=_F32)))
    j = jnp.arange(L, dtype=_F32)
    diff = j[:, None] - j[None, :]
    intra = jnp.where(diff[None] >= 0,
                      jnp.exp(jnp.maximum(diff, 0.0)[None] * log_gamma[:, None, None]), 0.0)
    rowb = lambda t: jnp.broadcast_to(t.T[:, :, None], (h, L, LANES))
    k_to_end = jnp.exp((L - 1 - j)[:, None] * log_gamma[None, :])
    q_from_start = jnp.exp((j + 1)[:, None] * log_gamma[None, :])
    chunk_decay = jnp.broadcast_to(jnp.exp(L * log_gamma)[:, None, None], (h, RET_DK, RET_DV))
    return intra, rowb(k_to_end), rowb(q_from_start), chunk_decay


def _mix_xattn_kernel(x_ref, ymla_ref, yret_ref, wout_ref, gx_ref, wxq_ref, kv_ref, wxo_ref, o_ref):
    half = TM_MIX // 2
    rows = [slice(0, half), slice(half, TM_MIX)]
    cols = lambda hd: slice(hd * XATTN_HEAD, (hd + 1) * XATTN_HEAD)
    vcols = lambda hd: slice(D_MODEL + hd * XATTN_HEAD, D_MODEL + (hd + 1) * XATTN_HEAD)

    def out_proj(r):
        return (x_ref[0, r, :] + _dot(ymla_ref[0, r, :], wout_ref[:MLA_WIDTH, :])
                + _dot(yret_ref[0, r, :], wout_ref[MLA_WIDTH:, :]))

    def queries(x1):
        h = _rms(x1, gx_ref[...]).astype(_BF16)
        return (_dot(h, wxq_ref[...]) * (1.0 / math.sqrt(XATTN_HEAD))).astype(_BF16)

    def scores(q):
        return [_dot_nt(q[:, cols(hd)], kv_ref[0, :, cols(hd)]) for hd in range(N_XATTN_HEADS)]

    def attend(ss):
        probs = [jnp.exp(s - jnp.max(s, axis=-1, keepdims=True)) for s in ss]
        heads = [(_dot(p.astype(_BF16), kv_ref[0, :, vcols(hd)]) / jnp.sum(p, axis=-1, keepdims=True)).astype(_BF16)
                 for hd, p in enumerate(probs)]
        return jnp.concatenate(heads, axis=1)

    x1 = [out_proj(r) for r in rows]
    q0 = queries(x1[0])
    q1 = queries(x1[1])
    s0 = scores(q0)
    s1 = scores(q1)
    a0 = attend(s0)
    o_ref[0, rows[0], :] = x1[0] + _dot(a0, wxo_ref[...])
    a1 = attend(s1)
    o_ref[0, rows[1], :] = x1[1] + _dot(a1, wxo_ref[...])


def _mix_xattn(x, y_mla, y_ret, w_out, g_xattn, w_xq, kv_mem, w_xo, later_weights):
    b, s, d = x.shape
    m, n = kv_mem.shape[1:]
    tok = lambda w: pl.BlockSpec((1, TM_MIX, w), lambda bi, i: (bi, i, 0))
    cast_in, cast_out, cast_shapes = _cast_plumbing(later_weights, (b, s // TM_MIX))
    in_specs = [tok(d), tok(MLA_WIDTH), tok(RET_WIDTH), _const_spec(w_out.shape), _const_spec((1, d)),
                _const_spec(w_xq.shape), pl.BlockSpec((1, m, n), lambda bi, i: (bi, 0, 0)),
                _const_spec(w_xo.shape)]
    return pl.pallas_call(
        _with_casts(_mix_xattn_kernel, len(in_specs), 1, len(later_weights)),
        out_shape=(jax.ShapeDtypeStruct((b, s, d), _F32), *cast_shapes),
        grid=(b, s // TM_MIX),
        in_specs=in_specs + cast_in,
        out_specs=(tok(d), *cast_out),
        compiler_params=_params(("arbitrary", "arbitrary")),
        name="mix_xattn",
    )(x, y_mla, y_ret, w_out, g_xattn.reshape(1, d), w_xq, kv_mem, w_xo, *later_weights)


def _conv_ffn_kernel(x_ref, g_ref, win_ref, cw_ref, cb_ref, wout_ref, gfin_ref, o_ref, gate_ref):
    halo = SUBLANES
    half = TM_FFN // 2

    @pl.when(pl.program_id(1) == 0)
    def _():
        gate_ref[:halo, :] = jnp.zeros((halo, D_FF), _F32)

    rows = [slice(0, half), slice(half, TM_FFN)]
    xs = [x_ref[0, r, :] for r in rows]
    hs = [_rms(x, g_ref[...]).astype(_BF16) for x in xs]

    def project(i):
        gate = _dot(hs[i], win_ref[:, :D_FF])
        up = _dot(hs[i], win_ref[:, D_FF:])
        gate_ref[halo + i * half:halo + (i + 1) * half, :] = gate
        return gate, up

    def activate(i, gate, up):
        conv = cb_ref[...] + gate * cw_ref[CONV_W - 1:CONV_W, :]
        for tap in range(CONV_W - 1):
            back = CONV_W - 1 - tap
            lo = halo + i * half - back
            conv = conv + gate_ref[lo:lo + half, :] * cw_ref[tap:tap + 1, :]
        return (conv * jax.nn.sigmoid(conv) * up).astype(_BF16)

    def finish(i, act):
        x3 = xs[i] + _dot(act, wout_ref[...])
        o_ref[0, rows[i], :] = _rms(x3, gfin_ref[...])

    g0, u0 = project(0)
    g1, u1 = project(1)
    a0 = activate(0, g0, u0)
    finish(0, a0)
    a1 = activate(1, g1, u1)
    finish(1, a1)
    gate_ref[:halo, :] = gate_ref[TM_FFN:, :]


def _conv_ffn(x, g_ffn, w_ffn_in, conv_w, conv_b, w_ffn_out, g_final):
    b, s, d = x.shape
    tok = pl.BlockSpec((1, TM_FFN, d), lambda bi, i: (bi, i, 0))
    return pl.pallas_call(
        _conv_ffn_kernel,
        out_shape=jax.ShapeDtypeStruct((b, s, d), _F32),
        grid=(b, s // TM_FFN),
        in_specs=[tok, _const_spec((1, d)), _const_spec(w_ffn_in.shape), _const_spec(conv_w.shape),
                  _const_spec((1, D_FF)), _const_spec(w_ffn_out.shape), _const_spec((1, d))],
        out_specs=tok,
        scratch_shapes=[pltpu.VMEM((TM_FFN + SUBLANES, D_FF), _F32)],
        compiler_params=_params(("arbitrary", "arbitrary")),
        name="conv_ffn",
    )(x, g_ffn.reshape(1, d), w_ffn_in, conv_w, conv_b.reshape(1, D_FF), w_ffn_out, g_final.reshape(1, d))


def kernel(x, mem, positions, g_mix, w_in, g_q_lat, w_uq, g_kv_lat, w_ukv, w_out, g_xattn, g_mem, w_xq,
           w_xkv, w_xo, g_ffn, w_ffn_in, conv_w, conv_b, w_ffn_out, g_final):
    assert w_in.shape[0] == 1, "one layer supported"
    l = 0
    w_in_t = w_in[l].T
    cos_t, sin_t, *wret_t, kv_mem = _prep(positions, w_in_t, mem, g_mem[l], w_xkv[l])
    wlat_t, wuq_p, wuk_p, wuvt_p = _permute_small_weights(w_in_t, w_uq[l], w_ukv[l])
    q, k, vt, y_ret, w_out_b, w_xq_b, w_xo_b = _in_proj(
        x, g_mix[l], wlat_t, wret_t, g_q_lat[l], wuq_p, g_kv_lat[l], wuk_p, wuvt_p, cos_t, sin_t,
        (w_out[l], w_xq[l], w_xo[l]))
    y_mla, w_ffn_in_b = _mla_attn(q, k, vt, (w_ffn_in[l],))
    x, w_ffn_out_b = _mix_xattn(x, y_mla, y_ret, w_out_b, g_xattn[l], w_xq_b, kv_mem, w_xo_b, (w_ffn_out[l],))
    return _conv_ffn(x, g_ffn[l], w_ffn_in_b, conv_w[l], conv_b[l], w_ffn_out_b, g_final)
```

```python
import functools
import math

import jax
import jax.numpy as jnp
from jax import lax
from jax.experimental import pallas as pl
from jax.experimental.pallas import tpu as pltpu

D_MODEL = 1024
EPS = 1e-6
ROPE_BASE = 10000.0
N_MLA_HEADS = 8
QK_NOPE = 64
QK_ROPE = 32
V_HEAD = 64
Q_LORA = 256
KV_LORA = 128
N_RET_HEADS = 4
RET_DK = 128
RET_DV = 128
RET_CHUNK = 128
MLA_WIDTH = N_MLA_HEADS * V_HEAD
RET_WIDTH = N_RET_HEADS * RET_DV
N_XATTN_HEADS = 4
XATTN_HEAD = D_MODEL // N_XATTN_HEADS
D_FF = 2816
CONV_W = 3

LANES = 128
SUBLANES = 8
_BF16_ROWS = 16
VMEM_LIMIT = 56 * 1024 * 1024

HEAD_PAD = LANES
ROPE_HALF = QK_ROPE // 2
MLA_QK_WIDTH = N_MLA_HEADS * HEAD_PAD
IN_LAT = Q_LORA + KV_LORA + HEAD_PAD

TM_TABLE = 1024
TM_IN = 1024
TQ_MLA = 256
TM_MIX = 1024
TM_FFN = 512

_BF16 = jnp.bfloat16
_F32 = jnp.float32


def _dot(a, b):
    return jnp.dot(a, b, preferred_element_type=_F32)


def _dot_nt(a, b):
    return lax.dot_general(a, b, (((1,), (1,)), ((), ())), preferred_element_type=_F32)


def _dot_tn(a, b):
    return lax.dot_general(a, b, (((0,), (0,)), ((), ())), preferred_element_type=_F32)


def _rms(x, g):
    inv = lax.rsqrt(jnp.mean(x * x, axis=-1, keepdims=True) + EPS)
    return x * inv * g


def _const_spec(shape):
    nd = len(shape)
    return pl.BlockSpec(shape, lambda *_: (0,) * nd, pipeline_mode=pl.Buffered(1))


def _params(sem):
    return pltpu.CompilerParams(dimension_semantics=sem, vmem_limit_bytes=VMEM_LIMIT)


def _with_casts(body, n_in, n_out, n_cast):
    def kern(*refs):
        ins, rest = refs[:n_in], refs[n_in:]
        cast_in, rest = rest[:n_cast], rest[n_cast:]
        outs, rest = rest[:n_out], rest[n_out:]
        cast_out, scratch = rest[:n_cast], rest[n_cast:]
        for src, dst in zip(cast_in, cast_out):
            dst[...] = src[...].astype(dst.dtype)
        body(*ins, *outs, *scratch)
    return kern


def _cast_plumbing(weights, grid):
    steps = math.prod(grid)
    if len(grid) == 1:
        block_of = lambda i: (i, 0)
    else:
        block_of = lambda bi, i: (bi * grid[1] + i, 0)
    in_specs, out_specs, out_shapes = [], [], []
    for w in weights:
        rows, cols = w.shape
        assert rows % steps == 0 and (rows // steps) % _BF16_ROWS == 0, (w.shape, steps)
        spec = pl.BlockSpec((rows // steps, cols), block_of)
        in_specs.append(spec)
        out_specs.append(spec)
        out_shapes.append(jax.ShapeDtypeStruct(w.shape, _BF16))
    return in_specs, out_specs, out_shapes


def _prep_kernel(pos_ref, inv_ref, wq_ref, wk_ref, wv_ref, wg_ref, mem_ref, gmem_ref, wxk_ref, wxv_ref, wxq_ref,
                 wxo_ref, cos_ref, sin_ref, wq_out, wk_out, wv_out, wg_out, wqk_ref, wvo_ref):
    half_rows = TM_IN // 2 // LANES
    lane = lax.broadcasted_iota(jnp.int32, (LANES, LANES), 1)
    column = lambda r: jnp.broadcast_to(pos_ref[r:r + 1, :], (LANES, LANES)).T
    for tile in range(TM_TABLE // TM_IN):
        for rr in range(half_rows):
            r0 = tile * 2 * half_rows + rr
            pos = jnp.where(lane < RET_DK // 2, column(r0), column(r0 + half_rows))
            ang = pos * inv_ref[...]
            out = slice((tile * half_rows + rr) * LANES, (tile * half_rows + rr + 1) * LANES)
            cos_ref[out, :] = jnp.cos(ang)
            sin_ref[out, :] = jnp.sin(ang)
    for src, dst in ((wq_ref, wq_out), (wk_ref, wk_out), (wv_ref, wv_out), (wg_ref, wg_out)):
        dst[...] = src[...].astype(_BF16)

    mem_n = _rms(mem_ref[0], gmem_ref[...]).astype(_BF16)
    k_h = _dot(mem_n, wxk_ref[...].astype(_BF16)).astype(_BF16)
    v_h = _dot(mem_n, wxv_ref[...].astype(_BF16)).astype(_BF16)
    wqk_ref[0] = (_dot_nt(wxq_ref[...].astype(_BF16), k_h) * (1.0 / math.sqrt(XATTN_HEAD))).astype(_BF16)
    wvo_ref[0] = _dot(v_h, wxo_ref[...].astype(_BF16)).astype(_BF16)


def _prep(positions, w_in_t, mem, g_mem, w_xkv, w_xq, w_xo):
    t = positions.size
    steps = t // TM_TABLE
    n_in, d = w_in_t.shape
    b, m, _ = mem.shape
    assert steps == b * N_XATTN_HEADS
    row0 = Q_LORA + KV_LORA + QK_ROPE
    rows = RET_WIDTH // steps
    assert RET_WIDTH % steps == 0 and rows % _BF16_ROWS == 0 and row0 % rows == 0
    assert n_in == row0 + 4 * RET_WIDTH
    pos = positions.astype(_F32).reshape(t // LANES, LANES)
    f_ret = 1.0 / (ROPE_BASE ** (jnp.arange(0, RET_DK, 2, dtype=_F32) / RET_DK))
    inv = jnp.concatenate([f_ret, f_ret])
    group = lambda j: pl.BlockSpec((rows, d), lambda i: (row0 // rows + j * steps + i, 0))
    table = pl.BlockSpec((TM_TABLE // 2, LANES), lambda i: (i, 0))
    w_out = pl.BlockSpec((rows, d), lambda i: (i, 0))
    nh = N_XATTN_HEADS
    head_cols = lambda off: pl.BlockSpec((d, XATTN_HEAD), lambda i: (0, off + i % nh))
    return pl.pallas_call(
        _prep_kernel,
        out_shape=(jax.ShapeDtypeStruct((t // 2, LANES), _F32),) * 2
                  + (jax.ShapeDtypeStruct((RET_WIDTH, d), _BF16),) * 4
                  + (jax.ShapeDtypeStruct((b, d, nh * m), _BF16), jax.ShapeDtypeStruct((b, nh * m, d), _BF16)),
        grid=(steps,),
        in_specs=[pl.BlockSpec((TM_TABLE // LANES, LANES), lambda i: (i, 0)),
                  pl.BlockSpec((1, LANES), lambda i: (0, 0)), group(0), group(1), group(2), group(3),
                  pl.BlockSpec((1, m, d), lambda i: (i // nh, 0, 0)), _const_spec((1, d)),
                  head_cols(0), head_cols(nh), head_cols(0),
                  pl.BlockSpec((XATTN_HEAD, d), lambda i: (i % nh, 0))],
        out_specs=(table, table, w_out, w_out, w_out, w_out,
                   pl.BlockSpec((1, d, m), lambda i: (i // nh, 0, i % nh)),
                   pl.BlockSpec((1, m, d), lambda i: (i // nh, i % nh, 0))),
        compiler_params=_params(("arbitrary",)),
        name="prep",
    )(pos, inv.reshape(1, LANES), w_in_t, w_in_t, w_in_t, w_in_t, mem, g_mem.reshape(1, d), w_xkv, w_xkv, w_xq, w_xo)


def _in_proj_kernel(x_ref, gmix_ref, wlat_ref, wrq_ref, wrk_ref, wrv_ref, wrg_ref, gq_ref, wuq_ref, gkv_ref, wuk_ref, wuvt_ref, cos_ref, sin_ref,
                    intra_ref, kend_ref, qstart_ref, decay_ref,
                    q_ref, k_ref, vt_ref, yret_ref, state_ref):
    h = _rms(x_ref[0], gmix_ref[...]).astype(_BF16)
    lane = lax.broadcasted_iota(jnp.int32, (TM_IN, LANES), 1)
    low = lane < RET_DK // 2

    lo = lax.broadcasted_iota(jnp.int32, (TM_IN // 2, LANES), 1) < RET_DK // 2

    def unpack(t):
        t_roll = pltpu.roll(t, RET_DK // 2, axis=1)
        return jnp.concatenate([jnp.where(lo, t, t_roll), jnp.where(lo, t_roll, t)], axis=0)

    cos_r, sin_full = unpack(cos_ref[...]), unpack(sin_ref[...])
    sin_r = jnp.where(low, -sin_full, sin_full)
    ratio = (RET_DK // 2) // ROPE_HALF
    rope_lane = (lane >= QK_NOPE) & (lane < QK_NOPE + QK_ROPE)
    src = jnp.where(lane < QK_NOPE + ROPE_HALF, lane - QK_NOPE, lane - QK_NOPE - ROPE_HALF) * ratio
    src = jnp.where(rope_lane, src, 0)
    g_cos = jnp.take_along_axis(cos_r, src, axis=1)
    g_sin = jnp.take_along_axis(sin_full, src, axis=1)
    cos_m = jnp.where(rope_lane, g_cos, 1.0)
    sin_m = jnp.where(rope_lane, jnp.where(lane < QK_NOPE + ROPE_HALF, -g_sin, g_sin), 0.0)

    first_half = lane < QK_NOPE + ROPE_HALF

    def swap_halves(t):
        return jnp.where(first_half, pltpu.roll(t, LANES - ROPE_HALF, axis=1), pltpu.roll(t, ROPE_HALF, axis=1))

    lat = _dot_nt(h, wlat_ref[...])
    c_q = lat[:, :Q_LORA]
    c_kv = lat[:, Q_LORA:Q_LORA + KV_LORA]
    kr = lat[:, Q_LORA + KV_LORA:IN_LAT]
    k_rope = kr * cos_m + swap_halves(kr) * sin_m

    cqn = _rms(c_q, gq_ref[...]).astype(_BF16)
    q = _dot(cqn, wuq_ref[...])
    scale = math.log2(math.e) / math.sqrt(QK_NOPE + QK_ROPE)
    for hd in range(N_MLA_HEADS):
        sl = slice(hd * HEAD_PAD, (hd + 1) * HEAD_PAD)
        q_ref[0, :, sl] = ((q[:, sl] * cos_m + swap_halves(q[:, sl]) * sin_m) * scale).astype(_BF16)

    ckvn = _rms(c_kv, gkv_ref[...]).astype(_BF16)
    k_nope = _dot(ckvn, wuk_ref[...])
    for hd in range(N_MLA_HEADS):
        sl = slice(hd * HEAD_PAD, (hd + 1) * HEAD_PAD)
        k_ref[0, :, sl] = (k_nope[:, sl] + k_rope).astype(_BF16)
    v_t = _dot_nt(wuvt_ref[...], ckvn).astype(_BF16)
    for j in range(TM_IN // TQ_MLA):
        vt_ref[0, j] = v_t[:, j * TQ_MLA:(j + 1) * TQ_MLA]

    def ret_rope(w_ref, mult):
        r = _dot_nt(h, w_ref[...])
        heads = []
        for hd in range(N_RET_HEADS):
            rh = r[:, hd * RET_DK:(hd + 1) * RET_DK]
            roped = rh * cos_r + pltpu.roll(rh, RET_DK // 2, axis=1) * sin_r
            if mult is not None:
                roped = roped * mult
            heads.append(roped.astype(_BF16))
        return heads

    rq = ret_rope(wrq_ref, None)
    rk = ret_rope(wrk_ref, RET_DK ** -0.5)
    rv = _dot_nt(h, wrv_ref[...]).astype(_BF16)
    rg = _dot_nt(h, wrg_ref[...]).astype(_BF16)
    _retention_tile(rq, rk, rv, rg, intra_ref, kend_ref, qstart_ref, decay_ref, state_ref, yret_ref)


def _retention_tile(q_heads, k_heads, v, g, intra_ref, kend_ref, qstart_ref, decay_ref, state_ref, o_ref):
    @pl.when(pl.program_id(1) == 0)
    def _():
        state_ref[...] = jnp.zeros(state_ref.shape, _F32)

    L = RET_CHUNK
    n_chunks = v.shape[0] // L
    units = [(c, hd) for c in range(n_chunks) for hd in range(N_RET_HEADS)]
    rows = lambda c: slice(c * L, (c + 1) * L)
    cols = lambda hd: slice(hd * RET_DV, (hd + 1) * RET_DV)
    scores, chunk_kv = {}, {}
    for c, hd in units:
        scores[c, hd] = _dot_nt(q_heads[hd][rows(c)], k_heads[hd][rows(c)])
    for c, hd in units:
        v_dec = (v[rows(c), cols(hd)].astype(_F32) * kend_ref[hd]).astype(_BF16)
        chunk_kv[c, hd] = _dot_tn(k_heads[hd][rows(c)], v_dec)
    prev_state = {}
    for hd in range(N_RET_HEADS):
        state = state_ref[hd]
        for c in range(n_chunks):
            prev_state[c, hd] = state.astype(_BF16)
            state = decay_ref[hd] * state + chunk_kv[c, hd]
        state_ref[hd] = state
    inner, cross = {}, {}
    for c, hd in units:
        inner[c, hd] = _dot((scores[c, hd] * intra_ref[hd]).astype(_BF16), v[rows(c), cols(hd)])
    for c, hd in units:
        cross[c, hd] = _dot(q_heads[hd][rows(c)], prev_state[c, hd])
    for c, hd in units:
        out = inner[c, hd] + cross[c, hd] * qstart_ref[hd]
        mu = jnp.mean(out, axis=-1, keepdims=True)
        cen = out - mu
        var = jnp.mean(cen * cen, axis=-1, keepdims=True)
        normed = cen * lax.rsqrt(var + EPS)
        gate = g[rows(c), cols(hd)].astype(_F32)
        o_ref[0, rows(c), cols(hd)] = (normed * (gate * jax.nn.sigmoid(gate))).astype(o_ref.dtype)


def _permute_small_weights(w_in_t, w_uq, w_ukv):
    o_kr = Q_LORA + KV_LORA
    d = w_in_t.shape[1]
    wlat_t = jnp.concatenate([w_in_t[:o_kr], jnp.zeros((QK_NOPE, d), w_in_t.dtype), w_in_t[o_kr:o_kr + QK_ROPE],
                              jnp.zeros((HEAD_PAD - QK_NOPE - QK_ROPE, d), w_in_t.dtype)], axis=0).astype(_BF16)
    wq = w_uq.reshape(Q_LORA, N_MLA_HEADS, QK_NOPE + QK_ROPE)
    nope, r1, r2 = wq[..., :QK_NOPE], wq[..., QK_NOPE:QK_NOPE + ROPE_HALF], wq[..., QK_NOPE + ROPE_HALF:]
    z32 = jnp.zeros((Q_LORA, N_MLA_HEADS, HEAD_PAD - QK_NOPE - QK_ROPE), w_uq.dtype)
    wuq_p = jnp.concatenate([nope, r1, r2, z32], axis=-1).reshape(Q_LORA, MLA_QK_WIDTH).astype(_BF16)

    wkv = w_ukv.reshape(KV_LORA, N_MLA_HEADS, QK_NOPE + V_HEAD)
    zk = jnp.zeros((KV_LORA, N_MLA_HEADS, HEAD_PAD - QK_NOPE), w_ukv.dtype)
    wk = jnp.concatenate([wkv[..., :QK_NOPE], zk], axis=-1).reshape(KV_LORA, MLA_QK_WIDTH)
    wv_t = wkv[..., QK_NOPE:].reshape(KV_LORA, MLA_WIDTH).T
    return wlat_t, wuq_p, wk.astype(_BF16), wv_t.astype(_BF16)


def _in_proj(x, g_mix, wlat_t, wret_t, g_q_lat, wuq_p, g_kv_lat, wuk_p, wuvt_p, cos_t, sin_t, later_weights):
    b, s, d = x.shape
    nt = s // TM_IN
    per = TM_IN // TQ_MLA
    tok = lambda w: pl.BlockSpec((1, TM_IN, w), lambda bi, i: (bi, i, 0))
    tab = pl.BlockSpec((TM_IN // 2, LANES), lambda bi, i: (bi * nt + i, 0))
    bf = lambda w: jax.ShapeDtypeStruct((b, s, w), _BF16)
    ret_tables = _retention_tables()
    cast_in, cast_out, cast_shapes = _cast_plumbing(later_weights, (b, nt))
    in_specs = ([tok(d), _const_spec((1, d)), _const_spec(wlat_t.shape)] + [_const_spec(w.shape) for w in wret_t]
                + [_const_spec((1, Q_LORA)), _const_spec(wuq_p.shape), _const_spec((1, KV_LORA)),
                   _const_spec(wuk_p.shape), _const_spec(wuvt_p.shape), tab, tab]
                + [_const_spec(t.shape) for t in ret_tables])
    out_specs = (tok(MLA_QK_WIDTH), tok(MLA_QK_WIDTH),
                 pl.BlockSpec((1, per, MLA_WIDTH, TQ_MLA), lambda bi, i: (bi, i, 0, 0)), tok(RET_WIDTH))
    return pl.pallas_call(
        _with_casts(_in_proj_kernel, len(in_specs), len(out_specs), len(later_weights)),
        out_shape=(bf(MLA_QK_WIDTH), bf(MLA_QK_WIDTH),
                   jax.ShapeDtypeStruct((b, s // TQ_MLA, MLA_WIDTH, TQ_MLA), _BF16), bf(RET_WIDTH), *cast_shapes),
        grid=(b, nt),
        in_specs=in_specs + cast_in,
        out_specs=(*out_specs, *cast_out),
        scratch_shapes=[pltpu.VMEM((N_RET_HEADS, RET_DK, RET_DV), _F32)],
        compiler_params=_params(("arbitrary", "arbitrary")),
        name="in_proj",
    )(x, g_mix.reshape(1, d), wlat_t, *wret_t, g_q_lat.reshape(1, Q_LORA), wuq_p, g_kv_lat.reshape(1, KV_LORA),
      wuk_p, wuvt_p, cos_t, sin_t, *ret_tables, *later_weights)


_MASK_VALUE = -0.7 * float(jnp.finfo(jnp.float32).max)
_MLA_AHEAD = 4
_MLA_TILES_PER_ITER = 5
_MLA_DENOM_ROWS = 16


def _mla_attn_kernel(qlo_ref, qhi_ref, k_ref, vt_ref, o_ref, q_ref, s_ref, m_ref, acc_ref):
    tq = TQ_MLA
    n_tiles = k_ref.shape[1] // tq
    lo = pl.program_id(1)
    hi = n_tiles - 1 - lo
    q_ref[0] = qlo_ref[0]
    q_ref[1] = qhi_ref[0]
    m_ref[...] = jnp.full(m_ref.shape, _MASK_VALUE, _F32)
    acc_ref[...] = jnp.zeros(acc_ref.shape, _F32)
    ones_rows = jnp.ones((_MLA_DENOM_ROWS, tq), _BF16)

    def step_args(t):
        sel = (t > lo).astype(jnp.int32)
        return sel, t - 1 - sel * lo

    def scores(sel, kk, hd):
        sl = slice(hd * HEAD_PAD, (hd + 1) * HEAD_PAD)
        ks = pl.multiple_of(kk * tq, tq)
        s_ref[hd] = _dot_nt(k_ref[0, pl.ds(ks, tq), sl], q_ref[sel, :, sl])

    def softmax(sel, hd, keep):
        row = slice(hd, hd + 1)

        def st():
            s = s_ref[hd]
            return s if keep is None else jnp.where(keep, s, _MASK_VALUE)

        m_prev = m_ref[sel, row, :]
        m_next = jnp.maximum(m_prev, jnp.max(st(), axis=0, keepdims=True))
        m_ref[sel, row, :] = m_next
        return jnp.exp2(m_prev - m_next), jnp.exp2(st() - m_next).astype(_BF16)

    def accumulate(sel, kk, hd, alpha, p):
        vt = jnp.concatenate([vt_ref[0, kk, hd * V_HEAD:(hd + 1) * V_HEAD, :], ones_rows], axis=0)
        acc_ref[sel, hd] = acc_ref[sel, hd] * alpha + _dot(vt, p)

    def tile(sel, kk, diagonal, nxt):
        keep = None
        if diagonal:
            key = lax.broadcasted_iota(jnp.int32, (tq, tq), 0)
            qry = lax.broadcasted_iota(jnp.int32, (tq, tq), 1)
            keep = key <= qry
        pending = None
        for hd in range(N_MLA_HEADS):
            ahead = hd + _MLA_AHEAD
            if ahead < N_MLA_HEADS:
                scores(sel, kk, ahead)
            elif nxt is not None:
                scores(nxt[0], nxt[1], ahead - N_MLA_HEADS)
            current = softmax(sel, hd, keep)
            if pending is not None:
                accumulate(sel, kk, hd - 1, *pending)
            pending = current
        accumulate(sel, kk, N_MLA_HEADS - 1, *pending)

    for hd in range(_MLA_AHEAD):
        scores(0, lo, hd)
    tile(0, lo, True, step_args(1))

    def body(it, carry):
        for u in range(_MLA_TILES_PER_ITER):
            t = 1 + it * _MLA_TILES_PER_ITER + u
            last = t + 1 == n_tiles
            sel_n, kk_n = step_args(t + 1)
            tile(*step_args(t), False, (jnp.where(last, 1, sel_n), jnp.where(last, hi, kk_n)))
        return carry

    lax.fori_loop(0, (n_tiles - 1) // _MLA_TILES_PER_ITER, body, 0)
    tile(1, hi, True, None)

    for sel, qt in ((0, lo), (1, hi)):
        out_t = jnp.concatenate([acc_ref[sel, hd, :V_HEAD, :] / acc_ref[sel, hd, V_HEAD:V_HEAD + 1, :]
                                 for hd in range(N_MLA_HEADS)], axis=0)
        o_ref[0, pl.ds(pl.multiple_of(qt * tq, tq), tq), :] = out_t.T.astype(o_ref.dtype)


def _mla_attn(q, k, vt, later_weights):
    b, s, w = q.shape
    nk = s // TQ_MLA
    assert nk % 2 == 0 and (nk - 1) % _MLA_TILES_PER_ITER == 0
    cast_in, cast_out, cast_shapes = _cast_plumbing(later_weights, (b, nk // 2))
    in_specs = [pl.BlockSpec((1, TQ_MLA, w), lambda bi, i: (bi, i, 0)),
                pl.BlockSpec((1, TQ_MLA, w), lambda bi, i: (bi, nk - 1 - i, 0)),
                pl.BlockSpec((1, s, w), lambda bi, i: (bi, 0, 0)),
                pl.BlockSpec((1, nk, MLA_WIDTH, TQ_MLA), lambda bi, i: (bi, 0, 0, 0))]
    return pl.pallas_call(
        _with_casts(_mla_attn_kernel, len(in_specs), 1, len(later_weights)),
        out_shape=(jax.ShapeDtypeStruct((b, s, MLA_WIDTH), _BF16), *cast_shapes),
        grid=(b, nk // 2),
        in_specs=in_specs + cast_in,
        out_specs=(pl.BlockSpec((1, s, MLA_WIDTH), lambda bi, i: (bi, 0, 0)), *cast_out),
        scratch_shapes=[pltpu.VMEM((2, TQ_MLA, w), _BF16),
                        pltpu.VMEM((N_MLA_HEADS, TQ_MLA, TQ_MLA), _F32),
                        pltpu.VMEM((2, N_MLA_HEADS, TQ_MLA), _F32),
                        pltpu.VMEM((2, N_MLA_HEADS, V_HEAD + _MLA_DENOM_ROWS, TQ_MLA), _F32)],
        compiler_params=_params(("arbitrary", "arbitrary")),
        name="mla_attn",
    )(q, q, k, vt, *later_weights)


def _retention_tables():
    h, L = N_RET_HEADS, RET_CHUNK
    log_gamma = jnp.log(1.0 - 2.0 ** (-5.0 - jnp.arange(h, dtype=_F32)))
    j = jnp.arange(L, dtype=_F32)
    diff = j[:, None] - j[None, :]
    intra = jnp.where(diff[None] >= 0,
                      jnp.exp(jnp.maximum(diff, 0.0)[None] * log_gamma[:, None, None]), 0.0)
    rowb = lambda t: jnp.broadcast_to(t.T[:, :, None], (h, L, LANES))
    k_to_end = jnp.exp((L - 1 - j)[:, None] * log_gamma[None, :])
    q_from_start = jnp.exp((j + 1)[:, None] * log_gamma[None, :])
    chunk_decay = jnp.broadcast_to(jnp.exp(L * log_gamma)[:, None, None], (h, RET_DK, RET_DV))
    return intra, rowb(k_to_end), rowb(q_from_start), chunk_decay


def _mix_xattn_kernel(x_ref, ymla_ref, yret_ref, wout_ref, gx_ref, wqk_ref, wvo_ref, o_ref):
    half = TM_MIX // 2
    rows = [slice(0, half), slice(half, TM_MIX)]
    mem_len = wqk_ref.shape[2] // N_XATTN_HEADS

    def out_proj(r):
        return (x_ref[0, r, :] + _dot(ymla_ref[0, r, :], wout_ref[:MLA_WIDTH, :])
                + _dot(yret_ref[0, r, :], wout_ref[MLA_WIDTH:, :]))

    def scores(x1):
        h = _rms(x1, gx_ref[...]).astype(_BF16)
        return _dot(h, wqk_ref[0])

    def softmax(s):
        heads = []
        for hd in range(N_XATTN_HEADS):
            blk = s[:, hd * mem_len:(hd + 1) * mem_len]
            p = jnp.exp(blk - jnp.max(blk, axis=-1, keepdims=True))
            heads.append((p / jnp.sum(p, axis=-1, keepdims=True)).astype(_BF16))
        return jnp.concatenate(heads, axis=1)

    x1 = [out_proj(r) for r in rows]
    s0 = scores(x1[0])
    s1 = scores(x1[1])
    p0 = softmax(s0)
    o_ref[0, rows[0], :] = x1[0] + _dot(p0, wvo_ref[0])
    p1 = softmax(s1)
    o_ref[0, rows[1], :] = x1[1] + _dot(p1, wvo_ref[0])


def _mix_xattn(x, y_mla, y_ret, w_out, g_xattn, wqk, wvo, later_weights):
    b, s, d = x.shape
    tok = lambda w: pl.BlockSpec((1, TM_MIX, w), lambda bi, i: (bi, i, 0))
    per_batch = lambda a: pl.BlockSpec((1,) + a.shape[1:], lambda bi, i: (bi, 0, 0))
    cast_in, cast_out, cast_shapes = _cast_plumbing(later_weights, (b, s // TM_MIX))
    in_specs = [tok(d), tok(MLA_WIDTH), tok(RET_WIDTH), _const_spec(w_out.shape), _const_spec((1, d)),
                per_batch(wqk), per_batch(wvo)]
    return pl.pallas_call(
        _with_casts(_mix_xattn_kernel, len(in_specs), 1, len(later_weights)),
        out_shape=(jax.ShapeDtypeStruct((b, s, d), _F32), *cast_shapes),
        grid=(b, s // TM_MIX),
        in_specs=in_specs + cast_in,
        out_specs=(tok(d), *cast_out),
        compiler_params=_params(("arbitrary", "arbitrary")),
        name="mix_xattn",
    )(x, y_mla, y_ret, w_out, g_xattn.reshape(1, d), wqk, wvo, *later_weights)


def _conv_ffn_kernel(x_ref, g_ref, win_ref, cw_ref, cb_ref, wout_ref, gfin_ref, o_ref, gate_ref):
    halo = SUBLANES
    half = TM_FFN // 2

    @pl.when(pl.program_id(1) == 0)
    def _():
        gate_ref[:halo, :] = jnp.zeros((halo, D_FF), _F32)

    rows = [slice(0, half), slice(half, TM_FFN)]
    xs = [x_ref[0, r, :] for r in rows]
    hs = [_rms(x, g_ref[...]).astype(_BF16) for x in xs]

    def project(i):
        gate = _dot(hs[i], win_ref[:, :D_FF])
        up = _dot(hs[i], win_ref[:, D_FF:])
        gate_ref[halo + i * half:halo + (i + 1) * half, :] = gate
        return gate, up

    def activate(i, gate, up):
        conv = cb_ref[...] + gate * cw_ref[CONV_W - 1:CONV_W, :]
        for tap in range(CONV_W - 1):
            back = CONV_W - 1 - tap
            lo = halo + i * half - back
            conv = conv + gate_ref[lo:lo + half, :] * cw_ref[tap:tap + 1, :]
        return (conv * jax.nn.sigmoid(conv) * up).astype(_BF16)

    def finish(i, act):
        x3 = xs[i] + _dot(act, wout_ref[...])
        o_ref[0, rows[i], :] = _rms(x3, gfin_ref[...])

    g0, u0 = project(0)
    g1, u1 = project(1)
    a0 = activate(0, g0, u0)
    finish(0, a0)
    a1 = activate(1, g1, u1)
    finish(1, a1)
    gate_ref[:halo, :] = gate_ref[TM_FFN:, :]


def _conv_ffn(x, g_ffn, w_ffn_in, conv_w, conv_b, w_ffn_out, g_final):
    b, s, d = x.shape
    tok = pl.BlockSpec((1, TM_FFN, d), lambda bi, i: (bi, i, 0))
    return pl.pallas_call(
        _conv_ffn_kernel,
        out_shape=jax.ShapeDtypeStruct((b, s, d), _F32),
        grid=(b, s // TM_FFN),
        in_specs=[tok, _const_spec((1, d)), _const_spec(w_ffn_in.shape), _const_spec(conv_w.shape),
                  _const_spec((1, D_FF)), _const_spec(w_ffn_out.shape), _const_spec((1, d))],
        out_specs=tok,
        scratch_shapes=[pltpu.VMEM((TM_FFN + SUBLANES, D_FF), _F32)],
        compiler_params=_params(("arbitrary", "arbitrary")),
        name="conv_ffn",
    )(x, g_ffn.reshape(1, d), w_ffn_in, conv_w, conv_b.reshape(1, D_FF), w_ffn_out, g_final.reshape(1, d))


def kernel(x, mem, positions, g_mix, w_in, g_q_lat, w_uq, g_kv_lat, w_ukv, w_out, g_xattn, g_mem, w_xq,
           w_xkv, w_xo, g_ffn, w_ffn_in, conv_w, conv_b, w_ffn_out, g_final):
    assert w_in.shape[0] == 1, "one layer supported"
    l = 0
    w_in_t = w_in[l].T
    cos_t, sin_t, *wret_t, wqk, wvo = _prep(positions, w_in_t, mem, g_mem[l], w_xkv[l], w_xq[l], w_xo[l])
    wlat_t, wuq_p, wuk_p, wuvt_p = _permute_small_weights(w_in_t, w_uq[l], w_ukv[l])
    q, k, vt, y_ret, w_out_b = _in_proj(
        x, g_mix[l], wlat_t, wret_t, g_q_lat[l], wuq_p, g_kv_lat[l], wuk_p, wuvt_p, cos_t, sin_t, (w_out[l],))
    y_mla, w_ffn_in_b = _mla_attn(q, k, vt, (w_ffn_in[l],))
    x, w_ffn_out_b = _mix_xattn(x, y_mla, y_ret, w_out_b, g_xattn[l], wqk, wvo, (w_ffn_out[l],))
    return _conv_ffn(x, g_ffn[l], w_ffn_in_b, conv_w[l], conv_b[l], w_ffn_out_b, g_final)
```

```python
import functools
import math

import jax
import jax.numpy as jnp
from jax import lax
from jax.experimental import pallas as pl
from jax.experimental.pallas import tpu as pltpu

D_MODEL = 1024
EPS = 1e-6
ROPE_BASE = 10000.0
N_MLA_HEADS = 8
QK_NOPE = 64
QK_ROPE = 32
V_HEAD = 64
Q_LORA = 256
KV_LORA = 128
N_RET_HEADS = 4
RET_DK = 128
RET_DV = 128
RET_CHUNK = 128
MLA_WIDTH = N_MLA_HEADS * V_HEAD
RET_WIDTH = N_RET_HEADS * RET_DV
N_XATTN_HEADS = 4
XATTN_HEAD = D_MODEL // N_XATTN_HEADS
D_FF = 2816
CONV_W = 3

LANES = 128
SUBLANES = 8
_BF16_ROWS = 16
VMEM_LIMIT = 56 * 1024 * 1024

HEAD_PAD = LANES
ROPE_HALF = QK_ROPE // 2
MLA_QK_WIDTH = N_MLA_HEADS * HEAD_PAD
IN_LAT = Q_LORA + KV_LORA + HEAD_PAD

TM_TABLE = 1024
TM_IN = 1024
TQ_MLA = 256
TM_MIX = 1024
TM_FFN = 512

_BF16 = jnp.bfloat16
_F32 = jnp.float32


def _dot(a, b):
    return jnp.dot(a, b, preferred_element_type=_F32)


def _dot_nt(a, b):
    return lax.dot_general(a, b, (((1,), (1,)), ((), ())), preferred_element_type=_F32)


def _dot_tn(a, b):
    return lax.dot_general(a, b, (((0,), (0,)), ((), ())), preferred_element_type=_F32)


def _rms(x, g):
    inv = lax.rsqrt(jnp.mean(x * x, axis=-1, keepdims=True) + EPS)
    return x * inv * g


def _const_spec(shape):
    nd = len(shape)
    return pl.BlockSpec(shape, lambda *_: (0,) * nd, pipeline_mode=pl.Buffered(1))


def _params(sem):
    return pltpu.CompilerParams(dimension_semantics=sem, vmem_limit_bytes=VMEM_LIMIT)


def _with_casts(body, n_in, n_out, n_cast):
    def kern(*refs):
        ins, rest = refs[:n_in], refs[n_in:]
        cast_in, rest = rest[:n_cast], rest[n_cast:]
        outs, rest = rest[:n_out], rest[n_out:]
        cast_out, scratch = rest[:n_cast], rest[n_cast:]
        for src, dst in zip(cast_in, cast_out):
            dst[...] = src[...].astype(dst.dtype)
        body(*ins, *outs, *scratch)
    return kern


def _cast_plumbing(weights, grid):
    steps = math.prod(grid)
    if len(grid) == 1:
        block_of = lambda i: (i, 0)
    else:
        block_of = lambda bi, i: (bi * grid[1] + i, 0)
    in_specs, out_specs, out_shapes = [], [], []
    for w in weights:
        rows, cols = w.shape
        assert rows % steps == 0 and (rows // steps) % _BF16_ROWS == 0, (w.shape, steps)
        spec = pl.BlockSpec((rows // steps, cols), block_of)
        in_specs.append(spec)
        out_specs.append(spec)
        out_shapes.append(jax.ShapeDtypeStruct(w.shape, _BF16))
    return in_specs, out_specs, out_shapes


def _prep_kernel(pos_ref, inv_ref, wlat_ref, wq_ref, wk_ref, wv_ref, wg_ref, mem_ref, gmem_ref, wxk_ref, wxv_ref,
                 wxq_ref, wxo_ref, cos_ref, sin_ref, wlat_out, wq_out, wk_out, wv_out, wg_out, wqk_ref, wvo_ref,
                 *, lat_blocks):
    half_rows = TM_IN // 2 // LANES
    lane = lax.broadcasted_iota(jnp.int32, (LANES, LANES), 1)
    column = lambda r: jnp.broadcast_to(pos_ref[r:r + 1, :], (LANES, LANES)).T
    for tile in range(TM_TABLE // TM_IN):
        for rr in range(half_rows):
            r0 = tile * 2 * half_rows + rr
            pos = jnp.where(lane < RET_DK // 2, column(r0), column(r0 + half_rows))
            ang = pos * inv_ref[...]
            out = slice((tile * half_rows + rr) * LANES, (tile * half_rows + rr + 1) * LANES)
            cos_ref[out, :] = jnp.cos(ang)
            sin_ref[out, :] = jnp.sin(ang)
    for src, dst in ((wq_ref, wq_out), (wk_ref, wk_out), (wv_ref, wv_out), (wg_ref, wg_out)):
        dst[...] = src[...].astype(_BF16)
    wlat_out[...] = jnp.where(pl.program_id(0) < lat_blocks, wlat_ref[...], 0.0).astype(_BF16)

    mem_n = _rms(mem_ref[0], gmem_ref[...]).astype(_BF16)
    k_h = _dot(mem_n, wxk_ref[...].astype(_BF16)).astype(_BF16)
    v_h = _dot(mem_n, wxv_ref[...].astype(_BF16)).astype(_BF16)
    wqk_ref[0] = (_dot_nt(wxq_ref[...].astype(_BF16), k_h) * (1.0 / math.sqrt(XATTN_HEAD))).astype(_BF16)
    wvo_ref[0] = _dot(v_h, wxo_ref[...].astype(_BF16)).astype(_BF16)


def _prep(positions, w_in_t, mem, g_mem, w_xkv, w_xq, w_xo):
    t = positions.size
    steps = t // TM_TABLE
    n_in, d = w_in_t.shape
    b, m, _ = mem.shape
    assert steps == b * N_XATTN_HEADS
    row0 = Q_LORA + KV_LORA + QK_ROPE
    rows = RET_WIDTH // steps
    assert RET_WIDTH % steps == 0 and rows % _BF16_ROWS == 0 and row0 % rows == 0
    assert n_in == row0 + 4 * RET_WIDTH
    pos = positions.astype(_F32).reshape(t // LANES, LANES)
    f_ret = 1.0 / (ROPE_BASE ** (jnp.arange(0, RET_DK, 2, dtype=_F32) / RET_DK))
    inv = jnp.concatenate([f_ret, f_ret])
    group = lambda j: pl.BlockSpec((rows, d), lambda i: (row0 // rows + j * steps + i, 0))
    lat_blocks = row0 // rows
    kr_dst = (Q_LORA + KV_LORA + QK_NOPE) // rows
    assert QK_ROPE == rows and IN_LAT // rows == steps and (Q_LORA + KV_LORA) % rows == 0
    lat_src = pl.BlockSpec((rows, d), lambda i: (jnp.minimum(i, lat_blocks - 1), 0))
    lat_dst = pl.BlockSpec((rows, d), lambda i: (jnp.where(i < lat_blocks - 1, i, jnp.where(
        i == lat_blocks - 1, kr_dst, jnp.where(i <= kr_dst, i - 1, i))), 0))
    table = pl.BlockSpec((TM_TABLE // 2, LANES), lambda i: (i, 0))
    w_out = pl.BlockSpec((rows, d), lambda i: (i, 0))
    nh = N_XATTN_HEADS
    head_cols = lambda off: pl.BlockSpec((d, XATTN_HEAD), lambda i: (0, off + i % nh))
    return pl.pallas_call(
        functools.partial(_prep_kernel, lat_blocks=lat_blocks),
        out_shape=(jax.ShapeDtypeStruct((t // 2, LANES), _F32),) * 2
                  + (jax.ShapeDtypeStruct((IN_LAT, d), _BF16),)
                  + (jax.ShapeDtypeStruct((RET_WIDTH, d), _BF16),) * 4
                  + (jax.ShapeDtypeStruct((b, d, nh * m), _BF16), jax.ShapeDtypeStruct((b, nh * m, d), _BF16)),
        grid=(steps,),
        in_specs=[pl.BlockSpec((TM_TABLE // LANES, LANES), lambda i: (i, 0)),
                  pl.BlockSpec((1, LANES), lambda i: (0, 0)), lat_src, group(0), group(1), group(2), group(3),
                  pl.BlockSpec((1, m, d), lambda i: (i // nh, 0, 0)), _const_spec((1, d)),
                  head_cols(0), head_cols(nh), head_cols(0),
                  pl.BlockSpec((XATTN_HEAD, d), lambda i: (i % nh, 0))],
        out_specs=(table, table, lat_dst, w_out, w_out, w_out, w_out,
                   pl.BlockSpec((1, d, m), lambda i: (i // nh, 0, i % nh)),
                   pl.BlockSpec((1, m, d), lambda i: (i // nh, i % nh, 0))),
        compiler_params=_params(("arbitrary",)),
        name="prep",
    )(pos, inv.reshape(1, LANES), w_in_t, w_in_t, w_in_t, w_in_t, w_in_t, mem, g_mem.reshape(1, d), w_xkv, w_xkv,
      w_xq, w_xo)


def _in_proj_kernel(x_ref, gmix_ref, wlat_ref, wrq_ref, wrk_ref, wrv_ref, wrg_ref, gq_ref, wuq_ref, gkv_ref, wuk_ref, wuvt_ref, cos_ref, sin_ref,
                    intra_ref, kend_ref, qstart_ref, decay_ref,
                    q_ref, k_ref, vt_ref, yret_ref, state_ref):
    h = _rms(x_ref[0], gmix_ref[...]).astype(_BF16)
    lane = lax.broadcasted_iota(jnp.int32, (TM_IN, LANES), 1)
    low = lane < RET_DK // 2

    lo = lax.broadcasted_iota(jnp.int32, (TM_IN // 2, LANES), 1) < RET_DK // 2

    def unpack(t):
        t_roll = pltpu.roll(t, RET_DK // 2, axis=1)
        return jnp.concatenate([jnp.where(lo, t, t_roll), jnp.where(lo, t_roll, t)], axis=0)

    cos_r, sin_full = unpack(cos_ref[...]), unpack(sin_ref[...])
    sin_r = jnp.where(low, -sin_full, sin_full)
    ratio = (RET_DK // 2) // ROPE_HALF
    rope_lane = (lane >= QK_NOPE) & (lane < QK_NOPE + QK_ROPE)
    src = jnp.where(lane < QK_NOPE + ROPE_HALF, lane - QK_NOPE, lane - QK_NOPE - ROPE_HALF) * ratio
    src = jnp.where(rope_lane, src, 0)
    g_cos = jnp.take_along_axis(cos_r, src, axis=1)
    g_sin = jnp.take_along_axis(sin_full, src, axis=1)
    cos_m = jnp.where(rope_lane, g_cos, 1.0)
    sin_m = jnp.where(rope_lane, jnp.where(lane < QK_NOPE + ROPE_HALF, -g_sin, g_sin), 0.0)

    first_half = lane < QK_NOPE + ROPE_HALF

    def swap_halves(t):
        return jnp.where(first_half, pltpu.roll(t, LANES - ROPE_HALF, axis=1), pltpu.roll(t, ROPE_HALF, axis=1))

    lat = _dot_nt(h, wlat_ref[...])
    c_q = lat[:, :Q_LORA]
    c_kv = lat[:, Q_LORA:Q_LORA + KV_LORA]
    kr = lat[:, Q_LORA + KV_LORA:IN_LAT]
    k_rope = kr * cos_m + swap_halves(kr) * sin_m

    cqn = _rms(c_q, gq_ref[...]).astype(_BF16)
    q = _dot(cqn, wuq_ref[...])
    scale = math.log2(math.e) / math.sqrt(QK_NOPE + QK_ROPE)
    for hd in range(N_MLA_HEADS):
        sl = slice(hd * HEAD_PAD, (hd + 1) * HEAD_PAD)
        q_ref[0, :, sl] = ((q[:, sl] * cos_m + swap_halves(q[:, sl]) * sin_m) * scale).astype(_BF16)

    ckvn = _rms(c_kv, gkv_ref[...]).astype(_BF16)
    k_nope = _dot(ckvn, wuk_ref[...])
    for hd in range(N_MLA_HEADS):
        sl = slice(hd * HEAD_PAD, (hd + 1) * HEAD_PAD)
        k_ref[0, :, sl] = (k_nope[:, sl] + k_rope).astype(_BF16)
    v_t = _dot_nt(wuvt_ref[...], ckvn).astype(_BF16)
    for j in range(TM_IN // TQ_MLA):
        vt_ref[0, j] = v_t[:, j * TQ_MLA:(j + 1) * TQ_MLA]

    def ret_rope(w_ref, mult):
        r = _dot_nt(h, w_ref[...])
        heads = []
        for hd in range(N_RET_HEADS):
            rh = r[:, hd * RET_DK:(hd + 1) * RET_DK]
            roped = rh * cos_r + pltpu.roll(rh, RET_DK // 2, axis=1) * sin_r
            if mult is not None:
                roped = roped * mult
            heads.append(roped.astype(_BF16))
        return heads

    rq = ret_rope(wrq_ref, None)
    rk = ret_rope(wrk_ref, RET_DK ** -0.5)
    rv = _dot_nt(h, wrv_ref[...]).astype(_BF16)
    rg = _dot_nt(h, wrg_ref[...]).astype(_BF16)
    _retention_tile(rq, rk, rv, rg, intra_ref, kend_ref, qstart_ref, decay_ref, state_ref, yret_ref)


def _retention_tile(q_heads, k_heads, v, g, intra_ref, kend_ref, qstart_ref, decay_ref, state_ref, o_ref):
    @pl.when(pl.program_id(1) == 0)
    def _():
        state_ref[...] = jnp.zeros(state_ref.shape, _F32)

    L = RET_CHUNK
    n_chunks = v.shape[0] // L
    units = [(c, hd) for c in range(n_chunks) for hd in range(N_RET_HEADS)]
    rows = lambda c: slice(c * L, (c + 1) * L)
    cols = lambda hd: slice(hd * RET_DV, (hd + 1) * RET_DV)
    scores, chunk_kv = {}, {}
    for c, hd in units:
        scores[c, hd] = _dot_nt(q_heads[hd][rows(c)], k_heads[hd][rows(c)])
    for c, hd in units:
        v_dec = (v[rows(c), cols(hd)].astype(_F32) * kend_ref[hd]).astype(_BF16)
        chunk_kv[c, hd] = _dot_tn(k_heads[hd][rows(c)], v_dec)
    prev_state = {}
    for hd in range(N_RET_HEADS):
        state = state_ref[hd]
        for c in range(n_chunks):
            prev_state[c, hd] = state.astype(_BF16)
            state = decay_ref[hd] * state + chunk_kv[c, hd]
        state_ref[hd] = state
    inner, cross = {}, {}
    for c, hd in units:
        inner[c, hd] = _dot((scores[c, hd] * intra_ref[hd]).astype(_BF16), v[rows(c), cols(hd)])
    for c, hd in units:
        cross[c, hd] = _dot(q_heads[hd][rows(c)], prev_state[c, hd])
    for c, hd in units:
        out = inner[c, hd] + cross[c, hd] * qstart_ref[hd]
        mu = jnp.mean(out, axis=-1, keepdims=True)
        cen = out - mu
        var = jnp.mean(cen * cen, axis=-1, keepdims=True)
        normed = cen * lax.rsqrt(var + EPS)
        gate = g[rows(c), cols(hd)].astype(_F32)
        o_ref[0, rows(c), cols(hd)] = (normed * (gate * jax.nn.sigmoid(gate))).astype(o_ref.dtype)


def _permute_up_weights(w_uq, w_ukv):
    wq = w_uq.reshape(Q_LORA, N_MLA_HEADS, QK_NOPE + QK_ROPE)
    nope, r1, r2 = wq[..., :QK_NOPE], wq[..., QK_NOPE:QK_NOPE + ROPE_HALF], wq[..., QK_NOPE + ROPE_HALF:]
    z32 = jnp.zeros((Q_LORA, N_MLA_HEADS, HEAD_PAD - QK_NOPE - QK_ROPE), w_uq.dtype)
    wuq_p = jnp.concatenate([nope, r1, r2, z32], axis=-1).reshape(Q_LORA, MLA_QK_WIDTH).astype(_BF16)

    wkv = w_ukv.reshape(KV_LORA, N_MLA_HEADS, QK_NOPE + V_HEAD)
    zk = jnp.zeros((KV_LORA, N_MLA_HEADS, HEAD_PAD - QK_NOPE), w_ukv.dtype)
    wk = jnp.concatenate([wkv[..., :QK_NOPE], zk], axis=-1).reshape(KV_LORA, MLA_QK_WIDTH)
    wv_t = wkv[..., QK_NOPE:].reshape(KV_LORA, MLA_WIDTH).T
    return wuq_p, wk.astype(_BF16), wv_t.astype(_BF16)


def _in_proj(x, g_mix, wlat_t, wret_t, g_q_lat, wuq_p, g_kv_lat, wuk_p, wuvt_p, cos_t, sin_t, later_weights):
    b, s, d = x.shape
    nt = s // TM_IN
    per = TM_IN // TQ_MLA
    tok = lambda w: pl.BlockSpec((1, TM_IN, w), lambda bi, i: (bi, i, 0))
    tab = pl.BlockSpec((TM_IN // 2, LANES), lambda bi, i: (bi * nt + i, 0))
    bf = lambda w: jax.ShapeDtypeStruct((b, s, w), _BF16)
    ret_tables = _retention_tables()
    cast_in, cast_out, cast_shapes = _cast_plumbing(later_weights, (b, nt))
    in_specs = ([tok(d), _const_spec((1, d)), _const_spec(wlat_t.shape)] + [_const_spec(w.shape) for w in wret_t]
                + [_const_spec((1, Q_LORA)), _const_spec(wuq_p.shape), _const_spec((1, KV_LORA)),
                   _const_spec(wuk_p.shape), _const_spec(wuvt_p.shape), tab, tab]
                + [_const_spec(t.shape) for t in ret_tables])
    out_specs = (tok(MLA_QK_WIDTH), tok(MLA_QK_WIDTH),
                 pl.BlockSpec((1, per, MLA_WIDTH, TQ_MLA), lambda bi, i: (bi, i, 0, 0)), tok(RET_WIDTH))
    return pl.pallas_call(
        _with_casts(_in_proj_kernel, len(in_specs), len(out_specs), len(later_weights)),
        out_shape=(bf(MLA_QK_WIDTH), bf(MLA_QK_WIDTH),
                   jax.ShapeDtypeStruct((b, s // TQ_MLA, MLA_WIDTH, TQ_MLA), _BF16), bf(RET_WIDTH), *cast_shapes),
        grid=(b, nt),
        in_specs=in_specs + cast_in,
        out_specs=(*out_specs, *cast_out),
        scratch_shapes=[pltpu.VMEM((N_RET_HEADS, RET_DK, RET_DV), _F32)],
        compiler_params=_params(("arbitrary", "arbitrary")),
        name="in_proj",
    )(x, g_mix.reshape(1, d), wlat_t, *wret_t, g_q_lat.reshape(1, Q_LORA), wuq_p, g_kv_lat.reshape(1, KV_LORA),
      wuk_p, wuvt_p, cos_t, sin_t, *ret_tables, *later_weights)


_MASK_VALUE = -0.7 * float(jnp.finfo(jnp.float32).max)
_MLA_AHEAD = 4
_MLA_TILES_PER_ITER = 5
_MLA_DENOM_ROWS = 16


def _mla_attn_kernel(qlo_ref, qhi_ref, k_ref, vt_ref, o_ref, q_ref, s_ref, m_ref, acc_ref):
    tq = TQ_MLA
    n_tiles = k_ref.shape[1] // tq
    lo = pl.program_id(1)
    hi = n_tiles - 1 - lo
    q_ref[0] = qlo_ref[0]
    q_ref[1] = qhi_ref[0]
    m_ref[...] = jnp.full(m_ref.shape, _MASK_VALUE, _F32)
    acc_ref[...] = jnp.zeros(acc_ref.shape, _F32)
    ones_rows = jnp.ones((_MLA_DENOM_ROWS, tq), _BF16)

    def step_args(t):
        sel = (t > lo).astype(jnp.int32)
        return sel, t - 1 - sel * lo

    def scores(sel, kk, hd):
        sl = slice(hd * HEAD_PAD, (hd + 1) * HEAD_PAD)
        ks = pl.multiple_of(kk * tq, tq)
        s_ref[hd] = _dot_nt(k_ref[0, pl.ds(ks, tq), sl], q_ref[sel, :, sl])

    def softmax(sel, hd, keep):
        row = slice(hd, hd + 1)

        def st():
            s = s_ref[hd]
            return s if keep is None else jnp.where(keep, s, _MASK_VALUE)

        m_prev = m_ref[sel, row, :]
        m_next = jnp.maximum(m_prev, jnp.max(st(), axis=0, keepdims=True))
        m_ref[sel, row, :] = m_next
        return jnp.exp2(m_prev - m_next), jnp.exp2(st() - m_next).astype(_BF16)

    def accumulate(sel, kk, hd, alpha, p):
        vt = jnp.concatenate([vt_ref[0, kk, hd * V_HEAD:(hd + 1) * V_HEAD, :], ones_rows], axis=0)
        acc_ref[sel, hd] = acc_ref[sel, hd] * alpha + _dot(vt, p)

    def tile(sel, kk, diagonal, nxt):
        keep = None
        if diagonal:
            key = lax.broadcasted_iota(jnp.int32, (tq, tq), 0)
            qry = lax.broadcasted_iota(jnp.int32, (tq, tq), 1)
            keep = key <= qry
        pending = None
        for hd in range(N_MLA_HEADS):
            ahead = hd + _MLA_AHEAD
            if ahead < N_MLA_HEADS:
                scores(sel, kk, ahead)
            elif nxt is not None:
                scores(nxt[0], nxt[1], ahead - N_MLA_HEADS)
            current = softmax(sel, hd, keep)
            if pending is not None:
                accumulate(sel, kk, hd - 1, *pending)
            pending = current
        accumulate(sel, kk, N_MLA_HEADS - 1, *pending)

    for hd in range(_MLA_AHEAD):
        scores(0, lo, hd)
    tile(0, lo, True, step_args(1))

    def body(it, carry):
        for u in range(_MLA_TILES_PER_ITER):
            t = 1 + it * _MLA_TILES_PER_ITER + u
            last = t + 1 == n_tiles
            sel_n, kk_n = step_args(t + 1)
            tile(*step_args(t), False, (jnp.where(last, 1, sel_n), jnp.where(last, hi, kk_n)))
        return carry

    lax.fori_loop(0, (n_tiles - 1) // _MLA_TILES_PER_ITER, body, 0)
    tile(1, hi, True, None)

    for sel, qt in ((0, lo), (1, hi)):
        out_t = jnp.concatenate([acc_ref[sel, hd, :V_HEAD, :] / acc_ref[sel, hd, V_HEAD:V_HEAD + 1, :]
                                 for hd in range(N_MLA_HEADS)], axis=0)
        o_ref[0, pl.ds(pl.multiple_of(qt * tq, tq), tq), :] = out_t.T.astype(o_ref.dtype)


def _mla_attn(q, k, vt, later_weights):
    b, s, w = q.shape
    nk = s // TQ_MLA
    assert nk % 2 == 0 and (nk - 1) % _MLA_TILES_PER_ITER == 0
    cast_in, cast_out, cast_shapes = _cast_plumbing(later_weights, (b, nk // 2))
    in_specs = [pl.BlockSpec((1, TQ_MLA, w), lambda bi, i: (bi, i, 0)),
                pl.BlockSpec((1, TQ_MLA, w), lambda bi, i: (bi, nk - 1 - i, 0)),
                pl.BlockSpec((1, s, w), lambda bi, i: (bi, 0, 0)),
                pl.BlockSpec((1, nk, MLA_WIDTH, TQ_MLA), lambda bi, i: (bi, 0, 0, 0))]
    return pl.pallas_call(
        _with_casts(_mla_attn_kernel, len(in_specs), 1, len(later_weights)),
        out_shape=(jax.ShapeDtypeStruct((b, s, MLA_WIDTH), _BF16), *cast_shapes),
        grid=(b, nk // 2),
        in_specs=in_specs + cast_in,
        out_specs=(pl.BlockSpec((1, s, MLA_WIDTH), lambda bi, i: (bi, 0, 0)), *cast_out),
        scratch_shapes=[pltpu.VMEM((2, TQ_MLA, w), _BF16),
                        pltpu.VMEM((N_MLA_HEADS, TQ_MLA, TQ_MLA), _F32),
                        pltpu.VMEM((2, N_MLA_HEADS, TQ_MLA), _F32),
                        pltpu.VMEM((2, N_MLA_HEADS, V_HEAD + _MLA_DENOM_ROWS, TQ_MLA), _F32)],
        compiler_params=_params(("arbitrary", "arbitrary")),
        name="mla_attn",
    )(q, q, k, vt, *later_weights)


def _retention_tables():
    h, L = N_RET_HEADS, RET_CHUNK
    log_gamma = jnp.log(1.0 - 2.0 ** (-5.0 - jnp.arange(h, dtype=_F32)))
    j = jnp.arange(L, dtype=_F32)
    diff = j[:, None] - j[None, :]
    intra = jnp.where(diff[None] >= 0,
                      jnp.exp(jnp.maximum(diff, 0.0)[None] * log_gamma[:, None, None]), 0.0)
    rowb = lambda t: jnp.broadcast_to(t.T[:, :, None], (h, L, LANES))
    k_to_end = jnp.exp((L - 1 - j)[:, None] * log_gamma[None, :])
    q_from_start = jnp.exp((j + 1)[:, None] * log_gamma[None, :])
    chunk_decay = jnp.broadcast_to(jnp.exp(L * log_gamma)[:, None, None], (h, RET_DK, RET_DV))
    return intra, rowb(k_to_end), rowb(q_from_start), chunk_decay


def _mix_xattn_kernel(x_ref, ymla_ref, yret_ref, wout_ref, gx_ref, wqk_ref, wvo_ref, o_ref):
    half = TM_MIX // 2
    rows = [slice(0, half), slice(half, TM_MIX)]
    mem_len = wqk_ref.shape[2] // N_XATTN_HEADS

    def out_proj(r):
        return (x_ref[0, r, :] + _dot(ymla_ref[0, r, :], wout_ref[:MLA_WIDTH, :])
                + _dot(yret_ref[0, r, :], wout_ref[MLA_WIDTH:, :]))

    def scores(x1):
        h = _rms(x1, gx_ref[...]).astype(_BF16)
        return _dot(h, wqk_ref[0])

    def softmax(s):
        heads = []
        for hd in range(N_XATTN_HEADS):
            blk = s[:, hd * mem_len:(hd + 1) * mem_len]
            p = jnp.exp(blk - jnp.max(blk, axis=-1, keepdims=True))
            heads.append((p / jnp.sum(p, axis=-1, keepdims=True)).astype(_BF16))
        return jnp.concatenate(heads, axis=1)

    x1 = [out_proj(r) for r in rows]
    s0 = scores(x1[0])
    s1 = scores(x1[1])
    p0 = softmax(s0)
    o_ref[0, rows[0], :] = x1[0] + _dot(p0, wvo_ref[0])
    p1 = softmax(s1)
    o_ref[0, rows[1], :] = x1[1] + _dot(p1, wvo_ref[0])


def _mix_xattn(x, y_mla, y_ret, w_out, g_xattn, wqk, wvo, later_weights):
    b, s, d = x.shape
    tok = lambda w: pl.BlockSpec((1, TM_MIX, w), lambda bi, i: (bi, i, 0))
    per_batch = lambda a: pl.BlockSpec((1,) + a.shape[1:], lambda bi, i: (bi, 0, 0))
    cast_in, cast_out, cast_shapes = _cast_plumbing(later_weights, (b, s // TM_MIX))
    in_specs = [tok(d), tok(MLA_WIDTH), tok(RET_WIDTH), _const_spec(w_out.shape), _const_spec((1, d)),
                per_batch(wqk), per_batch(wvo)]
    return pl.pallas_call(
        _with_casts(_mix_xattn_kernel, len(in_specs), 1, len(later_weights)),
        out_shape=(jax.ShapeDtypeStruct((b, s, d), _F32), *cast_shapes),
        grid=(b, s // TM_MIX),
        in_specs=in_specs + cast_in,
        out_specs=(tok(d), *cast_out),
        compiler_params=_params(("arbitrary", "arbitrary")),
        name="mix_xattn",
    )(x, y_mla, y_ret, w_out, g_xattn.reshape(1, d), wqk, wvo, *later_weights)


def _conv_ffn_kernel(x_ref, g_ref, win_ref, cw_ref, cb_ref, wout_ref, gfin_ref, o_ref, gate_ref):
    halo = SUBLANES
    half = TM_FFN // 2

    @pl.when(pl.program_id(1) == 0)
    def _():
        gate_ref[:halo, :] = jnp.zeros((halo, D_FF), _F32)

    rows = [slice(0, half), slice(half, TM_FFN)]
    xs = [x_ref[0, r, :] for r in rows]
    hs = [_rms(x, g_ref[...]).astype(_BF16) for x in xs]

    def project(i):
        gate = _dot(hs[i], win_ref[:, :D_FF])
        up = _dot(hs[i], win_ref[:, D_FF:])
        gate_ref[halo + i * half:halo + (i + 1) * half, :] = gate
        return gate, up

    def activate(i, gate, up):
        conv = cb_ref[...] + gate * cw_ref[CONV_W - 1:CONV_W, :]
        for tap in range(CONV_W - 1):
            back = CONV_W - 1 - tap
            lo = halo + i * half - back
            conv = conv + gate_ref[lo:lo + half, :] * cw_ref[tap:tap + 1, :]
        return (conv * jax.nn.sigmoid(conv) * up).astype(_BF16)

    def finish(i, act):
        x3 = xs[i] + _dot(act, wout_ref[...])
        o_ref[0, rows[i], :] = _rms(x3, gfin_ref[...])

    g0, u0 = project(0)
    g1, u1 = project(1)
    a0 = activate(0, g0, u0)
    finish(0, a0)
    a1 = activate(1, g1, u1)
    finish(1, a1)
    gate_ref[:halo, :] = gate_ref[TM_FFN:, :]


def _conv_ffn(x, g_ffn, w_ffn_in, conv_w, conv_b, w_ffn_out, g_final):
    b, s, d = x.shape
    tok = pl.BlockSpec((1, TM_FFN, d), lambda bi, i: (bi, i, 0))
    return pl.pallas_call(
        _conv_ffn_kernel,
        out_shape=jax.ShapeDtypeStruct((b, s, d), _F32),
        grid=(b, s // TM_FFN),
        in_specs=[tok, _const_spec((1, d)), _const_spec(w_ffn_in.shape), _const_spec(conv_w.shape),
                  _const_spec((1, D_FF)), _const_spec(w_ffn_out.shape), _const_spec((1, d))],
        out_specs=tok,
        scratch_shapes=[pltpu.VMEM((TM_FFN + SUBLANES, D_FF), _F32)],
        compiler_params=_params(("arbitrary", "arbitrary")),
        name="conv_ffn",
    )(x, g_ffn.reshape(1, d), w_ffn_in, conv_w, conv_b.reshape(1, D_FF), w_ffn_out, g_final.reshape(1, d))


def kernel(x, mem, positions, g_mix, w_in, g_q_lat, w_uq, g_kv_lat, w_ukv, w_out, g_xattn, g_mem, w_xq,
           w_xkv, w_xo, g_ffn, w_ffn_in, conv_w, conv_b, w_ffn_out, g_final):
    assert w_in.shape[0] == 1, "one layer supported"
    l = 0
    w_in_t = w_in[l].T
    cos_t, sin_t, wlat_t, *wret_t, wqk, wvo = _prep(positions, w_in_t, mem, g_mem[l], w_xkv[l], w_xq[l], w_xo[l])
    wuq_p, wuk_p, wuvt_p = _permute_up_weights(w_uq[l], w_ukv[l])
    q, k, vt, y_ret, w_out_b = _in_proj(
        x, g_mix[l], wlat_t, wret_t, g_q_lat[l], wuq_p, g_kv_lat[l], wuk_p, wuvt_p, cos_t, sin_t, (w_out[l],))
    y_mla, w_ffn_in_b = _mla_attn(q, k, vt, (w_ffn_in[l],))
    x, w_ffn_out_b = _mix_xattn(x, y_mla, y_ret, w_out_b, g_xattn[l], wqk, wvo, (w_ffn_out[l],))
    return _conv_ffn(x, g_ffn[l], w_ffn_in_b, conv_w[l], conv_b[l], w_ffn_out_b, g_final)
```

```python
import functools
import math

import jax
import jax.numpy as jnp
from jax import lax
from jax.experimental import pallas as pl
from jax.experimental.pallas import tpu as pltpu

D_MODEL = 1024
EPS = 1e-6
ROPE_BASE = 10000.0
N_MLA_HEADS = 8
QK_NOPE = 64
QK_ROPE = 32
V_HEAD = 64
Q_LORA = 256
KV_LORA = 128
N_RET_HEADS = 4
RET_DK = 128
RET_DV = 128
RET_CHUNK = 128
MLA_WIDTH = N_MLA_HEADS * V_HEAD
RET_WIDTH = N_RET_HEADS * RET_DV
N_XATTN_HEADS = 4
XATTN_HEAD = D_MODEL // N_XATTN_HEADS
D_FF = 2816
CONV_W = 3

LANES = 128
SUBLANES = 8
_BF16_ROWS = 16
VMEM_LIMIT = 56 * 1024 * 1024

HEAD_PAD = LANES
ROPE_HALF = QK_ROPE // 2
MLA_QK_WIDTH = N_MLA_HEADS * HEAD_PAD
IN_LAT = Q_LORA + KV_LORA + HEAD_PAD

TM_TABLE = 1024
TM_IN = 1024
TQ_MLA = 256
TM_MIX = 1024
TM_FFN = 512

_BF16 = jnp.bfloat16
_F32 = jnp.float32


def _dot(a, b):
    return jnp.dot(a, b, preferred_element_type=_F32)


def _dot_nt(a, b):
    return lax.dot_general(a, b, (((1,), (1,)), ((), ())), preferred_element_type=_F32)


def _dot_tn(a, b):
    return lax.dot_general(a, b, (((0,), (0,)), ((), ())), preferred_element_type=_F32)


def _rms(x, g):
    inv = lax.rsqrt(jnp.mean(x * x, axis=-1, keepdims=True) + EPS)
    return x * inv * g


def _const_spec(shape):
    nd = len(shape)
    return pl.BlockSpec(shape, lambda *_: (0,) * nd, pipeline_mode=pl.Buffered(1))


def _params(sem):
    return pltpu.CompilerParams(dimension_semantics=sem, vmem_limit_bytes=VMEM_LIMIT)


def _with_casts(body, n_in, n_out, n_cast):
    def kern(*refs):
        ins, rest = refs[:n_in], refs[n_in:]
        cast_in, rest = rest[:n_cast], rest[n_cast:]
        outs, rest = rest[:n_out], rest[n_out:]
        cast_out, scratch = rest[:n_cast], rest[n_cast:]
        for src, dst in zip(cast_in, cast_out):
            dst[...] = src[...].astype(dst.dtype)
        body(*ins, *outs, *scratch)
    return kern


def _cast_plumbing(weights, grid):
    steps = math.prod(grid)
    if len(grid) == 1:
        block_of = lambda i: (i, 0)
    else:
        block_of = lambda bi, i: (bi * grid[1] + i, 0)
    in_specs, out_specs, out_shapes = [], [], []
    for w in weights:
        rows, cols = w.shape
        assert rows % steps == 0 and (rows // steps) % _BF16_ROWS == 0, (w.shape, steps)
        spec = pl.BlockSpec((rows // steps, cols), block_of)
        in_specs.append(spec)
        out_specs.append(spec)
        out_shapes.append(jax.ShapeDtypeStruct(w.shape, _BF16))
    return in_specs, out_specs, out_shapes


def _prep_kernel(pos_ref, inv_ref, wlat_ref, wq_ref, wk_ref, wv_ref, wg_ref, mem_ref, gmem_ref, wxk_ref, wxv_ref,
                 wxq_ref, wxo_ref, cos_ref, sin_ref, wlat_out, wq_out, wk_out, wv_out, wg_out, wqk_ref, wvo_ref,
                 *, lat_blocks):
    half_rows = TM_IN // 2 // LANES
    lane = lax.broadcasted_iota(jnp.int32, (LANES, LANES), 1)
    column = lambda r: jnp.broadcast_to(pos_ref[r:r + 1, :], (LANES, LANES)).T
    for tile in range(TM_TABLE // TM_IN):
        for rr in range(half_rows):
            r0 = tile * 2 * half_rows + rr
            pos = jnp.where(lane < RET_DK // 2, column(r0), column(r0 + half_rows))
            ang = pos * inv_ref[...]
            out = slice((tile * half_rows + rr) * LANES, (tile * half_rows + rr + 1) * LANES)
            cos_ref[out, :] = jnp.cos(ang)
            sin_ref[out, :] = jnp.sin(ang)
    for src, dst in ((wq_ref, wq_out), (wk_ref, wk_out), (wv_ref, wv_out), (wg_ref, wg_out)):
        dst[...] = src[...].astype(_BF16)
    wlat_out[...] = jnp.where(pl.program_id(0) < lat_blocks, wlat_ref[...], 0.0).astype(_BF16)

    mem_n = _rms(mem_ref[0], gmem_ref[...]).astype(_BF16)
    k_h = _dot(mem_n, wxk_ref[...].astype(_BF16)).astype(_BF16)
    v_h = _dot(mem_n, wxv_ref[...].astype(_BF16)).astype(_BF16)
    wqk_ref[0] = (_dot_nt(wxq_ref[...].astype(_BF16), k_h) * (1.0 / math.sqrt(XATTN_HEAD))).astype(_BF16)
    wvo_ref[0] = _dot(v_h, wxo_ref[...].astype(_BF16)).astype(_BF16)


def _prep(positions, w_in_t, mem, g_mem, w_xkv, w_xq, w_xo):
    t = positions.size
    steps = t // TM_TABLE
    n_in, d = w_in_t.shape
    b, m, _ = mem.shape
    assert steps == b * N_XATTN_HEADS
    row0 = Q_LORA + KV_LORA + QK_ROPE
    rows = RET_WIDTH // steps
    assert RET_WIDTH % steps == 0 and rows % _BF16_ROWS == 0 and row0 % rows == 0
    assert n_in == row0 + 4 * RET_WIDTH
    pos = positions.astype(_F32).reshape(t // LANES, LANES)
    f_ret = 1.0 / (ROPE_BASE ** (jnp.arange(0, RET_DK, 2, dtype=_F32) / RET_DK))
    inv = jnp.concatenate([f_ret, f_ret])
    group = lambda j: pl.BlockSpec((rows, d), lambda i: (row0 // rows + j * steps + i, 0))
    lat_blocks = row0 // rows
    kr_dst = (Q_LORA + KV_LORA + QK_NOPE) // rows
    assert QK_ROPE == rows and IN_LAT // rows == steps and (Q_LORA + KV_LORA) % rows == 0
    lat_src = pl.BlockSpec((rows, d), lambda i: (jnp.minimum(i, lat_blocks - 1), 0))
    lat_dst = pl.BlockSpec((rows, d), lambda i: (jnp.where(i < lat_blocks - 1, i, jnp.where(
        i == lat_blocks - 1, kr_dst, jnp.where(i <= kr_dst, i - 1, i))), 0))
    table = pl.BlockSpec((TM_TABLE // 2, LANES), lambda i: (i, 0))
    w_out = pl.BlockSpec((rows, d), lambda i: (i, 0))
    nh = N_XATTN_HEADS
    head_cols = lambda off: pl.BlockSpec((d, XATTN_HEAD), lambda i: (0, off + i % nh))
    return pl.pallas_call(
        functools.partial(_prep_kernel, lat_blocks=lat_blocks),
        out_shape=(jax.ShapeDtypeStruct((t // 2, LANES), _F32),) * 2
                  + (jax.ShapeDtypeStruct((IN_LAT, d), _BF16),)
                  + (jax.ShapeDtypeStruct((RET_WIDTH, d), _BF16),) * 4
                  + (jax.ShapeDtypeStruct((b, d, nh * m), _BF16), jax.ShapeDtypeStruct((b, nh * m, d), _BF16)),
        grid=(steps,),
        in_specs=[pl.BlockSpec((TM_TABLE // LANES, LANES), lambda i: (i, 0)),
                  pl.BlockSpec((1, LANES), lambda i: (0, 0)), lat_src, group(0), group(1), group(2), group(3),
                  pl.BlockSpec((1, m, d), lambda i: (i // nh, 0, 0)), _const_spec((1, d)),
                  head_cols(0), head_cols(nh), head_cols(0),
                  pl.BlockSpec((XATTN_HEAD, d), lambda i: (i % nh, 0))],
        out_specs=(table, table, lat_dst, w_out, w_out, w_out, w_out,
                   pl.BlockSpec((1, d, m), lambda i: (i // nh, 0, i % nh)),
                   pl.BlockSpec((1, m, d), lambda i: (i // nh, i % nh, 0))),
        compiler_params=_params(("arbitrary",)),
        name="prep",
    )(pos, inv.reshape(1, LANES), w_in_t, w_in_t, w_in_t, w_in_t, w_in_t, mem, g_mem.reshape(1, d), w_xkv, w_xkv,
      w_xq, w_xo)


def _in_proj_kernel(x_ref, gmix_ref, wlat_ref, wrq_ref, wrk_ref, wrv_ref, wrg_ref, gq_ref, wuq_ref, gkv_ref, wuk_ref, wuvt_ref, cos_ref, sin_ref,
                    intra_ref, kend_ref, qstart_ref, decay_ref,
                    q_ref, k_ref, vt_ref, yret_ref, gate_ref, state_ref):
    h = _rms(x_ref[0], gmix_ref[...]).astype(_BF16)
    lane = lax.broadcasted_iota(jnp.int32, (TM_IN, LANES), 1)
    low = lane < RET_DK // 2

    lo = lax.broadcasted_iota(jnp.int32, (TM_IN // 2, LANES), 1) < RET_DK // 2

    def unpack(t):
        t_roll = pltpu.roll(t, RET_DK // 2, axis=1)
        return jnp.concatenate([jnp.where(lo, t, t_roll), jnp.where(lo, t_roll, t)], axis=0)

    cos_r, sin_full = unpack(cos_ref[...]), unpack(sin_ref[...])
    sin_r = jnp.where(low, -sin_full, sin_full)
    ratio = (RET_DK // 2) // ROPE_HALF
    rope_lane = (lane >= QK_NOPE) & (lane < QK_NOPE + QK_ROPE)
    src = jnp.where(lane < QK_NOPE + ROPE_HALF, lane - QK_NOPE, lane - QK_NOPE - ROPE_HALF) * ratio
    src = jnp.where(rope_lane, src, 0)
    g_cos = jnp.take_along_axis(cos_r, src, axis=1)
    g_sin = jnp.take_along_axis(sin_full, src, axis=1)
    cos_m = jnp.where(rope_lane, g_cos, 1.0)
    sin_m = jnp.where(rope_lane, jnp.where(lane < QK_NOPE + ROPE_HALF, -g_sin, g_sin), 0.0)

    first_half = lane < QK_NOPE + ROPE_HALF

    def swap_halves(t):
        return jnp.where(first_half, pltpu.roll(t, LANES - ROPE_HALF, axis=1), pltpu.roll(t, ROPE_HALF, axis=1))

    lat = _dot_nt(h, wlat_ref[...])
    c_q = lat[:, :Q_LORA]
    c_kv = lat[:, Q_LORA:Q_LORA + KV_LORA]
    kr = lat[:, Q_LORA + KV_LORA:IN_LAT]
    k_rope = kr * cos_m + swap_halves(kr) * sin_m

    cqn = _rms(c_q, gq_ref[...]).astype(_BF16)
    q = _dot(cqn, wuq_ref[...])
    scale = math.log2(math.e) / math.sqrt(QK_NOPE + QK_ROPE)
    for hd in range(N_MLA_HEADS):
        sl = slice(hd * HEAD_PAD, (hd + 1) * HEAD_PAD)
        q_ref[0, :, sl] = ((q[:, sl] * cos_m + swap_halves(q[:, sl]) * sin_m) * scale).astype(_BF16)

    ckvn = _rms(c_kv, gkv_ref[...]).astype(_BF16)
    k_nope = _dot(ckvn, wuk_ref[...])
    for hd in range(N_MLA_HEADS):
        sl = slice(hd * HEAD_PAD, (hd + 1) * HEAD_PAD)
        k_ref[0, :, sl] = (k_nope[:, sl] + k_rope).astype(_BF16)
    v_t = _dot_nt(wuvt_ref[...], ckvn).astype(_BF16)
    for j in range(TM_IN // TQ_MLA):
        vt_ref[0, j] = v_t[:, j * TQ_MLA:(j + 1) * TQ_MLA]

    def ret_rope(w_ref, mult):
        r = _dot_nt(h, w_ref[...])
        heads = []
        for hd in range(N_RET_HEADS):
            rh = r[:, hd * RET_DK:(hd + 1) * RET_DK]
            roped = rh * cos_r + pltpu.roll(rh, RET_DK // 2, axis=1) * sin_r
            if mult is not None:
                roped = roped * mult
            heads.append(roped.astype(_BF16))
        return heads

    rq = ret_rope(wrq_ref, None)
    rk = ret_rope(wrk_ref, RET_DK ** -0.5)
    rv = _dot_nt(h, wrv_ref[...]).astype(_BF16)
    rg = _dot_nt(h, wrg_ref[...]).astype(_BF16)
    gate_ref[0] = rg
    _retention_tile(rq, rk, rv, intra_ref, kend_ref, qstart_ref, decay_ref, state_ref, yret_ref)


def _retention_tile(q_heads, k_heads, v, intra_ref, kend_ref, qstart_ref, decay_ref, state_ref, o_ref):
    @pl.when(pl.program_id(1) == 0)
    def _():
        state_ref[...] = jnp.zeros(state_ref.shape, _F32)

    L = RET_CHUNK
    n_chunks = v.shape[0] // L
    units = [(c, hd) for c in range(n_chunks) for hd in range(N_RET_HEADS)]
    rows = lambda c: slice(c * L, (c + 1) * L)
    cols = lambda hd: slice(hd * RET_DV, (hd + 1) * RET_DV)
    scores, chunk_kv = {}, {}
    for c, hd in units:
        scores[c, hd] = _dot_nt(q_heads[hd][rows(c)], k_heads[hd][rows(c)])
    for c, hd in units:
        v_dec = (v[rows(c), cols(hd)].astype(_F32) * kend_ref[hd]).astype(_BF16)
        chunk_kv[c, hd] = _dot_tn(k_heads[hd][rows(c)], v_dec)
    prev_state = {}
    for hd in range(N_RET_HEADS):
        state = state_ref[hd]
        for c in range(n_chunks):
            prev_state[c, hd] = state.astype(_BF16)
            state = decay_ref[hd] * state + chunk_kv[c, hd]
        state_ref[hd] = state
    inner, cross = {}, {}
    for c, hd in units:
        inner[c, hd] = _dot((scores[c, hd] * intra_ref[hd]).astype(_BF16), v[rows(c), cols(hd)])
    for c, hd in units:
        cross[c, hd] = _dot(q_heads[hd][rows(c)], prev_state[c, hd])
    for c, hd in units:
        out = inner[c, hd] + cross[c, hd] * qstart_ref[hd]
        o_ref[0, rows(c), cols(hd)] = out.astype(o_ref.dtype)


def _permute_up_weights(w_uq, w_ukv):
    wq = w_uq.reshape(Q_LORA, N_MLA_HEADS, QK_NOPE + QK_ROPE)
    nope, r1, r2 = wq[..., :QK_NOPE], wq[..., QK_NOPE:QK_NOPE + ROPE_HALF], wq[..., QK_NOPE + ROPE_HALF:]
    z32 = jnp.zeros((Q_LORA, N_MLA_HEADS, HEAD_PAD - QK_NOPE - QK_ROPE), w_uq.dtype)
    wuq_p = jnp.concatenate([nope, r1, r2, z32], axis=-1).reshape(Q_LORA, MLA_QK_WIDTH).astype(_BF16)

    wkv = w_ukv.reshape(KV_LORA, N_MLA_HEADS, QK_NOPE + V_HEAD)
    zk = jnp.zeros((KV_LORA, N_MLA_HEADS, HEAD_PAD - QK_NOPE), w_ukv.dtype)
    wk = jnp.concatenate([wkv[..., :QK_NOPE], zk], axis=-1).reshape(KV_LORA, MLA_QK_WIDTH)
    wv_t = wkv[..., QK_NOPE:].reshape(KV_LORA, MLA_WIDTH).T
    return wuq_p, wk.astype(_BF16), wv_t.astype(_BF16)


def _in_proj(x, g_mix, wlat_t, wret_t, g_q_lat, wuq_p, g_kv_lat, wuk_p, wuvt_p, cos_t, sin_t, later_weights):
    b, s, d = x.shape
    nt = s // TM_IN
    per = TM_IN // TQ_MLA
    tok = lambda w: pl.BlockSpec((1, TM_IN, w), lambda bi, i: (bi, i, 0))
    tab = pl.BlockSpec((TM_IN // 2, LANES), lambda bi, i: (bi * nt + i, 0))
    bf = lambda w: jax.ShapeDtypeStruct((b, s, w), _BF16)
    ret_tables = _retention_tables()
    cast_in, cast_out, cast_shapes = _cast_plumbing(later_weights, (b, nt))
    in_specs = ([tok(d), _const_spec((1, d)), _const_spec(wlat_t.shape)] + [_const_spec(w.shape) for w in wret_t]
                + [_const_spec((1, Q_LORA)), _const_spec(wuq_p.shape), _const_spec((1, KV_LORA)),
                   _const_spec(wuk_p.shape), _const_spec(wuvt_p.shape), tab, tab]
                + [_const_spec(t.shape) for t in ret_tables])
    out_specs = (tok(MLA_QK_WIDTH), tok(MLA_QK_WIDTH),
                 pl.BlockSpec((1, per, MLA_WIDTH, TQ_MLA), lambda bi, i: (bi, i, 0, 0)), tok(RET_WIDTH), tok(RET_WIDTH))
    return pl.pallas_call(
        _with_casts(_in_proj_kernel, len(in_specs), len(out_specs), len(later_weights)),
        out_shape=(bf(MLA_QK_WIDTH), bf(MLA_QK_WIDTH),
                   jax.ShapeDtypeStruct((b, s // TQ_MLA, MLA_WIDTH, TQ_MLA), _BF16), bf(RET_WIDTH), bf(RET_WIDTH),
                   *cast_shapes),
        grid=(b, nt),
        in_specs=in_specs + cast_in,
        out_specs=(*out_specs, *cast_out),
        scratch_shapes=[pltpu.VMEM((N_RET_HEADS, RET_DK, RET_DV), _F32)],
        compiler_params=_params(("arbitrary", "arbitrary")),
        name="in_proj",
    )(x, g_mix.reshape(1, d), wlat_t, *wret_t, g_q_lat.reshape(1, Q_LORA), wuq_p, g_kv_lat.reshape(1, KV_LORA),
      wuk_p, wuvt_p, cos_t, sin_t, *ret_tables, *later_weights)


_MASK_VALUE = -0.7 * float(jnp.finfo(jnp.float32).max)
_MLA_AHEAD = 4
_MLA_TILES_PER_ITER = 5
_MLA_DENOM_ROWS = 16


def _mla_attn_kernel(qlo_ref, qhi_ref, k_ref, vt_ref, o_ref, q_ref, s_ref, m_ref, acc_ref):
    tq = TQ_MLA
    n_tiles = k_ref.shape[1] // tq
    lo = pl.program_id(1)
    hi = n_tiles - 1 - lo
    q_ref[0] = qlo_ref[0]
    q_ref[1] = qhi_ref[0]
    m_ref[...] = jnp.full(m_ref.shape, _MASK_VALUE, _F32)
    acc_ref[...] = jnp.zeros(acc_ref.shape, _F32)
    ones_rows = jnp.ones((_MLA_DENOM_ROWS, tq), _BF16)

    def step_args(t):
        sel = (t > lo).astype(jnp.int32)
        return sel, t - 1 - sel * lo

    def scores(sel, kk, hd):
        sl = slice(hd * HEAD_PAD, (hd + 1) * HEAD_PAD)
        ks = pl.multiple_of(kk * tq, tq)
        s_ref[hd] = _dot_nt(k_ref[0, pl.ds(ks, tq), sl], q_ref[sel, :, sl])

    def softmax(sel, hd, keep):
        row = slice(hd, hd + 1)

        def st():
            s = s_ref[hd]
            return s if keep is None else jnp.where(keep, s, _MASK_VALUE)

        m_prev = m_ref[sel, row, :]
        m_next = jnp.maximum(m_prev, jnp.max(st(), axis=0, keepdims=True))
        m_ref[sel, row, :] = m_next
        return jnp.exp2(m_prev - m_next), jnp.exp2(st() - m_next).astype(_BF16)

    def accumulate(sel, kk, hd, alpha, p):
        vt = jnp.concatenate([vt_ref[0, kk, hd * V_HEAD:(hd + 1) * V_HEAD, :], ones_rows], axis=0)
        acc_ref[sel, hd] = acc_ref[sel, hd] * alpha + _dot(vt, p)

    def tile(sel, kk, diagonal, nxt):
        keep = None
        if diagonal:
            key = lax.broadcasted_iota(jnp.int32, (tq, tq), 0)
            qry = lax.broadcasted_iota(jnp.int32, (tq, tq), 1)
            keep = key <= qry
        pending = None
        for hd in range(N_MLA_HEADS):
            ahead = hd + _MLA_AHEAD
            if ahead < N_MLA_HEADS:
                scores(sel, kk, ahead)
            elif nxt is not None:
                scores(nxt[0], nxt[1], ahead - N_MLA_HEADS)
            current = softmax(sel, hd, keep)
            if pending is not None:
                accumulate(sel, kk, hd - 1, *pending)
            pending = current
        accumulate(sel, kk, N_MLA_HEADS - 1, *pending)

    for hd in range(_MLA_AHEAD):
        scores(0, lo, hd)
    tile(0, lo, True, step_args(1))

    def body(it, carry):
        for u in range(_MLA_TILES_PER_ITER):
            t = 1 + it * _MLA_TILES_PER_ITER + u
            last = t + 1 == n_tiles
            sel_n, kk_n = step_args(t + 1)
            tile(*step_args(t), False, (jnp.where(last, 1, sel_n), jnp.where(last, hi, kk_n)))
        return carry

    lax.fori_loop(0, (n_tiles - 1) // _MLA_TILES_PER_ITER, body, 0)
    tile(1, hi, True, None)

    for sel, qt in ((0, lo), (1, hi)):
        out_t = jnp.concatenate([acc_ref[sel, hd, :V_HEAD, :] / acc_ref[sel, hd, V_HEAD:V_HEAD + 1, :]
                                 for hd in range(N_MLA_HEADS)], axis=0)
        o_ref[0, pl.ds(pl.multiple_of(qt * tq, tq), tq), :] = out_t.T.astype(o_ref.dtype)


def _mla_attn(q, k, vt, later_weights):
    b, s, w = q.shape
    nk = s // TQ_MLA
    assert nk % 2 == 0 and (nk - 1) % _MLA_TILES_PER_ITER == 0
    cast_in, cast_out, cast_shapes = _cast_plumbing(later_weights, (b, nk // 2))
    in_specs = [pl.BlockSpec((1, TQ_MLA, w), lambda bi, i: (bi, i, 0)),
                pl.BlockSpec((1, TQ_MLA, w), lambda bi, i: (bi, nk - 1 - i, 0)),
                pl.BlockSpec((1, s, w), lambda bi, i: (bi, 0, 0)),
                pl.BlockSpec((1, nk, MLA_WIDTH, TQ_MLA), lambda bi, i: (bi, 0, 0, 0))]
    return pl.pallas_call(
        _with_casts(_mla_attn_kernel, len(in_specs), 1, len(later_weights)),
        out_shape=(jax.ShapeDtypeStruct((b, s, MLA_WIDTH), _BF16), *cast_shapes),
        grid=(b, nk // 2),
        in_specs=in_specs + cast_in,
        out_specs=(pl.BlockSpec((1, s, MLA_WIDTH), lambda bi, i: (bi, 0, 0)), *cast_out),
        scratch_shapes=[pltpu.VMEM((2, TQ_MLA, w), _BF16),
                        pltpu.VMEM((N_MLA_HEADS, TQ_MLA, TQ_MLA), _F32),
                        pltpu.VMEM((2, N_MLA_HEADS, TQ_MLA), _F32),
                        pltpu.VMEM((2, N_MLA_HEADS, V_HEAD + _MLA_DENOM_ROWS, TQ_MLA), _F32)],
        compiler_params=_params(("arbitrary", "arbitrary")),
        name="mla_attn",
    )(q, q, k, vt, *later_weights)


def _retention_tables():
    h, L = N_RET_HEADS, RET_CHUNK
    log_gamma = jnp.log(1.0 - 2.0 ** (-5.0 - jnp.arange(h, dtype=_F32)))
    j = jnp.arange(L, dtype=_F32)
    diff = j[:, None] - j[None, :]
    intra = jnp.where(diff[None] >= 0,
                      jnp.exp(jnp.maximum(diff, 0.0)[None] * log_gamma[:, None, None]), 0.0)
    rowb = lambda t: jnp.broadcast_to(t.T[:, :, None], (h, L, LANES))
    k_to_end = jnp.exp((L - 1 - j)[:, None] * log_gamma[None, :])
    q_from_start = jnp.exp((j + 1)[:, None] * log_gamma[None, :])
    chunk_decay = jnp.broadcast_to(jnp.exp(L * log_gamma)[:, None, None], (h, RET_DK, RET_DV))
    return intra, rowb(k_to_end), rowb(q_from_start), chunk_decay


def _mix_xattn_kernel(x_ref, ymla_ref, yret_ref, gate_ref, wout_ref, gx_ref, wqk_ref, wvo_ref, o_ref):
    half = TM_MIX // 2
    rows = [slice(0, half), slice(half, TM_MIX)]
    mem_len = wqk_ref.shape[2] // N_XATTN_HEADS

    def retention_out(r):
        heads = []
        for hd in range(N_RET_HEADS):
            c = slice(hd * RET_DV, (hd + 1) * RET_DV)
            out = yret_ref[0, r, c].astype(_F32)
            mu = jnp.mean(out, axis=-1, keepdims=True)
            cen = out - mu
            var = jnp.mean(cen * cen, axis=-1, keepdims=True)
            g = gate_ref[0, r, c].astype(_F32)
            heads.append((cen * lax.rsqrt(var + EPS) * (g * jax.nn.sigmoid(g))).astype(_BF16))
        return jnp.concatenate(heads, axis=1)

    def out_proj(r):
        return (x_ref[0, r, :] + _dot(ymla_ref[0, r, :], wout_ref[:MLA_WIDTH, :])
                + _dot(retention_out(r), wout_ref[MLA_WIDTH:, :]))

    def scores(x1):
        h = _rms(x1, gx_ref[...]).astype(_BF16)
        return _dot(h, wqk_ref[0])

    def softmax(s):
        heads = []
        for hd in range(N_XATTN_HEADS):
            blk = s[:, hd * mem_len:(hd + 1) * mem_len]
            p = jnp.exp(blk - jnp.max(blk, axis=-1, keepdims=True))
            heads.append((p / jnp.sum(p, axis=-1, keepdims=True)).astype(_BF16))
        return jnp.concatenate(heads, axis=1)

    x1 = [out_proj(r) for r in rows]
    s0 = scores(x1[0])
    s1 = scores(x1[1])
    p0 = softmax(s0)
    o_ref[0, rows[0], :] = x1[0] + _dot(p0, wvo_ref[0])
    p1 = softmax(s1)
    o_ref[0, rows[1], :] = x1[1] + _dot(p1, wvo_ref[0])


def _mix_xattn(x, y_mla, y_ret, ret_gate, w_out, g_xattn, wqk, wvo, later_weights):
    b, s, d = x.shape
    tok = lambda w: pl.BlockSpec((1, TM_MIX, w), lambda bi, i: (bi, i, 0))
    per_batch = lambda a: pl.BlockSpec((1,) + a.shape[1:], lambda bi, i: (bi, 0, 0))
    cast_in, cast_out, cast_shapes = _cast_plumbing(later_weights, (b, s // TM_MIX))
    in_specs = [tok(d), tok(MLA_WIDTH), tok(RET_WIDTH), tok(RET_WIDTH), _const_spec(w_out.shape), _const_spec((1, d)),
                per_batch(wqk), per_batch(wvo)]
    return pl.pallas_call(
        _with_casts(_mix_xattn_kernel, len(in_specs), 1, len(later_weights)),
        out_shape=(jax.ShapeDtypeStruct((b, s, d), _F32), *cast_shapes),
        grid=(b, s // TM_MIX),
        in_specs=in_specs + cast_in,
        out_specs=(tok(d), *cast_out),
        compiler_params=_params(("arbitrary", "arbitrary")),
        name="mix_xattn",
    )(x, y_mla, y_ret, ret_gate, w_out, g_xattn.reshape(1, d), wqk, wvo, *later_weights)


def _conv_ffn_kernel(x_ref, g_ref, win_ref, cw_ref, cb_ref, wout_ref, gfin_ref, o_ref, gate_ref):
    halo = SUBLANES
    half = TM_FFN // 2

    @pl.when(pl.program_id(1) == 0)
    def _():
        gate_ref[:halo, :] = jnp.zeros((halo, D_FF), _F32)

    rows = [slice(0, half), slice(half, TM_FFN)]
    xs = [x_ref[0, r, :] for r in rows]
    hs = [_rms(x, g_ref[...]).astype(_BF16) for x in xs]

    def project(i):
        gate = _dot(hs[i], win_ref[:, :D_FF])
        up = _dot(hs[i], win_ref[:, D_FF:])
        gate_ref[halo + i * half:halo + (i + 1) * half, :] = gate
        return gate, up

    def activate(i, gate, up):
        conv = cb_ref[...] + gate * cw_ref[CONV_W - 1:CONV_W, :]
        for tap in range(CONV_W - 1):
            back = CONV_W - 1 - tap
            lo = halo + i * half - back
            conv = conv + gate_ref[lo:lo + half, :] * cw_ref[tap:tap + 1, :]
        return (conv * jax.nn.sigmoid(conv) * up).astype(_BF16)

    def finish(i, act):
        x3 = xs[i] + _dot(act, wout_ref[...])
        o_ref[0, rows[i], :] = _rms(x3, gfin_ref[...])

    g0, u0 = project(0)
    g1, u1 = project(1)
    a0 = activate(0, g0, u0)
    finish(0, a0)
    a1 = activate(1, g1, u1)
    finish(1, a1)
    gate_ref[:halo, :] = gate_ref[TM_FFN:, :]


def _conv_ffn(x, g_ffn, w_ffn_in, conv_w, conv_b, w_ffn_out, g_final):
    b, s, d = x.shape
    tok = pl.BlockSpec((1, TM_FFN, d), lambda bi, i: (bi, i, 0))
    return pl.pallas_call(
        _conv_ffn_kernel,
        out_shape=jax.ShapeDtypeStruct((b, s, d), _F32),
        grid=(b, s // TM_FFN),
        in_specs=[tok, _const_spec((1, d)), _const_spec(w_ffn_in.shape), _const_spec(conv_w.shape),
                  _const_spec((1, D_FF)), _const_spec(w_ffn_out.shape), _const_spec((1, d))],
        out_specs=tok,
        scratch_shapes=[pltpu.VMEM((TM_FFN + SUBLANES, D_FF), _F32)],
        compiler_params=_params(("arbitrary", "arbitrary")),
        name="conv_ffn",
    )(x, g_ffn.reshape(1, d), w_ffn_in, conv_w, conv_b.reshape(1, D_FF), w_ffn_out, g_final.reshape(1, d))


def kernel(x, mem, positions, g_mix, w_in, g_q_lat, w_uq, g_kv_lat, w_ukv, w_out, g_xattn, g_mem, w_xq,
           w_xkv, w_xo, g_ffn, w_ffn_in, conv_w, conv_b, w_ffn_out, g_final):
    assert w_in.shape[0] == 1, "one layer supported"
    l = 0
    w_in_t = w_in[l].T
    cos_t, sin_t, wlat_t, *wret_t, wqk, wvo = _prep(positions, w_in_t, mem, g_mem[l], w_xkv[l], w_xq[l], w_xo[l])
    wuq_p, wuk_p, wuvt_p = _permute_up_weights(w_uq[l], w_ukv[l])
    q, k, vt, y_ret, ret_gate, w_out_b = _in_proj(
        x, g_mix[l], wlat_t, wret_t, g_q_lat[l], wuq_p, g_kv_lat[l], wuk_p, wuvt_p, cos_t, sin_t, (w_out[l],))
    y_mla, w_ffn_in_b = _mla_attn(q, k, vt, (w_ffn_in[l],))
    x, w_ffn_out_b = _mix_xattn(x, y_mla, y_ret, ret_gate, w_out_b, g_xattn[l], wqk, wvo, (w_ffn_out[l],))
    return _conv_ffn(x, g_ffn[l], w_ffn_in_b, conv_w[l], conv_b[l], w_ffn_out_b, g_final)
```

```python
import functools
import math

import jax
import jax.numpy as jnp
from jax import lax
from jax.experimental import pallas as pl
from jax.experimental.pallas import tpu as pltpu

D_MODEL = 1024
EPS = 1e-6
ROPE_BASE = 10000.0
N_MLA_HEADS = 8
QK_NOPE = 64
QK_ROPE = 32
V_HEAD = 64
Q_LORA = 256
KV_LORA = 128
N_RET_HEADS = 4
RET_DK = 128
RET_DV = 128
RET_CHUNK = 128
MLA_WIDTH = N_MLA_HEADS * V_HEAD
RET_WIDTH = N_RET_HEADS * RET_DV
N_XATTN_HEADS = 4
XATTN_HEAD = D_MODEL // N_XATTN_HEADS
D_FF = 2816
CONV_W = 3

LANES = 128
SUBLANES = 8
_BF16_ROWS = 16
VMEM_LIMIT = 56 * 1024 * 1024

HEAD_PAD = LANES
ROPE_HALF = QK_ROPE // 2
MLA_QK_WIDTH = N_MLA_HEADS * HEAD_PAD
IN_LAT = Q_LORA + KV_LORA + HEAD_PAD

TM_TABLE = 1024
TM_IN = 1024
TQ_MLA = 256
TM_MIX = 1024
TM_FFN = 512

_BF16 = jnp.bfloat16
_F32 = jnp.float32


def _dot(a, b):
    return jnp.dot(a, b, preferred_element_type=_F32)


def _dot_nt(a, b):
    return lax.dot_general(a, b, (((1,), (1,)), ((), ())), preferred_element_type=_F32)


def _dot_tn(a, b):
    return lax.dot_general(a, b, (((0,), (0,)), ((), ())), preferred_element_type=_F32)


def _rms(x, g):
    inv = lax.rsqrt(jnp.mean(x * x, axis=-1, keepdims=True) + EPS)
    return x * inv * g


def _const_spec(shape):
    nd = len(shape)
    return pl.BlockSpec(shape, lambda *_: (0,) * nd, pipeline_mode=pl.Buffered(1))


def _params(sem):
    return pltpu.CompilerParams(dimension_semantics=sem, vmem_limit_bytes=VMEM_LIMIT)


def _with_casts(body, n_in, n_out, n_cast):
    def kern(*refs):
        ins, rest = refs[:n_in], refs[n_in:]
        cast_in, rest = rest[:n_cast], rest[n_cast:]
        outs, rest = rest[:n_out], rest[n_out:]
        cast_out, scratch = rest[:n_cast], rest[n_cast:]
        for src, dst in zip(cast_in, cast_out):
            dst[...] = src[...].astype(dst.dtype)
        body(*ins, *outs, *scratch)
    return kern


def _cast_plumbing(weights, grid):
    steps = math.prod(grid)
    if len(grid) == 1:
        block_of = lambda i: (i, 0)
    else:
        block_of = lambda bi, i: (bi * grid[1] + i, 0)
    in_specs, out_specs, out_shapes = [], [], []
    for w in weights:
        rows, cols = w.shape
        assert rows % steps == 0 and (rows // steps) % _BF16_ROWS == 0, (w.shape, steps)
        spec = pl.BlockSpec((rows // steps, cols), block_of)
        in_specs.append(spec)
        out_specs.append(spec)
        out_shapes.append(jax.ShapeDtypeStruct(w.shape, _BF16))
    return in_specs, out_specs, out_shapes


def _prep_kernel(pos_ref, inv_ref, wlat_ref, wq_ref, wk_ref, wv_ref, wg_ref, mem_ref, gmem_ref, wxk_ref, wxv_ref,
                 wxq_ref, wxo_ref, cos_ref, sin_ref, wlat_out, wq_out, wk_out, wv_out, wg_out, wqk_ref, wvo_ref,
                 *, lat_blocks, n_batch):
    half_rows = TM_IN // 2 // LANES
    lane = lax.broadcasted_iota(jnp.int32, (LANES, LANES), 1)
    column = lambda r: jnp.broadcast_to(pos_ref[r:r + 1, :], (LANES, LANES)).T
    for tile in range(TM_TABLE // TM_IN):
        for rr in range(half_rows):
            r0 = tile * 2 * half_rows + rr
            pos = jnp.where(lane < RET_DK // 2, column(r0), column(r0 + half_rows))
            ang = pos * inv_ref[...]
            out = slice((tile * half_rows + rr) * LANES, (tile * half_rows + rr + 1) * LANES)
            cos_ref[out, :] = jnp.cos(ang)
            sin_ref[out, :] = jnp.sin(ang)
    for src, dst in ((wq_ref, wq_out), (wk_ref, wk_out), (wv_ref, wv_out), (wg_ref, wg_out)):
        dst[...] = src[...].astype(_BF16)
    wlat_out[...] = jnp.where(pl.program_id(0) < lat_blocks, wlat_ref[...], 0.0).astype(_BF16)

    mem_n = _rms(mem_ref[pl.program_id(0) % n_batch], gmem_ref[...]).astype(_BF16)
    k_h = _dot(mem_n, wxk_ref[...].astype(_BF16)).astype(_BF16)
    v_h = _dot(mem_n, wxv_ref[...].astype(_BF16)).astype(_BF16)
    wqk_ref[0] = (_dot_nt(wxq_ref[...].astype(_BF16), k_h) * (1.0 / math.sqrt(XATTN_HEAD))).astype(_BF16)
    wvo_ref[0] = _dot(v_h, wxo_ref[...].astype(_BF16)).astype(_BF16)


def _prep(positions, w_in_t, mem, g_mem, w_xkv, w_xq, w_xo):
    t = positions.size
    steps = t // TM_TABLE
    n_in, d = w_in_t.shape
    b, m, _ = mem.shape
    assert steps == b * N_XATTN_HEADS
    row0 = Q_LORA + KV_LORA + QK_ROPE
    rows = RET_WIDTH // steps
    assert RET_WIDTH % steps == 0 and rows % _BF16_ROWS == 0 and row0 % rows == 0
    assert n_in == row0 + 4 * RET_WIDTH
    pos = positions.astype(_F32).reshape(t // LANES, LANES)
    f_ret = 1.0 / (ROPE_BASE ** (jnp.arange(0, RET_DK, 2, dtype=_F32) / RET_DK))
    inv = jnp.concatenate([f_ret, f_ret])
    group = lambda j: pl.BlockSpec((rows, d), lambda i: (row0 // rows + j * steps + i, 0))
    lat_blocks = row0 // rows
    kr_dst = (Q_LORA + KV_LORA + QK_NOPE) // rows
    assert QK_ROPE == rows and IN_LAT // rows == steps and (Q_LORA + KV_LORA) % rows == 0
    lat_src = pl.BlockSpec((rows, d), lambda i: (jnp.minimum(i, lat_blocks - 1), 0))
    lat_dst = pl.BlockSpec((rows, d), lambda i: (jnp.where(i < lat_blocks - 1, i, jnp.where(
        i == lat_blocks - 1, kr_dst, jnp.where(i <= kr_dst, i - 1, i))), 0))
    table = pl.BlockSpec((TM_TABLE // 2, LANES), lambda i: (i, 0))
    w_out = pl.BlockSpec((rows, d), lambda i: (i, 0))
    nh = N_XATTN_HEADS
    head_cols = lambda off: pl.BlockSpec((d, XATTN_HEAD), lambda i: (0, off + i // b))
    return pl.pallas_call(
        functools.partial(_prep_kernel, lat_blocks=lat_blocks, n_batch=b),
        out_shape=(jax.ShapeDtypeStruct((t // 2, LANES), _F32),) * 2
                  + (jax.ShapeDtypeStruct((IN_LAT, d), _BF16),)
                  + (jax.ShapeDtypeStruct((RET_WIDTH, d), _BF16),) * 4
                  + (jax.ShapeDtypeStruct((b, d, nh * m), _BF16), jax.ShapeDtypeStruct((b, nh * m, d), _BF16)),
        grid=(steps,),
        in_specs=[pl.BlockSpec((TM_TABLE // LANES, LANES), lambda i: (i, 0)),
                  pl.BlockSpec((1, LANES), lambda i: (0, 0)), lat_src, group(0), group(1), group(2), group(3),
                  _const_spec(mem.shape), _const_spec((1, d)),
                  head_cols(0), head_cols(nh), head_cols(0),
                  pl.BlockSpec((XATTN_HEAD, d), lambda i: (i // b, 0))],
        out_specs=(table, table, lat_dst, w_out, w_out, w_out, w_out,
                   pl.BlockSpec((1, d, m), lambda i: (i % b, 0, i // b)),
                   pl.BlockSpec((1, m, d), lambda i: (i % b, i // b, 0))),
        compiler_params=_params(("arbitrary",)),
        name="prep",
    )(pos, inv.reshape(1, LANES), w_in_t, w_in_t, w_in_t, w_in_t, w_in_t, mem, g_mem.reshape(1, d), w_xkv, w_xkv,
      w_xq, w_xo)


def _in_proj_kernel(x_ref, gmix_ref, wlat_ref, wrq_ref, wrk_ref, wrv_ref, wrg_ref, gq_ref, wuq_ref, gkv_ref, wuk_ref, wuvt_ref, cos_ref, sin_ref,
                    intra_ref, kend_ref, qstart_ref, decay_ref,
                    q_ref, k_ref, vt_ref, yret_ref, gate_ref, state_ref):
    h = _rms(x_ref[0], gmix_ref[...]).astype(_BF16)
    lane = lax.broadcasted_iota(jnp.int32, (TM_IN, LANES), 1)
    low = lane < RET_DK // 2

    lo = lax.broadcasted_iota(jnp.int32, (TM_IN // 2, LANES), 1) < RET_DK // 2

    def unpack(t):
        t_roll = pltpu.roll(t, RET_DK // 2, axis=1)
        return jnp.concatenate([jnp.where(lo, t, t_roll), jnp.where(lo, t_roll, t)], axis=0)

    cos_r, sin_full = unpack(cos_ref[...]), unpack(sin_ref[...])
    sin_r = jnp.where(low, -sin_full, sin_full)
    ratio = (RET_DK // 2) // ROPE_HALF
    rope_lane = (lane >= QK_NOPE) & (lane < QK_NOPE + QK_ROPE)
    src = jnp.where(lane < QK_NOPE + ROPE_HALF, lane - QK_NOPE, lane - QK_NOPE - ROPE_HALF) * ratio
    src = jnp.where(rope_lane, src, 0)
    g_cos = jnp.take_along_axis(cos_r, src, axis=1)
    g_sin = jnp.take_along_axis(sin_full, src, axis=1)
    cos_m = jnp.where(rope_lane, g_cos, 1.0)
    sin_m = jnp.where(rope_lane, jnp.where(lane < QK_NOPE + ROPE_HALF, -g_sin, g_sin), 0.0)

    first_half = lane < QK_NOPE + ROPE_HALF

    def swap_halves(t):
        return jnp.where(first_half, pltpu.roll(t, LANES - ROPE_HALF, axis=1), pltpu.roll(t, ROPE_HALF, axis=1))

    lat = _dot_nt(h, wlat_ref[...])
    c_q = lat[:, :Q_LORA]
    c_kv = lat[:, Q_LORA:Q_LORA + KV_LORA]
    kr = lat[:, Q_LORA + KV_LORA:IN_LAT]
    k_rope = kr * cos_m + swap_halves(kr) * sin_m

    cqn = _rms(c_q, gq_ref[...]).astype(_BF16)
    q = _dot(cqn, wuq_ref[...])
    scale = math.log2(math.e) / math.sqrt(QK_NOPE + QK_ROPE)
    for hd in range(N_MLA_HEADS):
        sl = slice(hd * HEAD_PAD, (hd + 1) * HEAD_PAD)
        q_ref[0, :, sl] = ((q[:, sl] * cos_m + swap_halves(q[:, sl]) * sin_m) * scale).astype(_BF16)

    ckvn = _rms(c_kv, gkv_ref[...]).astype(_BF16)
    k_nope = _dot(ckvn, wuk_ref[...])
    for hd in range(N_MLA_HEADS):
        sl = slice(hd * HEAD_PAD, (hd + 1) * HEAD_PAD)
        k_ref[0, :, sl] = (k_nope[:, sl] + k_rope).astype(_BF16)
    v_t = _dot_nt(wuvt_ref[...], ckvn).astype(_BF16)
    for j in range(TM_IN // TQ_MLA):
        vt_ref[0, j] = v_t[:, j * TQ_MLA:(j + 1) * TQ_MLA]

    def ret_rope(w_ref, mult):
        r = _dot_nt(h, w_ref[...])
        heads = []
        for hd in range(N_RET_HEADS):
            rh = r[:, hd * RET_DK:(hd + 1) * RET_DK]
            roped = rh * cos_r + pltpu.roll(rh, RET_DK // 2, axis=1) * sin_r
            if mult is not None:
                roped = roped * mult
            heads.append(roped.astype(_BF16))
        return heads

    rq = ret_rope(wrq_ref, None)
    rk = ret_rope(wrk_ref, RET_DK ** -0.5)
    rv = _dot_nt(h, wrv_ref[...]).astype(_BF16)
    rg = _dot_nt(h, wrg_ref[...]).astype(_BF16)
    gate_ref[0] = rg
    _retention_tile(rq, rk, rv, intra_ref, kend_ref, qstart_ref, decay_ref, state_ref, yret_ref)


def _retention_tile(q_heads, k_heads, v, intra_ref, kend_ref, qstart_ref, decay_ref, state_ref, o_ref):
    @pl.when(pl.program_id(1) == 0)
    def _():
        state_ref[...] = jnp.zeros(state_ref.shape, _F32)

    L = RET_CHUNK
    n_chunks = v.shape[0] // L
    units = [(c, hd) for c in range(n_chunks) for hd in range(N_RET_HEADS)]
    rows = lambda c: slice(c * L, (c + 1) * L)
    cols = lambda hd: slice(hd * RET_DV, (hd + 1) * RET_DV)
    scores, chunk_kv = {}, {}
    for c, hd in units:
        scores[c, hd] = _dot_nt(q_heads[hd][rows(c)], k_heads[hd][rows(c)])
    for c, hd in units:
        v_dec = (v[rows(c), cols(hd)].astype(_F32) * kend_ref[hd]).astype(_BF16)
        chunk_kv[c, hd] = _dot_tn(k_heads[hd][rows(c)], v_dec)
    prev_state = {}
    for hd in range(N_RET_HEADS):
        state = state_ref[hd]
        for c in range(n_chunks):
            prev_state[c, hd] = state.astype(_BF16)
            state = decay_ref[hd] * state + chunk_kv[c, hd]
        state_ref[hd] = state
    inner, cross = {}, {}
    for c, hd in units:
        inner[c, hd] = _dot((scores[c, hd] * intra_ref[hd]).astype(_BF16), v[rows(c), cols(hd)])
    for c, hd in units:
        cross[c, hd] = _dot(q_heads[hd][rows(c)], prev_state[c, hd])
    for c, hd in units:
        out = inner[c, hd] + cross[c, hd] * qstart_ref[hd]
        o_ref[0, rows(c), cols(hd)] = out.astype(o_ref.dtype)


def _permute_up_weights(w_uq, w_ukv):
    wq = w_uq.reshape(Q_LORA, N_MLA_HEADS, QK_NOPE + QK_ROPE)
    nope, r1, r2 = wq[..., :QK_NOPE], wq[..., QK_NOPE:QK_NOPE + ROPE_HALF], wq[..., QK_NOPE + ROPE_HALF:]
    z32 = jnp.zeros((Q_LORA, N_MLA_HEADS, HEAD_PAD - QK_NOPE - QK_ROPE), w_uq.dtype)
    wuq_p = jnp.concatenate([nope, r1, r2, z32], axis=-1).reshape(Q_LORA, MLA_QK_WIDTH).astype(_BF16)

    wkv = w_ukv.reshape(KV_LORA, N_MLA_HEADS, QK_NOPE + V_HEAD)
    zk = jnp.zeros((KV_LORA, N_MLA_HEADS, HEAD_PAD - QK_NOPE), w_ukv.dtype)
    wk = jnp.concatenate([wkv[..., :QK_NOPE], zk], axis=-1).reshape(KV_LORA, MLA_QK_WIDTH)
    wv_t = wkv[..., QK_NOPE:].reshape(KV_LORA, MLA_WIDTH).T
    return wuq_p, wk.astype(_BF16), wv_t.astype(_BF16)


def _in_proj(x, g_mix, wlat_t, wret_t, g_q_lat, wuq_p, g_kv_lat, wuk_p, wuvt_p, cos_t, sin_t, later_weights):
    b, s, d = x.shape
    nt = s // TM_IN
    per = TM_IN // TQ_MLA
    tok = lambda w: pl.BlockSpec((1, TM_IN, w), lambda bi, i: (bi, i, 0))
    tab = pl.BlockSpec((TM_IN // 2, LANES), lambda bi, i: (bi * nt + i, 0))
    bf = lambda w: jax.ShapeDtypeStruct((b, s, w), _BF16)
    ret_tables = _retention_tables()
    cast_in, cast_out, cast_shapes = _cast_plumbing(later_weights, (b, nt))
    in_specs = ([tok(d), _const_spec((1, d)), _const_spec(wlat_t.shape)] + [_const_spec(w.shape) for w in wret_t]
                + [_const_spec((1, Q_LORA)), _const_spec(wuq_p.shape), _const_spec((1, KV_LORA)),
                   _const_spec(wuk_p.shape), _const_spec(wuvt_p.shape), tab, tab]
                + [_const_spec(t.shape) for t in ret_tables])
    out_specs = (tok(MLA_QK_WIDTH), tok(MLA_QK_WIDTH),
                 pl.BlockSpec((1, per, MLA_WIDTH, TQ_MLA), lambda bi, i: (bi, i, 0, 0)), tok(RET_WIDTH), tok(RET_WIDTH))
    return pl.pallas_call(
        _with_casts(_in_proj_kernel, len(in_specs), len(out_specs), len(later_weights)),
        out_shape=(bf(MLA_QK_WIDTH), bf(MLA_QK_WIDTH),
                   jax.ShapeDtypeStruct((b, s // TQ_MLA, MLA_WIDTH, TQ_MLA), _BF16), bf(RET_WIDTH), bf(RET_WIDTH),
                   *cast_shapes),
        grid=(b, nt),
        in_specs=in_specs + cast_in,
        out_specs=(*out_specs, *cast_out),
        scratch_shapes=[pltpu.VMEM((N_RET_HEADS, RET_DK, RET_DV), _F32)],
        compiler_params=_params(("arbitrary", "arbitrary")),
        name="in_proj",
    )(x, g_mix.reshape(1, d), wlat_t, *wret_t, g_q_lat.reshape(1, Q_LORA), wuq_p, g_kv_lat.reshape(1, KV_LORA),
      wuk_p, wuvt_p, cos_t, sin_t, *ret_tables, *later_weights)


_MASK_VALUE = -0.7 * float(jnp.finfo(jnp.float32).max)
_MLA_AHEAD = 4
_MLA_TILES_PER_ITER = 5
_MLA_DENOM_ROWS = 16


def _mla_attn_kernel(qlo_ref, qhi_ref, k_ref, vt_ref, o_ref, q_ref, s_ref, m_ref, acc_ref):
    tq = TQ_MLA
    n_tiles = k_ref.shape[1] // tq
    lo = pl.program_id(1)
    hi = n_tiles - 1 - lo
    q_ref[0] = qlo_ref[0]
    q_ref[1] = qhi_ref[0]
    m_ref[...] = jnp.full(m_ref.shape, _MASK_VALUE, _F32)
    acc_ref[...] = jnp.zeros(acc_ref.shape, _F32)
    ones_rows = jnp.ones((_MLA_DENOM_ROWS, tq), _BF16)

    def step_args(t):
        sel = (t > lo).astype(jnp.int32)
        return sel, t - 1 - sel * lo

    def scores(sel, kk, hd):
        sl = slice(hd * HEAD_PAD, (hd + 1) * HEAD_PAD)
        ks = pl.multiple_of(kk * tq, tq)
        s_ref[hd] = _dot_nt(k_ref[0, pl.ds(ks, tq), sl], q_ref[sel, :, sl])

    def softmax(sel, hd, keep):
        row = slice(hd, hd + 1)

        def st():
            s = s_ref[hd]
            return s if keep is None else jnp.where(keep, s, _MASK_VALUE)

        m_prev = m_ref[sel, row, :]
        m_next = jnp.maximum(m_prev, jnp.max(st(), axis=0, keepdims=True))
        m_ref[sel, row, :] = m_next
        return jnp.exp2(m_prev - m_next), jnp.exp2(st() - m_next).astype(_BF16)

    def accumulate(sel, kk, hd, alpha, p):
        vt = jnp.concatenate([vt_ref[0, kk, hd * V_HEAD:(hd + 1) * V_HEAD, :], ones_rows], axis=0)
        acc_ref[sel, hd] = acc_ref[sel, hd] * alpha + _dot(vt, p)

    def tile(sel, kk, diagonal, nxt):
        keep = None
        if diagonal:
            key = lax.broadcasted_iota(jnp.int32, (tq, tq), 0)
            qry = lax.broadcasted_iota(jnp.int32, (tq, tq), 1)
            keep = key <= qry
        pending = None
        for hd in range(N_MLA_HEADS):
            ahead = hd + _MLA_AHEAD
            if ahead < N_MLA_HEADS:
                scores(sel, kk, ahead)
            elif nxt is not None:
                scores(nxt[0], nxt[1], ahead - N_MLA_HEADS)
            current = softmax(sel, hd, keep)
            if pending is not None:
                accumulate(sel, kk, hd - 1, *pending)
            pending = current
        accumulate(sel, kk, N_MLA_HEADS - 1, *pending)

    for hd in range(_MLA_AHEAD):
        scores(0, lo, hd)
    tile(0, lo, True, step_args(1))

    def body(it, carry):
        for u in range(_MLA_TILES_PER_ITER):
            t = 1 + it * _MLA_TILES_PER_ITER + u
            last = t + 1 == n_tiles
            sel_n, kk_n = step_args(t + 1)
            tile(*step_args(t), False, (jnp.where(last, 1, sel_n), jnp.where(last, hi, kk_n)))
        return carry

    lax.fori_loop(0, (n_tiles - 1) // _MLA_TILES_PER_ITER, body, 0)
    tile(1, hi, True, None)

    for sel, qt in ((0, lo), (1, hi)):
        out_t = jnp.concatenate([acc_ref[sel, hd, :V_HEAD, :] / acc_ref[sel, hd, V_HEAD:V_HEAD + 1, :]
                                 for hd in range(N_MLA_HEADS)], axis=0)
        o_ref[0, pl.ds(pl.multiple_of(qt * tq, tq), tq), :] = out_t.T.astype(o_ref.dtype)


def _mla_attn(q, k, vt, later_weights):
    b, s, w = q.shape
    nk = s // TQ_MLA
    assert nk % 2 == 0 and (nk - 1) % _MLA_TILES_PER_ITER == 0
    cast_in, cast_out, cast_shapes = _cast_plumbing(later_weights, (b, nk // 2))
    in_specs = [pl.BlockSpec((1, TQ_MLA, w), lambda bi, i: (bi, i, 0)),
                pl.BlockSpec((1, TQ_MLA, w), lambda bi, i: (bi, nk - 1 - i, 0)),
                pl.BlockSpec((1, s, w), lambda bi, i: (bi, 0, 0)),
                pl.BlockSpec((1, nk, MLA_WIDTH, TQ_MLA), lambda bi, i: (bi, 0, 0, 0))]
    return pl.pallas_call(
        _with_casts(_mla_attn_kernel, len(in_specs), 1, len(later_weights)),
        out_shape=(jax.ShapeDtypeStruct((b, s, MLA_WIDTH), _BF16), *cast_shapes),
        grid=(b, nk // 2),
        in_specs=in_specs + cast_in,
        out_specs=(pl.BlockSpec((1, s, MLA_WIDTH), lambda bi, i: (bi, 0, 0)), *cast_out),
        scratch_shapes=[pltpu.VMEM((2, TQ_MLA, w), _BF16),
                        pltpu.VMEM((N_MLA_HEADS, TQ_MLA, TQ_MLA), _F32),
                        pltpu.VMEM((2, N_MLA_HEADS, TQ_MLA), _F32),
                        pltpu.VMEM((2, N_MLA_HEADS, V_HEAD + _MLA_DENOM_ROWS, TQ_MLA), _F32)],
        compiler_params=_params(("arbitrary", "arbitrary")),
        name="mla_attn",
    )(q, q, k, vt, *later_weights)


def _retention_tables():
    h, L = N_RET_HEADS, RET_CHUNK
    log_gamma = jnp.log(1.0 - 2.0 ** (-5.0 - jnp.arange(h, dtype=_F32)))
    j = jnp.arange(L, dtype=_F32)
    diff = j[:, None] - j[None, :]
    intra = jnp.where(diff[None] >= 0,
                      jnp.exp(jnp.maximum(diff, 0.0)[None] * log_gamma[:, None, None]), 0.0)
    rowb = lambda t: jnp.broadcast_to(t.T[:, :, None], (h, L, LANES))
    k_to_end = jnp.exp((L - 1 - j)[:, None] * log_gamma[None, :])
    q_from_start = jnp.exp((j + 1)[:, None] * log_gamma[None, :])
    chunk_decay = jnp.broadcast_to(jnp.exp(L * log_gamma)[:, None, None], (h, RET_DK, RET_DV))
    return intra, rowb(k_to_end), rowb(q_from_start), chunk_decay


def _mix_xattn_kernel(x_ref, ymla_ref, yret_ref, gate_ref, wout_ref, gx_ref, wqk_ref, wvo_ref, o_ref):
    half = TM_MIX // 2
    rows = [slice(0, half), slice(half, TM_MIX)]
    mem_len = wqk_ref.shape[2] // N_XATTN_HEADS

    def retention_out(r):
        heads = []
        for hd in range(N_RET_HEADS):
            c = slice(hd * RET_DV, (hd + 1) * RET_DV)
            out = yret_ref[0, r, c].astype(_F32)
            mu = jnp.mean(out, axis=-1, keepdims=True)
            cen = out - mu
            var = jnp.mean(cen * cen, axis=-1, keepdims=True)
            g = gate_ref[0, r, c].astype(_F32)
            heads.append((cen * lax.rsqrt(var + EPS) * (g * jax.nn.sigmoid(g))).astype(_BF16))
        return jnp.concatenate(heads, axis=1)

    def out_proj(r):
        return (x_ref[0, r, :] + _dot(ymla_ref[0, r, :], wout_ref[:MLA_WIDTH, :])
                + _dot(retention_out(r), wout_ref[MLA_WIDTH:, :]))

    def scores(x1):
        h = _rms(x1, gx_ref[...]).astype(_BF16)
        return _dot(h, wqk_ref[0])

    def softmax(s):
        heads = []
        for hd in range(N_XATTN_HEADS):
            blk = s[:, hd * mem_len:(hd + 1) * mem_len]
            p = jnp.exp(blk - jnp.max(blk, axis=-1, keepdims=True))
            heads.append((p / jnp.sum(p, axis=-1, keepdims=True)).astype(_BF16))
        return jnp.concatenate(heads, axis=1)

    x1 = [out_proj(r) for r in rows]
    s0 = scores(x1[0])
    s1 = scores(x1[1])
    p0 = softmax(s0)
    o_ref[0, rows[0], :] = x1[0] + _dot(p0, wvo_ref[0])
    p1 = softmax(s1)
    o_ref[0, rows[1], :] = x1[1] + _dot(p1, wvo_ref[0])


def _mix_xattn(x, y_mla, y_ret, ret_gate, w_out, g_xattn, wqk, wvo, later_weights):
    b, s, d = x.shape
    tok = lambda w: pl.BlockSpec((1, TM_MIX, w), lambda bi, i: (bi, i, 0))
    per_batch = lambda a: pl.BlockSpec((1,) + a.shape[1:], lambda bi, i: (bi, 0, 0))
    cast_in, cast_out, cast_shapes = _cast_plumbing(later_weights, (b, s // TM_MIX))
    in_specs = [tok(d), tok(MLA_WIDTH), tok(RET_WIDTH), tok(RET_WIDTH), _const_spec(w_out.shape), _const_spec((1, d)),
                per_batch(wqk), per_batch(wvo)]
    return pl.pallas_call(
        _with_casts(_mix_xattn_kernel, len(in_specs), 1, len(later_weights)),
        out_shape=(jax.ShapeDtypeStruct((b, s, d), _F32), *cast_shapes),
        grid=(b, s // TM_MIX),
        in_specs=in_specs + cast_in,
        out_specs=(tok(d), *cast_out),
        compiler_params=_params(("arbitrary", "arbitrary")),
        name="mix_xattn",
    )(x, y_mla, y_ret, ret_gate, w_out, g_xattn.reshape(1, d), wqk, wvo, *later_weights)


def _conv_ffn_kernel(x_ref, g_ref, win_ref, cw_ref, cb_ref, wout_ref, gfin_ref, o_ref, gate_ref):
    halo = SUBLANES
    half = TM_FFN // 2

    @pl.when(pl.program_id(1) == 0)
    def _():
        gate_ref[:halo, :] = jnp.zeros((halo, D_FF), _F32)

    rows = [slice(0, half), slice(half, TM_FFN)]
    xs = [x_ref[0, r, :] for r in rows]
    hs = [_rms(x, g_ref[...]).astype(_BF16) for x in xs]

    def project(i):
        gate = _dot(hs[i], win_ref[:, :D_FF])
        up = _dot(hs[i], win_ref[:, D_FF:])
        gate_ref[halo + i * half:halo + (i + 1) * half, :] = gate
        return gate, up

    def activate(i, gate, up):
        conv = cb_ref[...] + gate * cw_ref[CONV_W - 1:CONV_W, :]
        for tap in range(CONV_W - 1):
            back = CONV_W - 1 - tap
            lo = halo + i * half - back
            conv = conv + gate_ref[lo:lo + half, :] * cw_ref[tap:tap + 1, :]
        return (conv * jax.nn.sigmoid(conv) * up).astype(_BF16)

    def finish(i, act):
        x3 = xs[i] + _dot(act, wout_ref[...])
        o_ref[0, rows[i], :] = _rms(x3, gfin_ref[...])

    g0, u0 = project(0)
    g1, u1 = project(1)
    a0 = activate(0, g0, u0)
    finish(0, a0)
    a1 = activate(1, g1, u1)
    finish(1, a1)
    gate_ref[:halo, :] = gate_ref[TM_FFN:, :]


def _conv_ffn(x, g_ffn, w_ffn_in, conv_w, conv_b, w_ffn_out, g_final):
    b, s, d = x.shape
    tok = pl.BlockSpec((1, TM_FFN, d), lambda bi, i: (bi, i, 0))
    return pl.pallas_call(
        _conv_ffn_kernel,
        out_shape=jax.ShapeDtypeStruct((b, s, d), _F32),
        grid=(b, s // TM_FFN),
        in_specs=[tok, _const_spec((1, d)), _const_spec(w_ffn_in.shape), _const_spec(conv_w.shape),
                  _const_spec((1, D_FF)), _const_spec(w_ffn_out.shape), _const_spec((1, d))],
        out_specs=tok,
        scratch_shapes=[pltpu.VMEM((TM_FFN + SUBLANES, D_FF), _F32)],
        compiler_params=_params(("arbitrary", "arbitrary")),
        name="conv_ffn",
    )(x, g_ffn.reshape(1, d), w_ffn_in, conv_w, conv_b.reshape(1, D_FF), w_ffn_out, g_final.reshape(1, d))


def kernel(x, mem, positions, g_mix, w_in, g_q_lat, w_uq, g_kv_lat, w_ukv, w_out, g_xattn, g_mem, w_xq,
           w_xkv, w_xo, g_ffn, w_ffn_in, conv_w, conv_b, w_ffn_out, g_final):
    assert w_in.shape[0] == 1, "one layer supported"
    l = 0
    w_in_t = w_in[l].T
    cos_t, sin_t, wlat_t, *wret_t, wqk, wvo = _prep(positions, w_in_t, mem, g_mem[l], w_xkv[l], w_xq[l], w_xo[l])
    wuq_p, wuk_p, wuvt_p = _permute_up_weights(w_uq[l], w_ukv[l])
    q, k, vt, y_ret, ret_gate, w_out_b = _in_proj(
        x, g_mix[l], wlat_t, wret_t, g_q_lat[l], wuq_p, g_kv_lat[l], wuk_p, wuvt_p, cos_t, sin_t, (w_out[l],))
    y_mla, w_ffn_in_b = _mla_attn(q, k, vt, (w_ffn_in[l],))
    x, w_ffn_out_b = _mix_xattn(x, y_mla, y_ret, ret_gate, w_out_b, g_xattn[l], wqk, wvo, (w_ffn_out[l],))
    return _conv_ffn(x, g_ffn[l], w_ffn_in_b, conv_w[l], conv_b[l], w_ffn_out_b, g_final)
```

```python
import functools
import math

import jax
import jax.numpy as jnp
from jax import lax
from jax.experimental import pallas as pl
from jax.experimental.pallas import tpu as pltpu

D_MODEL = 1024
EPS = 1e-6
ROPE_BASE = 10000.0
N_MLA_HEADS = 8
QK_NOPE = 64
QK_ROPE = 32
V_HEAD = 64
Q_LORA = 256
KV_LORA = 128
N_RET_HEADS = 4
RET_DK = 128
RET_DV = 128
RET_CHUNK = 128
MLA_WIDTH = N_MLA_HEADS * V_HEAD
RET_WIDTH = N_RET_HEADS * RET_DV
N_XATTN_HEADS = 4
XATTN_HEAD = D_MODEL // N_XATTN_HEADS
D_FF = 2816
CONV_W = 3

LANES = 128
SUBLANES = 8
_BF16_ROWS = 16
VMEM_LIMIT = 56 * 1024 * 1024

HEAD_PAD = LANES
ROPE_HALF = QK_ROPE // 2
MLA_QK_WIDTH = N_MLA_HEADS * HEAD_PAD
IN_LAT = Q_LORA + KV_LORA + HEAD_PAD

TM_TABLE = 1024
TM_IN = 1024
TQ_MLA = 256
TM_MIX = 1024
TM_FFN = 512

_BF16 = jnp.bfloat16
_F32 = jnp.float32


def _dot(a, b):
    return jnp.dot(a, b, preferred_element_type=_F32)


def _dot_nt(a, b):
    return lax.dot_general(a, b, (((1,), (1,)), ((), ())), preferred_element_type=_F32)


def _dot_tn(a, b):
    return lax.dot_general(a, b, (((0,), (0,)), ((), ())), preferred_element_type=_F32)


def _rms(x, g):
    inv = lax.rsqrt(jnp.mean(x * x, axis=-1, keepdims=True) + EPS)
    return x * inv * g


def _const_spec(shape):
    nd = len(shape)
    return pl.BlockSpec(shape, lambda *_: (0,) * nd, pipeline_mode=pl.Buffered(1))


def _params(sem):
    return pltpu.CompilerParams(dimension_semantics=sem, vmem_limit_bytes=VMEM_LIMIT)


def _with_casts(body, n_in, n_out, n_cast):
    def kern(*refs):
        ins, rest = refs[:n_in], refs[n_in:]
        cast_in, rest = rest[:n_cast], rest[n_cast:]
        outs, rest = rest[:n_out], rest[n_out:]
        cast_out, scratch = rest[:n_cast], rest[n_cast:]
        for src, dst in zip(cast_in, cast_out):
            dst[...] = src[...].astype(dst.dtype)
        body(*ins, *outs, *scratch)
    return kern


def _cast_plumbing(weights, grid):
    steps = math.prod(grid)
    if len(grid) == 1:
        block_of = lambda i: (i, 0)
    else:
        block_of = lambda bi, i: (bi * grid[1] + i, 0)
    in_specs, out_specs, out_shapes = [], [], []
    for w in weights:
        rows, cols = w.shape
        assert rows % steps == 0 and (rows // steps) % _BF16_ROWS == 0, (w.shape, steps)
        spec = pl.BlockSpec((rows // steps, cols), block_of)
        in_specs.append(spec)
        out_specs.append(spec)
        out_shapes.append(jax.ShapeDtypeStruct(w.shape, _BF16))
    return in_specs, out_specs, out_shapes


def _prep_kernel(pos_ref, inv_ref, wlat_ref, wq_ref, wk_ref, wv_ref, wg_ref, mem_ref, gmem_ref, wxk_ref, wxv_ref,
                 wxq_ref, wxo_ref, cos_ref, sin_ref, wlat_out, wq_out, wk_out, wv_out, wg_out, wqk_ref, wvo_ref,
                 *, lat_blocks, n_batch):
    half_rows = TM_IN // 2 // LANES
    lane = lax.broadcasted_iota(jnp.int32, (LANES, LANES), 1)
    column = lambda r: jnp.broadcast_to(pos_ref[r:r + 1, :], (LANES, LANES)).T
    for tile in range(TM_TABLE // TM_IN):
        for rr in range(half_rows):
            r0 = tile * 2 * half_rows + rr
            pos = jnp.where(lane < RET_DK // 2, column(r0), column(r0 + half_rows))
            ang = pos * inv_ref[...]
            out = slice((tile * half_rows + rr) * LANES, (tile * half_rows + rr + 1) * LANES)
            cos_ref[out, :] = jnp.cos(ang)
            sin_ref[out, :] = jnp.sin(ang)
    for src, dst in ((wq_ref, wq_out), (wk_ref, wk_out), (wv_ref, wv_out), (wg_ref, wg_out)):
        dst[...] = src[...].astype(_BF16)
    wlat_out[...] = jnp.where(pl.program_id(0) < lat_blocks, wlat_ref[...], 0.0).astype(_BF16)

    mem_n = _rms(mem_ref[pl.program_id(0) % n_batch], gmem_ref[...]).astype(_BF16)
    k_h = _dot(mem_n, wxk_ref[...].astype(_BF16)).astype(_BF16)
    v_h = _dot(mem_n, wxv_ref[...].astype(_BF16)).astype(_BF16)
    wqk_ref[0] = (_dot_nt(wxq_ref[...].astype(_BF16), k_h) * (1.0 / math.sqrt(XATTN_HEAD))).astype(_BF16)
    wvo_ref[0] = _dot(v_h, wxo_ref[...].astype(_BF16)).astype(_BF16)


def _prep(positions, w_in_t, mem, g_mem, w_xkv, w_xq, w_xo):
    t = positions.size
    steps = t // TM_TABLE
    n_in, d = w_in_t.shape
    b, m, _ = mem.shape
    assert steps == b * N_XATTN_HEADS
    row0 = Q_LORA + KV_LORA + QK_ROPE
    rows = RET_WIDTH // steps
    assert RET_WIDTH % steps == 0 and rows % _BF16_ROWS == 0 and row0 % rows == 0
    assert n_in == row0 + 4 * RET_WIDTH
    pos = positions.astype(_F32).reshape(t // LANES, LANES)
    f_ret = 1.0 / (ROPE_BASE ** (jnp.arange(0, RET_DK, 2, dtype=_F32) / RET_DK))
    inv = jnp.concatenate([f_ret, f_ret])
    group = lambda j: pl.BlockSpec((rows, d), lambda i: (row0 // rows + j * steps + i, 0))
    lat_blocks = row0 // rows
    kr_dst = (Q_LORA + KV_LORA + QK_NOPE) // rows
    assert QK_ROPE == rows and IN_LAT // rows == steps and (Q_LORA + KV_LORA) % rows == 0
    lat_src = pl.BlockSpec((rows, d), lambda i: (jnp.minimum(i, lat_blocks - 1), 0))
    lat_dst = pl.BlockSpec((rows, d), lambda i: (jnp.where(i < lat_blocks - 1, i, jnp.where(
        i == lat_blocks - 1, kr_dst, jnp.where(i <= kr_dst, i - 1, i))), 0))
    table = pl.BlockSpec((TM_TABLE // 2, LANES), lambda i: (i, 0))
    w_out = pl.BlockSpec((rows, d), lambda i: (i, 0))
    nh = N_XATTN_HEADS
    head_cols = lambda off: pl.BlockSpec((d, XATTN_HEAD), lambda i: (0, off + i // b))
    return pl.pallas_call(
        functools.partial(_prep_kernel, lat_blocks=lat_blocks, n_batch=b),
        out_shape=(jax.ShapeDtypeStruct((t // 2, LANES), _F32),) * 2
                  + (jax.ShapeDtypeStruct((IN_LAT, d), _BF16),)
                  + (jax.ShapeDtypeStruct((RET_WIDTH, d), _BF16),) * 4
                  + (jax.ShapeDtypeStruct((b, d, nh * m), _BF16), jax.ShapeDtypeStruct((b, nh * m, d), _BF16)),
        grid=(steps,),
        in_specs=[pl.BlockSpec((TM_TABLE // LANES, LANES), lambda i: (i, 0)),
                  pl.BlockSpec((1, LANES), lambda i: (0, 0)), lat_src, group(0), group(1), group(2), group(3),
                  _const_spec(mem.shape), _const_spec((1, d)),
                  head_cols(0), head_cols(nh), head_cols(0),
                  pl.BlockSpec((XATTN_HEAD, d), lambda i: (i // b, 0))],
        out_specs=(table, table, lat_dst, w_out, w_out, w_out, w_out,
                   pl.BlockSpec((1, d, m), lambda i: (i % b, 0, i // b)),
                   pl.BlockSpec((1, m, d), lambda i: (i % b, i // b, 0))),
        compiler_params=_params(("arbitrary",)),
        name="prep",
    )(pos, inv.reshape(1, LANES), w_in_t, w_in_t, w_in_t, w_in_t, w_in_t, mem, g_mem.reshape(1, d), w_xkv, w_xkv,
      w_xq, w_xo)


def _in_proj_kernel(x_ref, gmix_ref, wlat_ref, wrq_ref, wrk_ref, wrv_ref, wrg_ref, gq_ref, wuq_ref, gkv_ref, wuk_ref, wuvt_ref, cos_ref, sin_ref,
                    intra_ref, kend_ref, qstart_ref, decay_ref,
                    q_ref, k_ref, vt_ref, yret_ref, gate_ref, state_ref):
    h = _rms(x_ref[0], gmix_ref[...]).astype(_BF16)
    lane = lax.broadcasted_iota(jnp.int32, (TM_IN, LANES), 1)
    low = lane < RET_DK // 2

    lo = lax.broadcasted_iota(jnp.int32, (TM_IN // 2, LANES), 1) < RET_DK // 2

    def unpack(t):
        t_roll = pltpu.roll(t, RET_DK // 2, axis=1)
        return jnp.concatenate([jnp.where(lo, t, t_roll), jnp.where(lo, t_roll, t)], axis=0)

    cos_r, sin_full = unpack(cos_ref[...]), unpack(sin_ref[...])
    sin_r = jnp.where(low, -sin_full, sin_full)
    ratio = (RET_DK // 2) // ROPE_HALF
    rope_lane = (lane >= QK_NOPE) & (lane < QK_NOPE + QK_ROPE)
    src = jnp.where(lane < QK_NOPE + ROPE_HALF, lane - QK_NOPE, lane - QK_NOPE - ROPE_HALF) * ratio
    src = jnp.where(rope_lane, src, 0)
    g_cos = jnp.take_along_axis(cos_r, src, axis=1)
    g_sin = jnp.take_along_axis(sin_full, src, axis=1)
    cos_m = jnp.where(rope_lane, g_cos, 1.0)
    sin_m = jnp.where(rope_lane, jnp.where(lane < QK_NOPE + ROPE_HALF, -g_sin, g_sin), 0.0)

    first_half = lane < QK_NOPE + ROPE_HALF

    def swap_halves(t):
        return jnp.where(first_half, pltpu.roll(t, LANES - ROPE_HALF, axis=1), pltpu.roll(t, ROPE_HALF, axis=1))

    lat = _dot_nt(h, wlat_ref[...])
    c_q = lat[:, :Q_LORA]
    c_kv = lat[:, Q_LORA:Q_LORA + KV_LORA]
    kr = lat[:, Q_LORA + KV_LORA:IN_LAT]
    k_rope = kr * cos_m + swap_halves(kr) * sin_m

    def ret_rope(w_ref, mult):
        r = _dot_nt(h, w_ref[...])
        heads = []
        for hd in range(N_RET_HEADS):
            rh = r[:, hd * RET_DK:(hd + 1) * RET_DK]
            roped = rh * cos_r + pltpu.roll(rh, RET_DK // 2, axis=1) * sin_r
            if mult is not None:
                roped = roped * mult
            heads.append(roped.astype(_BF16))
        return heads

    rq = ret_rope(wrq_ref, None)
    rk = ret_rope(wrk_ref, RET_DK ** -0.5)
    rv = _dot_nt(h, wrv_ref[...]).astype(_BF16)

    cqn = _rms(c_q, gq_ref[...]).astype(_BF16)
    q = _dot(cqn, wuq_ref[...])
    scale = math.log2(math.e) / math.sqrt(QK_NOPE + QK_ROPE)
    for hd in range(N_MLA_HEADS):
        sl = slice(hd * HEAD_PAD, (hd + 1) * HEAD_PAD)
        q_ref[0, :, sl] = ((q[:, sl] * cos_m + swap_halves(q[:, sl]) * sin_m) * scale).astype(_BF16)

    ckvn = _rms(c_kv, gkv_ref[...]).astype(_BF16)
    k_nope = _dot(ckvn, wuk_ref[...])
    for hd in range(N_MLA_HEADS):
        sl = slice(hd * HEAD_PAD, (hd + 1) * HEAD_PAD)
        k_ref[0, :, sl] = (k_nope[:, sl] + k_rope).astype(_BF16)
    v_t = _dot_nt(wuvt_ref[...], ckvn).astype(_BF16)
    for j in range(TM_IN // TQ_MLA):
        vt_ref[0, j] = v_t[:, j * TQ_MLA:(j + 1) * TQ_MLA]

    rg = _dot_nt(h, wrg_ref[...]).astype(_BF16)
    gate_ref[0] = rg
    _retention_tile(rq, rk, rv, intra_ref, kend_ref, qstart_ref, decay_ref, state_ref, yret_ref)


def _retention_tile(q_heads, k_heads, v, intra_ref, kend_ref, qstart_ref, decay_ref, state_ref, o_ref):
    @pl.when(pl.program_id(1) == 0)
    def _():
        state_ref[...] = jnp.zeros(state_ref.shape, _F32)

    L = RET_CHUNK
    n_chunks = v.shape[0] // L
    units = [(c, hd) for c in range(n_chunks) for hd in range(N_RET_HEADS)]
    rows = lambda c: slice(c * L, (c + 1) * L)
    cols = lambda hd: slice(hd * RET_DV, (hd + 1) * RET_DV)
    scores, chunk_kv = {}, {}
    for c, hd in units:
        scores[c, hd] = _dot_nt(q_heads[hd][rows(c)], k_heads[hd][rows(c)])
    for c, hd in units:
        v_dec = (v[rows(c), cols(hd)].astype(_F32) * kend_ref[hd]).astype(_BF16)
        chunk_kv[c, hd] = _dot_tn(k_heads[hd][rows(c)], v_dec)
    prev_state = {}
    for hd in range(N_RET_HEADS):
        state = state_ref[hd]
        for c in range(n_chunks):
            prev_state[c, hd] = state.astype(_BF16)
            state = decay_ref[hd] * state + chunk_kv[c, hd]
        state_ref[hd] = state
    inner, cross = {}, {}
    for c, hd in units:
        inner[c, hd] = _dot((scores[c, hd] * intra_ref[hd]).astype(_BF16), v[rows(c), cols(hd)])
    for c, hd in units:
        cross[c, hd] = _dot(q_heads[hd][rows(c)], prev_state[c, hd])
    for c, hd in units:
        out = inner[c, hd] + cross[c, hd] * qstart_ref[hd]
        o_ref[0, rows(c), cols(hd)] = out.astype(o_ref.dtype)


def _permute_up_weights(w_uq, w_ukv):
    wq = w_uq.reshape(Q_LORA, N_MLA_HEADS, QK_NOPE + QK_ROPE)
    nope, r1, r2 = wq[..., :QK_NOPE], wq[..., QK_NOPE:QK_NOPE + ROPE_HALF], wq[..., QK_NOPE + ROPE_HALF:]
    z32 = jnp.zeros((Q_LORA, N_MLA_HEADS, HEAD_PAD - QK_NOPE - QK_ROPE), w_uq.dtype)
    wuq_p = jnp.concatenate([nope, r1, r2, z32], axis=-1).reshape(Q_LORA, MLA_QK_WIDTH).astype(_BF16)

    wkv = w_ukv.reshape(KV_LORA, N_MLA_HEADS, QK_NOPE + V_HEAD)
    zk = jnp.zeros((KV_LORA, N_MLA_HEADS, HEAD_PAD - QK_NOPE), w_ukv.dtype)
    wk = jnp.concatenate([wkv[..., :QK_NOPE], zk], axis=-1).reshape(KV_LORA, MLA_QK_WIDTH)
    wv_t = wkv[..., QK_NOPE:].reshape(KV_LORA, MLA_WIDTH).T
    return wuq_p, wk.astype(_BF16), wv_t.astype(_BF16)


def _in_proj(x, g_mix, wlat_t, wret_t, g_q_lat, wuq_p, g_kv_lat, wuk_p, wuvt_p, cos_t, sin_t, later_weights):
    b, s, d = x.shape
    nt = s // TM_IN
    per = TM_IN // TQ_MLA
    tok = lambda w: pl.BlockSpec((1, TM_IN, w), lambda bi, i: (bi, i, 0))
    tab = pl.BlockSpec((TM_IN // 2, LANES), lambda bi, i: (bi * nt + i, 0))
    bf = lambda w: jax.ShapeDtypeStruct((b, s, w), _BF16)
    ret_tables = _retention_tables()
    cast_in, cast_out, cast_shapes = _cast_plumbing(later_weights, (b, nt))
    in_specs = ([tok(d), _const_spec((1, d)), _const_spec(wlat_t.shape)] + [_const_spec(w.shape) for w in wret_t]
                + [_const_spec((1, Q_LORA)), _const_spec(wuq_p.shape), _const_spec((1, KV_LORA)),
                   _const_spec(wuk_p.shape), _const_spec(wuvt_p.shape), tab, tab]
                + [_const_spec(t.shape) for t in ret_tables])
    out_specs = (tok(MLA_QK_WIDTH), tok(MLA_QK_WIDTH),
                 pl.BlockSpec((1, per, MLA_WIDTH, TQ_MLA), lambda bi, i: (bi, i, 0, 0)), tok(RET_WIDTH), tok(RET_WIDTH))
    return pl.pallas_call(
        _with_casts(_in_proj_kernel, len(in_specs), len(out_specs), len(later_weights)),
        out_shape=(bf(MLA_QK_WIDTH), bf(MLA_QK_WIDTH),
                   jax.ShapeDtypeStruct((b, s // TQ_MLA, MLA_WIDTH, TQ_MLA), _BF16), bf(RET_WIDTH), bf(RET_WIDTH),
                   *cast_shapes),
        grid=(b, nt),
        in_specs=in_specs + cast_in,
        out_specs=(*out_specs, *cast_out),
        scratch_shapes=[pltpu.VMEM((N_RET_HEADS, RET_DK, RET_DV), _F32)],
        compiler_params=_params(("arbitrary", "arbitrary")),
        name="in_proj",
    )(x, g_mix.reshape(1, d), wlat_t, *wret_t, g_q_lat.reshape(1, Q_LORA), wuq_p, g_kv_lat.reshape(1, KV_LORA),
      wuk_p, wuvt_p, cos_t, sin_t, *ret_tables, *later_weights)


_MASK_VALUE = -0.7 * float(jnp.finfo(jnp.float32).max)
_MLA_AHEAD = 4
_MLA_TILES_PER_ITER = 5
_MLA_DENOM_ROWS = 16


def _mla_attn_kernel(qlo_ref, qhi_ref, k_ref, vt_ref, o_ref, q_ref, s_ref, m_ref, acc_ref):
    tq = TQ_MLA
    n_tiles = k_ref.shape[1] // tq
    lo = pl.program_id(1)
    hi = n_tiles - 1 - lo
    q_ref[0] = qlo_ref[0]
    q_ref[1] = qhi_ref[0]
    m_ref[...] = jnp.full(m_ref.shape, _MASK_VALUE, _F32)
    acc_ref[...] = jnp.zeros(acc_ref.shape, _F32)
    ones_rows = jnp.ones((_MLA_DENOM_ROWS, tq), _BF16)

    def step_args(t):
        sel = (t > lo).astype(jnp.int32)
        return sel, t - 1 - sel * lo

    def scores(sel, kk, hd):
        sl = slice(hd * HEAD_PAD, (hd + 1) * HEAD_PAD)
        ks = pl.multiple_of(kk * tq, tq)
        s_ref[hd] = _dot_nt(k_ref[0, pl.ds(ks, tq), sl], q_ref[sel, :, sl])

    def softmax(sel, hd, keep):
        row = slice(hd, hd + 1)

        def st():
            s = s_ref[hd]
            return s if keep is None else jnp.where(keep, s, _MASK_VALUE)

        m_prev = m_ref[sel, row, :]
        m_next = jnp.maximum(m_prev, jnp.max(st(), axis=0, keepdims=True))
        m_ref[sel, row, :] = m_next
        return jnp.exp2(m_prev - m_next), jnp.exp2(st() - m_next).astype(_BF16)

    def accumulate(sel, kk, hd, alpha, p):
        vt = jnp.concatenate([vt_ref[0, kk, hd * V_HEAD:(hd + 1) * V_HEAD, :], ones_rows], axis=0)
        acc_ref[sel, hd] = acc_ref[sel, hd] * alpha + _dot(vt, p)

    def tile(sel, kk, diagonal, nxt):
        keep = None
        if diagonal:
            key = lax.broadcasted_iota(jnp.int32, (tq, tq), 0)
            qry = lax.broadcasted_iota(jnp.int32, (tq, tq), 1)
            keep = key <= qry
        pending = None
        for hd in range(N_MLA_HEADS):
            ahead = hd + _MLA_AHEAD
            if ahead < N_MLA_HEADS:
                scores(sel, kk, ahead)
            elif nxt is not None:
                scores(nxt[0], nxt[1], ahead - N_MLA_HEADS)
            current = softmax(sel, hd, keep)
            if pending is not None:
                accumulate(sel, kk, hd - 1, *pending)
            pending = current
        accumulate(sel, kk, N_MLA_HEADS - 1, *pending)

    for hd in range(_MLA_AHEAD):
        scores(0, lo, hd)
    tile(0, lo, True, step_args(1))

    def body(it, carry):
        for u in range(_MLA_TILES_PER_ITER):
            t = 1 + it * _MLA_TILES_PER_ITER + u
            last = t + 1 == n_tiles
            sel_n, kk_n = step_args(t + 1)
            tile(*step_args(t), False, (jnp.where(last, 1, sel_n), jnp.where(last, hi, kk_n)))
        return carry

    lax.fori_loop(0, (n_tiles - 1) // _MLA_TILES_PER_ITER, body, 0)
    tile(1, hi, True, None)

    for sel, qt in ((0, lo), (1, hi)):
        out_t = jnp.concatenate([acc_ref[sel, hd, :V_HEAD, :] / acc_ref[sel, hd, V_HEAD:V_HEAD + 1, :]
                                 for hd in range(N_MLA_HEADS)], axis=0)
        o_ref[0, pl.ds(pl.multiple_of(qt * tq, tq), tq), :] = out_t.T.astype(o_ref.dtype)


def _mla_attn(q, k, vt, later_weights):
    b, s, w = q.shape
    nk = s // TQ_MLA
    assert nk % 2 == 0 and (nk - 1) % _MLA_TILES_PER_ITER == 0
    cast_in, cast_out, cast_shapes = _cast_plumbing(later_weights, (b, nk // 2))
    in_specs = [pl.BlockSpec((1, TQ_MLA, w), lambda bi, i: (bi, i, 0)),
                pl.BlockSpec((1, TQ_MLA, w), lambda bi, i: (bi, nk - 1 - i, 0)),
                pl.BlockSpec((1, s, w), lambda bi, i: (bi, 0, 0)),
                pl.BlockSpec((1, nk, MLA_WIDTH, TQ_MLA), lambda bi, i: (bi, 0, 0, 0))]
    return pl.pallas_call(
        _with_casts(_mla_attn_kernel, len(in_specs), 1, len(later_weights)),
        out_shape=(jax.ShapeDtypeStruct((b, s, MLA_WIDTH), _BF16), *cast_shapes),
        grid=(b, nk // 2),
        in_specs=in_specs + cast_in,
        out_specs=(pl.BlockSpec((1, s, MLA_WIDTH), lambda bi, i: (bi, 0, 0)), *cast_out),
        scratch_shapes=[pltpu.VMEM((2, TQ_MLA, w), _BF16),
                        pltpu.VMEM((N_MLA_HEADS, TQ_MLA, TQ_MLA), _F32),
                        pltpu.VMEM((2, N_MLA_HEADS, TQ_MLA), _F32),
                        pltpu.VMEM((2, N_MLA_HEADS, V_HEAD + _MLA_DENOM_ROWS, TQ_MLA), _F32)],
        compiler_params=_params(("arbitrary", "arbitrary")),
        name="mla_attn",
    )(q, q, k, vt, *later_weights)


def _retention_tables():
    h, L = N_RET_HEADS, RET_CHUNK
    log_gamma = jnp.log(1.0 - 2.0 ** (-5.0 - jnp.arange(h, dtype=_F32)))
    j = jnp.arange(L, dtype=_F32)
    diff = j[:, None] - j[None, :]
    intra = jnp.where(diff[None] >= 0,
                      jnp.exp(jnp.maximum(diff, 0.0)[None] * log_gamma[:, None, None]), 0.0)
    rowb = lambda t: jnp.broadcast_to(t.T[:, :, None], (h, L, LANES))
    k_to_end = jnp.exp((L - 1 - j)[:, None] * log_gamma[None, :])
    q_from_start = jnp.exp((j + 1)[:, None] * log_gamma[None, :])
    chunk_decay = jnp.broadcast_to(jnp.exp(L * log_gamma)[:, None, None], (h, RET_DK, RET_DV))
    return intra, rowb(k_to_end), rowb(q_from_start), chunk_decay


def _mix_xattn_kernel(x_ref, ymla_ref, yret_ref, gate_ref, wout_ref, gx_ref, wqk_ref, wvo_ref, o_ref):
    half = TM_MIX // 2
    rows = [slice(0, half), slice(half, TM_MIX)]
    mem_len = wqk_ref.shape[2] // N_XATTN_HEADS

    def retention_out(r):
        heads = []
        for hd in range(N_RET_HEADS):
            c = slice(hd * RET_DV, (hd + 1) * RET_DV)
            out = yret_ref[0, r, c].astype(_F32)
            mu = jnp.mean(out, axis=-1, keepdims=True)
            cen = out - mu
            var = jnp.mean(cen * cen, axis=-1, keepdims=True)
            g = gate_ref[0, r, c].astype(_F32)
            heads.append((cen * lax.rsqrt(var + EPS) * (g * jax.nn.sigmoid(g))).astype(_BF16))
        return jnp.concatenate(heads, axis=1)

    def out_proj(r):
        return (x_ref[0, r, :] + _dot(ymla_ref[0, r, :], wout_ref[:MLA_WIDTH, :])
                + _dot(retention_out(r), wout_ref[MLA_WIDTH:, :]))

    def scores(x1):
        h = _rms(x1, gx_ref[...]).astype(_BF16)
        return _dot(h, wqk_ref[0])

    def softmax(s):
        heads = []
        for hd in range(N_XATTN_HEADS):
            blk = s[:, hd * mem_len:(hd + 1) * mem_len]
            p = jnp.exp(blk - jnp.max(blk, axis=-1, keepdims=True))
            heads.append((p / jnp.sum(p, axis=-1, keepdims=True)).astype(_BF16))
        return jnp.concatenate(heads, axis=1)

    x1 = [out_proj(r) for r in rows]
    s0 = scores(x1[0])
    s1 = scores(x1[1])
    p0 = softmax(s0)
    o_ref[0, rows[0], :] = x1[0] + _dot(p0, wvo_ref[0])
    p1 = softmax(s1)
    o_ref[0, rows[1], :] = x1[1] + _dot(p1, wvo_ref[0])


def _mix_xattn(x, y_mla, y_ret, ret_gate, w_out, g_xattn, wqk, wvo, later_weights):
    b, s, d = x.shape
    tok = lambda w: pl.BlockSpec((1, TM_MIX, w), lambda bi, i: (bi, i, 0))
    per_batch = lambda a: pl.BlockSpec((1,) + a.shape[1:], lambda bi, i: (bi, 0, 0))
    cast_in, cast_out, cast_shapes = _cast_plumbing(later_weights, (b, s // TM_MIX))
    in_specs = [tok(d), tok(MLA_WIDTH), tok(RET_WIDTH), tok(RET_WIDTH), _const_spec(w_out.shape), _const_spec((1, d)),
                per_batch(wqk), per_batch(wvo)]
    return pl.pallas_call(
        _with_casts(_mix_xattn_kernel, len(in_specs), 1, len(later_weights)),
        out_shape=(jax.ShapeDtypeStruct((b, s, d), _F32), *cast_shapes),
        grid=(b, s // TM_MIX),
        in_specs=in_specs + cast_in,
        out_specs=(tok(d), *cast_out),
        compiler_params=_params(("arbitrary", "arbitrary")),
        name="mix_xattn",
    )(x, y_mla, y_ret, ret_gate, w_out, g_xattn.reshape(1, d), wqk, wvo, *later_weights)


def _conv_ffn_kernel(x_ref, g_ref, win_ref, cw_ref, cb_ref, wout_ref, gfin_ref, o_ref, gate_ref):
    halo = SUBLANES
    half = TM_FFN // 2

    @pl.when(pl.program_id(1) == 0)
    def _():
        gate_ref[:halo, :] = jnp.zeros((halo, D_FF), _F32)

    rows = [slice(0, half), slice(half, TM_FFN)]
    xs = [x_ref[0, r, :] for r in rows]
    hs = [_rms(x, g_ref[...]).astype(_BF16) for x in xs]

    def project(i):
        gate = _dot(hs[i], win_ref[:, :D_FF])
        up = _dot(hs[i], win_ref[:, D_FF:])
        gate_ref[halo + i * half:halo + (i + 1) * half, :] = gate
        return gate, up

    def activate(i, gate, up):
        conv = cb_ref[...] + gate * cw_ref[CONV_W - 1:CONV_W, :]
        for tap in range(CONV_W - 1):
            back = CONV_W - 1 - tap
            lo = halo + i * half - back
            conv = conv + gate_ref[lo:lo + half, :] * cw_ref[tap:tap + 1, :]
        return (conv * jax.nn.sigmoid(conv) * up).astype(_BF16)

    def finish(i, act):
        x3 = xs[i] + _dot(act, wout_ref[...])
        o_ref[0, rows[i], :] = _rms(x3, gfin_ref[...])

    g0, u0 = project(0)
    g1, u1 = project(1)
    a0 = activate(0, g0, u0)
    finish(0, a0)
    a1 = activate(1, g1, u1)
    finish(1, a1)
    gate_ref[:halo, :] = gate_ref[TM_FFN:, :]


def _conv_ffn(x, g_ffn, w_ffn_in, conv_w, conv_b, w_ffn_out, g_final):
    b, s, d = x.shape
    tok = pl.BlockSpec((1, TM_FFN, d), lambda bi, i: (bi, i, 0))
    return pl.pallas_call(
        _conv_ffn_kernel,
        out_shape=jax.ShapeDtypeStruct((b, s, d), _F32),
        grid=(b, s // TM_FFN),
        in_specs=[tok, _const_spec((1, d)), _const_spec(w_ffn_in.shape), _const_spec(conv_w.shape),
                  _const_spec((1, D_FF)), _const_spec(w_ffn_out.shape), _const_spec((1, d))],
        out_specs=tok,
        scratch_shapes=[pltpu.VMEM((TM_FFN + SUBLANES, D_FF), _F32)],
        compiler_params=_params(("arbitrary", "arbitrary")),
        name="conv_ffn",
    )(x, g_ffn.reshape(1, d), w_ffn_in, conv_w, conv_b.reshape(1, D_FF), w_ffn_out, g_final.reshape(1, d))


def kernel(x, mem, positions, g_mix, w_in, g_q_lat, w_uq, g_kv_lat, w_ukv, w_out, g_xattn, g_mem, w_xq,
           w_xkv, w_xo, g_ffn, w_ffn_in, conv_w, conv_b, w_ffn_out, g_final):
    assert w_in.shape[0] == 1, "one layer supported"
    l = 0
    w_in_t = w_in[l].T
    cos_t, sin_t, wlat_t, *wret_t, wqk, wvo = _prep(positions, w_in_t, mem, g_mem[l], w_xkv[l], w_xq[l], w_xo[l])
    wuq_p, wuk_p, wuvt_p = _permute_up_weights(w_uq[l], w_ukv[l])
    q, k, vt, y_ret, ret_gate, w_out_b = _in_proj(
        x, g_mix[l], wlat_t, wret_t, g_q_lat[l], wuq_p, g_kv_lat[l], wuk_p, wuvt_p, cos_t, sin_t, (w_out[l],))
    y_mla, w_ffn_in_b = _mla_attn(q, k, vt, (w_ffn_in[l],))
    x, w_ffn_out_b = _mix_xattn(x, y_mla, y_ret, ret_gate, w_out_b, g_xattn[l], wqk, wvo, (w_ffn_out[l],))
    return _conv_ffn(x, g_ffn[l], w_ffn_in_b, conv_w[l], conv_b[l], w_ffn_out_b, g_final)
```

```python
import functools
import math

import jax
import jax.numpy as jnp
from jax import lax
from jax.experimental import pallas as pl
from jax.experimental.pallas import tpu as pltpu

D_MODEL = 1024
EPS = 1e-6
ROPE_BASE = 10000.0
N_MLA_HEADS = 8
QK_NOPE = 64
QK_ROPE = 32
V_HEAD = 64
Q_LORA = 256
KV_LORA = 128
N_RET_HEADS = 4
RET_DK = 128
RET_DV = 128
RET_CHUNK = 128
MLA_WIDTH = N_MLA_HEADS * V_HEAD
RET_WIDTH = N_RET_HEADS * RET_DV
N_XATTN_HEADS = 4
XATTN_HEAD = D_MODEL // N_XATTN_HEADS
D_FF = 2816
CONV_W = 3

LANES = 128
SUBLANES = 8
_BF16_ROWS = 16
VMEM_LIMIT = 56 * 1024 * 1024

HEAD_PAD = LANES
ROPE_HALF = QK_ROPE // 2
MLA_QK_WIDTH = N_MLA_HEADS * HEAD_PAD
IN_LAT = Q_LORA + KV_LORA + HEAD_PAD

TM_TABLE = 1024
TM_IN = 1024
TQ_MLA = 256
TM_MIX = 1024
TM_FFN = 512

_BF16 = jnp.bfloat16
_F32 = jnp.float32


def _dot(a, b):
    return jnp.dot(a, b, preferred_element_type=_F32)


def _dot_nt(a, b):
    return lax.dot_general(a, b, (((1,), (1,)), ((), ())), preferred_element_type=_F32)


def _dot_tn(a, b):
    return lax.dot_general(a, b, (((0,), (0,)), ((), ())), preferred_element_type=_F32)


def _rms(x, g):
    inv = lax.rsqrt(jnp.mean(x * x, axis=-1, keepdims=True) + EPS)
    return x * inv * g


def _const_spec(shape):
    nd = len(shape)
    return pl.BlockSpec(shape, lambda *_: (0,) * nd, pipeline_mode=pl.Buffered(1))


def _params(sem):
    return pltpu.CompilerParams(dimension_semantics=sem, vmem_limit_bytes=VMEM_LIMIT)


def _with_casts(body, n_in, n_out, n_cast):
    def kern(*refs):
        ins, rest = refs[:n_in], refs[n_in:]
        cast_in, rest = rest[:n_cast], rest[n_cast:]
        outs, rest = rest[:n_out], rest[n_out:]
        cast_out, scratch = rest[:n_cast], rest[n_cast:]
        for src, dst in zip(cast_in, cast_out):
            dst[...] = src[...].astype(dst.dtype)
        body(*ins, *outs, *scratch)
    return kern


def _cast_plumbing(weights, grid):
    steps = math.prod(grid)
    if len(grid) == 1:
        block_of = lambda i: (i, 0)
    else:
        block_of = lambda bi, i: (bi * grid[1] + i, 0)
    in_specs, out_specs, out_shapes = [], [], []
    for w in weights:
        rows, cols = w.shape
        assert rows % steps == 0 and (rows // steps) % _BF16_ROWS == 0, (w.shape, steps)
        spec = pl.BlockSpec((rows // steps, cols), block_of)
        in_specs.append(spec)
        out_specs.append(spec)
        out_shapes.append(jax.ShapeDtypeStruct(w.shape, _BF16))
    return in_specs, out_specs, out_shapes


def _prep_kernel(pos_ref, inv_ref, wlat_ref, wq_ref, wk_ref, wv_ref, wg_ref, mem_ref, gmem_ref, wxk_ref, wxv_ref,
                 wxq_ref, wxo_ref, cos_ref, sin_ref, wlat_out, wq_out, wk_out, wv_out, wg_out, wqk_ref, wvo_ref,
                 *, lat_blocks, n_batch):
    half_rows = TM_IN // 2 // LANES
    lane = lax.broadcasted_iota(jnp.int32, (LANES, LANES), 1)
    column = lambda r: jnp.broadcast_to(pos_ref[r:r + 1, :], (LANES, LANES)).T
    for tile in range(TM_TABLE // TM_IN):
        for rr in range(half_rows):
            r0 = tile * 2 * half_rows + rr
            pos = jnp.where(lane < RET_DK // 2, column(r0), column(r0 + half_rows))
            ang = pos * inv_ref[...]
            out = slice((tile * half_rows + rr) * LANES, (tile * half_rows + rr + 1) * LANES)
            cos_ref[out, :] = jnp.cos(ang)
            sin_ref[out, :] = jnp.sin(ang)
    for src, dst in ((wq_ref, wq_out), (wk_ref, wk_out), (wv_ref, wv_out), (wg_ref, wg_out)):
        dst[...] = src[...].astype(_BF16)
    wlat_out[...] = jnp.where(pl.program_id(0) < lat_blocks, wlat_ref[...], 0.0).astype(_BF16)

    mem_n = _rms(mem_ref[pl.program_id(0) % n_batch], gmem_ref[...]).astype(_BF16)
    k_h = _dot(mem_n, wxk_ref[...].astype(_BF16)).astype(_BF16)
    v_h = _dot(mem_n, wxv_ref[...].astype(_BF16)).astype(_BF16)
    wqk_ref[0] = (_dot_nt(wxq_ref[...].astype(_BF16), k_h) * (1.0 / math.sqrt(XATTN_HEAD))).astype(_BF16)
    wvo_ref[0] = _dot(v_h, wxo_ref[...].astype(_BF16)).astype(_BF16)


def _prep(positions, w_in_t, mem, g_mem, w_xkv, w_xq, w_xo):
    t = positions.size
    steps = t // TM_TABLE
    n_in, d = w_in_t.shape
    b, m, _ = mem.shape
    assert steps == b * N_XATTN_HEADS
    row0 = Q_LORA + KV_LORA + QK_ROPE
    rows = RET_WIDTH // steps
    assert RET_WIDTH % steps == 0 and rows % _BF16_ROWS == 0 and row0 % rows == 0
    assert n_in == row0 + 4 * RET_WIDTH
    pos = positions.astype(_F32).reshape(t // LANES, LANES)
    f_ret = 1.0 / (ROPE_BASE ** (jnp.arange(0, RET_DK, 2, dtype=_F32) / RET_DK))
    inv = jnp.concatenate([f_ret, f_ret])
    group = lambda j: pl.BlockSpec((rows, d), lambda i: (row0 // rows + j * steps + i, 0))
    lat_blocks = row0 // rows
    kr_dst = (Q_LORA + KV_LORA + QK_NOPE) // rows
    assert QK_ROPE == rows and IN_LAT // rows == steps and (Q_LORA + KV_LORA) % rows == 0
    lat_src = pl.BlockSpec((rows, d), lambda i: (jnp.minimum(i, lat_blocks - 1), 0))
    lat_dst = pl.BlockSpec((rows, d), lambda i: (jnp.where(i < lat_blocks - 1, i, jnp.where(
        i == lat_blocks - 1, kr_dst, jnp.where(i <= kr_dst, i - 1, i))), 0))
    table = pl.BlockSpec((TM_TABLE // 2, LANES), lambda i: (i, 0))
    w_out = pl.BlockSpec((rows, d), lambda i: (i, 0))
    nh = N_XATTN_HEADS
    head_cols = lambda off: pl.BlockSpec((d, XATTN_HEAD), lambda i: (0, off + i // b))
    return pl.pallas_call(
        functools.partial(_prep_kernel, lat_blocks=lat_blocks, n_batch=b),
        out_shape=(jax.ShapeDtypeStruct((t // 2, LANES), _F32),) * 2
                  + (jax.ShapeDtypeStruct((IN_LAT, d), _BF16),)
                  + (jax.ShapeDtypeStruct((RET_WIDTH, d), _BF16),) * 4
                  + (jax.ShapeDtypeStruct((b, d, nh * m), _BF16), jax.ShapeDtypeStruct((b, nh * m, d), _BF16)),
        grid=(steps,),
        in_specs=[pl.BlockSpec((TM_TABLE // LANES, LANES), lambda i: (i, 0)),
                  pl.BlockSpec((1, LANES), lambda i: (0, 0)), lat_src, group(0), group(1), group(2), group(3),
                  _const_spec(mem.shape), _const_spec((1, d)),
                  head_cols(0), head_cols(nh), head_cols(0),
                  pl.BlockSpec((XATTN_HEAD, d), lambda i: (i // b, 0))],
        out_specs=(table, table, lat_dst, w_out, w_out, w_out, w_out,
                   pl.BlockSpec((1, d, m), lambda i: (i % b, 0, i // b)),
                   pl.BlockSpec((1, m, d), lambda i: (i % b, i // b, 0))),
        compiler_params=_params(("arbitrary",)),
        name="prep",
    )(pos, inv.reshape(1, LANES), w_in_t, w_in_t, w_in_t, w_in_t, w_in_t, mem, g_mem.reshape(1, d), w_xkv, w_xkv,
      w_xq, w_xo)


def _in_proj_kernel(x_ref, gmix_ref, wlat_ref, wrq_ref, wrk_ref, wrv_ref, wrg_ref, gq_ref, wuq_ref, gkv_ref, wuk_ref, wuvt_ref, cos_ref, sin_ref,
                    intra_ref, kend_ref, qstart_ref, decay_ref,
                    q_ref, k_ref, vt_ref, yret_ref, gate_ref, state_ref):
    h = _rms(x_ref[0], gmix_ref[...]).astype(_BF16)
    lane = lax.broadcasted_iota(jnp.int32, (TM_IN, LANES), 1)
    low = lane < RET_DK // 2

    lo = lax.broadcasted_iota(jnp.int32, (TM_IN // 2, LANES), 1) < RET_DK // 2

    def unpack(t):
        t_roll = pltpu.roll(t, RET_DK // 2, axis=1)
        return jnp.concatenate([jnp.where(lo, t, t_roll), jnp.where(lo, t_roll, t)], axis=0)

    cos_r, sin_full = unpack(cos_ref[...]), unpack(sin_ref[...])
    sin_r = jnp.where(low, -sin_full, sin_full)
    ratio = (RET_DK // 2) // ROPE_HALF
    rope_lane = (lane >= QK_NOPE) & (lane < QK_NOPE + QK_ROPE)
    src = jnp.where(lane < QK_NOPE + ROPE_HALF, lane - QK_NOPE, lane - QK_NOPE - ROPE_HALF) * ratio
    src = jnp.where(rope_lane, src, 0)
    g_cos = jnp.take_along_axis(cos_r, src, axis=1)
    g_sin = jnp.take_along_axis(sin_full, src, axis=1)
    cos_m = jnp.where(rope_lane, g_cos, 1.0)
    sin_m = jnp.where(rope_lane, jnp.where(lane < QK_NOPE + ROPE_HALF, -g_sin, g_sin), 0.0)

    first_half = lane < QK_NOPE + ROPE_HALF

    def swap_halves(t):
        return jnp.where(first_half, pltpu.roll(t, LANES - ROPE_HALF, axis=1), pltpu.roll(t, ROPE_HALF, axis=1))

    lat = _dot_nt(h, wlat_ref[...])
    c_q = lat[:, :Q_LORA]
    c_kv = lat[:, Q_LORA:Q_LORA + KV_LORA]
    kr = lat[:, Q_LORA + KV_LORA:IN_LAT]
    k_rope = kr * cos_m + swap_halves(kr) * sin_m

    def ret_rope(w_ref, mult):
        r = _dot_nt(h, w_ref[...])
        heads = []
        for hd in range(N_RET_HEADS):
            rh = r[:, hd * RET_DK:(hd + 1) * RET_DK]
            roped = rh * cos_r + pltpu.roll(rh, RET_DK // 2, axis=1) * sin_r
            if mult is not None:
                roped = roped * mult
            heads.append(roped.astype(_BF16))
        return heads

    rq = ret_rope(wrq_ref, None)
    rk = ret_rope(wrk_ref, RET_DK ** -0.5)
    rv = _dot_nt(h, wrv_ref[...]).astype(_BF16)

    cqn = _rms(c_q, gq_ref[...]).astype(_BF16)
    q = _dot(cqn, wuq_ref[...])
    scale = math.log2(math.e) / math.sqrt(QK_NOPE + QK_ROPE)
    for hd in range(N_MLA_HEADS):
        sl = slice(hd * HEAD_PAD, (hd + 1) * HEAD_PAD)
        q_ref[0, :, sl] = ((q[:, sl] * cos_m + swap_halves(q[:, sl]) * sin_m) * scale).astype(_BF16)

    ckvn = _rms(c_kv, gkv_ref[...]).astype(_BF16)
    k_nope = _dot(ckvn, wuk_ref[...])
    for hd in range(N_MLA_HEADS):
        sl = slice(hd * HEAD_PAD, (hd + 1) * HEAD_PAD)
        k_ref[0, :, sl] = (k_nope[:, sl] + k_rope).astype(_BF16)
    v_t = _dot_nt(wuvt_ref[...], ckvn).astype(_BF16)
    for j in range(TM_IN // TQ_MLA):
        vt_ref[0, j] = v_t[:, j * TQ_MLA:(j + 1) * TQ_MLA]

    rg = _dot_nt(h, wrg_ref[...]).astype(_BF16)
    gate_ref[0] = rg
    _retention_tile(rq, rk, rv, intra_ref, kend_ref, qstart_ref, decay_ref, state_ref, yret_ref)


def _retention_tile(q_heads, k_heads, v, intra_ref, kend_ref, qstart_ref, decay_ref, state_ref, o_ref):
    @pl.when(pl.program_id(1) == 0)
    def _():
        state_ref[...] = jnp.zeros(state_ref.shape, _F32)

    L = RET_CHUNK
    n_chunks = v.shape[0] // L
    units = [(c, hd) for c in range(n_chunks) for hd in range(N_RET_HEADS)]
    rows = lambda c: slice(c * L, (c + 1) * L)
    cols = lambda hd: slice(hd * RET_DV, (hd + 1) * RET_DV)
    scores, chunk_kv = {}, {}
    for c, hd in units:
        scores[c, hd] = _dot_nt(q_heads[hd][rows(c)], k_heads[hd][rows(c)])
    for c, hd in units:
        v_dec = (v[rows(c), cols(hd)].astype(_F32) * kend_ref[hd]).astype(_BF16)
        chunk_kv[c, hd] = _dot_tn(k_heads[hd][rows(c)], v_dec)
    prev_state = {}
    for hd in range(N_RET_HEADS):
        state = state_ref[hd]
        for c in range(n_chunks):
            prev_state[c, hd] = state.astype(_BF16)
            state = decay_ref[hd] * state + chunk_kv[c, hd]
        state_ref[hd] = state
    inner, cross = {}, {}
    for c, hd in units:
        inner[c, hd] = _dot((scores[c, hd] * intra_ref[hd]).astype(_BF16), v[rows(c), cols(hd)])
    for c, hd in units:
        cross[c, hd] = _dot(q_heads[hd][rows(c)], prev_state[c, hd])
    for c, hd in units:
        out = inner[c, hd] + cross[c, hd] * qstart_ref[hd]
        o_ref[0, rows(c), cols(hd)] = out.astype(o_ref.dtype)


def _permute_up_weights(w_uq, w_ukv):
    wq = w_uq.reshape(Q_LORA, N_MLA_HEADS, QK_NOPE + QK_ROPE)
    nope, r1, r2 = wq[..., :QK_NOPE], wq[..., QK_NOPE:QK_NOPE + ROPE_HALF], wq[..., QK_NOPE + ROPE_HALF:]
    z32 = jnp.zeros((Q_LORA, N_MLA_HEADS, HEAD_PAD - QK_NOPE - QK_ROPE), w_uq.dtype)
    wuq_p = jnp.concatenate([nope, r1, r2, z32], axis=-1).reshape(Q_LORA, MLA_QK_WIDTH).astype(_BF16)

    wkv = w_ukv.reshape(KV_LORA, N_MLA_HEADS, QK_NOPE + V_HEAD)
    zk = jnp.zeros((KV_LORA, N_MLA_HEADS, HEAD_PAD - QK_NOPE), w_ukv.dtype)
    wk = jnp.concatenate([wkv[..., :QK_NOPE], zk], axis=-1).reshape(KV_LORA, MLA_QK_WIDTH)
    wv_t = wkv[..., QK_NOPE:].reshape(KV_LORA, MLA_WIDTH).T
    return wuq_p, wk.astype(_BF16), wv_t.astype(_BF16)


def _in_proj(x, g_mix, wlat_t, wret_t, g_q_lat, wuq_p, g_kv_lat, wuk_p, wuvt_p, cos_t, sin_t, later_weights):
    b, s, d = x.shape
    nt = s // TM_IN
    per = TM_IN // TQ_MLA
    tok = lambda w: pl.BlockSpec((1, TM_IN, w), lambda bi, i: (bi, i, 0))
    tab = pl.BlockSpec((TM_IN // 2, LANES), lambda bi, i: (bi * nt + i, 0))
    bf = lambda w: jax.ShapeDtypeStruct((b, s, w), _BF16)
    ret_tables = _retention_tables()
    cast_in, cast_out, cast_shapes = _cast_plumbing(later_weights, (b, nt))
    in_specs = ([tok(d), _const_spec((1, d)), _const_spec(wlat_t.shape)] + [_const_spec(w.shape) for w in wret_t]
                + [_const_spec((1, Q_LORA)), _const_spec(wuq_p.shape), _const_spec((1, KV_LORA)),
                   _const_spec(wuk_p.shape), _const_spec(wuvt_p.shape), tab, tab]
                + [_const_spec(t.shape) for t in ret_tables])
    out_specs = (tok(MLA_QK_WIDTH), tok(MLA_QK_WIDTH),
                 pl.BlockSpec((1, per, MLA_WIDTH, TQ_MLA), lambda bi, i: (bi, i, 0, 0)), tok(RET_WIDTH), tok(RET_WIDTH))
    return pl.pallas_call(
        _with_casts(_in_proj_kernel, len(in_specs), len(out_specs), len(later_weights)),
        out_shape=(bf(MLA_QK_WIDTH), bf(MLA_QK_WIDTH),
                   jax.ShapeDtypeStruct((b, s // TQ_MLA, MLA_WIDTH, TQ_MLA), _BF16), bf(RET_WIDTH), bf(RET_WIDTH),
                   *cast_shapes),
        grid=(b, nt),
        in_specs=in_specs + cast_in,
        out_specs=(*out_specs, *cast_out),
        scratch_shapes=[pltpu.VMEM((N_RET_HEADS, RET_DK, RET_DV), _F32)],
        compiler_params=_params(("arbitrary", "arbitrary")),
        name="in_proj",
    )(x, g_mix.reshape(1, d), wlat_t, *wret_t, g_q_lat.reshape(1, Q_LORA), wuq_p, g_kv_lat.reshape(1, KV_LORA),
      wuk_p, wuvt_p, cos_t, sin_t, *ret_tables, *later_weights)


_MASK_VALUE = -0.7 * float(jnp.finfo(jnp.float32).max)
_MLA_AHEAD = 4
_MLA_PV_LATE = 2
_MLA_TILES_PER_ITER = 5
_MLA_DENOM_ROWS = 16


def _mla_attn_kernel(qlo_ref, qhi_ref, k_ref, vt_ref, o_ref, q_ref, s_ref, m_ref, acc_ref):
    tq = TQ_MLA
    n_tiles = k_ref.shape[1] // tq
    lo = pl.program_id(1)
    hi = n_tiles - 1 - lo
    q_ref[0] = qlo_ref[0]
    q_ref[1] = qhi_ref[0]
    m_ref[...] = jnp.full(m_ref.shape, _MASK_VALUE, _F32)
    acc_ref[...] = jnp.zeros(acc_ref.shape, _F32)
    ones_rows = jnp.ones((_MLA_DENOM_ROWS, tq), _BF16)

    def step_args(t):
        sel = (t > lo).astype(jnp.int32)
        return sel, t - 1 - sel * lo

    def scores(sel, kk, hd):
        sl = slice(hd * HEAD_PAD, (hd + 1) * HEAD_PAD)
        ks = pl.multiple_of(kk * tq, tq)
        s_ref[hd] = _dot_nt(k_ref[0, pl.ds(ks, tq), sl], q_ref[sel, :, sl])

    def softmax(sel, hd, keep):
        row = slice(hd, hd + 1)

        def st():
            s = s_ref[hd]
            return s if keep is None else jnp.where(keep, s, _MASK_VALUE)

        m_prev = m_ref[sel, row, :]
        m_next = jnp.maximum(m_prev, jnp.max(st(), axis=0, keepdims=True))
        m_ref[sel, row, :] = m_next
        return jnp.exp2(m_prev - m_next), jnp.exp2(st() - m_next).astype(_BF16)

    def accumulate(sel, kk, hd, alpha, p):
        vt = jnp.concatenate([vt_ref[0, kk, hd * V_HEAD:(hd + 1) * V_HEAD, :], ones_rows], axis=0)
        acc_ref[sel, hd] = acc_ref[sel, hd] * alpha + _dot(vt, p)

    def tile(sel, kk, diagonal, nxt):
        keep = None
        if diagonal:
            key = lax.broadcasted_iota(jnp.int32, (tq, tq), 0)
            qry = lax.broadcasted_iota(jnp.int32, (tq, tq), 1)
            keep = key <= qry
        pending = []
        for hd in range(N_MLA_HEADS):
            ahead = hd + _MLA_AHEAD
            if ahead < N_MLA_HEADS:
                scores(sel, kk, ahead)
            elif nxt is not None:
                scores(nxt[0], nxt[1], ahead - N_MLA_HEADS)
            pending.append((hd, softmax(sel, hd, keep)))
            if len(pending) > _MLA_PV_LATE:
                h0, args = pending.pop(0)
                accumulate(sel, kk, h0, *args)
        for h0, args in pending:
            accumulate(sel, kk, h0, *args)

    for hd in range(_MLA_AHEAD):
        scores(0, lo, hd)
    tile(0, lo, True, step_args(1))

    def body(it, carry):
        for u in range(_MLA_TILES_PER_ITER):
            t = 1 + it * _MLA_TILES_PER_ITER + u
            last = t + 1 == n_tiles
            sel_n, kk_n = step_args(t + 1)
            tile(*step_args(t), False, (jnp.where(last, 1, sel_n), jnp.where(last, hi, kk_n)))
        return carry

    lax.fori_loop(0, (n_tiles - 1) // _MLA_TILES_PER_ITER, body, 0)
    tile(1, hi, True, None)

    for sel, qt in ((0, lo), (1, hi)):
        out_t = jnp.concatenate([acc_ref[sel, hd, :V_HEAD, :] / acc_ref[sel, hd, V_HEAD:V_HEAD + 1, :]
                                 for hd in range(N_MLA_HEADS)], axis=0)
        o_ref[0, pl.ds(pl.multiple_of(qt * tq, tq), tq), :] = out_t.T.astype(o_ref.dtype)


def _mla_attn(q, k, vt, later_weights):
    b, s, w = q.shape
    nk = s // TQ_MLA
    assert nk % 2 == 0 and (nk - 1) % _MLA_TILES_PER_ITER == 0
    cast_in, cast_out, cast_shapes = _cast_plumbing(later_weights, (b, nk // 2))
    in_specs = [pl.BlockSpec((1, TQ_MLA, w), lambda bi, i: (bi, i, 0)),
                pl.BlockSpec((1, TQ_MLA, w), lambda bi, i: (bi, nk - 1 - i, 0)),
                pl.BlockSpec((1, s, w), lambda bi, i: (bi, 0, 0)),
                pl.BlockSpec((1, nk, MLA_WIDTH, TQ_MLA), lambda bi, i: (bi, 0, 0, 0))]
    return pl.pallas_call(
        _with_casts(_mla_attn_kernel, len(in_specs), 1, len(later_weights)),
        out_shape=(jax.ShapeDtypeStruct((b, s, MLA_WIDTH), _BF16), *cast_shapes),
        grid=(b, nk // 2),
        in_specs=in_specs + cast_in,
        out_specs=(pl.BlockSpec((1, s, MLA_WIDTH), lambda bi, i: (bi, 0, 0)), *cast_out),
        scratch_shapes=[pltpu.VMEM((2, TQ_MLA, w), _BF16),
                        pltpu.VMEM((N_MLA_HEADS, TQ_MLA, TQ_MLA), _F32),
                        pltpu.VMEM((2, N_MLA_HEADS, TQ_MLA), _F32),
                        pltpu.VMEM((2, N_MLA_HEADS, V_HEAD + _MLA_DENOM_ROWS, TQ_MLA), _F32)],
        compiler_params=_params(("arbitrary", "arbitrary")),
        name="mla_attn",
    )(q, q, k, vt, *later_weights)


def _retention_tables():
    h, L = N_RET_HEADS, RET_CHUNK
    log_gamma = jnp.log(1.0 - 2.0 ** (-5.0 - jnp.arange(h, dtype=_F32)))
    j = jnp.arange(L, dtype=_F32)
    diff = j[:, None] - j[None, :]
    intra = jnp.where(diff[None] >= 0,
                      jnp.exp(jnp.maximum(diff, 0.0)[None] * log_gamma[:, None, None]), 0.0)
    rowb = lambda t: jnp.broadcast_to(t.T[:, :, None], (h, L, LANES))
    k_to_end = jnp.exp((L - 1 - j)[:, None] * log_gamma[None, :])
    q_from_start = jnp.exp((j + 1)[:, None] * log_gamma[None, :])
    chunk_decay = jnp.broadcast_to(jnp.exp(L * log_gamma)[:, None, None], (h, RET_DK, RET_DV))
    return intra, rowb(k_to_end), rowb(q_from_start), chunk_decay


def _mix_xattn_kernel(x_ref, ymla_ref, yret_ref, gate_ref, wout_ref, gx_ref, wqk_ref, wvo_ref, o_ref):
    half = TM_MIX // 2
    rows = [slice(0, half), slice(half, TM_MIX)]
    mem_len = wqk_ref.shape[2] // N_XATTN_HEADS

    def retention_out(r):
        heads = []
        for hd in range(N_RET_HEADS):
            c = slice(hd * RET_DV, (hd + 1) * RET_DV)
            out = yret_ref[0, r, c].astype(_F32)
            mu = jnp.mean(out, axis=-1, keepdims=True)
            cen = out - mu
            var = jnp.mean(cen * cen, axis=-1, keepdims=True)
            g = gate_ref[0, r, c].astype(_F32)
            heads.append((cen * lax.rsqrt(var + EPS) * (g * jax.nn.sigmoid(g))).astype(_BF16))
        return jnp.concatenate(heads, axis=1)

    def out_proj(r):
        return (x_ref[0, r, :] + _dot(ymla_ref[0, r, :], wout_ref[:MLA_WIDTH, :])
                + _dot(retention_out(r), wout_ref[MLA_WIDTH:, :]))

    def scores(x1):
        h = _rms(x1, gx_ref[...]).astype(_BF16)
        return _dot(h, wqk_ref[0])

    def softmax(s):
        heads = []
        for hd in range(N_XATTN_HEADS):
            blk = s[:, hd * mem_len:(hd + 1) * mem_len]
            p = jnp.exp(blk - jnp.max(blk, axis=-1, keepdims=True))
            heads.append((p / jnp.sum(p, axis=-1, keepdims=True)).astype(_BF16))
        return jnp.concatenate(heads, axis=1)

    x1 = [out_proj(r) for r in rows]
    s0 = scores(x1[0])
    s1 = scores(x1[1])
    p0 = softmax(s0)
    o_ref[0, rows[0], :] = x1[0] + _dot(p0, wvo_ref[0])
    p1 = softmax(s1)
    o_ref[0, rows[1], :] = x1[1] + _dot(p1, wvo_ref[0])


def _mix_xattn(x, y_mla, y_ret, ret_gate, w_out, g_xattn, wqk, wvo, later_weights):
    b, s, d = x.shape
    tok = lambda w: pl.BlockSpec((1, TM_MIX, w), lambda bi, i: (bi, i, 0))
    per_batch = lambda a: pl.BlockSpec((1,) + a.shape[1:], lambda bi, i: (bi, 0, 0))
    cast_in, cast_out, cast_shapes = _cast_plumbing(later_weights, (b, s // TM_MIX))
    in_specs = [tok(d), tok(MLA_WIDTH), tok(RET_WIDTH), tok(RET_WIDTH), _const_spec(w_out.shape), _const_spec((1, d)),
                per_batch(wqk), per_batch(wvo)]
    return pl.pallas_call(
        _with_casts(_mix_xattn_kernel, len(in_specs), 1, len(later_weights)),
        out_shape=(jax.ShapeDtypeStruct((b, s, d), _F32), *cast_shapes),
        grid=(b, s // TM_MIX),
        in_specs=in_specs + cast_in,
        out_specs=(tok(d), *cast_out),
        compiler_params=_params(("arbitrary", "arbitrary")),
        name="mix_xattn",
    )(x, y_mla, y_ret, ret_gate, w_out, g_xattn.reshape(1, d), wqk, wvo, *later_weights)


def _conv_ffn_kernel(x_ref, g_ref, win_ref, cw_ref, cb_ref, wout_ref, gfin_ref, o_ref, gate_ref):
    halo = SUBLANES
    half = TM_FFN // 2

    @pl.when(pl.program_id(1) == 0)
    def _():
        gate_ref[:halo, :] = jnp.zeros((halo, D_FF), _F32)

    rows = [slice(0, half), slice(half, TM_FFN)]
    xs = [x_ref[0, r, :] for r in rows]
    hs = [_rms(x, g_ref[...]).astype(_BF16) for x in xs]

    def project(i):
        gate = _dot(hs[i], win_ref[:, :D_FF])
        up = _dot(hs[i], win_ref[:, D_FF:])
        gate_ref[halo + i * half:halo + (i + 1) * half, :] = gate
        return gate, up

    def activate(i, gate, up):
        conv = cb_ref[...] + gate * cw_ref[CONV_W - 1:CONV_W, :]
        for tap in range(CONV_W - 1):
            back = CONV_W - 1 - tap
            lo = halo + i * half - back
            conv = conv + gate_ref[lo:lo + half, :] * cw_ref[tap:tap + 1, :]
        return (conv * jax.nn.sigmoid(conv) * up).astype(_BF16)

    def finish(i, act):
        x3 = xs[i] + _dot(act, wout_ref[...])
        o_ref[0, rows[i], :] = _rms(x3, gfin_ref[...])

    g0, u0 = project(0)
    g1, u1 = project(1)
    a0 = activate(0, g0, u0)
    finish(0, a0)
    a1 = activate(1, g1, u1)
    finish(1, a1)
    gate_ref[:halo, :] = gate_ref[TM_FFN:, :]


def _conv_ffn(x, g_ffn, w_ffn_in, conv_w, conv_b, w_ffn_out, g_final):
    b, s, d = x.shape
    tok = pl.BlockSpec((1, TM_FFN, d), lambda bi, i: (bi, i, 0))
    return pl.pallas_call(
        _conv_ffn_kernel,
        out_shape=jax.ShapeDtypeStruct((b, s, d), _F32),
        grid=(b, s // TM_FFN),
        in_specs=[tok, _const_spec((1, d)), _const_spec(w_ffn_in.shape), _const_spec(conv_w.shape),
                  _const_spec((1, D_FF)), _const_spec(w_ffn_out.shape), _const_spec((1, d))],
        out_specs=tok,
        scratch_shapes=[pltpu.VMEM((TM_FFN + SUBLANES, D_FF), _F32)],
        compiler_params=_params(("arbitrary", "arbitrary")),
        name="conv_ffn",
    )(x, g_ffn.reshape(1, d), w_ffn_in, conv_w, conv_b.reshape(1, D_FF), w_ffn_out, g_final.reshape(1, d))


def kernel(x, mem, positions, g_mix, w_in, g_q_lat, w_uq, g_kv_lat, w_ukv, w_out, g_xattn, g_mem, w_xq,
           w_xkv, w_xo, g_ffn, w_ffn_in, conv_w, conv_b, w_ffn_out, g_final):
    assert w_in.shape[0] == 1, "one layer supported"
    l = 0
    w_in_t = w_in[l].T
    cos_t, sin_t, wlat_t, *wret_t, wqk, wvo = _prep(positions, w_in_t, mem, g_mem[l], w_xkv[l], w_xq[l], w_xo[l])
    wuq_p, wuk_p, wuvt_p = _permute_up_weights(w_uq[l], w_ukv[l])
    q, k, vt, y_ret, ret_gate, w_out_b = _in_proj(
        x, g_mix[l], wlat_t, wret_t, g_q_lat[l], wuq_p, g_kv_lat[l], wuk_p, wuvt_p, cos_t, sin_t, (w_out[l],))
    y_mla, w_ffn_in_b = _mla_attn(q, k, vt, (w_ffn_in[l],))
    x, w_ffn_out_b = _mix_xattn(x, y_mla, y_ret, ret_gate, w_out_b, g_xattn[l], wqk, wvo, (w_ffn_out[l],))
    return _conv_ffn(x, g_ffn[l], w_ffn_in_b, conv_w[l], conv_b[l], w_ffn_out_b, g_final)
```

```python
import functools
import math

import jax
import jax.numpy as jnp
from jax import lax
from jax.experimental import pallas as pl
from jax.experimental.pallas import tpu as pltpu

D_MODEL = 1024
EPS = 1e-6
ROPE_BASE = 10000.0
N_MLA_HEADS = 8
QK_NOPE = 64
QK_ROPE = 32
V_HEAD = 64
Q_LORA = 256
KV_LORA = 128
N_RET_HEADS = 4
RET_DK = 128
RET_DV = 128
RET_CHUNK = 128
MLA_WIDTH = N_MLA_HEADS * V_HEAD
RET_WIDTH = N_RET_HEADS * RET_DV
N_XATTN_HEADS = 4
XATTN_HEAD = D_MODEL // N_XATTN_HEADS
D_FF = 2816
CONV_W = 3

LANES = 128
SUBLANES = 8
_BF16_ROWS = 16
VMEM_LIMIT = 56 * 1024 * 1024

HEAD_PAD = LANES
ROPE_HALF = QK_ROPE // 2
MLA_QK_WIDTH = N_MLA_HEADS * HEAD_PAD
IN_LAT = Q_LORA + KV_LORA + HEAD_PAD

TM_TABLE = 1024
TM_IN = 1024
TQ_MLA = 256
TM_MIX = 1024
TM_FFN = 512

_BF16 = jnp.bfloat16
_F32 = jnp.float32


def _dot(a, b):
    return jnp.dot(a, b, preferred_element_type=_F32)


def _dot_nt(a, b):
    return lax.dot_general(a, b, (((1,), (1,)), ((), ())), preferred_element_type=_F32)


def _dot_tn(a, b):
    return lax.dot_general(a, b, (((0,), (0,)), ((), ())), preferred_element_type=_F32)


def _rms(x, g):
    inv = lax.rsqrt(jnp.mean(x * x, axis=-1, keepdims=True) + EPS)
    return x * inv * g


def _const_spec(shape):
    nd = len(shape)
    return pl.BlockSpec(shape, lambda *_: (0,) * nd, pipeline_mode=pl.Buffered(1))


def _params(sem):
    return pltpu.CompilerParams(dimension_semantics=sem, vmem_limit_bytes=VMEM_LIMIT)


def _with_casts(body, n_in, n_out, n_cast):
    def kern(*refs):
        ins, rest = refs[:n_in], refs[n_in:]
        cast_in, rest = rest[:n_cast], rest[n_cast:]
        outs, rest = rest[:n_out], rest[n_out:]
        cast_out, scratch = rest[:n_cast], rest[n_cast:]
        for src, dst in zip(cast_in, cast_out):
            dst[...] = src[...].astype(dst.dtype)
        body(*ins, *outs, *scratch)
    return kern


def _cast_plumbing(weights, grid):
    steps = math.prod(grid)
    if len(grid) == 1:
        block_of = lambda i: (i, 0)
    else:
        block_of = lambda bi, i: (bi * grid[1] + i, 0)
    in_specs, out_specs, out_shapes = [], [], []
    for w in weights:
        rows, cols = w.shape
        assert rows % steps == 0 and (rows // steps) % _BF16_ROWS == 0, (w.shape, steps)
        spec = pl.BlockSpec((rows // steps, cols), block_of)
        in_specs.append(spec)
        out_specs.append(spec)
        out_shapes.append(jax.ShapeDtypeStruct(w.shape, _BF16))
    return in_specs, out_specs, out_shapes


def _prep_kernel(pos_ref, inv_ref, wlat_ref, wq_ref, wk_ref, wv_ref, wg_ref, mem_ref, gmem_ref, wxk_ref, wxv_ref,
                 wxq_ref, wxo_ref, cos_ref, sin_ref, wlat_out, wq_out, wk_out, wv_out, wg_out, wqk_ref, wvo_ref,
                 *, lat_blocks, n_batch):
    half_rows = TM_IN // 2 // LANES
    lane = lax.broadcasted_iota(jnp.int32, (LANES, LANES), 1)
    column = lambda r: jnp.broadcast_to(pos_ref[r:r + 1, :], (LANES, LANES)).T
    for tile in range(TM_TABLE // TM_IN):
        for rr in range(half_rows):
            r0 = tile * 2 * half_rows + rr
            pos = jnp.where(lane < RET_DK // 2, column(r0), column(r0 + half_rows))
            ang = pos * inv_ref[...]
            out = slice((tile * half_rows + rr) * LANES, (tile * half_rows + rr + 1) * LANES)
            cos_ref[out, :] = jnp.cos(ang)
            sin_ref[out, :] = jnp.sin(ang)
    for src, dst in ((wq_ref, wq_out), (wk_ref, wk_out), (wv_ref, wv_out), (wg_ref, wg_out)):
        dst[...] = src[...].astype(_BF16)
    wlat_out[...] = jnp.where(pl.program_id(0) < lat_blocks, wlat_ref[...], 0.0).astype(_BF16)

    mem_n = _rms(mem_ref[pl.program_id(0) % n_batch], gmem_ref[...]).astype(_BF16)
    k_h = _dot(mem_n, wxk_ref[...].astype(_BF16)).astype(_BF16)
    v_h = _dot(mem_n, wxv_ref[...].astype(_BF16)).astype(_BF16)
    wqk_ref[0] = (_dot_nt(wxq_ref[...].astype(_BF16), k_h) * (1.0 / math.sqrt(XATTN_HEAD))).astype(_BF16)
    wvo_ref[0] = _dot(v_h, wxo_ref[...].astype(_BF16)).astype(_BF16)


def _prep(positions, w_in_t, mem, g_mem, w_xkv, w_xq, w_xo):
    t = positions.size
    steps = t // TM_TABLE
    n_in, d = w_in_t.shape
    b, m, _ = mem.shape
    assert steps == b * N_XATTN_HEADS
    row0 = Q_LORA + KV_LORA + QK_ROPE
    rows = RET_WIDTH // steps
    assert RET_WIDTH % steps == 0 and rows % _BF16_ROWS == 0 and row0 % rows == 0
    assert n_in == row0 + 4 * RET_WIDTH
    pos = positions.astype(_F32).reshape(t // LANES, LANES)
    f_ret = 1.0 / (ROPE_BASE ** (jnp.arange(0, RET_DK, 2, dtype=_F32) / RET_DK))
    inv = jnp.concatenate([f_ret, f_ret])
    group = lambda j: pl.BlockSpec((rows, d), lambda i: (row0 // rows + j * steps + i, 0))
    lat_blocks = row0 // rows
    kr_dst = (Q_LORA + KV_LORA + QK_NOPE) // rows
    assert QK_ROPE == rows and IN_LAT // rows == steps and (Q_LORA + KV_LORA) % rows == 0
    lat_src = pl.BlockSpec((rows, d), lambda i: (jnp.minimum(i, lat_blocks - 1), 0))
    lat_dst = pl.BlockSpec((rows, d), lambda i: (jnp.where(i < lat_blocks - 1, i, jnp.where(
        i == lat_blocks - 1, kr_dst, jnp.where(i <= kr_dst, i - 1, i))), 0))
    table = pl.BlockSpec((TM_TABLE // 2, LANES), lambda i: (i, 0))
    w_out = pl.BlockSpec((rows, d), lambda i: (i, 0))
    nh = N_XATTN_HEADS
    head_cols = lambda off: pl.BlockSpec((d, XATTN_HEAD), lambda i: (0, off + i // b))
    return pl.pallas_call(
        functools.partial(_prep_kernel, lat_blocks=lat_blocks, n_batch=b),
        out_shape=(jax.ShapeDtypeStruct((t // 2, LANES), _F32),) * 2
                  + (jax.ShapeDtypeStruct((IN_LAT, d), _BF16),)
                  + (jax.ShapeDtypeStruct((RET_WIDTH, d), _BF16),) * 4
                  + (jax.ShapeDtypeStruct((b, d, nh * m), _BF16), jax.ShapeDtypeStruct((b, nh * m, d), _BF16)),
        grid=(steps,),
        in_specs=[pl.BlockSpec((TM_TABLE // LANES, LANES), lambda i: (i, 0)),
                  pl.BlockSpec((1, LANES), lambda i: (0, 0)), lat_src, group(0), group(1), group(2), group(3),
                  _const_spec(mem.shape), _const_spec((1, d)),
                  head_cols(0), head_cols(nh), head_cols(0),
                  pl.BlockSpec((XATTN_HEAD, d), lambda i: (i // b, 0))],
        out_specs=(table, table, lat_dst, w_out, w_out, w_out, w_out,
                   pl.BlockSpec((1, d, m), lambda i: (i % b, 0, i // b)),
                   pl.BlockSpec((1, m, d), lambda i: (i % b, i // b, 0))),
        compiler_params=_params(("arbitrary",)),
        name="prep",
    )(pos, inv.reshape(1, LANES), w_in_t, w_in_t, w_in_t, w_in_t, w_in_t, mem, g_mem.reshape(1, d), w_xkv, w_xkv,
      w_xq, w_xo)


def _in_proj_kernel(x_ref, gmix_ref, wlat_ref, wrq_ref, wrk_ref, wrv_ref, wrg_ref, gq_ref, wuq_ref, gkv_ref, wuk_ref, wuvt_ref, cos_ref, sin_ref,
                    intra_ref, kend_ref, qstart_ref, decay_ref,
                    q_ref, k_ref, vt_ref, yret_ref, gate_ref, state_ref):
    h = _rms(x_ref[0], gmix_ref[...]).astype(_BF16)
    lane = lax.broadcasted_iota(jnp.int32, (TM_IN, LANES), 1)
    low = lane < RET_DK // 2

    lo = lax.broadcasted_iota(jnp.int32, (TM_IN // 2, LANES), 1) < RET_DK // 2

    def unpack(t):
        t_roll = pltpu.roll(t, RET_DK // 2, axis=1)
        return jnp.concatenate([jnp.where(lo, t, t_roll), jnp.where(lo, t_roll, t)], axis=0)

    cos_r, sin_full = unpack(cos_ref[...]), unpack(sin_ref[...])
    sin_r = jnp.where(low, -sin_full, sin_full)
    ratio = (RET_DK // 2) // ROPE_HALF
    rope_lane = (lane >= QK_NOPE) & (lane < QK_NOPE + QK_ROPE)
    src = jnp.where(lane < QK_NOPE + ROPE_HALF, lane - QK_NOPE, lane - QK_NOPE - ROPE_HALF) * ratio
    src = jnp.where(rope_lane, src, 0)
    g_cos = jnp.take_along_axis(cos_r, src, axis=1)
    g_sin = jnp.take_along_axis(sin_full, src, axis=1)
    cos_m = jnp.where(rope_lane, g_cos, 1.0)
    sin_m = jnp.where(rope_lane, jnp.where(lane < QK_NOPE + ROPE_HALF, -g_sin, g_sin), 0.0)

    first_half = lane < QK_NOPE + ROPE_HALF

    def swap_halves(t):
        return jnp.where(first_half, pltpu.roll(t, LANES - ROPE_HALF, axis=1), pltpu.roll(t, ROPE_HALF, axis=1))

    lat = _dot_nt(h, wlat_ref[...])
    c_q = lat[:, :Q_LORA]
    c_kv = lat[:, Q_LORA:Q_LORA + KV_LORA]
    kr = lat[:, Q_LORA + KV_LORA:IN_LAT]
    k_rope = kr * cos_m + swap_halves(kr) * sin_m

    def ret_rope(w_ref, mult):
        r = _dot_nt(h, w_ref[...])
        heads = []
        for hd in range(N_RET_HEADS):
            rh = r[:, hd * RET_DK:(hd + 1) * RET_DK]
            roped = rh * cos_r + pltpu.roll(rh, RET_DK // 2, axis=1) * sin_r
            if mult is not None:
                roped = roped * mult
            heads.append(roped.astype(_BF16))
        return heads

    rq = ret_rope(wrq_ref, None)
    rk = ret_rope(wrk_ref, RET_DK ** -0.5)
    rv = _dot_nt(h, wrv_ref[...]).astype(_BF16)

    cqn = _rms(c_q, gq_ref[...]).astype(_BF16)
    q = _dot(cqn, wuq_ref[...])
    scale = math.log2(math.e) / math.sqrt(QK_NOPE + QK_ROPE)
    for hd in range(N_MLA_HEADS):
        sl = slice(hd * HEAD_PAD, (hd + 1) * HEAD_PAD)
        q_ref[0, :, sl] = ((q[:, sl] * cos_m + swap_halves(q[:, sl]) * sin_m) * scale).astype(_BF16)

    ckvn = _rms(c_kv, gkv_ref[...]).astype(_BF16)
    k_nope = _dot(ckvn, wuk_ref[...])
    for hd in range(N_MLA_HEADS):
        sl = slice(hd * HEAD_PAD, (hd + 1) * HEAD_PAD)
        k_ref[0, :, sl] = (k_nope[:, sl] + k_rope).astype(_BF16)
    v_t = _dot_nt(wuvt_ref[...], ckvn).astype(_BF16)
    for j in range(TM_IN // TQ_MLA):
        vt_ref[0, j] = v_t[:, j * TQ_MLA:(j + 1) * TQ_MLA]

    rg = _dot_nt(h, wrg_ref[...]).astype(_BF16)
    gate_ref[0] = rg
    _retention_tile(rq, rk, rv, intra_ref, kend_ref, qstart_ref, decay_ref, state_ref, yret_ref)


def _retention_tile(q_heads, k_heads, v, intra_ref, kend_ref, qstart_ref, decay_ref, state_ref, o_ref):
    @pl.when(pl.program_id(1) == 0)
    def _():
        state_ref[...] = jnp.zeros(state_ref.shape, _F32)

    L = RET_CHUNK
    n_chunks = v.shape[0] // L
    units = [(c, hd) for c in range(n_chunks) for hd in range(N_RET_HEADS)]
    rows = lambda c: slice(c * L, (c + 1) * L)
    cols = lambda hd: slice(hd * RET_DV, (hd + 1) * RET_DV)
    scores, chunk_kv = {}, {}
    for c, hd in units:
        scores[c, hd] = _dot_nt(q_heads[hd][rows(c)], k_heads[hd][rows(c)])
    for c, hd in units:
        v_dec = (v[rows(c), cols(hd)].astype(_F32) * kend_ref[hd]).astype(_BF16)
        chunk_kv[c, hd] = _dot_tn(k_heads[hd][rows(c)], v_dec)
    prev_state = {}
    for hd in range(N_RET_HEADS):
        state = state_ref[hd]
        for c in range(n_chunks):
            prev_state[c, hd] = state.astype(_BF16)
            state = decay_ref[hd] * state + chunk_kv[c, hd]
        state_ref[hd] = state
    inner, cross = {}, {}
    for c, hd in units:
        inner[c, hd] = _dot((scores[c, hd] * intra_ref[hd]).astype(_BF16), v[rows(c), cols(hd)])
    for c, hd in units:
        cross[c, hd] = _dot(q_heads[hd][rows(c)], prev_state[c, hd])
    for c, hd in units:
        out = inner[c, hd] + cross[c, hd] * qstart_ref[hd]
        o_ref[0, rows(c), cols(hd)] = out.astype(o_ref.dtype)


def _permute_up_weights(w_uq, w_ukv):
    wq = w_uq.reshape(Q_LORA, N_MLA_HEADS, QK_NOPE + QK_ROPE)
    nope, r1, r2 = wq[..., :QK_NOPE], wq[..., QK_NOPE:QK_NOPE + ROPE_HALF], wq[..., QK_NOPE + ROPE_HALF:]
    z32 = jnp.zeros((Q_LORA, N_MLA_HEADS, HEAD_PAD - QK_NOPE - QK_ROPE), w_uq.dtype)
    wuq_p = jnp.concatenate([nope, r1, r2, z32], axis=-1).reshape(Q_LORA, MLA_QK_WIDTH).astype(_BF16)

    wkv = w_ukv.reshape(KV_LORA, N_MLA_HEADS, QK_NOPE + V_HEAD)
    zk = jnp.zeros((KV_LORA, N_MLA_HEADS, HEAD_PAD - QK_NOPE), w_ukv.dtype)
    wk = jnp.concatenate([wkv[..., :QK_NOPE], zk], axis=-1).reshape(KV_LORA, MLA_QK_WIDTH)
    wv_t = wkv[..., QK_NOPE:].reshape(KV_LORA, MLA_WIDTH).T
    return wuq_p, wk.astype(_BF16), wv_t.astype(_BF16)


def _in_proj(x, g_mix, wlat_t, wret_t, g_q_lat, wuq_p, g_kv_lat, wuk_p, wuvt_p, cos_t, sin_t, later_weights):
    b, s, d = x.shape
    nt = s // TM_IN
    per = TM_IN // TQ_MLA
    tok = lambda w: pl.BlockSpec((1, TM_IN, w), lambda bi, i: (bi, i, 0))
    tab = pl.BlockSpec((TM_IN // 2, LANES), lambda bi, i: (bi * nt + i, 0))
    bf = lambda w: jax.ShapeDtypeStruct((b, s, w), _BF16)
    ret_tables = _retention_tables()
    cast_in, cast_out, cast_shapes = _cast_plumbing(later_weights, (b, nt))
    in_specs = ([tok(d), _const_spec((1, d)), _const_spec(wlat_t.shape)] + [_const_spec(w.shape) for w in wret_t]
                + [_const_spec((1, Q_LORA)), _const_spec(wuq_p.shape), _const_spec((1, KV_LORA)),
                   _const_spec(wuk_p.shape), _const_spec(wuvt_p.shape), tab, tab]
                + [_const_spec(t.shape) for t in ret_tables])
    out_specs = (tok(MLA_QK_WIDTH), tok(MLA_QK_WIDTH),
                 pl.BlockSpec((1, per, MLA_WIDTH, TQ_MLA), lambda bi, i: (bi, i, 0, 0)), tok(RET_WIDTH), tok(RET_WIDTH))
    return pl.pallas_call(
        _with_casts(_in_proj_kernel, len(in_specs), len(out_specs), len(later_weights)),
        out_shape=(bf(MLA_QK_WIDTH), bf(MLA_QK_WIDTH),
                   jax.ShapeDtypeStruct((b, s // TQ_MLA, MLA_WIDTH, TQ_MLA), _BF16), bf(RET_WIDTH), bf(RET_WIDTH),
                   *cast_shapes),
        grid=(b, nt),
        in_specs=in_specs + cast_in,
        out_specs=(*out_specs, *cast_out),
        scratch_shapes=[pltpu.VMEM((N_RET_HEADS, RET_DK, RET_DV), _F32)],
        compiler_params=_params(("arbitrary", "arbitrary")),
        name="in_proj",
    )(x, g_mix.reshape(1, d), wlat_t, *wret_t, g_q_lat.reshape(1, Q_LORA), wuq_p, g_kv_lat.reshape(1, KV_LORA),
      wuk_p, wuvt_p, cos_t, sin_t, *ret_tables, *later_weights)


_MASK_VALUE = -0.7 * float(jnp.finfo(jnp.float32).max)
_MLA_AHEAD = 4
_MLA_TILES_PER_ITER = 5
_MLA_DENOM_ROWS = 16


def _mla_attn_kernel(qlo_ref, qhi_ref, k_ref, vt_ref, o_ref, q_ref, s_ref, m_ref, acc_ref):
    tq = TQ_MLA
    n_tiles = k_ref.shape[1] // tq
    lo = pl.program_id(1)
    hi = n_tiles - 1 - lo
    q_ref[0] = qlo_ref[0]
    q_ref[1] = qhi_ref[0]
    m_ref[...] = jnp.full(m_ref.shape, _MASK_VALUE, _F32)
    acc_ref[...] = jnp.zeros(acc_ref.shape, _F32)
    ones_rows = jnp.ones((_MLA_DENOM_ROWS, tq), _BF16)

    def step_args(t):
        sel = (t > lo).astype(jnp.int32)
        return sel, t - 1 - sel * lo

    def scores(sel, kk, hd):
        sl = slice(hd * HEAD_PAD, (hd + 1) * HEAD_PAD)
        ks = pl.multiple_of(kk * tq, tq)
        s_ref[hd] = _dot_nt(k_ref[0, pl.ds(ks, tq), sl], q_ref[sel, :, sl])

    def softmax(sel, hd, keep):
        row = slice(hd, hd + 1)

        def st():
            s = s_ref[hd]
            return s if keep is None else jnp.where(keep, s, _MASK_VALUE)

        m_prev = m_ref[sel, row, :]
        m_next = jnp.maximum(m_prev, jnp.max(st(), axis=0, keepdims=True))
        m_ref[sel, row, :] = m_next
        return jnp.exp2(m_prev - m_next), jnp.exp2(st() - m_next).astype(_BF16)

    def accumulate(sel, kk, hd, alpha, p):
        vt = jnp.concatenate([vt_ref[0, kk, hd * V_HEAD:(hd + 1) * V_HEAD, :], ones_rows], axis=0)
        acc_ref[sel, hd] = acc_ref[sel, hd] * alpha + _dot(vt, p)

    def tile(sel, kk, diagonal, nxt):
        keep = None
        if diagonal:
            key = lax.broadcasted_iota(jnp.int32, (tq, tq), 0)
            qry = lax.broadcasted_iota(jnp.int32, (tq, tq), 1)
            keep = key <= qry
        for hd in range(N_MLA_HEADS):
            ahead = hd + _MLA_AHEAD
            if ahead < N_MLA_HEADS:
                scores(sel, kk, ahead)
            elif nxt is not None:
                scores(nxt[0], nxt[1], ahead - N_MLA_HEADS)
            accumulate(sel, kk, hd, *softmax(sel, hd, keep))

    for hd in range(_MLA_AHEAD):
        scores(0, lo, hd)
    tile(0, lo, True, step_args(1))

    def body(it, carry):
        for u in range(_MLA_TILES_PER_ITER):
            t = 1 + it * _MLA_TILES_PER_ITER + u
            last = t + 1 == n_tiles
            sel_n, kk_n = step_args(t + 1)
            tile(*step_args(t), False, (jnp.where(last, 1, sel_n), jnp.where(last, hi, kk_n)))
        return carry

    lax.fori_loop(0, (n_tiles - 1) // _MLA_TILES_PER_ITER, body, 0)
    tile(1, hi, True, None)

    for sel, qt in ((0, lo), (1, hi)):
        out_t = jnp.concatenate([acc_ref[sel, hd, :V_HEAD, :] / acc_ref[sel, hd, V_HEAD:V_HEAD + 1, :]
                                 for hd in range(N_MLA_HEADS)], axis=0)
        o_ref[0, pl.ds(pl.multiple_of(qt * tq, tq), tq), :] = out_t.T.astype(o_ref.dtype)


def _mla_attn(q, k, vt, later_weights):
    b, s, w = q.shape
    nk = s // TQ_MLA
    assert nk % 2 == 0 and (nk - 1) % _MLA_TILES_PER_ITER == 0
    cast_in, cast_out, cast_shapes = _cast_plumbing(later_weights, (b, nk // 2))
    in_specs = [pl.BlockSpec((1, TQ_MLA, w), lambda bi, i: (bi, i, 0)),
                pl.BlockSpec((1, TQ_MLA, w), lambda bi, i: (bi, nk - 1 - i, 0)),
                pl.BlockSpec((1, s, w), lambda bi, i: (bi, 0, 0)),
                pl.BlockSpec((1, nk, MLA_WIDTH, TQ_MLA), lambda bi, i: (bi, 0, 0, 0))]
    return pl.pallas_call(
        _with_casts(_mla_attn_kernel, len(in_specs), 1, len(later_weights)),
        out_shape=(jax.ShapeDtypeStruct((b, s, MLA_WIDTH), _BF16), *cast_shapes),
        grid=(b, nk // 2),
        in_specs=in_specs + cast_in,
        out_specs=(pl.BlockSpec((1, s, MLA_WIDTH), lambda bi, i: (bi, 0, 0)), *cast_out),
        scratch_shapes=[pltpu.VMEM((2, TQ_MLA, w), _BF16),
                        pltpu.VMEM((N_MLA_HEADS, TQ_MLA, TQ_MLA), _F32),
                        pltpu.VMEM((2, N_MLA_HEADS, TQ_MLA), _F32),
                        pltpu.VMEM((2, N_MLA_HEADS, V_HEAD + _MLA_DENOM_ROWS, TQ_MLA), _F32)],
        compiler_params=_params(("arbitrary", "arbitrary")),
        name="mla_attn",
    )(q, q, k, vt, *later_weights)


def _retention_tables():
    h, L = N_RET_HEADS, RET_CHUNK
    log_gamma = jnp.log(1.0 - 2.0 ** (-5.0 - jnp.arange(h, dtype=_F32)))
    j = jnp.arange(L, dtype=_F32)
    diff = j[:, None] - j[None, :]
    intra = jnp.where(diff[None] >= 0,
                      jnp.exp(jnp.maximum(diff, 0.0)[None] * log_gamma[:, None, None]), 0.0)
    rowb = lambda t: jnp.broadcast_to(t.T[:, :, None], (h, L, LANES))
    k_to_end = jnp.exp((L - 1 - j)[:, None] * log_gamma[None, :])
    q_from_start = jnp.exp((j + 1)[:, None] * log_gamma[None, :])
    chunk_decay = jnp.broadcast_to(jnp.exp(L * log_gamma)[:, None, None], (h, RET_DK, RET_DV))
    return intra, rowb(k_to_end), rowb(q_from_start), chunk_decay


def _mix_xattn_kernel(x_ref, ymla_ref, yret_ref, gate_ref, wout_ref, gx_ref, wqk_ref, wvo_ref, o_ref):
    half = TM_MIX // 2
    rows = [slice(0, half), slice(half, TM_MIX)]
    mem_len = wqk_ref.shape[2] // N_XATTN_HEADS

    def retention_out(r):
        heads = []
        for hd in range(N_RET_HEADS):
            c = slice(hd * RET_DV, (hd + 1) * RET_DV)
            out = yret_ref[0, r, c].astype(_F32)
            mu = jnp.mean(out, axis=-1, keepdims=True)
            cen = out - mu
            var = jnp.mean(cen * cen, axis=-1, keepdims=True)
            g = gate_ref[0, r, c].astype(_F32)
            heads.append((cen * lax.rsqrt(var + EPS) * (g * jax.nn.sigmoid(g))).astype(_BF16))
        return jnp.concatenate(heads, axis=1)

    def out_proj(r):
        return (x_ref[0, r, :] + _dot(ymla_ref[0, r, :], wout_ref[:MLA_WIDTH, :])
                + _dot(retention_out(r), wout_ref[MLA_WIDTH:, :]))

    def scores(x1):
        h = _rms(x1, gx_ref[...]).astype(_BF16)
        return _dot(h, wqk_ref[0])

    def softmax(s):
        heads = []
        for hd in range(N_XATTN_HEADS):
            blk = s[:, hd * mem_len:(hd + 1) * mem_len]
            p = jnp.exp(blk - jnp.max(blk, axis=-1, keepdims=True))
            heads.append((p / jnp.sum(p, axis=-1, keepdims=True)).astype(_BF16))
        return jnp.concatenate(heads, axis=1)

    x1 = [out_proj(r) for r in rows]
    s0 = scores(x1[0])
    s1 = scores(x1[1])
    p0 = softmax(s0)
    o_ref[0, rows[0], :] = x1[0] + _dot(p0, wvo_ref[0])
    p1 = softmax(s1)
    o_ref[0, rows[1], :] = x1[1] + _dot(p1, wvo_ref[0])


def _mix_xattn(x, y_mla, y_ret, ret_gate, w_out, g_xattn, wqk, wvo, later_weights):
    b, s, d = x.shape
    tok = lambda w: pl.BlockSpec((1, TM_MIX, w), lambda bi, i: (bi, i, 0))
    per_batch = lambda a: pl.BlockSpec((1,) + a.shape[1:], lambda bi, i: (bi, 0, 0))
    cast_in, cast_out, cast_shapes = _cast_plumbing(later_weights, (b, s // TM_MIX))
    in_specs = [tok(d), tok(MLA_WIDTH), tok(RET_WIDTH), tok(RET_WIDTH), _const_spec(w_out.shape), _const_spec((1, d)),
                per_batch(wqk), per_batch(wvo)]
    return pl.pallas_call(
        _with_casts(_mix_xattn_kernel, len(in_specs), 1, len(later_weights)),
        out_shape=(jax.ShapeDtypeStruct((b, s, d), _F32), *cast_shapes),
        grid=(b, s // TM_MIX),
        in_specs=in_specs + cast_in,
        out_specs=(tok(d), *cast_out),
        compiler_params=_params(("arbitrary", "arbitrary")),
        name="mix_xattn",
    )(x, y_mla, y_ret, ret_gate, w_out, g_xattn.reshape(1, d), wqk, wvo, *later_weights)


def _conv_ffn_kernel(x_ref, g_ref, win_ref, cw_ref, cb_ref, wout_ref, gfin_ref, o_ref, gate_ref):
    halo = SUBLANES
    half = TM_FFN // 2

    @pl.when(pl.program_id(1) == 0)
    def _():
        gate_ref[:halo, :] = jnp.zeros((halo, D_FF), _F32)

    rows = [slice(0, half), slice(half, TM_FFN)]
    xs = [x_ref[0, r, :] for r in rows]
    hs = [_rms(x, g_ref[...]).astype(_BF16) for x in xs]

    def project(i):
        gate = _dot(hs[i], win_ref[:, :D_FF])
        up = _dot(hs[i], win_ref[:, D_FF:])
        gate_ref[halo + i * half:halo + (i + 1) * half, :] = gate
        return gate, up

    def activate(i, gate, up):
        conv = cb_ref[...] + gate * cw_ref[CONV_W - 1:CONV_W, :]
        for tap in range(CONV_W - 1):
            back = CONV_W - 1 - tap
            lo = halo + i * half - back
            conv = conv + gate_ref[lo:lo + half, :] * cw_ref[tap:tap + 1, :]
        return (conv * jax.nn.sigmoid(conv) * up).astype(_BF16)

    def finish(i, act):
        x3 = xs[i] + _dot(act, wout_ref[...])
        o_ref[0, rows[i], :] = _rms(x3, gfin_ref[...])

    g0, u0 = project(0)
    g1, u1 = project(1)
    a0 = activate(0, g0, u0)
    finish(0, a0)
    a1 = activate(1, g1, u1)
    finish(1, a1)
    gate_ref[:halo, :] = gate_ref[TM_FFN:, :]


def _conv_ffn(x, g_ffn, w_ffn_in, conv_w, conv_b, w_ffn_out, g_final):
    b, s, d = x.shape
    tok = pl.BlockSpec((1, TM_FFN, d), lambda bi, i: (bi, i, 0))
    return pl.pallas_call(
        _conv_ffn_kernel,
        out_shape=jax.ShapeDtypeStruct((b, s, d), _F32),
        grid=(b, s // TM_FFN),
        in_specs=[tok, _const_spec((1, d)), _const_spec(w_ffn_in.shape), _const_spec(conv_w.shape),
                  _const_spec((1, D_FF)), _const_spec(w_ffn_out.shape), _const_spec((1, d))],
        out_specs=tok,
        scratch_shapes=[pltpu.VMEM((TM_FFN + SUBLANES, D_FF), _F32)],
        compiler_params=_params(("arbitrary", "arbitrary")),
        name="conv_ffn",
    )(x, g_ffn.reshape(1, d), w_ffn_in, conv_w, conv_b.reshape(1, D_FF), w_ffn_out, g_final.reshape(1, d))


def kernel(x, mem, positions, g_mix, w_in, g_q_lat, w_uq, g_kv_lat, w_ukv, w_out, g_xattn, g_mem, w_xq,
           w_xkv, w_xo, g_ffn, w_ffn_in, conv_w, conv_b, w_ffn_out, g_final):
    assert w_in.shape[0] == 1, "one layer supported"
    l = 0
    w_in_t = w_in[l].T
    cos_t, sin_t, wlat_t, *wret_t, wqk, wvo = _prep(positions, w_in_t, mem, g_mem[l], w_xkv[l], w_xq[l], w_xo[l])
    wuq_p, wuk_p, wuvt_p = _permute_up_weights(w_uq[l], w_ukv[l])
    q, k, vt, y_ret, ret_gate, w_out_b = _in_proj(
        x, g_mix[l], wlat_t, wret_t, g_q_lat[l], wuq_p, g_kv_lat[l], wuk_p, wuvt_p, cos_t, sin_t, (w_out[l],))
    y_mla, w_ffn_in_b = _mla_attn(q, k, vt, (w_ffn_in[l],))
    x, w_ffn_out_b = _mix_xattn(x, y_mla, y_ret, ret_gate, w_out_b, g_xattn[l], wqk, wvo, (w_ffn_out[l],))
    return _conv_ffn(x, g_ffn[l], w_ffn_in_b, conv_w[l], conv_b[l], w_ffn_out_b, g_final)
```

```python
import functools
import math

import jax
import jax.numpy as jnp
from jax import lax
from jax.experimental import pallas as pl
from jax.experimental.pallas import tpu as pltpu

D_MODEL = 1024
EPS = 1e-6
ROPE_BASE = 10000.0
N_MLA_HEADS = 8
QK_NOPE = 64
QK_ROPE = 32
V_HEAD = 64
Q_LORA = 256
KV_LORA = 128
N_RET_HEADS = 4
RET_DK = 128
RET_DV = 128
RET_CHUNK = 128
MLA_WIDTH = N_MLA_HEADS * V_HEAD
RET_WIDTH = N_RET_HEADS * RET_DV
N_XATTN_HEADS = 4
XATTN_HEAD = D_MODEL // N_XATTN_HEADS
D_FF = 2816
CONV_W = 3

LANES = 128
SUBLANES = 8
_BF16_ROWS = 16
VMEM_LIMIT = 56 * 1024 * 1024

HEAD_PAD = LANES
ROPE_HALF = QK_ROPE // 2
MLA_QK_WIDTH = N_MLA_HEADS * HEAD_PAD
IN_LAT = Q_LORA + KV_LORA + HEAD_PAD

TM_TABLE = 1024
TM_IN = 1024
TQ_MLA = 256
TM_MIX = 1024
TM_FFN = 1024

_BF16 = jnp.bfloat16
_F32 = jnp.float32


def _dot(a, b):
    return jnp.dot(a, b, preferred_element_type=_F32)


def _dot_nt(a, b):
    return lax.dot_general(a, b, (((1,), (1,)), ((), ())), preferred_element_type=_F32)


def _dot_tn(a, b):
    return lax.dot_general(a, b, (((0,), (0,)), ((), ())), preferred_element_type=_F32)


def _rms(x, g):
    inv = lax.rsqrt(jnp.mean(x * x, axis=-1, keepdims=True) + EPS)
    return x * inv * g


def _const_spec(shape):
    nd = len(shape)
    return pl.BlockSpec(shape, lambda *_: (0,) * nd, pipeline_mode=pl.Buffered(1))


def _params(sem):
    return pltpu.CompilerParams(dimension_semantics=sem, vmem_limit_bytes=VMEM_LIMIT)


def _with_casts(body, n_in, n_out, n_cast):
    def kern(*refs):
        ins, rest = refs[:n_in], refs[n_in:]
        cast_in, rest = rest[:n_cast], rest[n_cast:]
        outs, rest = rest[:n_out], rest[n_out:]
        cast_out, scratch = rest[:n_cast], rest[n_cast:]
        for src, dst in zip(cast_in, cast_out):
            dst[...] = src[...].astype(dst.dtype)
        body(*ins, *outs, *scratch)
    return kern


def _cast_plumbing(weights, grid):
    steps = math.prod(grid)
    if len(grid) == 1:
        block_of = lambda i: (i, 0)
    else:
        block_of = lambda bi, i: (bi * grid[1] + i, 0)
    in_specs, out_specs, out_shapes = [], [], []
    for w in weights:
        rows, cols = w.shape
        assert rows % steps == 0 and (rows // steps) % _BF16_ROWS == 0, (w.shape, steps)
        spec = pl.BlockSpec((rows // steps, cols), block_of)
        in_specs.append(spec)
        out_specs.append(spec)
        out_shapes.append(jax.ShapeDtypeStruct(w.shape, _BF16))
    return in_specs, out_specs, out_shapes


def _prep_kernel(pos_ref, inv_ref, wlat_ref, wq_ref, wk_ref, wv_ref, wg_ref, mem_ref, gmem_ref, wxk_ref, wxv_ref,
                 wxq_ref, wxo_ref, cos_ref, sin_ref, wlat_out, wq_out, wk_out, wv_out, wg_out, wqk_ref, wvo_ref,
                 *, lat_blocks, n_batch):
    half_rows = TM_IN // 2 // LANES
    lane = lax.broadcasted_iota(jnp.int32, (LANES, LANES), 1)
    column = lambda r: jnp.broadcast_to(pos_ref[r:r + 1, :], (LANES, LANES)).T
    for tile in range(TM_TABLE // TM_IN):
        for rr in range(half_rows):
            r0 = tile * 2 * half_rows + rr
            pos = jnp.where(lane < RET_DK // 2, column(r0), column(r0 + half_rows))
            ang = pos * inv_ref[...]
            out = slice((tile * half_rows + rr) * LANES, (tile * half_rows + rr + 1) * LANES)
            cos_ref[out, :] = jnp.cos(ang)
            sin_ref[out, :] = jnp.sin(ang)
    for src, dst in ((wq_ref, wq_out), (wk_ref, wk_out), (wv_ref, wv_out), (wg_ref, wg_out)):
        dst[...] = src[...].astype(_BF16)
    wlat_out[...] = jnp.where(pl.program_id(0) < lat_blocks, wlat_ref[...], 0.0).astype(_BF16)

    mem_n = _rms(mem_ref[pl.program_id(0) % n_batch], gmem_ref[...]).astype(_BF16)
    k_h = _dot(mem_n, wxk_ref[...].astype(_BF16)).astype(_BF16)
    v_h = _dot(mem_n, wxv_ref[...].astype(_BF16)).astype(_BF16)
    wqk_ref[0] = (_dot_nt(wxq_ref[...].astype(_BF16), k_h) * (1.0 / math.sqrt(XATTN_HEAD))).astype(_BF16)
    wvo_ref[0] = _dot(v_h, wxo_ref[...].astype(_BF16)).astype(_BF16)


def _prep(positions, w_in_t, mem, g_mem, w_xkv, w_xq, w_xo):
    t = positions.size
    steps = t // TM_TABLE
    n_in, d = w_in_t.shape
    b, m, _ = mem.shape
    assert steps == b * N_XATTN_HEADS
    row0 = Q_LORA + KV_LORA + QK_ROPE
    rows = RET_WIDTH // steps
    assert RET_WIDTH % steps == 0 and rows % _BF16_ROWS == 0 and row0 % rows == 0
    assert n_in == row0 + 4 * RET_WIDTH
    pos = positions.astype(_F32).reshape(t // LANES, LANES)
    f_ret = 1.0 / (ROPE_BASE ** (jnp.arange(0, RET_DK, 2, dtype=_F32) / RET_DK))
    inv = jnp.concatenate([f_ret, f_ret])
    group = lambda j: pl.BlockSpec((rows, d), lambda i: (row0 // rows + j * steps + i, 0))
    lat_blocks = row0 // rows
    kr_dst = (Q_LORA + KV_LORA + QK_NOPE) // rows
    assert QK_ROPE == rows and IN_LAT // rows == steps and (Q_LORA + KV_LORA) % rows == 0
    lat_src = pl.BlockSpec((rows, d), lambda i: (jnp.minimum(i, lat_blocks - 1), 0))
    lat_dst = pl.BlockSpec((rows, d), lambda i: (jnp.where(i < lat_blocks - 1, i, jnp.where(
        i == lat_blocks - 1, kr_dst, jnp.where(i <= kr_dst, i - 1, i))), 0))
    table = pl.BlockSpec((TM_TABLE // 2, LANES), lambda i: (i, 0))
    w_out = pl.BlockSpec((rows, d), lambda i: (i, 0))
    nh = N_XATTN_HEADS
    head_cols = lambda off: pl.BlockSpec((d, XATTN_HEAD), lambda i: (0, off + i // b))
    return pl.pallas_call(
        functools.partial(_prep_kernel, lat_blocks=lat_blocks, n_batch=b),
        out_shape=(jax.ShapeDtypeStruct((t // 2, LANES), _F32),) * 2
                  + (jax.ShapeDtypeStruct((IN_LAT, d), _BF16),)
                  + (jax.ShapeDtypeStruct((RET_WIDTH, d), _BF16),) * 4
                  + (jax.ShapeDtypeStruct((b, d, nh * m), _BF16), jax.ShapeDtypeStruct((b, nh * m, d), _BF16)),
        grid=(steps,),
        in_specs=[pl.BlockSpec((TM_TABLE // LANES, LANES), lambda i: (i, 0)),
                  pl.BlockSpec((1, LANES), lambda i: (0, 0)), lat_src, group(0), group(1), group(2), group(3),
                  _const_spec(mem.shape), _const_spec((1, d)),
                  head_cols(0), head_cols(nh), head_cols(0),
                  pl.BlockSpec((XATTN_HEAD, d), lambda i: (i // b, 0))],
        out_specs=(table, table, lat_dst, w_out, w_out, w_out, w_out,
                   pl.BlockSpec((1, d, m), lambda i: (i % b, 0, i // b)),
                   pl.BlockSpec((1, m, d), lambda i: (i % b, i // b, 0))),
        compiler_params=_params(("arbitrary",)),
        name="prep",
    )(pos, inv.reshape(1, LANES), w_in_t, w_in_t, w_in_t, w_in_t, w_in_t, mem, g_mem.reshape(1, d), w_xkv, w_xkv,
      w_xq, w_xo)


def _in_proj_kernel(x_ref, gmix_ref, wlat_ref, wrq_ref, wrk_ref, wrv_ref, wrg_ref, gq_ref, wuq_ref, gkv_ref, wuk_ref, wuvt_ref, cos_ref, sin_ref,
                    intra_ref, kend_ref, qstart_ref, decay_ref,
                    q_ref, k_ref, vt_ref, yret_ref, gate_ref, state_ref):
    h = _rms(x_ref[0], gmix_ref[...]).astype(_BF16)
    lane = lax.broadcasted_iota(jnp.int32, (TM_IN, LANES), 1)
    low = lane < RET_DK // 2

    lo = lax.broadcasted_iota(jnp.int32, (TM_IN // 2, LANES), 1) < RET_DK // 2

    def unpack(t):
        t_roll = pltpu.roll(t, RET_DK // 2, axis=1)
        return jnp.concatenate([jnp.where(lo, t, t_roll), jnp.where(lo, t_roll, t)], axis=0)

    cos_r, sin_full = unpack(cos_ref[...]), unpack(sin_ref[...])
    sin_r = jnp.where(low, -sin_full, sin_full)
    ratio = (RET_DK // 2) // ROPE_HALF
    rope_lane = (lane >= QK_NOPE) & (lane < QK_NOPE + QK_ROPE)
    src = jnp.where(lane < QK_NOPE + ROPE_HALF, lane - QK_NOPE, lane - QK_NOPE - ROPE_HALF) * ratio
    src = jnp.where(rope_lane, src, 0)
    g_cos = jnp.take_along_axis(cos_r, src, axis=1)
    g_sin = jnp.take_along_axis(sin_full, src, axis=1)
    cos_m = jnp.where(rope_lane, g_cos, 1.0)
    sin_m = jnp.where(rope_lane, jnp.where(lane < QK_NOPE + ROPE_HALF, -g_sin, g_sin), 0.0)

    first_half = lane < QK_NOPE + ROPE_HALF

    def swap_halves(t):
        return jnp.where(first_half, pltpu.roll(t, LANES - ROPE_HALF, axis=1), pltpu.roll(t, ROPE_HALF, axis=1))

    lat = _dot_nt(h, wlat_ref[...])
    c_q = lat[:, :Q_LORA]
    c_kv = lat[:, Q_LORA:Q_LORA + KV_LORA]
    kr = lat[:, Q_LORA + KV_LORA:IN_LAT]
    k_rope = kr * cos_m + swap_halves(kr) * sin_m

    def ret_rope(w_ref, mult):
        r = _dot_nt(h, w_ref[...])
        heads = []
        for hd in range(N_RET_HEADS):
            rh = r[:, hd * RET_DK:(hd + 1) * RET_DK]
            roped = rh * cos_r + pltpu.roll(rh, RET_DK // 2, axis=1) * sin_r
            if mult is not None:
                roped = roped * mult
            heads.append(roped.astype(_BF16))
        return heads

    rq = ret_rope(wrq_ref, None)
    rk = ret_rope(wrk_ref, RET_DK ** -0.5)
    rv = _dot_nt(h, wrv_ref[...]).astype(_BF16)

    cqn = _rms(c_q, gq_ref[...]).astype(_BF16)
    q = _dot(cqn, wuq_ref[...])
    scale = math.log2(math.e) / math.sqrt(QK_NOPE + QK_ROPE)
    for hd in range(N_MLA_HEADS):
        sl = slice(hd * HEAD_PAD, (hd + 1) * HEAD_PAD)
        q_ref[0, :, sl] = ((q[:, sl] * cos_m + swap_halves(q[:, sl]) * sin_m) * scale).astype(_BF16)

    ckvn = _rms(c_kv, gkv_ref[...]).astype(_BF16)
    k_nope = _dot(ckvn, wuk_ref[...])
    for hd in range(N_MLA_HEADS):
        sl = slice(hd * HEAD_PAD, (hd + 1) * HEAD_PAD)
        k_ref[0, :, sl] = (k_nope[:, sl] + k_rope).astype(_BF16)
    v_t = _dot_nt(wuvt_ref[...], ckvn).astype(_BF16)
    for j in range(TM_IN // TQ_MLA):
        vt_ref[0, j] = v_t[:, j * TQ_MLA:(j + 1) * TQ_MLA]

    rg = _dot_nt(h, wrg_ref[...]).astype(_BF16)
    gate_ref[0] = rg
    _retention_tile(rq, rk, rv, intra_ref, kend_ref, qstart_ref, decay_ref, state_ref, yret_ref)


def _retention_tile(q_heads, k_heads, v, intra_ref, kend_ref, qstart_ref, decay_ref, state_ref, o_ref):
    @pl.when(pl.program_id(1) == 0)
    def _():
        state_ref[...] = jnp.zeros(state_ref.shape, _F32)

    L = RET_CHUNK
    n_chunks = v.shape[0] // L
    units = [(c, hd) for c in range(n_chunks) for hd in range(N_RET_HEADS)]
    rows = lambda c: slice(c * L, (c + 1) * L)
    cols = lambda hd: slice(hd * RET_DV, (hd + 1) * RET_DV)
    scores, chunk_kv = {}, {}
    for c, hd in units:
        scores[c, hd] = _dot_nt(q_heads[hd][rows(c)], k_heads[hd][rows(c)])
    for c, hd in units:
        v_dec = (v[rows(c), cols(hd)].astype(_F32) * kend_ref[hd]).astype(_BF16)
        chunk_kv[c, hd] = _dot_tn(k_heads[hd][rows(c)], v_dec)
    prev_state = {}
    for hd in range(N_RET_HEADS):
        state = state_ref[hd]
        for c in range(n_chunks):
            prev_state[c, hd] = state.astype(_BF16)
            state = decay_ref[hd] * state + chunk_kv[c, hd]
        state_ref[hd] = state
    inner, cross = {}, {}
    for c, hd in units:
        inner[c, hd] = _dot((scores[c, hd] * intra_ref[hd]).astype(_BF16), v[rows(c), cols(hd)])
    for c, hd in units:
        cross[c, hd] = _dot(q_heads[hd][rows(c)], prev_state[c, hd])
    for c, hd in units:
        out = inner[c, hd] + cross[c, hd] * qstart_ref[hd]
        o_ref[0, rows(c), cols(hd)] = out.astype(o_ref.dtype)


def _permute_up_weights(w_uq, w_ukv):
    wq = w_uq.reshape(Q_LORA, N_MLA_HEADS, QK_NOPE + QK_ROPE)
    nope, r1, r2 = wq[..., :QK_NOPE], wq[..., QK_NOPE:QK_NOPE + ROPE_HALF], wq[..., QK_NOPE + ROPE_HALF:]
    z32 = jnp.zeros((Q_LORA, N_MLA_HEADS, HEAD_PAD - QK_NOPE - QK_ROPE), w_uq.dtype)
    wuq_p = jnp.concatenate([nope, r1, r2, z32], axis=-1).reshape(Q_LORA, MLA_QK_WIDTH).astype(_BF16)

    wkv = w_ukv.reshape(KV_LORA, N_MLA_HEADS, QK_NOPE + V_HEAD)
    zk = jnp.zeros((KV_LORA, N_MLA_HEADS, HEAD_PAD - QK_NOPE), w_ukv.dtype)
    wk = jnp.concatenate([wkv[..., :QK_NOPE], zk], axis=-1).reshape(KV_LORA, MLA_QK_WIDTH)
    wv_t = wkv[..., QK_NOPE:].reshape(KV_LORA, MLA_WIDTH).T
    return wuq_p, wk.astype(_BF16), wv_t.astype(_BF16)


def _in_proj(x, g_mix, wlat_t, wret_t, g_q_lat, wuq_p, g_kv_lat, wuk_p, wuvt_p, cos_t, sin_t, later_weights):
    b, s, d = x.shape
    nt = s // TM_IN
    per = TM_IN // TQ_MLA
    tok = lambda w: pl.BlockSpec((1, TM_IN, w), lambda bi, i: (bi, i, 0))
    tab = pl.BlockSpec((TM_IN // 2, LANES), lambda bi, i: (bi * nt + i, 0))
    bf = lambda w: jax.ShapeDtypeStruct((b, s, w), _BF16)
    ret_tables = _retention_tables()
    cast_in, cast_out, cast_shapes = _cast_plumbing(later_weights, (b, nt))
    in_specs = ([tok(d), _const_spec((1, d)), _const_spec(wlat_t.shape)] + [_const_spec(w.shape) for w in wret_t]
                + [_const_spec((1, Q_LORA)), _const_spec(wuq_p.shape), _const_spec((1, KV_LORA)),
                   _const_spec(wuk_p.shape), _const_spec(wuvt_p.shape), tab, tab]
                + [_const_spec(t.shape) for t in ret_tables])
    out_specs = (tok(MLA_QK_WIDTH), tok(MLA_QK_WIDTH),
                 pl.BlockSpec((1, per, MLA_WIDTH, TQ_MLA), lambda bi, i: (bi, i, 0, 0)), tok(RET_WIDTH), tok(RET_WIDTH))
    return pl.pallas_call(
        _with_casts(_in_proj_kernel, len(in_specs), len(out_specs), len(later_weights)),
        out_shape=(bf(MLA_QK_WIDTH), bf(MLA_QK_WIDTH),
                   jax.ShapeDtypeStruct((b, s // TQ_MLA, MLA_WIDTH, TQ_MLA), _BF16), bf(RET_WIDTH), bf(RET_WIDTH),
                   *cast_shapes),
        grid=(b, nt),
        in_specs=in_specs + cast_in,
        out_specs=(*out_specs, *cast_out),
        scratch_shapes=[pltpu.VMEM((N_RET_HEADS, RET_DK, RET_DV), _F32)],
        compiler_params=_params(("arbitrary", "arbitrary")),
        name="in_proj",
    )(x, g_mix.reshape(1, d), wlat_t, *wret_t, g_q_lat.reshape(1, Q_LORA), wuq_p, g_kv_lat.reshape(1, KV_LORA),
      wuk_p, wuvt_p, cos_t, sin_t, *ret_tables, *later_weights)


_MASK_VALUE = -0.7 * float(jnp.finfo(jnp.float32).max)
_MLA_AHEAD = 4
_MLA_TILES_PER_ITER = 5
_MLA_DENOM_ROWS = 16


def _mla_attn_kernel(qlo_ref, qhi_ref, k_ref, vt_ref, o_ref, q_ref, s_ref, m_ref, acc_ref):
    tq = TQ_MLA
    n_tiles = k_ref.shape[1] // tq
    lo = pl.program_id(1)
    hi = n_tiles - 1 - lo
    q_ref[0] = qlo_ref[0]
    q_ref[1] = qhi_ref[0]
    m_ref[...] = jnp.full(m_ref.shape, _MASK_VALUE, _F32)
    acc_ref[...] = jnp.zeros(acc_ref.shape, _F32)
    ones_rows = jnp.ones((_MLA_DENOM_ROWS, tq), _BF16)

    def step_args(t):
        sel = (t > lo).astype(jnp.int32)
        return sel, t - 1 - sel * lo

    def scores(sel, kk, hd):
        sl = slice(hd * HEAD_PAD, (hd + 1) * HEAD_PAD)
        ks = pl.multiple_of(kk * tq, tq)
        s_ref[hd] = _dot_nt(k_ref[0, pl.ds(ks, tq), sl], q_ref[sel, :, sl])

    def softmax(sel, hd, keep):
        row = slice(hd, hd + 1)

        def st():
            s = s_ref[hd]
            return s if keep is None else jnp.where(keep, s, _MASK_VALUE)

        m_prev = m_ref[sel, row, :]
        m_next = jnp.maximum(m_prev, jnp.max(st(), axis=0, keepdims=True))
        m_ref[sel, row, :] = m_next
        return jnp.exp2(m_prev - m_next), jnp.exp2(st() - m_next).astype(_BF16)

    def accumulate(sel, kk, hd, alpha, p):
        vt = jnp.concatenate([vt_ref[0, kk, hd * V_HEAD:(hd + 1) * V_HEAD, :], ones_rows], axis=0)
        acc_ref[sel, hd] = acc_ref[sel, hd] * alpha + _dot(vt, p)

    def tile(sel, kk, diagonal, nxt):
        keep = None
        if diagonal:
            key = lax.broadcasted_iota(jnp.int32, (tq, tq), 0)
            qry = lax.broadcasted_iota(jnp.int32, (tq, tq), 1)
            keep = key <= qry
        pending = None
        for hd in range(N_MLA_HEADS):
            ahead = hd + _MLA_AHEAD
            if ahead < N_MLA_HEADS:
                scores(sel, kk, ahead)
            elif nxt is not None:
                scores(nxt[0], nxt[1], ahead - N_MLA_HEADS)
            current = softmax(sel, hd, keep)
            if pending is not None:
                accumulate(sel, kk, hd - 1, *pending)
            pending = current
        accumulate(sel, kk, N_MLA_HEADS - 1, *pending)

    for hd in range(_MLA_AHEAD):
        scores(0, lo, hd)
    tile(0, lo, True, step_args(1))

    def body(it, carry):
        for u in range(_MLA_TILES_PER_ITER):
            t = 1 + it * _MLA_TILES_PER_ITER + u
            last = t + 1 == n_tiles
            sel_n, kk_n = step_args(t + 1)
            tile(*step_args(t), False, (jnp.where(last, 1, sel_n), jnp.where(last, hi, kk_n)))
        return carry

    lax.fori_loop(0, (n_tiles - 1) // _MLA_TILES_PER_ITER, body, 0)
    tile(1, hi, True, None)

    for sel, qt in ((0, lo), (1, hi)):
        out_t = jnp.concatenate([acc_ref[sel, hd, :V_HEAD, :] / acc_ref[sel, hd, V_HEAD:V_HEAD + 1, :]
                                 for hd in range(N_MLA_HEADS)], axis=0)
        o_ref[0, pl.ds(pl.multiple_of(qt * tq, tq), tq), :] = out_t.T.astype(o_ref.dtype)


def _mla_attn(q, k, vt, later_weights):
    b, s, w = q.shape
    nk = s // TQ_MLA
    assert nk % 2 == 0 and (nk - 1) % _MLA_TILES_PER_ITER == 0
    cast_in, cast_out, cast_shapes = _cast_plumbing(later_weights, (b, nk // 2))
    in_specs = [pl.BlockSpec((1, TQ_MLA, w), lambda bi, i: (bi, i, 0)),
                pl.BlockSpec((1, TQ_MLA, w), lambda bi, i: (bi, nk - 1 - i, 0)),
                pl.BlockSpec((1, s, w), lambda bi, i: (bi, 0, 0)),
                pl.BlockSpec((1, nk, MLA_WIDTH, TQ_MLA), lambda bi, i: (bi, 0, 0, 0))]
    return pl.pallas_call(
        _with_casts(_mla_attn_kernel, len(in_specs), 1, len(later_weights)),
        out_shape=(jax.ShapeDtypeStruct((b, s, MLA_WIDTH), _BF16), *cast_shapes),
        grid=(b, nk // 2),
        in_specs=in_specs + cast_in,
        out_specs=(pl.BlockSpec((1, s, MLA_WIDTH), lambda bi, i: (bi, 0, 0)), *cast_out),
        scratch_shapes=[pltpu.VMEM((2, TQ_MLA, w), _BF16),
                        pltpu.VMEM((N_MLA_HEADS, TQ_MLA, TQ_MLA), _F32),
                        pltpu.VMEM((2, N_MLA_HEADS, TQ_MLA), _F32),
                        pltpu.VMEM((2, N_MLA_HEADS, V_HEAD + _MLA_DENOM_ROWS, TQ_MLA), _F32)],
        compiler_params=_params(("arbitrary", "arbitrary")),
        name="mla_attn",
    )(q, q, k, vt, *later_weights)


def _retention_tables():
    h, L = N_RET_HEADS, RET_CHUNK
    log_gamma = jnp.log(1.0 - 2.0 ** (-5.0 - jnp.arange(h, dtype=_F32)))
    j = jnp.arange(L, dtype=_F32)
    diff = j[:, None] - j[None, :]
    intra = jnp.where(diff[None] >= 0,
                      jnp.exp(jnp.maximum(diff, 0.0)[None] * log_gamma[:, None, None]), 0.0)
    rowb = lambda t: jnp.broadcast_to(t.T[:, :, None], (h, L, LANES))
    k_to_end = jnp.exp((L - 1 - j)[:, None] * log_gamma[None, :])
    q_from_start = jnp.exp((j + 1)[:, None] * log_gamma[None, :])
    chunk_decay = jnp.broadcast_to(jnp.exp(L * log_gamma)[:, None, None], (h, RET_DK, RET_DV))
    return intra, rowb(k_to_end), rowb(q_from_start), chunk_decay


def _mix_xattn_kernel(x_ref, ymla_ref, yret_ref, gate_ref, wout_ref, gx_ref, wqk_ref, wvo_ref, o_ref):
    half = TM_MIX // 2
    rows = [slice(0, half), slice(half, TM_MIX)]
    mem_len = wqk_ref.shape[2] // N_XATTN_HEADS

    def retention_out(r):
        heads = []
        for hd in range(N_RET_HEADS):
            c = slice(hd * RET_DV, (hd + 1) * RET_DV)
            out = yret_ref[0, r, c].astype(_F32)
            mu = jnp.mean(out, axis=-1, keepdims=True)
            cen = out - mu
            var = jnp.mean(cen * cen, axis=-1, keepdims=True)
            g = gate_ref[0, r, c].astype(_F32)
            heads.append((cen * lax.rsqrt(var + EPS) * (g * jax.nn.sigmoid(g))).astype(_BF16))
        return jnp.concatenate(heads, axis=1)

    def out_proj(r):
        return (x_ref[0, r, :] + _dot(ymla_ref[0, r, :], wout_ref[:MLA_WIDTH, :])
                + _dot(retention_out(r), wout_ref[MLA_WIDTH:, :]))

    def scores(x1):
        h = _rms(x1, gx_ref[...]).astype(_BF16)
        return _dot(h, wqk_ref[0])

    def softmax(s):
        heads = []
        for hd in range(N_XATTN_HEADS):
            blk = s[:, hd * mem_len:(hd + 1) * mem_len]
            p = jnp.exp(blk - jnp.max(blk, axis=-1, keepdims=True))
            heads.append((p / jnp.sum(p, axis=-1, keepdims=True)).astype(_BF16))
        return jnp.concatenate(heads, axis=1)

    x1 = [out_proj(r) for r in rows]
    s0 = scores(x1[0])
    s1 = scores(x1[1])
    p0 = softmax(s0)
    o_ref[0, rows[0], :] = x1[0] + _dot(p0, wvo_ref[0])
    p1 = softmax(s1)
    o_ref[0, rows[1], :] = x1[1] + _dot(p1, wvo_ref[0])


def _mix_xattn(x, y_mla, y_ret, ret_gate, w_out, g_xattn, wqk, wvo, later_weights):
    b, s, d = x.shape
    tok = lambda w: pl.BlockSpec((1, TM_MIX, w), lambda bi, i: (bi, i, 0))
    per_batch = lambda a: pl.BlockSpec((1,) + a.shape[1:], lambda bi, i: (bi, 0, 0))
    cast_in, cast_out, cast_shapes = _cast_plumbing(later_weights, (b, s // TM_MIX))
    in_specs = [tok(d), tok(MLA_WIDTH), tok(RET_WIDTH), tok(RET_WIDTH), _const_spec(w_out.shape), _const_spec((1, d)),
                per_batch(wqk), per_batch(wvo)]
    return pl.pallas_call(
        _with_casts(_mix_xattn_kernel, len(in_specs), 1, len(later_weights)),
        out_shape=(jax.ShapeDtypeStruct((b, s, d), _F32), *cast_shapes),
        grid=(b, s // TM_MIX),
        in_specs=in_specs + cast_in,
        out_specs=(tok(d), *cast_out),
        compiler_params=_params(("arbitrary", "arbitrary")),
        name="mix_xattn",
    )(x, y_mla, y_ret, ret_gate, w_out, g_xattn.reshape(1, d), wqk, wvo, *later_weights)


def _conv_ffn_kernel(x_ref, g_ref, win_ref, cw_ref, cb_ref, wout_ref, gfin_ref, o_ref, gate_ref):
    halo = SUBLANES
    half = TM_FFN // 2

    @pl.when(pl.program_id(1) == 0)
    def _():
        gate_ref[:halo, :] = jnp.zeros((halo, D_FF), _F32)

    rows = [slice(0, half), slice(half, TM_FFN)]
    xs = [x_ref[0, r, :] for r in rows]
    hs = [_rms(x, g_ref[...]).astype(_BF16) for x in xs]

    def project(i):
        gate = _dot(hs[i], win_ref[:, :D_FF])
        up = _dot(hs[i], win_ref[:, D_FF:])
        gate_ref[halo + i * half:halo + (i + 1) * half, :] = gate
        return gate, up

    def activate(i, gate, up):
        conv = cb_ref[...] + gate * cw_ref[CONV_W - 1:CONV_W, :]
        for tap in range(CONV_W - 1):
            back = CONV_W - 1 - tap
            lo = halo + i * half - back
            conv = conv + gate_ref[lo:lo + half, :] * cw_ref[tap:tap + 1, :]
        return (conv * jax.nn.sigmoid(conv) * up).astype(_BF16)

    def finish(i, act):
        x3 = xs[i] + _dot(act, wout_ref[...])
        o_ref[0, rows[i], :] = _rms(x3, gfin_ref[...])

    g0, u0 = project(0)
    g1, u1 = project(1)
    a0 = activate(0, g0, u0)
    finish(0, a0)
    a1 = activate(1, g1, u1)
    finish(1, a1)
    gate_ref[:halo, :] = gate_ref[TM_FFN:, :]


def _conv_ffn(x, g_ffn, w_ffn_in, conv_w, conv_b, w_ffn_out, g_final):
    b, s, d = x.shape
    tok = pl.BlockSpec((1, TM_FFN, d), lambda bi, i: (bi, i, 0))
    return pl.pallas_call(
        _conv_ffn_kernel,
        out_shape=jax.ShapeDtypeStruct((b, s, d), _F32),
        grid=(b, s // TM_FFN),
        in_specs=[tok, _const_spec((1, d)), _const_spec(w_ffn_in.shape), _const_spec(conv_w.shape),
                  _const_spec((1, D_FF)), _const_spec(w_ffn_out.shape), _const_spec((1, d))],
        out_specs=tok,
        scratch_shapes=[pltpu.VMEM((TM_FFN + SUBLANES, D_FF), _F32)],
        compiler_params=_params(("arbitrary", "arbitrary")),
        name="conv_ffn",
    )(x, g_ffn.reshape(1, d), w_ffn_in, conv_w, conv_b.reshape(1, D_FF), w_ffn_out, g_final.reshape(1, d))


def kernel(x, mem, positions, g_mix, w_in, g_q_lat, w_uq, g_kv_lat, w_ukv, w_out, g_xattn, g_mem, w_xq,
           w_xkv, w_xo, g_ffn, w_ffn_in, conv_w, conv_b, w_ffn_out, g_final):
    assert w_in.shape[0] == 1, "one layer supported"
    l = 0
    w_in_t = w_in[l].T
    cos_t, sin_t, wlat_t, *wret_t, wqk, wvo = _prep(positions, w_in_t, mem, g_mem[l], w_xkv[l], w_xq[l], w_xo[l])
    wuq_p, wuk_p, wuvt_p = _permute_up_weights(w_uq[l], w_ukv[l])
    q, k, vt, y_ret, ret_gate, w_out_b = _in_proj(
        x, g_mix[l], wlat_t, wret_t, g_q_lat[l], wuq_p, g_kv_lat[l], wuk_p, wuvt_p, cos_t, sin_t, (w_out[l],))
    y_mla, w_ffn_in_b = _mla_attn(q, k, vt, (w_ffn_in[l],))
    x, w_ffn_out_b = _mix_xattn(x, y_mla, y_ret, ret_gate, w_out_b, g_xattn[l], wqk, wvo, (w_ffn_out[l],))
    return _conv_ffn(x, g_ffn[l], w_ffn_in_b, conv_w[l], conv_b[l], w_ffn_out_b, g_final)
```

```python
import functools
import math

import jax
import jax.numpy as jnp
from jax import lax
from jax.experimental import pallas as pl
from jax.experimental.pallas import tpu as pltpu

D_MODEL = 1024
EPS = 1e-6
ROPE_BASE = 10000.0
N_MLA_HEADS = 8
QK_NOPE = 64
QK_ROPE = 32
V_HEAD = 64
Q_LORA = 256
KV_LORA = 128
N_RET_HEADS = 4
RET_DK = 128
RET_DV = 128
RET_CHUNK = 128
MLA_WIDTH = N_MLA_HEADS * V_HEAD
RET_WIDTH = N_RET_HEADS * RET_DV
N_XATTN_HEADS = 4
XATTN_HEAD = D_MODEL // N_XATTN_HEADS
D_FF = 2816
CONV_W = 3

LANES = 128
SUBLANES = 8
_BF16_ROWS = 16
VMEM_LIMIT = 56 * 1024 * 1024

HEAD_PAD = LANES
ROPE_HALF = QK_ROPE // 2
MLA_QK_WIDTH = N_MLA_HEADS * HEAD_PAD
IN_LAT = Q_LORA + KV_LORA + HEAD_PAD

TM_TABLE = 1024
TM_IN = 1024
TQ_MLA = 256
TM_MIX = 1024
TM_FFN = 512

_BF16 = jnp.bfloat16
_F32 = jnp.float32


def _dot(a, b):
    return jnp.dot(a, b, preferred_element_type=_F32)


def _dot_nt(a, b):
    return lax.dot_general(a, b, (((1,), (1,)), ((), ())), preferred_element_type=_F32)


def _dot_tn(a, b):
    return lax.dot_general(a, b, (((0,), (0,)), ((), ())), preferred_element_type=_F32)


def _rms(x, g):
    inv = lax.rsqrt(jnp.mean(x * x, axis=-1, keepdims=True) + EPS)
    return x * inv * g


def _const_spec(shape):
    nd = len(shape)
    return pl.BlockSpec(shape, lambda *_: (0,) * nd, pipeline_mode=pl.Buffered(1))


def _params(sem):
    return pltpu.CompilerParams(dimension_semantics=sem, vmem_limit_bytes=VMEM_LIMIT)


def _with_casts(body, n_in, n_out, n_cast):
    def kern(*refs):
        ins, rest = refs[:n_in], refs[n_in:]
        cast_in, rest = rest[:n_cast], rest[n_cast:]
        outs, rest = rest[:n_out], rest[n_out:]
        cast_out, scratch = rest[:n_cast], rest[n_cast:]
        for src, dst in zip(cast_in, cast_out):
            dst[...] = src[...].astype(dst.dtype)
        body(*ins, *outs, *scratch)
    return kern


def _cast_plumbing(weights, grid):
    steps = math.prod(grid)
    if len(grid) == 1:
        block_of = lambda i: (i, 0)
    else:
        block_of = lambda bi, i: (bi * grid[1] + i, 0)
    in_specs, out_specs, out_shapes = [], [], []
    for w in weights:
        rows, cols = w.shape
        assert rows % steps == 0 and (rows // steps) % _BF16_ROWS == 0, (w.shape, steps)
        spec = pl.BlockSpec((rows // steps, cols), block_of)
        in_specs.append(spec)
        out_specs.append(spec)
        out_shapes.append(jax.ShapeDtypeStruct(w.shape, _BF16))
    return in_specs, out_specs, out_shapes


def _prep_kernel(pos_ref, inv_ref, wlat_ref, wq_ref, wk_ref, wv_ref, wg_ref, mem_ref, gmem_ref, wxk_ref, wxv_ref,
                 wxq_ref, wxo_ref, cos_ref, sin_ref, wlat_out, wq_out, wk_out, wv_out, wg_out, wqk_ref, wvo_ref,
                 *, lat_blocks, n_batch):
    half_rows = TM_IN // 2 // LANES
    lane = lax.broadcasted_iota(jnp.int32, (LANES, LANES), 1)
    column = lambda r: jnp.broadcast_to(pos_ref[r:r + 1, :], (LANES, LANES)).T
    for tile in range(TM_TABLE // TM_IN):
        for rr in range(half_rows):
            r0 = tile * 2 * half_rows + rr
            pos = jnp.where(lane < RET_DK // 2, column(r0), column(r0 + half_rows))
            ang = pos * inv_ref[...]
            out = slice((tile * half_rows + rr) * LANES, (tile * half_rows + rr + 1) * LANES)
            cos_ref[out, :] = jnp.cos(ang)
            sin_ref[out, :] = jnp.sin(ang)
    for src, dst in ((wq_ref, wq_out), (wk_ref, wk_out), (wv_ref, wv_out), (wg_ref, wg_out)):
        dst[...] = src[...].astype(_BF16)
    wlat_out[...] = jnp.where(pl.program_id(0) < lat_blocks, wlat_ref[...], 0.0).astype(_BF16)

    mem_n = _rms(mem_ref[pl.program_id(0) % n_batch], gmem_ref[...]).astype(_BF16)
    k_h = _dot(mem_n, wxk_ref[...].astype(_BF16)).astype(_BF16)
    v_h = _dot(mem_n, wxv_ref[...].astype(_BF16)).astype(_BF16)
    wqk_ref[0] = (_dot_nt(wxq_ref[...].astype(_BF16), k_h) * (1.0 / math.sqrt(XATTN_HEAD))).astype(_BF16)
    wvo_ref[0] = _dot(v_h, wxo_ref[...].astype(_BF16)).astype(_BF16)


def _prep(positions, w_in_t, mem, g_mem, w_xkv, w_xq, w_xo):
    t = positions.size
    steps = t // TM_TABLE
    n_in, d = w_in_t.shape
    b, m, _ = mem.shape
    assert steps == b * N_XATTN_HEADS
    row0 = Q_LORA + KV_LORA + QK_ROPE
    rows = RET_WIDTH // steps
    assert RET_WIDTH % steps == 0 and rows % _BF16_ROWS == 0 and row0 % rows == 0
    assert n_in == row0 + 4 * RET_WIDTH
    pos = positions.astype(_F32).reshape(t // LANES, LANES)
    f_ret = 1.0 / (ROPE_BASE ** (jnp.arange(0, RET_DK, 2, dtype=_F32) / RET_DK))
    inv = jnp.concatenate([f_ret, f_ret])
    group = lambda j: pl.BlockSpec((rows, d), lambda i: (row0 // rows + j * steps + i, 0))
    lat_blocks = row0 // rows
    kr_dst = (Q_LORA + KV_LORA + QK_NOPE) // rows
    assert QK_ROPE == rows and IN_LAT // rows == steps and (Q_LORA + KV_LORA) % rows == 0
    lat_src = pl.BlockSpec((rows, d), lambda i: (jnp.minimum(i, lat_blocks - 1), 0))
    lat_dst = pl.BlockSpec((rows, d), lambda i: (jnp.where(i < lat_blocks - 1, i, jnp.where(
        i == lat_blocks - 1, kr_dst, jnp.where(i <= kr_dst, i - 1, i))), 0))
    table = pl.BlockSpec((TM_TABLE // 2, LANES), lambda i: (i, 0))
    w_out = pl.BlockSpec((rows, d), lambda i: (i, 0))
    nh = N_XATTN_HEADS
    head_cols = lambda off: pl.BlockSpec((d, XATTN_HEAD), lambda i: (0, off + i // b))
    return pl.pallas_call(
        functools.partial(_prep_kernel, lat_blocks=lat_blocks, n_batch=b),
        out_shape=(jax.ShapeDtypeStruct((t // 2, LANES), _F32),) * 2
                  + (jax.ShapeDtypeStruct((IN_LAT, d), _BF16),)
                  + (jax.ShapeDtypeStruct((RET_WIDTH, d), _BF16),) * 4
                  + (jax.ShapeDtypeStruct((b, d, nh * m), _BF16), jax.ShapeDtypeStruct((b, nh * m, d), _BF16)),
        grid=(steps,),
        in_specs=[pl.BlockSpec((TM_TABLE // LANES, LANES), lambda i: (i, 0)),
                  pl.BlockSpec((1, LANES), lambda i: (0, 0)), lat_src, group(0), group(1), group(2), group(3),
                  _const_spec(mem.shape), _const_spec((1, d)),
                  head_cols(0), head_cols(nh), head_cols(0),
                  pl.BlockSpec((XATTN_HEAD, d), lambda i: (i // b, 0))],
        out_specs=(table, table, lat_dst, w_out, w_out, w_out, w_out,
                   pl.BlockSpec((1, d, m), lambda i: (i % b, 0, i // b)),
                   pl.BlockSpec((1, m, d), lambda i: (i % b, i // b, 0))),
        compiler_params=_params(("arbitrary",)),
        name="prep",
    )(pos, inv.reshape(1, LANES), w_in_t, w_in_t, w_in_t, w_in_t, w_in_t, mem, g_mem.reshape(1, d), w_xkv, w_xkv,
      w_xq, w_xo)


def _in_proj_kernel(x_ref, gmix_ref, wlat_ref, wrq_ref, wrk_ref, wrv_ref, wrg_ref, gq_ref, wuq_ref, gkv_ref, wuk_ref,
                    wuvt_ref, cos_ref, sin_ref, intra_ref, kend_ref, qstart_ref, decay_ref,
                    q_ref, k_ref, vt_ref, yret_ref, gate_ref, state_ref):
    h = _rms(x_ref[0], gmix_ref[...]).astype(_BF16)
    lane = lax.broadcasted_iota(jnp.int32, (TM_IN, LANES), 1)
    low = lane < RET_DK // 2
    lo = lax.broadcasted_iota(jnp.int32, (TM_IN // 2, LANES), 1) < RET_DK // 2

    def unpack(t):
        t_roll = pltpu.roll(t, RET_DK // 2, axis=1)
        return jnp.concatenate([jnp.where(lo, t, t_roll), jnp.where(lo, t_roll, t)], axis=0)

    cos_r, sin_full = unpack(cos_ref[...]), unpack(sin_ref[...])
    sin_r = jnp.where(low, -sin_full, sin_full)
    ratio = (RET_DK // 2) // ROPE_HALF
    rope_lane = (lane >= QK_NOPE) & (lane < QK_NOPE + QK_ROPE)
    src = jnp.where(lane < QK_NOPE + ROPE_HALF, lane - QK_NOPE, lane - QK_NOPE - ROPE_HALF) * ratio
    src = jnp.where(rope_lane, src, 0)
    g_cos = jnp.take_along_axis(cos_r, src, axis=1)
    g_sin = jnp.take_along_axis(sin_full, src, axis=1)
    cos_m = jnp.where(rope_lane, g_cos, 1.0)
    sin_m = jnp.where(rope_lane, jnp.where(lane < QK_NOPE + ROPE_HALF, -g_sin, g_sin), 0.0)

    first_half = lane < QK_NOPE + ROPE_HALF

    def swap_halves(t):
        return jnp.where(first_half, pltpu.roll(t, LANES - ROPE_HALF, axis=1), pltpu.roll(t, ROPE_HALF, axis=1))

    lat = _dot_nt(h, wlat_ref[...])
    c_q = lat[:, :Q_LORA]
    c_kv = lat[:, Q_LORA:Q_LORA + KV_LORA]
    kr = lat[:, Q_LORA + KV_LORA:IN_LAT]
    k_rope = kr * cos_m + swap_halves(kr) * sin_m

    def ret_rope(w_ref, mult):
        r = _dot_nt(h, w_ref[...])
        heads = []
        for hd in range(N_RET_HEADS):
            rh = r[:, hd * RET_DK:(hd + 1) * RET_DK]
            roped = rh * cos_r + pltpu.roll(rh, RET_DK // 2, axis=1) * sin_r
            if mult is not None:
                roped = roped * mult
            heads.append(roped.astype(_BF16))
        return heads

    rq = ret_rope(wrq_ref, None)
    rk = ret_rope(wrk_ref, RET_DK ** -0.5)
    rv = _dot_nt(h, wrv_ref[...]).astype(_BF16)

    cqn = _rms(c_q, gq_ref[...]).astype(_BF16)
    q = _dot(cqn, wuq_ref[...])
    scale = math.log2(math.e) / math.sqrt(QK_NOPE + QK_ROPE)
    for hd in range(N_MLA_HEADS):
        sl = slice(hd * HEAD_PAD, (hd + 1) * HEAD_PAD)
        q_ref[0, :, sl] = ((q[:, sl] * cos_m + swap_halves(q[:, sl]) * sin_m) * scale).astype(_BF16)

    ckvn = _rms(c_kv, gkv_ref[...]).astype(_BF16)
    k_nope = _dot(ckvn, wuk_ref[...])
    for hd in range(N_MLA_HEADS):
        sl = slice(hd * HEAD_PAD, (hd + 1) * HEAD_PAD)
        k_ref[0, :, sl] = (k_nope[:, sl] + k_rope).astype(_BF16)
    v_t = _dot_nt(wuvt_ref[...], ckvn).astype(_BF16)
    for j in range(TM_IN // TQ_MLA):
        vt_ref[0, j] = v_t[:, j * TQ_MLA:(j + 1) * TQ_MLA]

    rg = _dot_nt(h, wrg_ref[...]).astype(_BF16)
    gate_ref[0] = rg
    _retention_tile(rq, rk, rv, intra_ref, kend_ref, qstart_ref, decay_ref, state_ref, yret_ref)


def _retention_tile(q_heads, k_heads, v, intra_ref, kend_ref, qstart_ref, decay_ref, state_ref, o_ref):
    @pl.when(pl.program_id(1) == 0)
    def _():
        state_ref[...] = jnp.zeros(state_ref.shape, _F32)

    L = RET_CHUNK
    n_chunks = v.shape[0] // L
    units = [(c, hd) for c in range(n_chunks) for hd in range(N_RET_HEADS)]
    rows = lambda c: slice(c * L, (c + 1) * L)
    cols = lambda hd: slice(hd * RET_DV, (hd + 1) * RET_DV)
    scores, chunk_kv = {}, {}
    for c, hd in units:
        scores[c, hd] = _dot_nt(q_heads[hd][rows(c)], k_heads[hd][rows(c)])
    for c, hd in units:
        v_dec = (v[rows(c), cols(hd)].astype(_F32) * kend_ref[hd]).astype(_BF16)
        chunk_kv[c, hd] = _dot_tn(k_heads[hd][rows(c)], v_dec)
    prev_state = {}
    for hd in range(N_RET_HEADS):
        state = state_ref[hd]
        for c in range(n_chunks):
            prev_state[c, hd] = state.astype(_BF16)
            state = decay_ref[hd] * state + chunk_kv[c, hd]
        state_ref[hd] = state
    inner, cross = {}, {}
    for c, hd in units:
        inner[c, hd] = _dot((scores[c, hd] * intra_ref[hd]).astype(_BF16), v[rows(c), cols(hd)])
    for c, hd in units:
        cross[c, hd] = _dot(q_heads[hd][rows(c)], prev_state[c, hd])
    for c, hd in units:
        out = inner[c, hd] + cross[c, hd] * qstart_ref[hd]
        o_ref[0, rows(c), cols(hd)] = out.astype(o_ref.dtype)


def _permute_up_weights(w_uq, w_ukv):
    wq = w_uq.reshape(Q_LORA, N_MLA_HEADS, QK_NOPE + QK_ROPE)
    nope, r1, r2 = wq[..., :QK_NOPE], wq[..., QK_NOPE:QK_NOPE + ROPE_HALF], wq[..., QK_NOPE + ROPE_HALF:]
    z32 = jnp.zeros((Q_LORA, N_MLA_HEADS, HEAD_PAD - QK_NOPE - QK_ROPE), w_uq.dtype)
    wuq_p = jnp.concatenate([nope, r1, r2, z32], axis=-1).reshape(Q_LORA, MLA_QK_WIDTH).astype(_BF16)

    wkv = w_ukv.reshape(KV_LORA, N_MLA_HEADS, QK_NOPE + V_HEAD)
    zk = jnp.zeros((KV_LORA, N_MLA_HEADS, HEAD_PAD - QK_NOPE), w_ukv.dtype)
    wk = jnp.concatenate([wkv[..., :QK_NOPE], zk], axis=-1).reshape(KV_LORA, MLA_QK_WIDTH)
    wv_t = wkv[..., QK_NOPE:].reshape(KV_LORA, MLA_WIDTH).T
    return wuq_p, wk.astype(_BF16), wv_t.astype(_BF16)


def _in_proj(x, g_mix, wlat_t, wret_t, g_q_lat, wuq_p, g_kv_lat, wuk_p, wuvt_p, cos_t, sin_t, later_weights):
    b, s, d = x.shape
    nt = s // TM_IN
    per = TM_IN // TQ_MLA
    tok = lambda w: pl.BlockSpec((1, TM_IN, w), lambda bi, i: (bi, i, 0))
    tab = pl.BlockSpec((TM_IN // 2, LANES), lambda bi, i: (bi * nt + i, 0))
    bf = lambda w: jax.ShapeDtypeStruct((b, s, w), _BF16)
    ret_tables = _retention_tables()
    cast_in, cast_out, cast_shapes = _cast_plumbing(later_weights, (b, nt))
    in_specs = ([tok(d), _const_spec((1, d)), _const_spec(wlat_t.shape)] + [_const_spec(w.shape) for w in wret_t]
                + [_const_spec((1, Q_LORA)), _const_spec(wuq_p.shape), _const_spec((1, KV_LORA)),
                   _const_spec(wuk_p.shape), _const_spec(wuvt_p.shape), tab, tab]
                + [_const_spec(t.shape) for t in ret_tables])
    out_specs = (tok(MLA_QK_WIDTH), tok(MLA_QK_WIDTH),
                 pl.BlockSpec((1, per, MLA_WIDTH, TQ_MLA), lambda bi, i: (bi, i, 0, 0)), tok(RET_WIDTH), tok(RET_WIDTH))
    return pl.pallas_call(
        _with_casts(_in_proj_kernel, len(in_specs), len(out_specs), len(later_weights)),
        out_shape=(bf(MLA_QK_WIDTH), bf(MLA_QK_WIDTH),
                   jax.ShapeDtypeStruct((b, s // TQ_MLA, MLA_WIDTH, TQ_MLA), _BF16), bf(RET_WIDTH), bf(RET_WIDTH),
                   *cast_shapes),
        grid=(b, nt),
        in_specs=in_specs + cast_in,
        out_specs=(*out_specs, *cast_out),
        scratch_shapes=[pltpu.VMEM((N_RET_HEADS, RET_DK, RET_DV), _F32)],
        compiler_params=_params(("arbitrary", "arbitrary")),
        name="in_proj",
    )(x, g_mix.reshape(1, d), wlat_t, *wret_t, g_q_lat.reshape(1, Q_LORA), wuq_p, g_kv_lat.reshape(1, KV_LORA),
      wuk_p, wuvt_p, cos_t, sin_t, *ret_tables, *later_weights)


_MASK_VALUE = -0.7 * float(jnp.finfo(jnp.float32).max)
_MLA_AHEAD = 4
_MLA_TILES_PER_ITER = 5
_MLA_DENOM_ROWS = 16


def _mla_attn_kernel(qlo_ref, qhi_ref, k_ref, vt_ref, o_ref, q_ref, s_ref, m_ref, acc_ref):
    tq = TQ_MLA
    n_tiles = k_ref.shape[1] // tq
    lo = pl.program_id(1)
    hi = n_tiles - 1 - lo
    q_ref[0] = qlo_ref[0]
    q_ref[1] = qhi_ref[0]
    m_ref[...] = jnp.full(m_ref.shape, _MASK_VALUE, _F32)
    acc_ref[...] = jnp.zeros(acc_ref.shape, _F32)
    ones_rows = jnp.ones((_MLA_DENOM_ROWS, tq), _BF16)

    def step_args(t):
        sel = (t > lo).astype(jnp.int32)
        return sel, t - 1 - sel * lo

    def scores(sel, kk, hd):
        sl = slice(hd * HEAD_PAD, (hd + 1) * HEAD_PAD)
        ks = pl.multiple_of(kk * tq, tq)
        s_ref[hd] = _dot_nt(k_ref[0, pl.ds(ks, tq), sl], q_ref[sel, :, sl])

    def softmax(sel, hd, keep):
        row = slice(hd, hd + 1)

        def st():
            s = s_ref[hd]
            return s if keep is None else jnp.where(keep, s, _MASK_VALUE)

        m_prev = m_ref[sel, row, :]
        m_next = jnp.maximum(m_prev, jnp.max(st(), axis=0, keepdims=True))
        m_ref[sel, row, :] = m_next
        return jnp.exp2(m_prev - m_next), jnp.exp2(st() - m_next).astype(_BF16)

    def accumulate(sel, kk, hd, alpha, p):
        vt = jnp.concatenate([vt_ref[0, kk, hd * V_HEAD:(hd + 1) * V_HEAD, :], ones_rows], axis=0)
        acc_ref[sel, hd] = acc_ref[sel, hd] * alpha + _dot(vt, p)

    def tile(sel, kk, diagonal, nxt):
        keep = None
        if diagonal:
            key = lax.broadcasted_iota(jnp.int32, (tq, tq), 0)
            qry = lax.broadcasted_iota(jnp.int32, (tq, tq), 1)
            keep = key <= qry
        pending = None
        for hd in range(N_MLA_HEADS):
            ahead = hd + _MLA_AHEAD
            if ahead < N_MLA_HEADS:
                scores(sel, kk, ahead)
            elif nxt is not None:
                scores(nxt[0], nxt[1], ahead - N_MLA_HEADS)
            current = softmax(sel, hd, keep)
            if pending is not None:
                accumulate(sel, kk, hd - 1, *pending)
            pending = current
        accumulate(sel, kk, N_MLA_HEADS - 1, *pending)

    for hd in range(_MLA_AHEAD):
        scores(0, lo, hd)
    tile(0, lo, True, step_args(1))

    def body(it, carry):
        for u in range(_MLA_TILES_PER_ITER):
            t = 1 + it * _MLA_TILES_PER_ITER + u
            last = t + 1 == n_tiles
            sel_n, kk_n = step_args(t + 1)
            tile(*step_args(t), False, (jnp.where(last, 1, sel_n), jnp.where(last, hi, kk_n)))
        return carry

    lax.fori_loop(0, (n_tiles - 1) // _MLA_TILES_PER_ITER, body, 0)
    tile(1, hi, True, None)

    for sel, qt in ((0, lo), (1, hi)):
        out_t = jnp.concatenate([acc_ref[sel, hd, :V_HEAD, :] / acc_ref[sel, hd, V_HEAD:V_HEAD + 1, :]
                                 for hd in range(N_MLA_HEADS)], axis=0)
        o_ref[0, pl.ds(pl.multiple_of(qt * tq, tq), tq), :] = out_t.T.astype(o_ref.dtype)


def _mla_attn(q, k, vt, later_weights):
    b, s, w = q.shape
    nk = s // TQ_MLA
    assert nk % 2 == 0 and (nk - 1) % _MLA_TILES_PER_ITER == 0
    cast_in, cast_out, cast_shapes = _cast_plumbing(later_weights, (b, nk // 2))
    in_specs = [pl.BlockSpec((1, TQ_MLA, w), lambda bi, i: (bi, i, 0)),
                pl.BlockSpec((1, TQ_MLA, w), lambda bi, i: (bi, nk - 1 - i, 0)),
                pl.BlockSpec((1, s, w), lambda bi, i: (bi, 0, 0)),
                pl.BlockSpec((1, nk, MLA_WIDTH, TQ_MLA), lambda bi, i: (bi, 0, 0, 0))]
    return pl.pallas_call(
        _with_casts(_mla_attn_kernel, len(in_specs), 1, len(later_weights)),
        out_shape=(jax.ShapeDtypeStruct((b, s, MLA_WIDTH), _BF16), *cast_shapes),
        grid=(b, nk // 2),
        in_specs=in_specs + cast_in,
        out_specs=(pl.BlockSpec((1, s, MLA_WIDTH), lambda bi, i: (bi, 0, 0)), *cast_out),
        scratch_shapes=[pltpu.VMEM((2, TQ_MLA, w), _BF16),
                        pltpu.VMEM((N_MLA_HEADS, TQ_MLA, TQ_MLA), _F32),
                        pltpu.VMEM((2, N_MLA_HEADS, TQ_MLA), _F32),
                        pltpu.VMEM((2, N_MLA_HEADS, V_HEAD + _MLA_DENOM_ROWS, TQ_MLA), _F32)],
        compiler_params=_params(("arbitrary", "arbitrary")),
        name="mla_attn",
    )(q, q, k, vt, *later_weights)


def _retention_tables():
    h, L = N_RET_HEADS, RET_CHUNK
    log_gamma = jnp.log(1.0 - 2.0 ** (-5.0 - jnp.arange(h, dtype=_F32)))
    j = jnp.arange(L, dtype=_F32)
    diff = j[:, None] - j[None, :]
    intra = jnp.where(diff[None] >= 0,
                      jnp.exp(jnp.maximum(diff, 0.0)[None] * log_gamma[:, None, None]), 0.0)
    rowb = lambda t: jnp.broadcast_to(t.T[:, :, None], (h, L, LANES))
    k_to_end = jnp.exp((L - 1 - j)[:, None] * log_gamma[None, :])
    q_from_start = jnp.exp((j + 1)[:, None] * log_gamma[None, :])
    chunk_decay = jnp.broadcast_to(jnp.exp(L * log_gamma)[:, None, None], (h, RET_DK, RET_DV))
    return intra, rowb(k_to_end), rowb(q_from_start), chunk_decay


def _mix_xattn_kernel(x_ref, ymla_ref, yret_ref, gate_ref, wout_ref, gx_ref, wqk_ref, wvo_ref, o_ref):
    half = TM_MIX // 2
    rows = [slice(0, half), slice(half, TM_MIX)]
    mem_len = wqk_ref.shape[2] // N_XATTN_HEADS

    def retention_out(r):
        heads = []
        for hd in range(N_RET_HEADS):
            c = slice(hd * RET_DV, (hd + 1) * RET_DV)
            out = yret_ref[0, r, c].astype(_F32)
            mu = jnp.mean(out, axis=-1, keepdims=True)
            cen = out - mu
            var = jnp.mean(cen * cen, axis=-1, keepdims=True)
            g = gate_ref[0, r, c].astype(_F32)
            heads.append((cen * lax.rsqrt(var + EPS) * (g * jax.nn.sigmoid(g))).astype(_BF16))
        return jnp.concatenate(heads, axis=1)

    def out_proj(r):
        return (x_ref[0, r, :] + _dot(ymla_ref[0, r, :], wout_ref[:MLA_WIDTH, :])
                + _dot(retention_out(r), wout_ref[MLA_WIDTH:, :]))

    def scores(x1):
        h = _rms(x1, gx_ref[...]).astype(_BF16)
        return _dot(h, wqk_ref[0])

    def softmax(s):
        heads = []
        for hd in range(N_XATTN_HEADS):
            blk = s[:, hd * mem_len:(hd + 1) * mem_len]
            p = jnp.exp(blk - jnp.max(blk, axis=-1, keepdims=True))
            heads.append((p / jnp.sum(p, axis=-1, keepdims=True)).astype(_BF16))
        return jnp.concatenate(heads, axis=1)

    x1 = [out_proj(r) for r in rows]
    s0 = scores(x1[0])
    s1 = scores(x1[1])
    p0 = softmax(s0)
    o_ref[0, rows[0], :] = x1[0] + _dot(p0, wvo_ref[0])
    p1 = softmax(s1)
    o_ref[0, rows[1], :] = x1[1] + _dot(p1, wvo_ref[0])


def _mix_xattn(x, y_mla, y_ret, ret_gate, w_out, g_xattn, wqk, wvo, later_weights):
    b, s, d = x.shape
    tok = lambda w: pl.BlockSpec((1, TM_MIX, w), lambda bi, i: (bi, i, 0))
    per_batch = lambda a: pl.BlockSpec((1,) + a.shape[1:], lambda bi, i: (bi, 0, 0))
    cast_in, cast_out, cast_shapes = _cast_plumbing(later_weights, (b, s // TM_MIX))
    in_specs = [tok(d), tok(MLA_WIDTH), tok(RET_WIDTH), tok(RET_WIDTH), _const_spec(w_out.shape), _const_spec((1, d)),
                per_batch(wqk), per_batch(wvo)]
    return pl.pallas_call(
        _with_casts(_mix_xattn_kernel, len(in_specs), 1, len(later_weights)),
        out_shape=(jax.ShapeDtypeStruct((b, s, d), _F32), *cast_shapes),
        grid=(b, s // TM_MIX),
        in_specs=in_specs + cast_in,
        out_specs=(tok(d), *cast_out),
        compiler_params=_params(("arbitrary", "arbitrary")),
        name="mix_xattn",
    )(x, y_mla, y_ret, ret_gate, w_out, g_xattn.reshape(1, d), wqk, wvo, *later_weights)


def _conv_ffn_kernel(x_ref, g_ref, win_ref, cw_ref, cb_ref, wout_ref, gfin_ref, o_ref, gate_ref):
    halo = SUBLANES
    half = TM_FFN // 2

    @pl.when(pl.program_id(1) == 0)
    def _():
        gate_ref[:halo, :] = jnp.zeros((halo, D_FF), _F32)

    rows = [slice(0, half), slice(half, TM_FFN)]
    xs = [x_ref[0, r, :] for r in rows]
    hs = [_rms(x, g_ref[...]).astype(_BF16) for x in xs]

    def project(i):
        gate = _dot(hs[i], win_ref[:, :D_FF])
        up = _dot(hs[i], win_ref[:, D_FF:])
        gate_ref[halo + i * half:halo + (i + 1) * half, :] = gate
        return gate, up

    def activate(i, gate, up):
        conv = cb_ref[...] + gate * cw_ref[CONV_W - 1:CONV_W, :]
        for tap in range(CONV_W - 1):
            back = CONV_W - 1 - tap
            lo = halo + i * half - back
            conv = conv + gate_ref[lo:lo + half, :] * cw_ref[tap:tap + 1, :]
        return (conv * jax.nn.sigmoid(conv) * up).astype(_BF16)

    def finish(i, act):
        x3 = xs[i] + _dot(act, wout_ref[...])
        o_ref[0, rows[i], :] = _rms(x3, gfin_ref[...])

    g0, u0 = project(0)
    g1, u1 = project(1)
    a0 = activate(0, g0, u0)
    finish(0, a0)
    a1 = activate(1, g1, u1)
    finish(1, a1)
    gate_ref[:halo, :] = gate_ref[TM_FFN:, :]


def _conv_ffn(x, g_ffn, w_ffn_in, conv_w, conv_b, w_ffn_out, g_final):
    b, s, d = x.shape
    tok = pl.BlockSpec((1, TM_FFN, d), lambda bi, i: (bi, i, 0))
    return pl.pallas_call(
        _conv_ffn_kernel,
        out_shape=jax.ShapeDtypeStruct((b, s, d), _F32),
        grid=(b, s // TM_FFN),
        in_specs=[tok, _const_spec((1, d)), _const_spec(w_ffn_in.shape), _const_spec(conv_w.shape),
                  _const_spec((1, D_FF)), _const_spec(w_ffn_out.shape), _const_spec((1, d))],
        out_specs=tok,
        scratch_shapes=[pltpu.VMEM((TM_FFN + SUBLANES, D_FF), _F32)],
        compiler_params=_params(("arbitrary", "arbitrary")),
        name="conv_ffn",
    )(x, g_ffn.reshape(1, d), w_ffn_in, conv_w, conv_b.reshape(1, D_FF), w_ffn_out, g_final.reshape(1, d))


def kernel(x, mem, positions, g_mix, w_in, g_q_lat, w_uq, g_kv_lat, w_ukv, w_out, g_xattn, g_mem, w_xq,
           w_xkv, w_xo, g_ffn, w_ffn_in, conv_w, conv_b, w_ffn_out, g_final):
    assert w_in.shape[0] == 1, "one layer supported"
    l = 0
    w_in_t = w_in[l].T
    cos_t, sin_t, wlat_t, *wret_t, wqk, wvo = _prep(positions, w_in_t, mem, g_mem[l], w_xkv[l], w_xq[l], w_xo[l])
    wuq_p, wuk_p, wuvt_p = _permute_up_weights(w_uq[l], w_ukv[l])
    q, k, vt, y_ret, ret_gate, w_out_b = _in_proj(
        x, g_mix[l], wlat_t, wret_t, g_q_lat[l], wuq_p, g_kv_lat[l], wuk_p, wuvt_p, cos_t, sin_t, (w_out[l],))
    y_mla, w_ffn_in_b = _mla_attn(q, k, vt, (w_ffn_in[l],))
    x, w_ffn_out_b = _mix_xattn(x, y_mla, y_ret, ret_gate, w_out_b, g_xattn[l], wqk, wvo, (w_ffn_out[l],))
    return _conv_ffn(x, g_ffn[l], w_ffn_in_b, conv_w[l], conv_b[l], w_ffn_out_b, g_final)
```

```python
import functools
import math

import jax
import jax.numpy as jnp
from jax import lax
from jax.experimental import pallas as pl
from jax.experimental.pallas import tpu as pltpu

D_MODEL = 1024
EPS = 1e-6
ROPE_BASE = 10000.0
N_MLA_HEADS = 8
QK_NOPE = 64
QK_ROPE = 32
V_HEAD = 64
Q_LORA = 256
KV_LORA = 128
N_RET_HEADS = 4
RET_DK = 128
RET_DV = 128
RET_CHUNK = 128
MLA_WIDTH = N_MLA_HEADS * V_HEAD
RET_WIDTH = N_RET_HEADS * RET_DV
N_XATTN_HEADS = 4
XATTN_HEAD = D_MODEL // N_XATTN_HEADS
D_FF = 2816
CONV_W = 3

LANES = 128
SUBLANES = 8
_BF16_ROWS = 16
VMEM_LIMIT = 56 * 1024 * 1024

HEAD_PAD = LANES
ROPE_HALF = QK_ROPE // 2
MLA_QK_WIDTH = N_MLA_HEADS * HEAD_PAD
IN_LAT = Q_LORA + KV_LORA + HEAD_PAD

TM_TABLE = 1024
TM_IN = 1024
TQ_MLA = 256
TM_MIX = 1024
TM_FFN = 512

_BF16 = jnp.bfloat16
_F32 = jnp.float32


def _dot(a, b):
    return jnp.dot(a, b, preferred_element_type=_F32)


def _dot_nt(a, b):
    return lax.dot_general(a, b, (((1,), (1,)), ((), ())), preferred_element_type=_F32)


def _dot_tn(a, b):
    return lax.dot_general(a, b, (((0,), (0,)), ((), ())), preferred_element_type=_F32)


def _rms(x, g):
    inv = lax.rsqrt(jnp.mean(x * x, axis=-1, keepdims=True) + EPS)
    return x * inv * g


def _const_spec(shape):
    nd = len(shape)
    return pl.BlockSpec(shape, lambda *_: (0,) * nd, pipeline_mode=pl.Buffered(1))


def _params(sem):
    return pltpu.CompilerParams(dimension_semantics=sem, vmem_limit_bytes=VMEM_LIMIT)


def _with_casts(body, n_in, n_out, n_cast):
    def kern(*refs):
        ins, rest = refs[:n_in], refs[n_in:]
        cast_in, rest = rest[:n_cast], rest[n_cast:]
        outs, rest = rest[:n_out], rest[n_out:]
        cast_out, scratch = rest[:n_cast], rest[n_cast:]
        for src, dst in zip(cast_in, cast_out):
            dst[...] = src[...].astype(dst.dtype)
        body(*ins, *outs, *scratch)
    return kern


def _cast_plumbing(weights, grid):
    steps = math.prod(grid)
    if len(grid) == 1:
        block_of = lambda i: (i, 0)
    else:
        block_of = lambda bi, i: (bi * grid[1] + i, 0)
    in_specs, out_specs, out_shapes = [], [], []
    for w in weights:
        rows, cols = w.shape
        assert rows % steps == 0 and (rows // steps) % _BF16_ROWS == 0, (w.shape, steps)
        spec = pl.BlockSpec((rows // steps, cols), block_of)
        in_specs.append(spec)
        out_specs.append(spec)
        out_shapes.append(jax.ShapeDtypeStruct(w.shape, _BF16))
    return in_specs, out_specs, out_shapes


def _prep_kernel(pos_ref, inv_ref, wlat_ref, wq_ref, wk_ref, wv_ref, wg_ref, mem_ref, gmem_ref, wxk_ref, wxv_ref,
                 wxq_ref, wxo_ref, cos_ref, sin_ref, wlat_out, wq_out, wk_out, wv_out, wg_out, wqk_ref, wvo_ref,
                 *, lat_blocks, n_batch):
    half_rows = TM_IN // 2 // LANES
    lane = lax.broadcasted_iota(jnp.int32, (LANES, LANES), 1)
    column = lambda r: jnp.broadcast_to(pos_ref[r:r + 1, :], (LANES, LANES)).T
    for tile in range(TM_TABLE // TM_IN):
        for rr in range(half_rows):
            r0 = tile * 2 * half_rows + rr
            pos = jnp.where(lane < RET_DK // 2, column(r0), column(r0 + half_rows))
            ang = pos * inv_ref[...]
            out = slice((tile * half_rows + rr) * LANES, (tile * half_rows + rr + 1) * LANES)
            cos_ref[out, :] = jnp.cos(ang)
            sin_ref[out, :] = jnp.sin(ang)
    for src, dst in ((wq_ref, wq_out), (wk_ref, wk_out), (wv_ref, wv_out), (wg_ref, wg_out)):
        dst[...] = src[...].astype(_BF16)
    wlat_out[...] = jnp.where(pl.program_id(0) < lat_blocks, wlat_ref[...], 0.0).astype(_BF16)

    mem_n = _rms(mem_ref[pl.program_id(0) % n_batch], gmem_ref[...]).astype(_BF16)
    k_h = _dot(mem_n, wxk_ref[...].astype(_BF16)).astype(_BF16)
    v_h = _dot(mem_n, wxv_ref[...].astype(_BF16)).astype(_BF16)
    wqk_ref[0] = (_dot_nt(wxq_ref[...].astype(_BF16), k_h) * (1.0 / math.sqrt(XATTN_HEAD))).astype(_BF16)
    wvo_ref[0] = _dot(v_h, wxo_ref[...].astype(_BF16)).astype(_BF16)


def _prep(positions, w_in_t, mem, g_mem, w_xkv, w_xq, w_xo):
    t = positions.size
    steps = t // TM_TABLE
    n_in, d = w_in_t.shape
    b, m, _ = mem.shape
    assert steps == b * N_XATTN_HEADS
    row0 = Q_LORA + KV_LORA + QK_ROPE
    rows = RET_WIDTH // steps
    assert RET_WIDTH % steps == 0 and rows % _BF16_ROWS == 0 and row0 % rows == 0
    assert n_in == row0 + 4 * RET_WIDTH
    pos = positions.astype(_F32).reshape(t // LANES, LANES)
    f_ret = 1.0 / (ROPE_BASE ** (jnp.arange(0, RET_DK, 2, dtype=_F32) / RET_DK))
    inv = jnp.concatenate([f_ret, f_ret])
    group = lambda j: pl.BlockSpec((rows, d), lambda i: (row0 // rows + j * steps + i, 0))
    lat_blocks = row0 // rows
    kr_dst = (Q_LORA + KV_LORA + QK_NOPE) // rows
    assert QK_ROPE == rows and IN_LAT // rows == steps and (Q_LORA + KV_LORA) % rows == 0
    lat_src = pl.BlockSpec((rows, d), lambda i: (jnp.minimum(i, lat_blocks - 1), 0))
    lat_dst = pl.BlockSpec((rows, d), lambda i: (jnp.where(i < lat_blocks - 1, i, jnp.where(
        i == lat_blocks - 1, kr_dst, jnp.where(i <= kr_dst, i - 1, i))), 0))
    table = pl.BlockSpec((TM_TABLE // 2, LANES), lambda i: (i, 0))
    w_out = pl.BlockSpec((rows, d), lambda i: (i, 0))
    nh = N_XATTN_HEADS
    head_cols = lambda off: pl.BlockSpec((d, XATTN_HEAD), lambda i: (0, off + i // b))
    return pl.pallas_call(
        functools.partial(_prep_kernel, lat_blocks=lat_blocks, n_batch=b),
        out_shape=(jax.ShapeDtypeStruct((t // 2, LANES), _F32),) * 2
                  + (jax.ShapeDtypeStruct((IN_LAT, d), _BF16),)
                  + (jax.ShapeDtypeStruct((RET_WIDTH, d), _BF16),) * 4
                  + (jax.ShapeDtypeStruct((b, d, nh * m), _BF16), jax.ShapeDtypeStruct((b, nh * m, d), _BF16)),
        grid=(steps,),
        in_specs=[pl.BlockSpec((TM_TABLE // LANES, LANES), lambda i: (i, 0)),
                  pl.BlockSpec((1, LANES), lambda i: (0, 0)), lat_src, group(0), group(1), group(2), group(3),
                  _const_spec(mem.shape), _const_spec((1, d)),
                  head_cols(0), head_cols(nh), head_cols(0),
                  pl.BlockSpec((XATTN_HEAD, d), lambda i: (i // b, 0))],
        out_specs=(table, table, lat_dst, w_out, w_out, w_out, w_out,
                   pl.BlockSpec((1, d, m), lambda i: (i % b, 0, i // b)),
                   pl.BlockSpec((1, m, d), lambda i: (i % b, i // b, 0))),
        compiler_params=_params(("arbitrary",)),
        name="prep",
    )(pos, inv.reshape(1, LANES), w_in_t, w_in_t, w_in_t, w_in_t, w_in_t, mem, g_mem.reshape(1, d), w_xkv, w_xkv,
      w_xq, w_xo)


def _in_proj_kernel(x_ref, gmix_ref, wlat_ref, wrq_ref, wrk_ref, wrv_ref, wrg_ref, gq_ref, wuq_ref, gkv_ref, wuk_ref,
                    wuvt_ref, cos_ref, sin_ref, intra_ref, kend_ref, qstart_ref, decay_ref,
                    q_ref, k_ref, vt_ref, yret_ref, gate_ref, state_ref):
    h = _rms(x_ref[0], gmix_ref[...]).astype(_BF16)
    lane = lax.broadcasted_iota(jnp.int32, (TM_IN, LANES), 1)
    low = lane < RET_DK // 2
    lo = lax.broadcasted_iota(jnp.int32, (TM_IN // 2, LANES), 1) < RET_DK // 2

    def unpack(t):
        t_roll = pltpu.roll(t, RET_DK // 2, axis=1)
        return jnp.concatenate([jnp.where(lo, t, t_roll), jnp.where(lo, t_roll, t)], axis=0)

    cos_r, sin_full = unpack(cos_ref[...]), unpack(sin_ref[...])
    sin_r = jnp.where(low, -sin_full, sin_full)
    ratio = (RET_DK // 2) // ROPE_HALF
    rope_lane = (lane >= QK_NOPE) & (lane < QK_NOPE + QK_ROPE)
    src = jnp.where(lane < QK_NOPE + ROPE_HALF, lane - QK_NOPE, lane - QK_NOPE - ROPE_HALF) * ratio
    src = jnp.where(rope_lane, src, 0)
    g_cos = jnp.take_along_axis(cos_r, src, axis=1)
    g_sin = jnp.take_along_axis(sin_full, src, axis=1)
    cos_m = jnp.where(rope_lane, g_cos, 1.0)
    sin_m = jnp.where(rope_lane, jnp.where(lane < QK_NOPE + ROPE_HALF, -g_sin, g_sin), 0.0)

    first_half = lane < QK_NOPE + ROPE_HALF

    def swap_halves(t):
        return jnp.where(first_half, pltpu.roll(t, LANES - ROPE_HALF, axis=1), pltpu.roll(t, ROPE_HALF, axis=1))

    lat = _dot_nt(h, wlat_ref[...])
    c_q = lat[:, :Q_LORA]
    c_kv = lat[:, Q_LORA:Q_LORA + KV_LORA]
    kr = lat[:, Q_LORA + KV_LORA:IN_LAT]
    k_rope = kr * cos_m + swap_halves(kr) * sin_m

    def ret_rope(w_ref, mult):
        r = _dot_nt(h, w_ref[...])
        heads = []
        for hd in range(N_RET_HEADS):
            rh = r[:, hd * RET_DK:(hd + 1) * RET_DK]
            roped = rh * cos_r + pltpu.roll(rh, RET_DK // 2, axis=1) * sin_r
            if mult is not None:
                roped = roped * mult
            heads.append(roped.astype(_BF16))
        return heads

    rq = ret_rope(wrq_ref, None)
    rk = ret_rope(wrk_ref, RET_DK ** -0.5)
    rv = _dot_nt(h, wrv_ref[...]).astype(_BF16)

    cqn = _rms(c_q, gq_ref[...]).astype(_BF16)
    q = _dot(cqn, wuq_ref[...])
    scale = math.log2(math.e) / math.sqrt(QK_NOPE + QK_ROPE)
    for hd in range(N_MLA_HEADS):
        sl = slice(hd * HEAD_PAD, (hd + 1) * HEAD_PAD)
        q_ref[0, :, sl] = ((q[:, sl] * cos_m + swap_halves(q[:, sl]) * sin_m) * scale).astype(_BF16)

    ckvn = _rms(c_kv, gkv_ref[...]).astype(_BF16)
    k_nope = _dot(ckvn, wuk_ref[...])
    for hd in range(N_MLA_HEADS):
        sl = slice(hd * HEAD_PAD, (hd + 1) * HEAD_PAD)
        k_ref[0, :, sl] = (k_nope[:, sl] + k_rope).astype(_BF16)
    v_t = _dot_nt(wuvt_ref[...], ckvn).astype(_BF16)
    for j in range(TM_IN // TQ_MLA):
        vt_ref[0, j] = v_t[:, j * TQ_MLA:(j + 1) * TQ_MLA]

    rg = _dot_nt(h, wrg_ref[...]).astype(_BF16)
    gate_ref[0] = rg
    _retention_tile(rq, rk, rv, intra_ref, kend_ref, qstart_ref, decay_ref, state_ref, yret_ref)


def _retention_tile(q_heads, k_heads, v, intra_ref, kend_ref, qstart_ref, decay_ref, state_ref, o_ref):
    @pl.when(pl.program_id(1) == 0)
    def _():
        state_ref[...] = jnp.zeros(state_ref.shape, _F32)

    L = RET_CHUNK
    n_chunks = v.shape[0] // L
    units = [(c, hd) for c in range(n_chunks) for hd in range(N_RET_HEADS)]
    rows = lambda c: slice(c * L, (c + 1) * L)
    cols = lambda hd: slice(hd * RET_DV, (hd + 1) * RET_DV)
    scores, chunk_kv = {}, {}
    for c, hd in units:
        scores[c, hd] = _dot_nt(q_heads[hd][rows(c)], k_heads[hd][rows(c)])
    for c, hd in units:
        v_dec = (v[rows(c), cols(hd)].astype(_F32) * kend_ref[hd]).astype(_BF16)
        chunk_kv[c, hd] = _dot_tn(k_heads[hd][rows(c)], v_dec)
    prev_state = {}
    for hd in range(N_RET_HEADS):
        state = state_ref[hd]
        for c in range(n_chunks):
            prev_state[c, hd] = state.astype(_BF16)
            state = decay_ref[hd] * state + chunk_kv[c, hd]
        state_ref[hd] = state
    inner, cross = {}, {}
    for c, hd in units:
        inner[c, hd] = _dot((scores[c, hd] * intra_ref[hd]).astype(_BF16), v[rows(c), cols(hd)])
    for c, hd in units:
        cross[c, hd] = _dot(q_heads[hd][rows(c)], prev_state[c, hd])
    for c, hd in units:
        out = inner[c, hd] + cross[c, hd] * qstart_ref[hd]
        o_ref[0, rows(c), cols(hd)] = out.astype(o_ref.dtype)


def _permute_up_weights(w_uq, w_ukv):
    wq = w_uq.reshape(Q_LORA, N_MLA_HEADS, QK_NOPE + QK_ROPE)
    nope, r1, r2 = wq[..., :QK_NOPE], wq[..., QK_NOPE:QK_NOPE + ROPE_HALF], wq[..., QK_NOPE + ROPE_HALF:]
    z32 = jnp.zeros((Q_LORA, N_MLA_HEADS, HEAD_PAD - QK_NOPE - QK_ROPE), w_uq.dtype)
    wuq_p = jnp.concatenate([nope, r1, r2, z32], axis=-1).reshape(Q_LORA, MLA_QK_WIDTH).astype(_BF16)

    wkv = w_ukv.reshape(KV_LORA, N_MLA_HEADS, QK_NOPE + V_HEAD)
    zk = jnp.zeros((KV_LORA, N_MLA_HEADS, HEAD_PAD - QK_NOPE), w_ukv.dtype)
    wk = jnp.concatenate([wkv[..., :QK_NOPE], zk], axis=-1).reshape(KV_LORA, MLA_QK_WIDTH)
    wv_t = wkv[..., QK_NOPE:].reshape(KV_LORA, MLA_WIDTH).T
    return wuq_p, wk.astype(_BF16), wv_t.astype(_BF16)


def _in_proj(x, g_mix, wlat_t, wret_t, g_q_lat, wuq_p, g_kv_lat, wuk_p, wuvt_p, cos_t, sin_t, later_weights):
    b, s, d = x.shape
    nt = s // TM_IN
    per = TM_IN // TQ_MLA
    tok = lambda w: pl.BlockSpec((1, TM_IN, w), lambda bi, i: (bi, i, 0))
    tab = pl.BlockSpec((TM_IN // 2, LANES), lambda bi, i: (bi * nt + i, 0))
    bf = lambda w: jax.ShapeDtypeStruct((b, s, w), _BF16)
    ret_tables = _retention_tables()
    cast_in, cast_out, cast_shapes = _cast_plumbing(later_weights, (b, nt))
    in_specs = ([tok(d), _const_spec((1, d)), _const_spec(wlat_t.shape)] + [_const_spec(w.shape) for w in wret_t]
                + [_const_spec((1, Q_LORA)), _const_spec(wuq_p.shape), _const_spec((1, KV_LORA)),
                   _const_spec(wuk_p.shape), _const_spec(wuvt_p.shape), tab, tab]
                + [_const_spec(t.shape) for t in ret_tables])
    out_specs = (tok(MLA_QK_WIDTH), tok(MLA_QK_WIDTH),
                 pl.BlockSpec((1, per, MLA_WIDTH, TQ_MLA), lambda bi, i: (bi, i, 0, 0)), tok(RET_WIDTH), tok(RET_WIDTH))
    return pl.pallas_call(
        _with_casts(_in_proj_kernel, len(in_specs), len(out_specs), len(later_weights)),
        out_shape=(bf(MLA_QK_WIDTH), bf(MLA_QK_WIDTH),
                   jax.ShapeDtypeStruct((b, s // TQ_MLA, MLA_WIDTH, TQ_MLA), _BF16), bf(RET_WIDTH), bf(RET_WIDTH),
                   *cast_shapes),
        grid=(b, nt),
        in_specs=in_specs + cast_in,
        out_specs=(*out_specs, *cast_out),
        scratch_shapes=[pltpu.VMEM((N_RET_HEADS, RET_DK, RET_DV), _F32)],
        compiler_params=_params(("arbitrary", "arbitrary")),
        name="in_proj",
    )(x, g_mix.reshape(1, d), wlat_t, *wret_t, g_q_lat.reshape(1, Q_LORA), wuq_p, g_kv_lat.reshape(1, KV_LORA),
      wuk_p, wuvt_p, cos_t, sin_t, *ret_tables, *later_weights)


_MASK_VALUE = -0.7 * float(jnp.finfo(jnp.float32).max)
_MLA_AHEAD = 5
_MLA_TILES_PER_ITER = 5
_MLA_DENOM_ROWS = 16


def _mla_attn_kernel(qlo_ref, qhi_ref, k_ref, vt_ref, o_ref, q_ref, s_ref, m_ref, acc_ref):
    tq = TQ_MLA
    n_tiles = k_ref.shape[1] // tq
    lo = pl.program_id(1)
    hi = n_tiles - 1 - lo
    q_ref[0] = qlo_ref[0]
    q_ref[1] = qhi_ref[0]
    m_ref[...] = jnp.full(m_ref.shape, _MASK_VALUE, _F32)
    acc_ref[...] = jnp.zeros(acc_ref.shape, _F32)
    ones_rows = jnp.ones((_MLA_DENOM_ROWS, tq), _BF16)

    def step_args(t):
        sel = (t > lo).astype(jnp.int32)
        return sel, t - 1 - sel * lo

    def scores(sel, kk, hd):
        sl = slice(hd * HEAD_PAD, (hd + 1) * HEAD_PAD)
        ks = pl.multiple_of(kk * tq, tq)
        s_ref[hd] = _dot_nt(k_ref[0, pl.ds(ks, tq), sl], q_ref[sel, :, sl])

    def softmax(sel, hd, keep):
        row = slice(hd, hd + 1)

        def st():
            s = s_ref[hd]
            return s if keep is None else jnp.where(keep, s, _MASK_VALUE)

        m_prev = m_ref[sel, row, :]
        m_next = jnp.maximum(m_prev, jnp.max(st(), axis=0, keepdims=True))
        m_ref[sel, row, :] = m_next
        return jnp.exp2(m_prev - m_next), jnp.exp2(st() - m_next).astype(_BF16)

    def accumulate(sel, kk, hd, alpha, p):
        vt = jnp.concatenate([vt_ref[0, kk, hd * V_HEAD:(hd + 1) * V_HEAD, :], ones_rows], axis=0)
        acc_ref[sel, hd] = acc_ref[sel, hd] * alpha + _dot(vt, p)

    def tile(sel, kk, diagonal, nxt):
        keep = None
        if diagonal:
            key = lax.broadcasted_iota(jnp.int32, (tq, tq), 0)
            qry = lax.broadcasted_iota(jnp.int32, (tq, tq), 1)
            keep = key <= qry
        pending = None
        for hd in range(N_MLA_HEADS):
            ahead = hd + _MLA_AHEAD
            if ahead < N_MLA_HEADS:
                scores(sel, kk, ahead)
            elif nxt is not None:
                scores(nxt[0], nxt[1], ahead - N_MLA_HEADS)
            current = softmax(sel, hd, keep)
            if pending is not None:
                accumulate(sel, kk, hd - 1, *pending)
            pending = current
        accumulate(sel, kk, N_MLA_HEADS - 1, *pending)

    for hd in range(_MLA_AHEAD):
        scores(0, lo, hd)
    tile(0, lo, True, step_args(1))

    def body(it, carry):
        for u in range(_MLA_TILES_PER_ITER):
            t = 1 + it * _MLA_TILES_PER_ITER + u
            last = t + 1 == n_tiles
            sel_n, kk_n = step_args(t + 1)
            tile(*step_args(t), False, (jnp.where(last, 1, sel_n), jnp.where(last, hi, kk_n)))
        return carry

    lax.fori_loop(0, (n_tiles - 1) // _MLA_TILES_PER_ITER, body, 0)
    tile(1, hi, True, None)

    for sel, qt in ((0, lo), (1, hi)):
        out_t = jnp.concatenate([acc_ref[sel, hd, :V_HEAD, :] / acc_ref[sel, hd, V_HEAD:V_HEAD + 1, :]
                                 for hd in range(N_MLA_HEADS)], axis=0)
        o_ref[0, pl.ds(pl.multiple_of(qt * tq, tq), tq), :] = out_t.T.astype(o_ref.dtype)


def _mla_attn(q, k, vt, later_weights):
    b, s, w = q.shape
    nk = s // TQ_MLA
    assert nk % 2 == 0 and (nk - 1) % _MLA_TILES_PER_ITER == 0
    cast_in, cast_out, cast_shapes = _cast_plumbing(later_weights, (b, nk // 2))
    in_specs = [pl.BlockSpec((1, TQ_MLA, w), lambda bi, i: (bi, i, 0)),
                pl.BlockSpec((1, TQ_MLA, w), lambda bi, i: (bi, nk - 1 - i, 0)),
                pl.BlockSpec((1, s, w), lambda bi, i: (bi, 0, 0)),
                pl.BlockSpec((1, nk, MLA_WIDTH, TQ_MLA), lambda bi, i: (bi, 0, 0, 0))]
    return pl.pallas_call(
        _with_casts(_mla_attn_kernel, len(in_specs), 1, len(later_weights)),
        out_shape=(jax.ShapeDtypeStruct((b, s, MLA_WIDTH), _BF16), *cast_shapes),
        grid=(b, nk // 2),
        in_specs=in_specs + cast_in,
        out_specs=(pl.BlockSpec((1, s, MLA_WIDTH), lambda bi, i: (bi, 0, 0)), *cast_out),
        scratch_shapes=[pltpu.VMEM((2, TQ_MLA, w), _BF16),
                        pltpu.VMEM((N_MLA_HEADS, TQ_MLA, TQ_MLA), _F32),
                        pltpu.VMEM((2, N_MLA_HEADS, TQ_MLA), _F32),
                        pltpu.VMEM((2, N_MLA_HEADS, V_HEAD + _MLA_DENOM_ROWS, TQ_MLA), _F32)],
        compiler_params=_params(("arbitrary", "arbitrary")),
        name="mla_attn",
    )(q, q, k, vt, *later_weights)


def _retention_tables():
    h, L = N_RET_HEADS, RET_CHUNK
    log_gamma = jnp.log(1.0 - 2.0 ** (-5.0 - jnp.arange(h, dtype=_F32)))
    j = jnp.arange(L, dtype=_F32)
    diff = j[:, None] - j[None, :]
    intra = jnp.where(diff[None] >= 0,
                      jnp.exp(jnp.maximum(diff, 0.0)[None] * log_gamma[:, None, None]), 0.0)
    rowb = lambda t: jnp.broadcast_to(t.T[:, :, None], (h, L, LANES))
    k_to_end = jnp.exp((L - 1 - j)[:, None] * log_gamma[None, :])
    q_from_start = jnp.exp((j + 1)[:, None] * log_gamma[None, :])
    chunk_decay = jnp.broadcast_to(jnp.exp(L * log_gamma)[:, None, None], (h, RET_DK, RET_DV))
    return intra, rowb(k_to_end), rowb(q_from_start), chunk_decay


def _mix_xattn_kernel(x_ref, ymla_ref, yret_ref, gate_ref, wout_ref, gx_ref, wqk_ref, wvo_ref, o_ref):
    half = TM_MIX // 2
    rows = [slice(0, half), slice(half, TM_MIX)]
    mem_len = wqk_ref.shape[2] // N_XATTN_HEADS

    def retention_out(r):
        heads = []
        for hd in range(N_RET_HEADS):
            c = slice(hd * RET_DV, (hd + 1) * RET_DV)
            out = yret_ref[0, r, c].astype(_F32)
            mu = jnp.mean(out, axis=-1, keepdims=True)
            cen = out - mu
            var = jnp.mean(cen * cen, axis=-1, keepdims=True)
            g = gate_ref[0, r, c].astype(_F32)
            heads.append((cen * lax.rsqrt(var + EPS) * (g * jax.nn.sigmoid(g))).astype(_BF16))
        return jnp.concatenate(heads, axis=1)

    def out_proj(r):
        return (x_ref[0, r, :] + _dot(ymla_ref[0, r, :], wout_ref[:MLA_WIDTH, :])
                + _dot(retention_out(r), wout_ref[MLA_WIDTH:, :]))

    def scores(x1):
        h = _rms(x1, gx_ref[...]).astype(_BF16)
        return _dot(h, wqk_ref[0])

    def softmax(s):
        heads = []
        for hd in range(N_XATTN_HEADS):
            blk = s[:, hd * mem_len:(hd + 1) * mem_len]
            p = jnp.exp(blk - jnp.max(blk, axis=-1, keepdims=True))
            heads.append((p / jnp.sum(p, axis=-1, keepdims=True)).astype(_BF16))
        return jnp.concatenate(heads, axis=1)

    x1 = [out_proj(r) for r in rows]
    s0 = scores(x1[0])
    s1 = scores(x1[1])
    p0 = softmax(s0)
    o_ref[0, rows[0], :] = x1[0] + _dot(p0, wvo_ref[0])
    p1 = softmax(s1)
    o_ref[0, rows[1], :] = x1[1] + _dot(p1, wvo_ref[0])


def _mix_xattn(x, y_mla, y_ret, ret_gate, w_out, g_xattn, wqk, wvo, later_weights):
    b, s, d = x.shape
    tok = lambda w: pl.BlockSpec((1, TM_MIX, w), lambda bi, i: (bi, i, 0))
    per_batch = lambda a: pl.BlockSpec((1,) + a.shape[1:], lambda bi, i: (bi, 0, 0))
    cast_in, cast_out, cast_shapes = _cast_plumbing(later_weights, (b, s // TM_MIX))
    in_specs = [tok(d), tok(MLA_WIDTH), tok(RET_WIDTH), tok(RET_WIDTH), _const_spec(w_out.shape), _const_spec((1, d)),
                per_batch(wqk), per_batch(wvo)]
    return pl.pallas_call(
        _with_casts(_mix_xattn_kernel, len(in_specs), 1, len(later_weights)),
        out_shape=(jax.ShapeDtypeStruct((b, s, d), _F32), *cast_shapes),
        grid=(b, s // TM_MIX),
        in_specs=in_specs + cast_in,
        out_specs=(tok(d), *cast_out),
        compiler_params=_params(("arbitrary", "arbitrary")),
        name="mix_xattn",
    )(x, y_mla, y_ret, ret_gate, w_out, g_xattn.reshape(1, d), wqk, wvo, *later_weights)


def _conv_ffn_kernel(x_ref, g_ref, win_ref, cw_ref, cb_ref, wout_ref, gfin_ref, o_ref, gate_ref):
    halo = SUBLANES
    half = TM_FFN // 2

    @pl.when(pl.program_id(1) == 0)
    def _():
        gate_ref[:halo, :] = jnp.zeros((halo, D_FF), _F32)

    rows = [slice(0, half), slice(half, TM_FFN)]
    xs = [x_ref[0, r, :] for r in rows]
    hs = [_rms(x, g_ref[...]).astype(_BF16) for x in xs]

    def project(i):
        gate = _dot(hs[i], win_ref[:, :D_FF])
        up = _dot(hs[i], win_ref[:, D_FF:])
        gate_ref[halo + i * half:halo + (i + 1) * half, :] = gate
        return gate, up

    def activate(i, gate, up):
        conv = cb_ref[...] + gate * cw_ref[CONV_W - 1:CONV_W, :]
        for tap in range(CONV_W - 1):
            back = CONV_W - 1 - tap
            lo = halo + i * half - back
            conv = conv + gate_ref[lo:lo + half, :] * cw_ref[tap:tap + 1, :]
        return (conv * jax.nn.sigmoid(conv) * up).astype(_BF16)

    def finish(i, act):
        x3 = xs[i] + _dot(act, wout_ref[...])
        o_ref[0, rows[i], :] = _rms(x3, gfin_ref[...])

    g0, u0 = project(0)
    g1, u1 = project(1)
    a0 = activate(0, g0, u0)
    finish(0, a0)
    a1 = activate(1, g1, u1)
    finish(1, a1)
    gate_ref[:halo, :] = gate_ref[TM_FFN:, :]


def _conv_ffn(x, g_ffn, w_ffn_in, conv_w, conv_b, w_ffn_out, g_final):
    b, s, d = x.shape
    tok = pl.BlockSpec((1, TM_FFN, d), lambda bi, i: (bi, i, 0))
    return pl.pallas_call(
        _conv_ffn_kernel,
        out_shape=jax.ShapeDtypeStruct((b, s, d), _F32),
        grid=(b, s // TM_FFN),
        in_specs=[tok, _const_spec((1, d)), _const_spec(w_ffn_in.shape), _const_spec(conv_w.shape),
                  _const_spec((1, D_FF)), _const_spec(w_ffn_out.shape), _const_spec((1, d))],
        out_specs=tok,
        scratch_shapes=[pltpu.VMEM((TM_FFN + SUBLANES, D_FF), _F32)],
        compiler_params=_params(("arbitrary", "arbitrary")),
        name="conv_ffn",
    )(x, g_ffn.reshape(1, d), w_ffn_in, conv_w, conv_b.reshape(1, D_FF), w_ffn_out, g_final.reshape(1, d))


def kernel(x, mem, positions, g_mix, w_in, g_q_lat, w_uq, g_kv_lat, w_ukv, w_out, g_xattn, g_mem, w_xq,
           w_xkv, w_xo, g_ffn, w_ffn_in, conv_w, conv_b, w_ffn_out, g_final):
    assert w_in.shape[0] == 1, "one layer supported"
    l = 0
    w_in_t = w_in[l].T
    cos_t, sin_t, wlat_t, *wret_t, wqk, wvo = _prep(positions, w_in_t, mem, g_mem[l], w_xkv[l], w_xq[l], w_xo[l])
    wuq_p, wuk_p, wuvt_p = _permute_up_weights(w_uq[l], w_ukv[l])
    q, k, vt, y_ret, ret_gate, w_out_b = _in_proj(
        x, g_mix[l], wlat_t, wret_t, g_q_lat[l], wuq_p, g_kv_lat[l], wuk_p, wuvt_p, cos_t, sin_t, (w_out[l],))
    y_mla, w_ffn_in_b = _mla_attn(q, k, vt, (w_ffn_in[l],))
    x, w_ffn_out_b = _mix_xattn(x, y_mla, y_ret, ret_gate, w_out_b, g_xattn[l], wqk, wvo, (w_ffn_out[l],))
    return _conv_ffn(x, g_ffn[l], w_ffn_in_b, conv_w[l], conv_b[l], w_ffn_out_b, g_final)
```

```python
import functools
import math

import jax
import jax.numpy as jnp
from jax import lax
from jax.experimental import pallas as pl
from jax.experimental.pallas import tpu as pltpu

D_MODEL = 1024
EPS = 1e-6
ROPE_BASE = 10000.0
N_MLA_HEADS = 8
QK_NOPE = 64
QK_ROPE = 32
V_HEAD = 64
Q_LORA = 256
KV_LORA = 128
N_RET_HEADS = 4
RET_DK = 128
RET_DV = 128
RET_CHUNK = 128
MLA_WIDTH = N_MLA_HEADS * V_HEAD
RET_WIDTH = N_RET_HEADS * RET_DV
N_XATTN_HEADS = 4
XATTN_HEAD = D_MODEL // N_XATTN_HEADS
D_FF = 2816
CONV_W = 3

LANES = 128
SUBLANES = 8
_BF16_ROWS = 16
VMEM_LIMIT = 56 * 1024 * 1024

HEAD_PAD = LANES
ROPE_HALF = QK_ROPE // 2
MLA_QK_WIDTH = N_MLA_HEADS * HEAD_PAD
IN_LAT = Q_LORA + KV_LORA + HEAD_PAD

TM_TABLE = 1024
TM_IN = 1024
TQ_MLA = 256
TM_MIX = 1024
TM_FFN = 512

_BF16 = jnp.bfloat16
_F32 = jnp.float32


def _dot(a, b):
    return jnp.dot(a, b, preferred_element_type=_F32)


def _dot_nt(a, b):
    return lax.dot_general(a, b, (((1,), (1,)), ((), ())), preferred_element_type=_F32)


def _dot_tn(a, b):
    return lax.dot_general(a, b, (((0,), (0,)), ((), ())), preferred_element_type=_F32)


def _rms(x, g):
    inv = lax.rsqrt(jnp.mean(x * x, axis=-1, keepdims=True) + EPS)
    return x * inv * g


def _const_spec(shape):
    nd = len(shape)
    return pl.BlockSpec(shape, lambda *_: (0,) * nd, pipeline_mode=pl.Buffered(1))


def _params(sem):
    return pltpu.CompilerParams(dimension_semantics=sem, vmem_limit_bytes=VMEM_LIMIT)


def _with_casts(body, n_in, n_out, n_cast):
    def kern(*refs):
        ins, rest = refs[:n_in], refs[n_in:]
        cast_in, rest = rest[:n_cast], rest[n_cast:]
        outs, rest = rest[:n_out], rest[n_out:]
        cast_out, scratch = rest[:n_cast], rest[n_cast:]
        for src, dst in zip(cast_in, cast_out):
            dst[...] = src[...].astype(dst.dtype)
        body(*ins, *outs, *scratch)
    return kern


def _cast_plumbing(weights, grid):
    steps = math.prod(grid)
    if len(grid) == 1:
        block_of = lambda i: (i, 0)
    else:
        block_of = lambda bi, i: (bi * grid[1] + i, 0)
    in_specs, out_specs, out_shapes = [], [], []
    for w in weights:
        rows, cols = w.shape
        assert rows % steps == 0 and (rows // steps) % _BF16_ROWS == 0, (w.shape, steps)
        spec = pl.BlockSpec((rows // steps, cols), block_of)
        in_specs.append(spec)
        out_specs.append(spec)
        out_shapes.append(jax.ShapeDtypeStruct(w.shape, _BF16))
    return in_specs, out_specs, out_shapes


def _prep_kernel(pos_ref, inv_ref, wlat_ref, wq_ref, wk_ref, wv_ref, wg_ref, mem_ref, gmem_ref, wxk_ref, wxv_ref,
                 wxq_ref, wxo_ref, cos_ref, sin_ref, wlat_out, wq_out, wk_out, wv_out, wg_out, wqk_ref, wvo_ref,
                 *, lat_blocks, n_batch):
    half_rows = TM_IN // 2 // LANES
    lane = lax.broadcasted_iota(jnp.int32, (LANES, LANES), 1)
    column = lambda r: jnp.broadcast_to(pos_ref[r:r + 1, :], (LANES, LANES)).T
    for tile in range(TM_TABLE // TM_IN):
        for rr in range(half_rows):
            r0 = tile * 2 * half_rows + rr
            pos = jnp.where(lane < RET_DK // 2, column(r0), column(r0 + half_rows))
            ang = pos * inv_ref[...]
            out = slice((tile * half_rows + rr) * LANES, (tile * half_rows + rr + 1) * LANES)
            cos_ref[out, :] = jnp.cos(ang)
            sin_ref[out, :] = jnp.sin(ang)
    for src, dst in ((wq_ref, wq_out), (wk_ref, wk_out), (wv_ref, wv_out), (wg_ref, wg_out)):
        dst[...] = src[...].astype(_BF16)
    wlat_out[...] = jnp.where(pl.program_id(0) < lat_blocks, wlat_ref[...], 0.0).astype(_BF16)

    mem_n = _rms(mem_ref[pl.program_id(0) % n_batch], gmem_ref[...]).astype(_BF16)
    k_h = _dot(mem_n, wxk_ref[...].astype(_BF16)).astype(_BF16)
    v_h = _dot(mem_n, wxv_ref[...].astype(_BF16)).astype(_BF16)
    wqk_ref[0] = (_dot_nt(wxq_ref[...].astype(_BF16), k_h) * (1.0 / math.sqrt(XATTN_HEAD))).astype(_BF16)
    wvo_ref[0] = _dot(v_h, wxo_ref[...].astype(_BF16)).astype(_BF16)


def _prep(positions, w_in_t, mem, g_mem, w_xkv, w_xq, w_xo):
    t = positions.size
    steps = t // TM_TABLE
    n_in, d = w_in_t.shape
    b, m, _ = mem.shape
    assert steps == b * N_XATTN_HEADS
    row0 = Q_LORA + KV_LORA + QK_ROPE
    rows = RET_WIDTH // steps
    assert RET_WIDTH % steps == 0 and rows % _BF16_ROWS == 0 and row0 % rows == 0
    assert n_in == row0 + 4 * RET_WIDTH
    pos = positions.astype(_F32).reshape(t // LANES, LANES)
    f_ret = 1.0 / (ROPE_BASE ** (jnp.arange(0, RET_DK, 2, dtype=_F32) / RET_DK))
    inv = jnp.concatenate([f_ret, f_ret])
    group = lambda j: pl.BlockSpec((rows, d), lambda i: (row0 // rows + j * steps + i, 0))
    lat_blocks = row0 // rows
    kr_dst = (Q_LORA + KV_LORA + QK_NOPE) // rows
    assert QK_ROPE == rows and IN_LAT // rows == steps and (Q_LORA + KV_LORA) % rows == 0
    lat_src = pl.BlockSpec((rows, d), lambda i: (jnp.minimum(i, lat_blocks - 1), 0))
    lat_dst = pl.BlockSpec((rows, d), lambda i: (jnp.where(i < lat_blocks - 1, i, jnp.where(
        i == lat_blocks - 1, kr_dst, jnp.where(i <= kr_dst, i - 1, i))), 0))
    table = pl.BlockSpec((TM_TABLE // 2, LANES), lambda i: (i, 0))
    w_out = pl.BlockSpec((rows, d), lambda i: (i, 0))
    nh = N_XATTN_HEADS
    head_cols = lambda off: pl.BlockSpec((d, XATTN_HEAD), lambda i: (0, off + i // b))
    return pl.pallas_call(
        functools.partial(_prep_kernel, lat_blocks=lat_blocks, n_batch=b),
        out_shape=(jax.ShapeDtypeStruct((t // 2, LANES), _F32),) * 2
                  + (jax.ShapeDtypeStruct((IN_LAT, d), _BF16),)
                  + (jax.ShapeDtypeStruct((RET_WIDTH, d), _BF16),) * 4
                  + (jax.ShapeDtypeStruct((b, d, nh * m), _BF16), jax.ShapeDtypeStruct((b, nh * m, d), _BF16)),
        grid=(steps,),
        in_specs=[pl.BlockSpec((TM_TABLE // LANES, LANES), lambda i: (i, 0)),
                  pl.BlockSpec((1, LANES), lambda i: (0, 0)), lat_src, group(0), group(1), group(2), group(3),
                  _const_spec(mem.shape), _const_spec((1, d)),
                  head_cols(0), head_cols(nh), head_cols(0),
                  pl.BlockSpec((XATTN_HEAD, d), lambda i: (i // b, 0))],
        out_specs=(table, table, lat_dst, w_out, w_out, w_out, w_out,
                   pl.BlockSpec((1, d, m), lambda i: (i % b, 0, i // b)),
                   pl.BlockSpec((1, m, d), lambda i: (i % b, i // b, 0))),
        compiler_params=_params(("arbitrary",)),
        name="prep",
    )(pos, inv.reshape(1, LANES), w_in_t, w_in_t, w_in_t, w_in_t, w_in_t, mem, g_mem.reshape(1, d), w_xkv, w_xkv,
      w_xq, w_xo)


def _in_proj_kernel(x_ref, gmix_ref, wlat_ref, wrq_ref, wrk_ref, wrv_ref, wrg_ref, gq_ref, wuq_ref, gkv_ref, wuk_ref,
                    wuvt_ref, cos_ref, sin_ref, intra_ref, kend_ref, qstart_ref, decay_ref,
                    q_ref, k_ref, vt_ref, yret_ref, gate_ref, state_ref):
    h = _rms(x_ref[0], gmix_ref[...]).astype(_BF16)
    lane = lax.broadcasted_iota(jnp.int32, (TM_IN, LANES), 1)
    low = lane < RET_DK // 2
    lo = lax.broadcasted_iota(jnp.int32, (TM_IN // 2, LANES), 1) < RET_DK // 2

    def unpack(t):
        t_roll = pltpu.roll(t, RET_DK // 2, axis=1)
        return jnp.concatenate([jnp.where(lo, t, t_roll), jnp.where(lo, t_roll, t)], axis=0)

    cos_r, sin_full = unpack(cos_ref[...]), unpack(sin_ref[...])
    sin_r = jnp.where(low, -sin_full, sin_full)
    ratio = (RET_DK // 2) // ROPE_HALF
    rope_lane = (lane >= QK_NOPE) & (lane < QK_NOPE + QK_ROPE)
    src = jnp.where(lane < QK_NOPE + ROPE_HALF, lane - QK_NOPE, lane - QK_NOPE - ROPE_HALF) * ratio
    src = jnp.where(rope_lane, src, 0)
    g_cos = jnp.take_along_axis(cos_r, src, axis=1)
    g_sin = jnp.take_along_axis(sin_full, src, axis=1)
    cos_m = jnp.where(rope_lane, g_cos, 1.0)
    sin_m = jnp.where(rope_lane, jnp.where(lane < QK_NOPE + ROPE_HALF, -g_sin, g_sin), 0.0)

    first_half = lane < QK_NOPE + ROPE_HALF

    def swap_halves(t):
        return jnp.where(first_half, pltpu.roll(t, LANES - ROPE_HALF, axis=1), pltpu.roll(t, ROPE_HALF, axis=1))

    lat = _dot_nt(h, wlat_ref[...])
    c_q = lat[:, :Q_LORA]
    c_kv = lat[:, Q_LORA:Q_LORA + KV_LORA]
    kr = lat[:, Q_LORA + KV_LORA:IN_LAT]
    k_rope = kr * cos_m + swap_halves(kr) * sin_m

    def ret_rope(w_ref, mult):
        r = _dot_nt(h, w_ref[...])
        heads = []
        for hd in range(N_RET_HEADS):
            rh = r[:, hd * RET_DK:(hd + 1) * RET_DK]
            roped = rh * cos_r + pltpu.roll(rh, RET_DK // 2, axis=1) * sin_r
            if mult is not None:
                roped = roped * mult
            heads.append(roped.astype(_BF16))
        return heads

    rq = ret_rope(wrq_ref, None)
    rk = ret_rope(wrk_ref, RET_DK ** -0.5)
    rv = _dot_nt(h, wrv_ref[...]).astype(_BF16)

    cqn = _rms(c_q, gq_ref[...]).astype(_BF16)
    q = _dot(cqn, wuq_ref[...])
    scale = math.log2(math.e) / math.sqrt(QK_NOPE + QK_ROPE)
    for hd in range(N_MLA_HEADS):
        sl = slice(hd * HEAD_PAD, (hd + 1) * HEAD_PAD)
        q_ref[0, :, sl] = ((q[:, sl] * cos_m + swap_halves(q[:, sl]) * sin_m) * scale).astype(_BF16)

    ckvn = _rms(c_kv, gkv_ref[...]).astype(_BF16)
    k_nope = _dot(ckvn, wuk_ref[...])
    for hd in range(N_MLA_HEADS):
        sl = slice(hd * HEAD_PAD, (hd + 1) * HEAD_PAD)
        k_ref[0, :, sl] = (k_nope[:, sl] + k_rope).astype(_BF16)
    v_t = _dot_nt(wuvt_ref[...], ckvn).astype(_BF16)
    for j in range(TM_IN // TQ_MLA):
        vt_ref[0, j] = v_t[:, j * TQ_MLA:(j + 1) * TQ_MLA]

    rg = _dot_nt(h, wrg_ref[...]).astype(_BF16)
    gate_ref[0] = rg
    _retention_tile(rq, rk, rv, intra_ref, kend_ref, qstart_ref, decay_ref, state_ref, yret_ref)


def _retention_tile(q_heads, k_heads, v, intra_ref, kend_ref, qstart_ref, decay_ref, state_ref, o_ref):
    @pl.when(pl.program_id(1) == 0)
    def _():
        state_ref[...] = jnp.zeros(state_ref.shape, _F32)

    L = RET_CHUNK
    n_chunks = v.shape[0] // L
    units = [(c, hd) for c in range(n_chunks) for hd in range(N_RET_HEADS)]
    rows = lambda c: slice(c * L, (c + 1) * L)
    cols = lambda hd: slice(hd * RET_DV, (hd + 1) * RET_DV)
    scores, chunk_kv = {}, {}
    for c, hd in units:
        scores[c, hd] = _dot_nt(q_heads[hd][rows(c)], k_heads[hd][rows(c)])
    for c, hd in units:
        v_dec = (v[rows(c), cols(hd)].astype(_F32) * kend_ref[hd]).astype(_BF16)
        chunk_kv[c, hd] = _dot_tn(k_heads[hd][rows(c)], v_dec)
    prev_state = {}
    for hd in range(N_RET_HEADS):
        state = state_ref[hd]
        for c in range(n_chunks):
            prev_state[c, hd] = state.astype(_BF16)
            state = decay_ref[hd] * state + chunk_kv[c, hd]
        state_ref[hd] = state
    inner, cross = {}, {}
    for c, hd in units:
        inner[c, hd] = _dot((scores[c, hd] * intra_ref[hd]).astype(_BF16), v[rows(c), cols(hd)])
    for c, hd in units:
        cross[c, hd] = _dot(q_heads[hd][rows(c)], prev_state[c, hd])
    for c, hd in units:
        out = inner[c, hd] + cross[c, hd] * qstart_ref[hd]
        o_ref[0, rows(c), cols(hd)] = out.astype(o_ref.dtype)


def _permute_up_weights(w_uq, w_ukv):
    wq = w_uq.reshape(Q_LORA, N_MLA_HEADS, QK_NOPE + QK_ROPE)
    nope, r1, r2 = wq[..., :QK_NOPE], wq[..., QK_NOPE:QK_NOPE + ROPE_HALF], wq[..., QK_NOPE + ROPE_HALF:]
    z32 = jnp.zeros((Q_LORA, N_MLA_HEADS, HEAD_PAD - QK_NOPE - QK_ROPE), w_uq.dtype)
    wuq_p = jnp.concatenate([nope, r1, r2, z32], axis=-1).reshape(Q_LORA, MLA_QK_WIDTH).astype(_BF16)

    wkv = w_ukv.reshape(KV_LORA, N_MLA_HEADS, QK_NOPE + V_HEAD)
    zk = jnp.zeros((KV_LORA, N_MLA_HEADS, HEAD_PAD - QK_NOPE), w_ukv.dtype)
    wk = jnp.concatenate([wkv[..., :QK_NOPE], zk], axis=-1).reshape(KV_LORA, MLA_QK_WIDTH)
    wv_t = wkv[..., QK_NOPE:].reshape(KV_LORA, MLA_WIDTH).T
    return wuq_p, wk.astype(_BF16), wv_t.astype(_BF16)


def _in_proj(x, g_mix, wlat_t, wret_t, g_q_lat, wuq_p, g_kv_lat, wuk_p, wuvt_p, cos_t, sin_t, later_weights):
    b, s, d = x.shape
    nt = s // TM_IN
    per = TM_IN // TQ_MLA
    tok = lambda w: pl.BlockSpec((1, TM_IN, w), lambda bi, i: (bi, i, 0))
    tab = pl.BlockSpec((TM_IN // 2, LANES), lambda bi, i: (bi * nt + i, 0))
    bf = lambda w: jax.ShapeDtypeStruct((b, s, w), _BF16)
    ret_tables = _retention_tables()
    cast_in, cast_out, cast_shapes = _cast_plumbing(later_weights, (b, nt))
    in_specs = ([tok(d), _const_spec((1, d)), _const_spec(wlat_t.shape)] + [_const_spec(w.shape) for w in wret_t]
                + [_const_spec((1, Q_LORA)), _const_spec(wuq_p.shape), _const_spec((1, KV_LORA)),
                   _const_spec(wuk_p.shape), _const_spec(wuvt_p.shape), tab, tab]
                + [_const_spec(t.shape) for t in ret_tables])
    out_specs = (tok(MLA_QK_WIDTH), tok(MLA_QK_WIDTH),
                 pl.BlockSpec((1, per, MLA_WIDTH, TQ_MLA), lambda bi, i: (bi, i, 0, 0)), tok(RET_WIDTH), tok(RET_WIDTH))
    return pl.pallas_call(
        _with_casts(_in_proj_kernel, len(in_specs), len(out_specs), len(later_weights)),
        out_shape=(bf(MLA_QK_WIDTH), bf(MLA_QK_WIDTH),
                   jax.ShapeDtypeStruct((b, s // TQ_MLA, MLA_WIDTH, TQ_MLA), _BF16), bf(RET_WIDTH), bf(RET_WIDTH),
                   *cast_shapes),
        grid=(b, nt),
        in_specs=in_specs + cast_in,
        out_specs=(*out_specs, *cast_out),
        scratch_shapes=[pltpu.VMEM((N_RET_HEADS, RET_DK, RET_DV), _F32)],
        compiler_params=_params(("arbitrary", "arbitrary")),
        name="in_proj",
    )(x, g_mix.reshape(1, d), wlat_t, *wret_t, g_q_lat.reshape(1, Q_LORA), wuq_p, g_kv_lat.reshape(1, KV_LORA),
      wuk_p, wuvt_p, cos_t, sin_t, *ret_tables, *later_weights)


_MASK_VALUE = -0.7 * float(jnp.finfo(jnp.float32).max)
_MLA_AHEAD = 4
_MLA_TILES_PER_ITER = 5
_MLA_DENOM_ROWS = 16


def _mla_attn_kernel(qlo_ref, qhi_ref, k_ref, vt_ref, o_ref, q_ref, s_ref, m_ref, acc_ref):
    tq = TQ_MLA
    n_tiles = k_ref.shape[1] // tq
    lo = pl.program_id(1)
    hi = n_tiles - 1 - lo
    q_ref[0] = qlo_ref[0]
    q_ref[1] = qhi_ref[0]
    m_ref[...] = jnp.full(m_ref.shape, _MASK_VALUE, _F32)
    acc_ref[...] = jnp.zeros(acc_ref.shape, _F32)
    ones_rows = jnp.ones((_MLA_DENOM_ROWS, tq), _BF16)

    def step_args(t):
        sel = (t > lo).astype(jnp.int32)
        return sel, t - 1 - sel * lo

    def scores(sel, kk, hd):
        sl = slice(hd * HEAD_PAD, (hd + 1) * HEAD_PAD)
        ks = pl.multiple_of(kk * tq, tq)
        s_ref[hd] = _dot_nt(k_ref[0, pl.ds(ks, tq), sl], q_ref[sel, :, sl])

    def softmax(sel, hd, keep):
        row = slice(hd, hd + 1)

        def st():
            s = s_ref[hd]
            return s if keep is None else jnp.where(keep, s, _MASK_VALUE)

        m_prev = m_ref[sel, row, :]
        m_next = jnp.maximum(m_prev, jnp.max(st(), axis=0, keepdims=True))
        m_ref[sel, row, :] = m_next
        return jnp.exp2(m_prev - m_next), jnp.exp2(st() - m_next).astype(_BF16)

    def accumulate(sel, kk, hd, alpha, p):
        vt = jnp.concatenate([vt_ref[0, kk, hd * V_HEAD:(hd + 1) * V_HEAD, :], ones_rows], axis=0)
        acc_ref[sel, hd] = acc_ref[sel, hd] * alpha + _dot(vt, p)

    def tile(sel, kk, diagonal, nxt):
        keep = None
        if diagonal:
            key = lax.broadcasted_iota(jnp.int32, (tq, tq), 0)
            qry = lax.broadcasted_iota(jnp.int32, (tq, tq), 1)
            keep = key <= qry
        pending = None
        for hd in range(N_MLA_HEADS):
            ahead = hd + _MLA_AHEAD
            if ahead < N_MLA_HEADS:
                scores(sel, kk, ahead)
            elif nxt is not None:
                scores(nxt[0], nxt[1], ahead - N_MLA_HEADS)
            current = softmax(sel, hd, keep)
            if pending is not None:
                accumulate(sel, kk, hd - 1, *pending)
            pending = current
        accumulate(sel, kk, N_MLA_HEADS - 1, *pending)

    for hd in range(_MLA_AHEAD):
        scores(0, lo, hd)
    tile(0, lo, True, step_args(1))

    def body(it, carry):
        for u in range(_MLA_TILES_PER_ITER):
            t = 1 + it * _MLA_TILES_PER_ITER + u
            last = t + 1 == n_tiles
            sel_n, kk_n = step_args(t + 1)
            tile(*step_args(t), False, (jnp.where(last, 1, sel_n), jnp.where(last, hi, kk_n)))
        return carry

    lax.fori_loop(0, (n_tiles - 1) // _MLA_TILES_PER_ITER, body, 0)
    tile(1, hi, True, None)

    for sel, qt in ((0, lo), (1, hi)):
        out_t = jnp.concatenate([acc_ref[sel, hd, :V_HEAD, :] / acc_ref[sel, hd, V_HEAD:V_HEAD + 1, :]
                                 for hd in range(N_MLA_HEADS)], axis=0)
        o_ref[0, pl.ds(pl.multiple_of(qt * tq, tq), tq), :] = out_t.T.astype(o_ref.dtype)


def _mla_attn(q, k, vt, later_weights):
    b, s, w = q.shape
    nk = s // TQ_MLA
    assert nk % 2 == 0 and (nk - 1) % _MLA_TILES_PER_ITER == 0
    cast_in, cast_out, cast_shapes = _cast_plumbing(later_weights, (b, nk // 2))
    in_specs = [pl.BlockSpec((1, TQ_MLA, w), lambda bi, i: (bi, i, 0)),
                pl.BlockSpec((1, TQ_MLA, w), lambda bi, i: (bi, nk - 1 - i, 0)),
                pl.BlockSpec((1, s, w), lambda bi, i: (bi, 0, 0)),
                pl.BlockSpec((1, nk, MLA_WIDTH, TQ_MLA), lambda bi, i: (bi, 0, 0, 0))]
    return pl.pallas_call(
        _with_casts(_mla_attn_kernel, len(in_specs), 1, len(later_weights)),
        out_shape=(jax.ShapeDtypeStruct((b, s, MLA_WIDTH), _BF16), *cast_shapes),
        grid=(b, nk // 2),
        in_specs=in_specs + cast_in,
        out_specs=(pl.BlockSpec((1, s, MLA_WIDTH), lambda bi, i: (bi, 0, 0)), *cast_out),
        scratch_shapes=[pltpu.VMEM((2, TQ_MLA, w), _BF16),
                        pltpu.VMEM((N_MLA_HEADS, TQ_MLA, TQ_MLA), _F32),
                        pltpu.VMEM((2, N_MLA_HEADS, TQ_MLA), _F32),
                        pltpu.VMEM((2, N_MLA_HEADS, V_HEAD + _MLA_DENOM_ROWS, TQ_MLA), _F32)],
        compiler_params=_params(("arbitrary", "arbitrary")),
        name="mla_attn",
    )(q, q, k, vt, *later_weights)


def _retention_tables():
    h, L = N_RET_HEADS, RET_CHUNK
    log_gamma = jnp.log(1.0 - 2.0 ** (-5.0 - jnp.arange(h, dtype=_F32)))
    j = jnp.arange(L, dtype=_F32)
    diff = j[:, None] - j[None, :]
    intra = jnp.where(diff[None] >= 0,
                      jnp.exp(jnp.maximum(diff, 0.0)[None] * log_gamma[:, None, None]), 0.0)
    rowb = lambda t: jnp.broadcast_to(t.T[:, :, None], (h, L, LANES))
    k_to_end = jnp.exp((L - 1 - j)[:, None] * log_gamma[None, :])
    q_from_start = jnp.exp((j + 1)[:, None] * log_gamma[None, :])
    chunk_decay = jnp.broadcast_to(jnp.exp(L * log_gamma)[:, None, None], (h, RET_DK, RET_DV))
    return intra, rowb(k_to_end), rowb(q_from_start), chunk_decay


def _mix_xattn_kernel(x_ref, ymla_ref, yret_ref, gate_ref, wout_ref, gx_ref, wqk_ref, wvo_ref, o_ref):
    half = TM_MIX // 2
    rows = [slice(0, half), slice(half, TM_MIX)]
    mem_len = wqk_ref.shape[2] // N_XATTN_HEADS

    def retention_out(r):
        heads = []
        for hd in range(N_RET_HEADS):
            c = slice(hd * RET_DV, (hd + 1) * RET_DV)
            out = yret_ref[0, r, c].astype(_F32)
            mu = jnp.mean(out, axis=-1, keepdims=True)
            cen = out - mu
            var = jnp.mean(cen * cen, axis=-1, keepdims=True)
            g = gate_ref[0, r, c].astype(_F32)
            heads.append((cen * lax.rsqrt(var + EPS) * (g * jax.nn.sigmoid(g))).astype(_BF16))
        return jnp.concatenate(heads, axis=1)

    def out_proj(r):
        return (x_ref[0, r, :] + _dot(ymla_ref[0, r, :], wout_ref[:MLA_WIDTH, :])
                + _dot(retention_out(r), wout_ref[MLA_WIDTH:, :]))

    def scores(x1):
        h = _rms(x1, gx_ref[...]).astype(_BF16)
        return _dot(h, wqk_ref[0])

    def softmax(s):
        heads = []
        for hd in range(N_XATTN_HEADS):
            blk = s[:, hd * mem_len:(hd + 1) * mem_len]
            p = jnp.exp(blk - jnp.max(blk, axis=-1, keepdims=True))
            heads.append((p / jnp.sum(p, axis=-1, keepdims=True)).astype(_BF16))
        return jnp.concatenate(heads, axis=1)

    x1 = [out_proj(r) for r in rows]
    s0 = scores(x1[0])
    s1 = scores(x1[1])
    p0 = softmax(s0)
    o_ref[0, rows[0], :] = x1[0] + _dot(p0, wvo_ref[0])
    p1 = softmax(s1)
    o_ref[0, rows[1], :] = x1[1] + _dot(p1, wvo_ref[0])


def _mix_xattn(x, y_mla, y_ret, ret_gate, w_out, g_xattn, wqk, wvo, later_weights):
    b, s, d = x.shape
    tok = lambda w: pl.BlockSpec((1, TM_MIX, w), lambda bi, i: (bi, i, 0))
    per_batch = lambda a: pl.BlockSpec((1,) + a.shape[1:], lambda bi, i: (bi, 0, 0))
    cast_in, cast_out, cast_shapes = _cast_plumbing(later_weights, (b, s // TM_MIX))
    in_specs = [tok(d), tok(MLA_WIDTH), tok(RET_WIDTH), tok(RET_WIDTH), _const_spec(w_out.shape), _const_spec((1, d)),
                per_batch(wqk), per_batch(wvo)]
    return pl.pallas_call(
        _with_casts(_mix_xattn_kernel, len(in_specs), 1, len(later_weights)),
        out_shape=(jax.ShapeDtypeStruct((b, s, d), _F32), *cast_shapes),
        grid=(b, s // TM_MIX),
        in_specs=in_specs + cast_in,
        out_specs=(tok(d), *cast_out),
        compiler_params=_params(("arbitrary", "arbitrary")),
        name="mix_xattn",
    )(x, y_mla, y_ret, ret_gate, w_out, g_xattn.reshape(1, d), wqk, wvo, *later_weights)


_MXU_TILE = 256
_FF_SPLIT = (D_FF // _MXU_TILE + 1) // 2 * _MXU_TILE
_FF_CHUNKS = (slice(0, _FF_SPLIT), slice(_FF_SPLIT, D_FF))


def _conv_ffn_kernel(x_ref, g_ref, win_ref, cw_ref, cb_ref, wout_ref, gfin_ref, o_ref, gate_ref):
    halo = SUBLANES
    half = TM_FFN // 2

    @pl.when(pl.program_id(1) == 0)
    def _():
        gate_ref[:halo, :] = jnp.zeros((halo, D_FF), _F32)

    rows = [slice(0, half), slice(half, TM_FFN)]
    xs = [x_ref[0, r, :] for r in rows]
    hs = [_rms(x, g_ref[...]).astype(_BF16) for x in xs]

    def project(i, c):
        gate = _dot(hs[i], win_ref[:, c])
        up = _dot(hs[i], win_ref[:, D_FF + c.start:D_FF + c.stop])
        gate_ref[halo + i * half:halo + (i + 1) * half, c] = gate
        return gate, up

    def activate(i, c, gate, up):
        conv = cb_ref[:, c] + gate * cw_ref[CONV_W - 1:CONV_W, c]
        for tap in range(CONV_W - 1):
            back = CONV_W - 1 - tap
            lo = halo + i * half - back
            conv = conv + gate_ref[lo:lo + half, c] * cw_ref[tap:tap + 1, c]
        return (conv * jax.nn.sigmoid(conv) * up).astype(_BF16)

    def finish(i, acts):
        x3 = xs[i] + _dot(jnp.concatenate(acts, axis=1), wout_ref[...])
        o_ref[0, rows[i], :] = _rms(x3, gfin_ref[...])

    proj = {(i, j): project(i, c) for i in range(2) for j, c in enumerate(_FF_CHUNKS)}
    for i in range(2):
        finish(i, [activate(i, c, *proj[i, j]) for j, c in enumerate(_FF_CHUNKS)])
    gate_ref[:halo, :] = gate_ref[TM_FFN:, :]


def _conv_ffn(x, g_ffn, w_ffn_in, conv_w, conv_b, w_ffn_out, g_final):
    b, s, d = x.shape
    tok = pl.BlockSpec((1, TM_FFN, d), lambda bi, i: (bi, i, 0))
    return pl.pallas_call(
        _conv_ffn_kernel,
        out_shape=jax.ShapeDtypeStruct((b, s, d), _F32),
        grid=(b, s // TM_FFN),
        in_specs=[tok, _const_spec((1, d)), _const_spec(w_ffn_in.shape), _const_spec(conv_w.shape),
                  _const_spec((1, D_FF)), _const_spec(w_ffn_out.shape), _const_spec((1, d))],
        out_specs=tok,
        scratch_shapes=[pltpu.VMEM((TM_FFN + SUBLANES, D_FF), _F32)],
        compiler_params=_params(("arbitrary", "arbitrary")),
        name="conv_ffn",
    )(x, g_ffn.reshape(1, d), w_ffn_in, conv_w, conv_b.reshape(1, D_FF), w_ffn_out, g_final.reshape(1, d))


def kernel(x, mem, positions, g_mix, w_in, g_q_lat, w_uq, g_kv_lat, w_ukv, w_out, g_xattn, g_mem, w_xq,
           w_xkv, w_xo, g_ffn, w_ffn_in, conv_w, conv_b, w_ffn_out, g_final):
    assert w_in.shape[0] == 1, "one layer supported"
    l = 0
    w_in_t = w_in[l].T
    cos_t, sin_t, wlat_t, *wret_t, wqk, wvo = _prep(positions, w_in_t, mem, g_mem[l], w_xkv[l], w_xq[l], w_xo[l])
    wuq_p, wuk_p, wuvt_p = _permute_up_weights(w_uq[l], w_ukv[l])
    q, k, vt, y_ret, ret_gate, w_out_b = _in_proj(
        x, g_mix[l], wlat_t, wret_t, g_q_lat[l], wuq_p, g_kv_lat[l], wuk_p, wuvt_p, cos_t, sin_t, (w_out[l],))
    y_mla, w_ffn_in_b = _mla_attn(q, k, vt, (w_ffn_in[l],))
    x, w_ffn_out_b = _mix_xattn(x, y_mla, y_ret, ret_gate, w_out_b, g_xattn[l], wqk, wvo, (w_ffn_out[l],))
    return _conv_ffn(x, g_ffn[l], w_ffn_in_b, conv_w[l], conv_b[l], w_ffn_out_b, g_final)
```
